```python
import math
import jax
import jax.numpy as jnp
from jax import lax
import numpy as np

D_MODEL = 1024
BATCH = 8
SEQ = 4096
DEPTH = 4

GRID_W = 64
CTX_LEN = 256
N_MIXERS = 3
NORM_EPS = 1e-6

MLA_HEADS = 16
MLA_Q_LORA = 384
MLA_KV_LORA = 256
MLA_NOPE = 64
MLA_ROPE = 32
MLA_V = 64
MLA_QK = MLA_NOPE + MLA_ROPE
ROPE_AXIS_FREQS = MLA_ROPE // 4
ROPE_THETA = 10000.0
Q_BLOCK = 128

S5_GROUP = 16
S5_GROUPS = D_MODEL // S5_GROUP
S5_STATE = 64
S5_DT_MIN = 0.01
S5_DT_MAX = 0.1
S5_MAX_RE = -1e-4

POOL_WINDOWS = (2, 4, 8, 16)
POOL_GROUP = D_MODEL // len(POOL_WINDOWS)

FFN_DIM = 2816
N_EXPERTS = 8
TOP_K = 2
EXPERT_DIM = 2816

N_MLA_LAYERS = (DEPTH + 2) // 3
N_S5_LAYERS = (DEPTH + 1) // 3
N_POOL_LAYERS = DEPTH // 3
N_DENSE_LAYERS = (DEPTH + 1) // 2
N_MOE_LAYERS = DEPTH // 2

kernel_name = 'hybrid_mla_s5_pool_moe_diffusion'


def rmsnorm(x, g):
    x32 = x.astype(jnp.float32)
    y = x32 * lax.rsqrt(jnp.mean(x32 * x32, axis=-1, keepdims=True) + NORM_EPS)
    return (y * g.astype(jnp.float32)).astype(x.dtype)


def modulate(h, shift, scale):
    return h * (1 + scale) + shift


def rope_2d(x, ang):
    b, l, h, r = x.shape
    xr = x.reshape(b, l, h, 2, 2, r // 4)
    cos = jnp.cos(ang).astype(x.dtype)[None, :, None]
    sin = jnp.sin(ang).astype(x.dtype)[None, :, None]
    x1, x2 = xr[..., 0, :], xr[..., 1, :]
    out = jnp.stack([x1 * cos - x2 * sin, x2 * cos + x1 * sin], axis=-2)
    return out.reshape(b, l, h, r)


def attend(q, k, v):
    s = jnp.einsum('bqhd,bkhd->bhqk', q, k, preferred_element_type=jnp.float32) * (MLA_QK ** -0.5)
    p = jax.nn.softmax(s, axis=-1).astype(v.dtype)
    return jnp.einsum('bhqk,bkhd->bqhd', p, v)


def mla_project_q(h, w_dq, q_norm, w_uq):
    b, l, _ = h.shape
    q = rmsnorm(h @ w_dq, q_norm) @ w_uq
    return q.reshape(b, l, MLA_HEADS, MLA_QK)


def mla_project_kv(h, w_dkv, kv_norm, w_ukv):
    b, l, _ = h.shape
    kv = h @ w_dkv
    c_kv = rmsnorm(kv[..., :MLA_KV_LORA], kv_norm)
    k_rope = kv[..., MLA_KV_LORA:][:, :, None, :]
    ukv = (c_kv @ w_ukv).reshape(b, l, MLA_HEADS, MLA_NOPE + MLA_V)
    return ukv[..., :MLA_NOPE], k_rope, ukv[..., MLA_NOPE:]


def mla_full_keys(k_nope, k_rope):
    return jnp.concatenate([k_nope, jnp.broadcast_to(k_rope, k_nope.shape[:-1] + (MLA_ROPE,))], axis=-1)


def mla_mixer(h_lat, h_ctx, ang, w_dq, q_norm, w_uq, w_dkv, kv_norm, w_ukv, w_o, ctx_out):
    b, l, _ = h_lat.shape
    lc = h_ctx.shape[1]
    q_l = mla_project_q(h_lat, w_dq, q_norm, w_uq)
    q_l = jnp.concatenate([q_l[..., :MLA_NOPE], rope_2d(q_l[..., MLA_NOPE:], ang)], axis=-1)
    kn_l, kr_l, v_l = mla_project_kv(h_lat, w_dkv, kv_norm, w_ukv)
    k_l = mla_full_keys(kn_l, rope_2d(kr_l, ang))
    kn_c, kr_c, v_c = mla_project_kv(h_ctx, w_dkv, kv_norm, w_ukv)
    k_c = mla_full_keys(kn_c, kr_c)
    k_all = jnp.concatenate([k_c, k_l], axis=1)
    v_all = jnp.concatenate([v_c, v_l], axis=1)
    n_blocks = l // Q_BLOCK
    q_blocks = q_l.reshape(b, n_blocks, Q_BLOCK, MLA_HEADS, MLA_QK).transpose(1, 0, 2, 3, 4)
    o_l = lax.map(lambda qb: attend(qb, k_all, v_all), q_blocks)
    o_l = o_l.transpose(1, 0, 2, 3, 4).reshape(b, l, MLA_HEADS * MLA_V)
    y_l = o_l @ w_o
    y_c = None
    if ctx_out:
        q_c = mla_project_q(h_ctx, w_dq, q_norm, w_uq)
        y_c = attend(q_c, k_c, v_c).reshape(b, lc, MLA_HEADS * MLA_V) @ w_o
    return y_l, y_c


def ssm_combine(e1, e2):
    a1, b1 = e1
    a2, b2 = e2
    return a1 * a2, a2 * b1 + b2


def ssm_scan(lam_bar, bu, reverse):
    a = jnp.broadcast_to(lam_bar, (1, bu.shape[1]) + lam_bar.shape)
    _, s = lax.associative_scan(ssm_combine, (a, bu), reverse=reverse, axis=1)
    return s


def s5_glu(y, w_glu):
    g = jax.nn.gelu(y)
    val, gate = jnp.split(g @ w_glu, 2, axis=-1)
    return val * jax.nn.sigmoid(gate)


def s5_mixer(h_lat, h_ctx, a_re, a_im, log_dt, b_re, b_im, c_re, c_im, d, w_glu, ctx_out):
    f32 = jnp.float32
    b, l, dm = h_lat.shape
    lc = h_ctx.shape[1]
    d_g = d.astype(f32).reshape(S5_GROUPS, S5_GROUP)
    u_l = h_lat.astype(f32).reshape(b, l, S5_GROUPS, S5_GROUP)
    u_c = h_ctx.astype(f32).reshape(b, lc, S5_GROUPS, S5_GROUP)
    y_l = d_g * u_l
    y_c = d_g * u_c if ctx_out else None
    steps = jnp.arange(1, l + 1, dtype=f32)
    for direction in range(2):
        reverse = direction == 1
        lam = lax.complex(jnp.minimum(a_re[direction].astype(f32), S5_MAX_RE), a_im[direction].astype(f32))
        lam_dt = lam * jnp.exp(log_dt[direction].astype(f32))[:, None]
        lam_bar = jnp.exp(lam_dt)
        b_bar = ((lam_bar - 1.0) / lam)[..., None] * lax.complex(b_re[direction].astype(f32), b_im[direction].astype(f32))
        c_mat = lax.complex(c_re[direction].astype(f32), c_im[direction].astype(f32))
        s_c = ssm_scan(lam_bar, jnp.einsum('blgn,gpn->blgp', u_c.astype(jnp.complex64), b_bar), reverse)
        s0 = s_c[:, 0] if reverse else s_c[:, -1]
        carry = jnp.exp(lam_dt[None] * (steps[::-1] if reverse else steps)[:, None, None])
        s_l = ssm_scan(lam_bar, jnp.einsum('blgn,gpn->blgp', u_l.astype(jnp.complex64), b_bar), reverse) + carry[None] * s0[:, None]
        y_l = y_l + jnp.real(jnp.einsum('blgp,gnp->blgn', s_l, c_mat))
        if ctx_out:
            y_c = y_c + jnp.real(jnp.einsum('blgp,gnp->blgn', s_c, c_mat))
    out_l = s5_glu(y_l.reshape(b, l, dm).astype(h_lat.dtype), w_glu)
    out_c = s5_glu(y_c.reshape(b, lc, dm).astype(h_ctx.dtype), w_glu) if ctx_out else None
    return out_l, out_c


def centred_pool_residual(h, window):
    b, l, ch = h.shape
    h32 = h.astype(jnp.float32)
    cs = jnp.concatenate([jnp.zeros((b, 1, ch), jnp.float32), jnp.cumsum(h32, axis=1)], axis=1)
    t = jnp.arange(l)
    lo = jnp.clip(t - window // 2, 0, l)
    hi = jnp.clip(t + window - window // 2, 0, l)
    mean = (cs[:, hi] - cs[:, lo]) / (hi - lo).astype(jnp.float32)[None, :, None]
    return mean - h32


def pool_mixer(h, w_pool, scale):
    b, l, dm = h.shape
    groups = [centred_pool_residual(h[..., g * POOL_GROUP:(g + 1) * POOL_GROUP], w) for g, w in enumerate(POOL_WINDOWS)]
    p = jnp.stack(groups, axis=2).astype(h.dtype)
    y = jnp.einsum('blgc,gcd->blgd', p, w_pool).reshape(b, l, dm)
    return y * scale


def swiglu(h, w_gu, w_down):
    g, u = jnp.split(h @ w_gu, 2, axis=-1)
    return (jax.nn.silu(g) * u) @ w_down


def moe_swiglu(h, w_router, b_router, w_gu, w_down):
    logits = jnp.einsum('bld,de->ble', h, w_router, preferred_element_type=jnp.float32) + b_router.astype(jnp.float32)
    top_val, top_idx = lax.top_k(logits, TOP_K)
    top_w = jax.nn.softmax(top_val, axis=-1)
    gates = jnp.einsum('blk,blke->ble', top_w, jax.nn.one_hot(top_idx, N_EXPERTS, dtype=jnp.float32)).astype(h.dtype)
    out = jnp.zeros_like(h)
    for e in range(N_EXPERTS):
        out = out + gates[..., e:e + 1] * swiglu(h, w_gu[e], w_down[e])
    return out


def setup_inputs(seed: int = 0) -> dict:
    key = jax.random.key(seed)
    ks = iter(list(jax.random.split(key, 40)))
    f32 = jnp.float32
    dm = D_MODEL
    g_, p_, n_ = S5_GROUPS, S5_STATE, S5_GROUP

    def nrm(shape, scale):
        return jax.random.normal(next(ks), shape, f32) * scale

    def gain(shape):
        return 1.0 + nrm(shape, 0.02)

    a_im0 = jnp.pi * jnp.arange(p_, dtype=f32)
    return {
        'x': nrm((BATCH, SEQ, dm), 1.0),
        'c': nrm((BATCH, dm), 1.0),
        'ctx': nrm((BATCH, CTX_LEN, dm), 1.0),
        'c_ctx': nrm((dm,), 1.0),
        'norm1_g': gain((DEPTH, dm)),
        'norm2_g': gain((DEPTH, dm)),
        'ada_w': nrm((DEPTH, dm, 6 * dm), 0.5 * dm ** -0.5),
        'ada_b': nrm((DEPTH, 6 * dm), 0.02),
        'final_norm_g': gain((dm,)),
        'mla_w_dq': nrm((N_MLA_LAYERS, dm, MLA_Q_LORA), dm ** -0.5),
        'mla_q_norm': gain((N_MLA_LAYERS, MLA_Q_LORA)),
        'mla_w_uq': nrm((N_MLA_LAYERS, MLA_Q_LORA, MLA_HEADS * MLA_QK), MLA_Q_LORA ** -0.5),
        'mla_w_dkv': nrm((N_MLA_LAYERS, dm, MLA_KV_LORA + MLA_ROPE), dm ** -0.5),
        'mla_kv_norm': gain((N_MLA_LAYERS, MLA_KV_LORA)),
        'mla_w_ukv': nrm((N_MLA_LAYERS, MLA_KV_LORA, MLA_HEADS * (MLA_NOPE + MLA_V)), MLA_KV_LORA ** -0.5),
        'mla_w_o': nrm((N_MLA_LAYERS, MLA_HEADS * MLA_V, dm), (MLA_HEADS * MLA_V) ** -0.5),
        's5_a_re': -0.5 + nrm((N_S5_LAYERS, 2, g_, p_), 0.01),
        's5_a_im': a_im0 + nrm((N_S5_LAYERS, 2, g_, p_), 0.01),
        's5_log_dt': jax.random.uniform(next(ks), (N_S5_LAYERS, 2, g_), f32, math.log(S5_DT_MIN), math.log(S5_DT_MAX)),
        's5_b_re': nrm((N_S5_LAYERS, 2, g_, p_, n_), (2 * n_) ** -0.5),
        's5_b_im': nrm((N_S5_LAYERS, 2, g_, p_, n_), (2 * n_) ** -0.5),
        's5_c_re': nrm((N_S5_LAYERS, 2, g_, n_, p_), p_ ** -0.5),
        's5_c_im': nrm((N_S5_LAYERS, 2, g_, n_, p_), p_ ** -0.5),
        's5_d': nrm((N_S5_LAYERS, dm), 1.0),
        's5_w_glu': nrm((N_S5_LAYERS, dm, 2 * dm), dm ** -0.5),
        'pool_w': nrm((N_POOL_LAYERS, len(POOL_WINDOWS), POOL_GROUP, POOL_GROUP), POOL_GROUP ** -0.5),
        'pool_scale': gain((N_POOL_LAYERS, dm)),
        'ffn_w_gu': nrm((N_DENSE_LAYERS, dm, 2 * FFN_DIM), dm ** -0.5),
        'ffn_w_down': nrm((N_DENSE_LAYERS, FFN_DIM, dm), FFN_DIM ** -0.5),
        'moe_w_router': nrm((N_MOE_LAYERS, dm, N_EXPERTS), dm ** -0.5),
        'moe_b_router': nrm((N_MOE_LAYERS, N_EXPERTS), 0.01),
        'moe_w_gu': nrm((N_MOE_LAYERS, N_EXPERTS, dm, 2 * EXPERT_DIM), dm ** -0.5),
        'moe_w_down': nrm((N_MOE_LAYERS, N_EXPERTS, EXPERT_DIM, dm), EXPERT_DIM ** -0.5),
    }


def reference(x, c, ctx, c_ctx, norm1_g, norm2_g, ada_w, ada_b, final_norm_g,
              mla_w_dq, mla_q_norm, mla_w_uq, mla_w_dkv, mla_kv_norm, mla_w_ukv, mla_w_o,
              s5_a_re, s5_a_im, s5_log_dt, s5_b_re, s5_b_im, s5_c_re, s5_c_im, s5_d, s5_w_glu,
              pool_w, pool_scale, ffn_w_gu, ffn_w_down,
              moe_w_router, moe_b_router, moe_w_gu, moe_w_down):
    b, l, dm = x.shape
    rows = l // GRID_W
    row = jnp.repeat(jnp.arange(rows, dtype=jnp.float32), GRID_W)
    col = jnp.tile(jnp.arange(GRID_W, dtype=jnp.float32), rows)
    inv_freq = 1.0 / (ROPE_THETA ** (jnp.arange(ROPE_AXIS_FREQS, dtype=jnp.float32) / ROPE_AXIS_FREQS))
    ang = jnp.stack([row[:, None] * inv_freq, col[:, None] * inv_freq], axis=1)
    silu_c = jax.nn.silu(c)
    silu_cc = jax.nn.silu(c_ctx)
    xl, xc = x, ctx
    for i in range(DEPTH):
        last = i == DEPTH - 1
        mod_l = [m[:, None, :] for m in jnp.split(silu_c @ ada_w[i] + ada_b[i], 6, axis=-1)]
        mod_c = jnp.split(silu_cc @ ada_w[i] + ada_b[i], 6, axis=-1)
        hl = modulate(rmsnorm(xl, norm1_g[i]), mod_l[0], mod_l[1])
        hc = modulate(rmsnorm(xc, norm1_g[i]), mod_c[0], mod_c[1])
        kind = i % N_MIXERS
        j = i // N_MIXERS
        if kind == 0:
            yl, yc = mla_mixer(hl, hc, ang, mla_w_dq[j], mla_q_norm[j], mla_w_uq[j], mla_w_dkv[j],
                               mla_kv_norm[j], mla_w_ukv[j], mla_w_o[j], not last)
        elif kind == 1:
            yl, yc = s5_mixer(hl, hc, s5_a_re[j], s5_a_im[j], s5_log_dt[j], s5_b_re[j], s5_b_im[j],
                              s5_c_re[j], s5_c_im[j], s5_d[j], s5_w_glu[j], not last)
        else:
            yl = pool_mixer(hl, pool_w[j], pool_scale[j])
            yc = None if last else pool_mixer(hc, pool_w[j], pool_scale[j])
        xl = xl + mod_l[2] * yl
        if not last:
            xc = xc + mod_c[2] * yc
        k = i // 2
        hl = modulate(rmsnorm(xl, norm2_g[i]), mod_l[3], mod_l[4])
        if i % 2 == 0:
            xl = xl + mod_l[5] * swiglu(hl, ffn_w_gu[k], ffn_w_down[k])
        else:
            xl = xl + mod_l[5] * moe_swiglu(hl, moe_w_router[k], moe_b_router[k], moe_w_gu[k], moe_w_down[k])
        if not last:
            hc = modulate(rmsnorm(xc, norm2_g[i]), mod_c[3], mod_c[4])
            if i % 2 == 0:
                xc = xc + mod_c[5] * swiglu(hc, ffn_w_gu[k], ffn_w_down[k])
            else:
                xc = xc + mod_c[5] * moe_swiglu(hc, moe_w_router[k], moe_b_router[k], moe_w_gu[k], moe_w_down[k])
    return rmsnorm(xl, final_norm_g)
```

```python
import functools
import math

import jax
import jax.numpy as jnp
import numpy as np
from jax import lax
from jax.experimental import pallas as pl
from jax.experimental.pallas import tpu as pltpu

F32 = jnp.float32
BF16 = jnp.bfloat16

D_MODEL = 1024
SEQ = 4096
DEPTH = 4
GRID_W = 64
CTX_LEN = 256
N_MIXERS = 3
NORM_EPS = 1e-6

MLA_HEADS = 16
MLA_Q_LORA = 384
MLA_KV_LORA = 256
MLA_NOPE = 64
MLA_ROPE = 32
MLA_V = 64
MLA_QK = MLA_NOPE + MLA_ROPE
ROPE_AXIS_FREQS = MLA_ROPE // 4
ROPE_THETA = 10000.0
HEAD_PAD = 128

S5_GROUP = 16
S5_GROUPS = D_MODEL // S5_GROUP
S5_STATE = 64
S5_MAX_RE = -1e-4
S5_CHUNK = 16

POOL_WINDOWS = (2, 4, 8, 16)
POOL_GROUP = D_MODEL // len(POOL_WINDOWS)

FFN_DIM = 2816
N_EXPERTS = 8
TOP_K = 2

T_TOK = SEQ + CTX_LEN
SEG = CTX_LEN
SEGS_PER_BATCH = T_TOK // SEG
LAT_SEGS = SEQ // SEG
N_MOD = 6

VMEM_LIMIT = 56 * 1024 * 1024


def _cparams(sem, vmem=VMEM_LIMIT):
    return pltpu.CompilerParams(dimension_semantics=sem, vmem_limit_bytes=vmem)


def _rms(x, g):
    return x * lax.rsqrt(jnp.mean(x * x, axis=-1, keepdims=True) + NORM_EPS) * g


def _norm_mod(x, g, shift, scale):
    return _rms(x, g) * (1.0 + scale) + shift


def _norm_mod_tile(x_ref, mod_ref, g_ref, shift_idx, out_dtype=BF16):
    parts = []
    for s in range(x_ref.shape[0] // SEG):
        m = mod_ref[s]
        x = x_ref[s * SEG:(s + 1) * SEG, :]
        parts.append(_norm_mod(x, g_ref[...], m[shift_idx:shift_idx + 1], m[shift_idx + 1:shift_idx + 2]).astype(out_dtype))
    return parts[0] if len(parts) == 1 else jnp.concatenate(parts, axis=0)


def _ada_kernel(c_ref, w_ref, b_ref, o_ref):
    s = jax.nn.silu(c_ref[...])
    o_ref[0] = jnp.dot(s, w_ref[0], precision=lax.Precision.HIGHEST, preferred_element_type=F32) + b_ref[0]


def _ada(cc, ada_w, ada_b):
    depth, d, n6 = ada_w.shape
    rows = cc.shape[0]
    tn = 1024
    return pl.pallas_call(
        _ada_kernel,
        grid=(depth, n6 // tn),
        in_specs=[pl.BlockSpec((rows, d), lambda i, j: (0, 0)),
                  pl.BlockSpec((1, d, tn), lambda i, j: (i, 0, j)),
                  pl.BlockSpec((1, 1, tn), lambda i, j: (i, 0, j))],
        out_specs=pl.BlockSpec((1, rows, tn), lambda i, j: (i, 0, j)),
        out_shape=jax.ShapeDtypeStruct((depth, rows, n6), F32),
        compiler_params=_cparams(("arbitrary", "arbitrary")),
        name="ada",
    )(cc, ada_w, ada_b.reshape(depth, 1, n6))


_W1_COLS = MLA_Q_LORA + MLA_KV_LORA + 2 * HEAD_PAD
_QW = MLA_HEADS * HEAD_PAD


def _mla_proj_kernel(x_ref, mod_ref, g1_ref, w1_ref, qn_ref, wq_ref, kvn_ref, wk_ref, we_ref, wv_ref,
                     cq_ref, sq_ref, ck_ref, sk_ref, q_ref, k_ref, v_ref):
    h = _norm_mod_tile(x_ref, mod_ref, g1_ref, 0)
    d = jnp.dot(h, w1_ref[...], preferred_element_type=F32)
    dq = d[:, :MLA_Q_LORA]
    ckv = d[:, MLA_Q_LORA:MLA_Q_LORA + MLA_KV_LORA]
    kr = d[:, MLA_Q_LORA + MLA_KV_LORA:MLA_Q_LORA + MLA_KV_LORA + HEAD_PAD]
    kr_sw = d[:, MLA_Q_LORA + MLA_KV_LORA + HEAD_PAD:]
    qn = _rms(dq, qn_ref[...]).astype(BF16)
    qq = jnp.dot(qn, wq_ref[...], preferred_element_type=F32)
    cq = cq_ref[...]
    sq = sq_ref[...]
    for hd in range(MLA_HEADS):
        lo = hd * HEAD_PAD
        q_ref[:, lo:lo + HEAD_PAD] = (qq[:, lo:lo + HEAD_PAD] * cq + qq[:, _QW + lo:_QW + lo + HEAD_PAD] * sq).astype(BF16)
    c = _rms(ckv, kvn_ref[...]).astype(BF16)
    kr_roped = (kr * ck_ref[...] + kr_sw * sk_ref[...]).astype(BF16)
    k = jnp.dot(c, wk_ref[...], preferred_element_type=F32) + jnp.dot(kr_roped, we_ref[...], preferred_element_type=F32)
    k_ref[...] = k.astype(BF16)
    v_ref[...] = jnp.dot(c, wv_ref[...], preferred_element_type=F32).astype(BF16)


def _mla_proj(x, modtab, g1, pw, rope):
    n, d = x.shape
    tm = SEG
    full = lambda a: pl.BlockSpec(a.shape, lambda g: (0,) * a.ndim)
    pos = lambda g: (g % SEGS_PER_BATCH, 0)
    cq, sq, ck, sk = rope
    return pl.pallas_call(
        _mla_proj_kernel,
        grid=(n // tm,),
        in_specs=[pl.BlockSpec((tm, d), lambda g: (g, 0)),
                  pl.BlockSpec((tm // SEG, N_MOD, d), lambda g: (g, 0, 0)),
                  full(g1), full(pw["w1"]), full(pw["qn"]), full(pw["wq"]), full(pw["kvn"]),
                  full(pw["wk"]), full(pw["we"]), full(pw["wv"]),
                  pl.BlockSpec((tm, HEAD_PAD), pos), pl.BlockSpec((tm, HEAD_PAD), pos),
                  pl.BlockSpec((tm, HEAD_PAD), pos), pl.BlockSpec((tm, HEAD_PAD), pos)],
        out_specs=[pl.BlockSpec((tm, _QW), lambda g: (g, 0)),
                   pl.BlockSpec((tm, _QW), lambda g: (g, 0)),
                   pl.BlockSpec((tm, MLA_HEADS * MLA_V), lambda g: (g, 0))],
        out_shape=[jax.ShapeDtypeStruct((n, _QW), BF16),
                   jax.ShapeDtypeStruct((n, _QW), BF16),
                   jax.ShapeDtypeStruct((n, MLA_HEADS * MLA_V), BF16)],
        compiler_params=_cparams(("parallel",)),
        name="mla_proj",
    )(x, modtab, g1, pw["w1"], pw["qn"], pw["wq"], pw["kvn"], pw["wk"], pw["we"], pw["wv"], cq, sq, ck, sk)


def _mla_weights(w_dq, q_norm, w_uq, w_dkv, kv_norm, w_ukv):
    d = w_dq.shape[0]
    swap = np.arange(MLA_ROPE) ^ ROPE_AXIS_FREQS
    w_kr = w_dkv[:, MLA_KV_LORA:]
    padr = lambda w: jnp.pad(w, ((0, 0), (0, HEAD_PAD - MLA_ROPE)))
    w1 = jnp.concatenate([w_dq, w_dkv[:, :MLA_KV_LORA], padr(w_kr), padr(w_kr[:, swap])], axis=1)
    uq = w_uq.reshape(MLA_Q_LORA, MLA_HEADS, MLA_QK)
    zpad = jnp.zeros((MLA_Q_LORA, MLA_HEADS, HEAD_PAD - MLA_QK), w_uq.dtype)
    wq_main = jnp.concatenate([uq, zpad], axis=-1).reshape(MLA_Q_LORA, _QW)
    uq_sw = jnp.concatenate([jnp.zeros((MLA_Q_LORA, MLA_HEADS, MLA_NOPE), w_uq.dtype),
                             uq[:, :, MLA_NOPE:][:, :, swap], zpad], axis=-1).reshape(MLA_Q_LORA, _QW)
    wq = jnp.concatenate([wq_main, uq_sw], axis=1)
    ukv = w_ukv.reshape(MLA_KV_LORA, MLA_HEADS, MLA_NOPE + MLA_V)
    wk = jnp.concatenate([ukv[:, :, :MLA_NOPE],
                          jnp.zeros((MLA_KV_LORA, MLA_HEADS, HEAD_PAD - MLA_NOPE), w_ukv.dtype)], axis=-1)
    wk = wk.reshape(MLA_KV_LORA, _QW)
    wv = ukv[:, :, MLA_NOPE:].reshape(MLA_KV_LORA, MLA_HEADS * MLA_V)
    e = np.zeros((HEAD_PAD, MLA_HEADS, HEAD_PAD), np.float32)
    for r in range(MLA_ROPE):
        e[r, :, MLA_NOPE + r] = 1.0
    we = jnp.asarray(e.reshape(HEAD_PAD, _QW))
    return {"w1": w1.astype(BF16), "qn": q_norm.reshape(1, -1), "wq": wq.astype(BF16),
            "kvn": kv_norm.reshape(1, -1), "wk": wk.astype(BF16), "we": we.astype(BF16), "wv": wv.astype(BF16)}


def _rope_tables():
    rows = SEQ // GRID_W
    row = jnp.repeat(jnp.arange(rows, dtype=F32), GRID_W)
    col = jnp.tile(jnp.arange(GRID_W, dtype=F32), rows)
    inv_freq = 1.0 / (ROPE_THETA ** (jnp.arange(ROPE_AXIS_FREQS, dtype=F32) / ROPE_AXIS_FREQS))
    ang = jnp.stack([row[:, None] * inv_freq, col[:, None] * inv_freq], axis=1)
    ang = jnp.concatenate([ang, jnp.zeros((CTX_LEN, 2, ROPE_AXIS_FREQS), F32)], axis=0)
    cos = jnp.cos(ang)
    sin = jnp.sin(ang)
    c32 = jnp.stack([cos, cos], axis=2).reshape(T_TOK, MLA_ROPE)
    s32 = jnp.stack([-sin, sin], axis=2).reshape(T_TOK, MLA_ROPE)
    scale = MLA_QK ** -0.5
    zq = jnp.zeros((T_TOK, HEAD_PAD - MLA_QK), F32)
    cq = jnp.concatenate([jnp.full((T_TOK, MLA_NOPE), scale, F32), c32 * scale, zq], axis=1)
    sq = jnp.concatenate([jnp.zeros((T_TOK, MLA_NOPE), F32), s32 * scale, zq], axis=1)
    zk = jnp.zeros((T_TOK, HEAD_PAD - MLA_ROPE), F32)
    ck = jnp.concatenate([c32, zk], axis=1)
    sk = jnp.concatenate([s32, zk], axis=1)
    return cq, sq, ck, sk


def _attend_pair(q, k, v):
    outs = []
    for j in range(2):
        s = lax.dot_general(q[:, j * HEAD_PAD:(j + 1) * HEAD_PAD], k[:, j * HEAD_PAD:(j + 1) * HEAD_PAD],
                            (((1,), (1,)), ((), ())), preferred_element_type=F32)
        m = jnp.max(s, axis=-1, keepdims=True)
        p = jnp.exp(s - m)
        l = jnp.sum(p, axis=-1, keepdims=True)
        o = jnp.dot(p.astype(BF16), v, preferred_element_type=F32)
        outs.append(o / l)
    lane = lax.broadcasted_iota(jnp.int32, outs[0].shape, 1)
    return jnp.where(lane < MLA_V, outs[0], outs[1])


def _attn_kernel(q_ref, k_ref, v_ref, o_ref, *, lat_tiles):
    qi = pl.program_id(2)

    @pl.when(qi < lat_tiles)
    def _():
        o_ref[0] = _attend_pair(q_ref[0], k_ref[0], v_ref[0]).astype(BF16)

    @pl.when(qi >= lat_tiles)
    def _():
        o_ref[0] = _attend_pair(q_ref[0], k_ref[0, SEQ:, :], v_ref[0, SEQ:, :]).astype(BF16)


def _attention(q, k, v):
    b = q.shape[0]
    tq = SEG
    lat_tiles = SEQ // tq
    n_tiles = lat_tiles + 1
    return pl.pallas_call(
        functools.partial(_attn_kernel, lat_tiles=lat_tiles),
        grid=(b, MLA_HEADS // 2, n_tiles),
        in_specs=[pl.BlockSpec((1, tq, 2 * HEAD_PAD), lambda bi, hp, qi: (bi, qi, hp)),
                  pl.BlockSpec((1, T_TOK, 2 * HEAD_PAD), lambda bi, hp, qi: (bi, 0, hp)),
                  pl.BlockSpec((1, T_TOK, 2 * MLA_V), lambda bi, hp, qi: (bi, 0, hp))],
        out_specs=pl.BlockSpec((1, tq, 2 * MLA_V), lambda bi, hp, qi: (bi, qi, hp)),
        out_shape=jax.ShapeDtypeStruct((b, T_TOK, MLA_HEADS * MLA_V), BF16),
        compiler_params=_cparams(("parallel", "parallel", "arbitrary")),
        name="attention",
    )(q, k, v)


def _gated_add(x_ref, mod_ref, y, gate_idx, o_ref):
    for s in range(x_ref.shape[0] // SEG):
        rows = slice(s * SEG, (s + 1) * SEG)
        g = mod_ref[s][gate_idx:gate_idx + 1]
        o_ref[rows, :] = x_ref[rows, :] + g * y[rows, :]


def _proj_res_kernel(x_ref, mod_ref, y_ref, w_ref, o_ref):
    y = jnp.dot(y_ref[...], w_ref[...], preferred_element_type=F32)
    _gated_add(x_ref, mod_ref, y, 2, o_ref)


def _proj_res(x, modtab, y, w, tm=512):
    n, d = x.shape
    return pl.pallas_call(
        _proj_res_kernel,
        grid=(n // tm,),
        in_specs=[pl.BlockSpec((tm, d), lambda g: (g, 0)),
                  pl.BlockSpec((tm // SEG, N_MOD, d), lambda g: (g, 0, 0)),
                  pl.BlockSpec((tm, y.shape[1]), lambda g: (g, 0)),
                  pl.BlockSpec(w.shape, lambda g: (0, 0))],
        out_specs=pl.BlockSpec((tm, d), lambda g: (g, 0)),
        out_shape=jax.ShapeDtypeStruct((n, d), F32),
        compiler_params=_cparams(("parallel",)),
        name="proj_res",
    )(x, modtab, y, w)


def _glu_res_kernel(x_ref, mod_ref, y_ref, w_ref, o_ref):
    d = x_ref.shape[1]
    g = jax.nn.gelu(y_ref[...]).astype(BF16)
    z = jnp.dot(g, w_ref[...], preferred_element_type=F32)
    y = z[:, :d] * jax.nn.sigmoid(z[:, d:])
    _gated_add(x_ref, mod_ref, y, 2, o_ref)


def _glu_res(x, modtab, y, w, tm=512):
    n, d = x.shape
    return pl.pallas_call(
        _glu_res_kernel,
        grid=(n // tm,),
        in_specs=[pl.BlockSpec((tm, d), lambda g: (g, 0)),
                  pl.BlockSpec((tm // SEG, N_MOD, d), lambda g: (g, 0, 0)),
                  pl.BlockSpec((tm, d), lambda g: (g, 0)),
                  pl.BlockSpec(w.shape, lambda g: (0, 0))],
        out_specs=pl.BlockSpec((tm, d), lambda g: (g, 0)),
        out_shape=jax.ShapeDtypeStruct((n, d), F32),
        compiler_params=_cparams(("parallel",)),
        name="glu_res",
    )(x, modtab, y, w)


def _normmod_kernel(x_ref, mod_ref, g_ref, o_ref):
    o_ref[...] = _norm_mod_tile(x_ref, mod_ref, g_ref, 0)


def _normmod(x, modtab, g1, tm=512):
    n, d = x.shape
    return pl.pallas_call(
        _normmod_kernel,
        grid=(n // tm,),
        in_specs=[pl.BlockSpec((tm, d), lambda g: (g, 0)),
                  pl.BlockSpec((tm // SEG, N_MOD, d), lambda g: (g, 0, 0)),
                  pl.BlockSpec(g1.shape, lambda g: (0, 0))],
        out_specs=pl.BlockSpec((tm, d), lambda g: (g, 0)),
        out_shape=jax.ShapeDtypeStruct((n, d), BF16),
        compiler_params=_cparams(("parallel",)),
        name="normmod",
    )(x, modtab, g1)


_S5_LAT_CHUNKS = SEQ // S5_CHUNK
_S5_CHUNKS = T_TOK // S5_CHUNK
_S5_W = S5_CHUNK * S5_GROUP


def _s5_kernel(u_ref, m_ref, win_ref, wof_ref, wor_ref, ar_ref, ai_ref, y_ref, z_ref, sf_ref, sr_ref, *, nb):
    u = u_ref[0]
    z_ref[...] = jnp.dot(u, win_ref[0], preferred_element_type=F32)
    half = _S5_W // 2
    ar = ar_ref[0]
    ai = ai_ref[0]
    lane = lax.broadcasted_iota(jnp.int32, (nb, _S5_W), 1)
    is_fwd = (lane % half) < S5_STATE

    def step(i, s):
        cf = jnp.where(i < _S5_CHUNKS - _S5_LAT_CHUNKS, i + _S5_LAT_CHUNKS, i - (_S5_CHUNKS - _S5_LAT_CHUNKS))
        cr = _S5_CHUNKS - 1 - i
        rf = pl.multiple_of(cf * nb, nb)
        rr = pl.multiple_of(cr * nb, nb)
        sf_ref[pl.ds(rf, nb), :] = s
        sr_ref[pl.ds(rr, nb), :] = s
        z = jnp.where(is_fwd, z_ref[pl.ds(rf, nb), :], z_ref[pl.ds(rr, nb), :])
        re = s[:, :half]
        im = s[:, half:]
        return jnp.concatenate([ar * re - ai * im + z[:, :half], ar * im + ai * re + z[:, half:]], axis=1)

    lax.fori_loop(0, _S5_CHUNKS, step, jnp.zeros((nb, _S5_W), F32))
    y = jnp.dot(u, m_ref[0], preferred_element_type=F32)
    y = y + jnp.dot(sf_ref[...].astype(BF16), wof_ref[0], preferred_element_type=F32)
    y = y + jnp.dot(sr_ref[...].astype(BF16), wor_ref[0], preferred_element_type=F32)
    y_ref[0] = y


def _s5_core(u, sp, nb):
    g, rows, w = u.shape
    blk = lambda a: pl.BlockSpec((1,) + a.shape[1:], lambda i: (i,) + (0,) * (a.ndim - 1))
    return pl.pallas_call(
        functools.partial(_s5_kernel, nb=nb),
        grid=(g,),
        in_specs=[blk(u), blk(sp["m"]), blk(sp["win"]), blk(sp["wof"]), blk(sp["wor"]), blk(sp["ar"]), blk(sp["ai"])],
        out_specs=pl.BlockSpec((1, rows, w), lambda i: (i, 0, 0)),
        out_shape=jax.ShapeDtypeStruct((g, rows, w), F32),
        scratch_shapes=[pltpu.VMEM((rows, w), F32), pltpu.VMEM((rows, w), F32), pltpu.VMEM((rows, w), F32)],
        compiler_params=_cparams(("parallel",)),
        name="s5_core",
    )(u, sp["m"], sp["win"], sp["wof"], sp["wor"], sp["ar"], sp["ai"])


def _s5_params(a_re, a_im, log_dt, b_re, b_im, c_re, c_im, d):
    c64 = jnp.complex64
    lam = lax.complex(jnp.minimum(a_re, S5_MAX_RE), a_im)
    lam_dt = lam * jnp.exp(log_dt)[..., None]
    lam_bar = jnp.exp(lam_dt)
    b_bar = ((lam_bar - 1.0) / lam)[..., None] * lax.complex(b_re, b_im)
    c_mat = lax.complex(c_re, c_im)
    taus = jnp.arange(S5_CHUNK + 1, dtype=F32)
    pw = jnp.exp(lam_dt[None] * taus[:, None, None, None].astype(c64))
    hi = lax.Precision.HIGHEST
    kern = jnp.real(jnp.einsum('dgnp,tdgp,dgpm->dgtnm', c_mat, pw[:S5_CHUNK], b_bar, precision=hi))
    eye = jnp.eye(S5_GROUP, dtype=F32)
    k0 = kern[0, :, 0] + kern[1, :, 0] + d.reshape(S5_GROUPS, S5_GROUP)[:, :, None] * eye
    lags = jnp.concatenate([kern[1, :, :0:-1], k0[:, None], kern[0, :, 1:]], axis=1)
    idx = (np.arange(S5_CHUNK)[None, :] - np.arange(S5_CHUNK)[:, None]) + S5_CHUNK - 1
    m = lags[:, idx]
    m = m.transpose(0, 1, 4, 2, 3).reshape(S5_GROUPS, _S5_W, _S5_W)
    wf = pw[S5_CHUNK - 1::-1][:S5_CHUNK, 0][..., None] * b_bar[0][None]
    wr = pw[:S5_CHUNK, 1][..., None] * b_bar[1][None]
    to_rows = lambda w: w.transpose(1, 0, 3, 2).reshape(S5_GROUPS, _S5_W, S5_STATE)
    wf, wr = to_rows(wf), to_rows(wr)
    win = jnp.concatenate([jnp.real(wf), jnp.real(wr), jnp.imag(wf), jnp.imag(wr)], axis=-1)
    of = c_mat[0][None] * pw[1:, 0][:, :, None, :]
    orv = c_mat[1][None] * pw[S5_CHUNK:0:-1, 1][:, :, None, :]
    to_cols = lambda w: w.transpose(1, 3, 0, 2).reshape(S5_GROUPS, S5_STATE, _S5_W)
    of, orv = to_cols(of), to_cols(orv)
    zeros = jnp.zeros_like(jnp.real(of))
    wof = jnp.concatenate([jnp.real(of), zeros, -jnp.imag(of), zeros], axis=1)
    wor = jnp.concatenate([zeros, jnp.real(orv), zeros, -jnp.imag(orv)], axis=1)
    a16 = pw[S5_CHUNK]
    ar = jnp.concatenate([jnp.real(a16[0]), jnp.real(a16[1])], axis=-1)[:, None, :]
    ai = jnp.concatenate([jnp.imag(a16[0]), jnp.imag(a16[1])], axis=-1)[:, None, :]
    return {"m": m.astype(BF16), "win": win.astype(BF16), "wof": wof.astype(BF16), "wor": wor.astype(BF16),
            "ar": ar, "ai": ai}


def _pool_kernel(xp_ref, xc_ref, xn_ref, mod_ref, g_ref, wp_ref, sc_ref, o_ref):
    seg = pl.program_id(0) % SEGS_PER_BATCH
    is_ctx = seg >= LAT_SEGS
    p0 = jnp.where(is_ctx, 0, seg * SEG)
    lseq = jnp.where(is_ctx, CTX_LEN, SEQ)
    m = mod_ref[0]
    g = g_ref[...]
    hs = [_norm_mod(r[...], g, m[0:1], m[1:2]) for r in (xp_ref, xc_ref, xn_ref)]
    hcat = jnp.concatenate([h.astype(BF16) for h in hs], axis=0)
    r = lax.broadcasted_iota(jnp.int32, (SEG, 3 * SEG), 0)
    s = lax.broadcasted_iota(jnp.int32, (SEG, 3 * SEG), 1)
    pt = p0 + r
    ps = p0 - SEG + s
    rcol = lax.broadcasted_iota(jnp.int32, (SEG, 1), 0) + p0
    x = xc_ref[...]
    gate = m[2:3]
    for gi, w in enumerate(POOL_WINDOWS):
        cols = slice(gi * POOL_GROUP, (gi + 1) * POOL_GROUP)
        lo = jnp.maximum(pt - w // 2, 0)
        hi = jnp.minimum(pt + w - w // 2, lseq)
        band = jnp.where((ps >= lo) & (ps < hi), 1.0, 0.0).astype(BF16)
        tot = jnp.dot(band, hcat[:, cols], preferred_element_type=F32)
        cnt = (jnp.minimum(rcol + w - w // 2, lseq) - jnp.maximum(rcol - w // 2, 0)).astype(F32)
        resid = tot / cnt - hs[1][:, cols]
        y = jnp.dot(resid.astype(BF16), wp_ref[gi], preferred_element_type=F32) * sc_ref[:, cols]
        o_ref[:, cols] = x[:, cols] + gate[:, cols] * y


def _pool(x, modtab, g1, w_pool, scale):
    n, d = x.shape
    nseg = n // SEG

    def prev(g):
        return (jnp.maximum(g - 1, 0), 0)

    def nxt(g):
        return (jnp.minimum(g + 1, nseg - 1), 0)

    return pl.pallas_call(
        _pool_kernel,
        grid=(nseg,),
        in_specs=[pl.BlockSpec((SEG, d), prev), pl.BlockSpec((SEG, d), lambda g: (g, 0)), pl.BlockSpec((SEG, d), nxt),
                  pl.BlockSpec((1, N_MOD, d), lambda g: (g, 0, 0)),
                  pl.BlockSpec(g1.shape, lambda g: (0, 0)),
                  pl.BlockSpec(w_pool.shape, lambda g: (0, 0, 0)),
                  pl.BlockSpec(scale.shape, lambda g: (0, 0))],
        out_specs=pl.BlockSpec((SEG, d), lambda g: (g, 0)),
        out_shape=jax.ShapeDtypeStruct((n, d), F32),
        compiler_params=_cparams(("parallel",)),
        name="pool",
    )(x, x, x, modtab, g1, w_pool, scale)


def _router_kernel(x_ref, mod_ref, g_ref, w_ref, b_ref, o_ref):
    h = _norm_mod_tile(x_ref, mod_ref, g_ref, 3, out_dtype=F32)
    logits = jnp.dot(h, w_ref[...], precision=lax.Precision.HIGHEST, preferred_element_type=F32) + b_ref[...]
    lane = lax.broadcasted_iota(jnp.int32, logits.shape, 1)
    neg = jnp.float32(-jnp.inf)
    lg = jnp.where(lane < N_EXPERTS, logits, neg)
    m1 = jnp.max(lg, axis=-1, keepdims=True)
    i1 = jnp.min(jnp.where(lg == m1, lane, HEAD_PAD), axis=-1, keepdims=True)
    lg2 = jnp.where(lane == i1, neg, lg)
    m2 = jnp.max(lg2, axis=-1, keepdims=True)
    i2 = jnp.min(jnp.where(lg2 == m2, lane, HEAD_PAD), axis=-1, keepdims=True)
    e2 = jnp.exp(m2 - m1)
    den = 1.0 + e2
    o_ref[...] = jnp.where(lane == i1, 1.0 / den, 0.0) + jnp.where(lane == i2, e2 / den, 0.0)


def _router(x, modtab, g2, w_router, b_router, tm=512):
    n, d = x.shape
    wr = jnp.pad(w_router, ((0, 0), (0, HEAD_PAD - N_EXPERTS)))
    br = jnp.pad(b_router, (0, HEAD_PAD - N_EXPERTS)).reshape(1, HEAD_PAD)
    return pl.pallas_call(
        _router_kernel,
        grid=(n // tm,),
        in_specs=[pl.BlockSpec((tm, d), lambda g: (g, 0)),
                  pl.BlockSpec((tm // SEG, N_MOD, d), lambda g: (g, 0, 0)),
                  pl.BlockSpec(g2.shape, lambda g: (0, 0)),
                  pl.BlockSpec(wr.shape, lambda g: (0, 0)),
                  pl.BlockSpec(br.shape, lambda g: (0, 0))],
        out_specs=pl.BlockSpec((tm, HEAD_PAD), lambda g: (g, 0)),
        out_shape=jax.ShapeDtypeStruct((n, HEAD_PAD), F32),
        compiler_params=_cparams(("parallel",)),
        name="router",
    )(x, modtab, g2, wr, br)


def _ffn_kernel(x_ref, mod_ref, g_ref, gates_ref, wg_ref, wu_ref, wd_ref, o_ref, h_ref, acc_ref, *, gated):
    e = pl.program_id(1)
    f = pl.program_id(2)

    @pl.when((e == 0) & (f == 0))
    def _():
        h_ref[...] = _norm_mod_tile(x_ref, mod_ref, g_ref, 3)
        acc_ref[...] = jnp.zeros_like(acc_ref)

    h = h_ref[...]
    a = jax.nn.silu(jnp.dot(h, wg_ref[0], preferred_element_type=F32)) * jnp.dot(h, wu_ref[0], preferred_element_type=F32)
    if gated:
        gates = gates_ref[...]
        lane = lax.broadcasted_iota(jnp.int32, gates.shape, 1)
        a = a * jnp.sum(jnp.where(lane == e, gates, 0.0), axis=-1, keepdims=True)
    acc_ref[...] += jnp.dot(a.astype(BF16), wd_ref[0], preferred_element_type=F32)

    @pl.when((e == pl.num_programs(1) - 1) & (f == pl.num_programs(2) - 1))
    def _():
        _gated_add(x_ref, mod_ref, acc_ref[...], 5, o_ref)


def _ffn(x, modtab, g2, gates, w_gu, w_down, tm=512, tf=1408):
    n, d = x.shape
    ne, _, f2 = w_gu.shape
    nf = (f2 // 2) // tf
    gated = gates is not None
    if not gated:
        gates = jnp.zeros((n, HEAD_PAD), F32)
    return pl.pallas_call(
        functools.partial(_ffn_kernel, gated=gated),
        grid=(n // tm, ne, nf),
        in_specs=[pl.BlockSpec((tm, d), lambda g, e, f: (g, 0)),
                  pl.BlockSpec((tm // SEG, N_MOD, d), lambda g, e, f: (g, 0, 0)),
                  pl.BlockSpec(g2.shape, lambda g, e, f: (0, 0)),
                  pl.BlockSpec((tm, HEAD_PAD), lambda g, e, f: (g, 0)),
                  pl.BlockSpec((1, d, tf), lambda g, e, f: (e, 0, f)),
                  pl.BlockSpec((1, d, tf), lambda g, e, f: (e, 0, nf + f)),
                  pl.BlockSpec((1, tf, d), lambda g, e, f: (e, f, 0))],
        out_specs=pl.BlockSpec((tm, d), lambda g, e, f: (g, 0)),
        out_shape=jax.ShapeDtypeStruct((n, d), F32),
        scratch_shapes=[pltpu.VMEM((tm, d), BF16), pltpu.VMEM((tm, d), F32)],
        compiler_params=_cparams(("parallel", "arbitrary", "arbitrary")),
        name="ffn",
    )(x, modtab, g2, gates, w_gu, w_gu, w_down)


def _final_kernel(x_ref, g_ref, o_ref):
    o_ref[0] = _rms(x_ref[0], g_ref[...])


def _final_norm(x3, g, tm=512):
    b, _, d = x3.shape
    return pl.pallas_call(
        _final_kernel,
        grid=(b, SEQ // tm),
        in_specs=[pl.BlockSpec((1, tm, d), lambda bi, t: (bi, t, 0)), pl.BlockSpec(g.shape, lambda bi, t: (0, 0))],
        out_specs=pl.BlockSpec((1, tm, d), lambda bi, t: (bi, t, 0)),
        out_shape=jax.ShapeDtypeStruct((b, SEQ, d), F32),
        compiler_params=_cparams(("parallel", "parallel")),
        name="final_norm",
    )(x3, g)


def kernel(x, c, ctx, c_ctx, norm1_g, norm2_g, ada_w, ada_b, final_norm_g, mla_w_dq, mla_q_norm, mla_w_uq, mla_w_dkv, mla_kv_norm, mla_w_ukv, mla_w_o, s5_a_re, s5_a_im, s5_log_dt, s5_b_re, s5_b_im, s5_c_re, s5_c_im, s5_d, s5_w_glu, pool_w, pool_scale, ffn_w_gu, ffn_w_down, moe_w_router, moe_b_router, moe_w_gu, moe_w_down):
    b, l, d = x.shape
    n = b * T_TOK
    xs = jnp.concatenate([x, ctx], axis=1).reshape(n, d)

    rows = -(-(b + 1) // 8) * 8
    cc = jnp.concatenate([c, c_ctx[None], jnp.zeros((rows - b - 1, d), F32)], axis=0)
    mods = _ada(cc, ada_w, ada_b)
    seg_src = np.array([bi if sj < LAT_SEGS else b for bi in range(b) for sj in range(SEGS_PER_BATCH)], np.int32)

    rope = _rope_tables()
    for i in range(DEPTH):
        last = i == DEPTH - 1
        modtab = mods[i].reshape(rows, N_MOD, d)[seg_src]
        g1 = norm1_g[i].reshape(1, d)
        g2 = norm2_g[i].reshape(1, d)
        kind = i % N_MIXERS
        j = i // N_MIXERS
        if kind == 0:
            pw = _mla_weights(mla_w_dq[j], mla_q_norm[j], mla_w_uq[j], mla_w_dkv[j], mla_kv_norm[j], mla_w_ukv[j])
            q, k, v = _mla_proj(xs, modtab, g1, pw, rope)
            o = _attention(q.reshape(b, T_TOK, -1), k.reshape(b, T_TOK, -1), v.reshape(b, T_TOK, -1))
            xs = _proj_res(xs, modtab, o.reshape(n, -1), mla_w_o[j].astype(BF16))
        elif kind == 1:
            h = _normmod(xs, modtab, g1)
            nch = T_TOK // S5_CHUNK
            u = h.reshape(b, nch, S5_CHUNK, S5_GROUPS, S5_GROUP).transpose(3, 1, 0, 2, 4).reshape(S5_GROUPS, nch * b, _S5_W)
            sp = _s5_params(s5_a_re[j], s5_a_im[j], s5_log_dt[j], s5_b_re[j], s5_b_im[j], s5_c_re[j], s5_c_im[j], s5_d[j])
            y = _s5_core(u, sp, b)
            y = y.reshape(S5_GROUPS, nch, b, S5_CHUNK, S5_GROUP).transpose(2, 1, 3, 0, 4).reshape(n, d)
            xs = _glu_res(xs, modtab, y, s5_w_glu[j].astype(BF16))
        else:
            xs = _pool(xs, modtab, g1, pool_w[j].astype(BF16), pool_scale[j].reshape(1, d))
        kk = i // 2
        if i % 2 == 0:
            xs = _ffn(xs, modtab, g2, None, ffn_w_gu[kk][None].astype(BF16), ffn_w_down[kk][None].astype(BF16))
        else:
            gates = _router(xs, modtab, g2, moe_w_router[kk], moe_b_router[kk])
            xs = _ffn(xs, modtab, g2, gates, moe_w_gu[kk].astype(BF16), moe_w_down[kk].astype(BF16))
    return _final_norm(xs.reshape(b, T_TOK, d), final_norm_g.reshape(1, d))
```

```python
import functools
import math

import jax
import jax.numpy as jnp
import numpy as np
from jax import lax
from jax.experimental import pallas as pl
from jax.experimental.pallas import tpu as pltpu

F32 = jnp.float32
BF16 = jnp.bfloat16

D_MODEL = 1024
SEQ = 4096
DEPTH = 4
GRID_W = 64
CTX_LEN = 256
N_MIXERS = 3
NORM_EPS = 1e-6

MLA_HEADS = 16
MLA_Q_LORA = 384
MLA_KV_LORA = 256
MLA_NOPE = 64
MLA_ROPE = 32
MLA_V = 64
MLA_QK = MLA_NOPE + MLA_ROPE
ROPE_AXIS_FREQS = MLA_ROPE // 4
ROPE_THETA = 10000.0
HEAD_PAD = 128

S5_GROUP = 16
S5_GROUPS = D_MODEL // S5_GROUP
S5_STATE = 64
S5_MAX_RE = -1e-4
S5_CHUNK = 16

POOL_WINDOWS = (2, 4, 8, 16)
POOL_GROUP = D_MODEL // len(POOL_WINDOWS)

FFN_DIM = 2816
N_EXPERTS = 8
TOP_K = 2

T_TOK = SEQ + CTX_LEN
SEG = CTX_LEN
SEGS_PER_BATCH = T_TOK // SEG
LAT_SEGS = SEQ // SEG
N_MOD = 6

VMEM_LIMIT = 56 * 1024 * 1024


def _cparams(sem, vmem=VMEM_LIMIT):
    return pltpu.CompilerParams(dimension_semantics=sem, vmem_limit_bytes=vmem)


def _rms(x, g):
    return x * lax.rsqrt(jnp.mean(x * x, axis=-1, keepdims=True) + NORM_EPS) * g


def _norm_mod(x, g, shift, scale):
    return _rms(x, g) * (1.0 + scale) + shift


def _norm_mod_tile(x_ref, mod_ref, g_ref, shift_idx, out_dtype=BF16):
    parts = []
    for s in range(x_ref.shape[0] // SEG):
        m = mod_ref[s]
        x = x_ref[s * SEG:(s + 1) * SEG, :]
        parts.append(_norm_mod(x, g_ref[...], m[shift_idx:shift_idx + 1], m[shift_idx + 1:shift_idx + 2]).astype(out_dtype))
    return parts[0] if len(parts) == 1 else jnp.concatenate(parts, axis=0)


def _ada_kernel(c_ref, w_ref, b_ref, o_ref):
    s = jax.nn.silu(c_ref[...])
    o_ref[0] = jnp.dot(s, w_ref[0], precision=lax.Precision.HIGHEST, preferred_element_type=F32) + b_ref[0]


def _ada(cc, ada_w, ada_b):
    depth, d, n6 = ada_w.shape
    rows = cc.shape[0]
    tn = 1024
    return pl.pallas_call(
        _ada_kernel,
        grid=(depth, n6 // tn),
        in_specs=[pl.BlockSpec((rows, d), lambda i, j: (0, 0)),
                  pl.BlockSpec((1, d, tn), lambda i, j: (i, 0, j)),
                  pl.BlockSpec((1, 1, tn), lambda i, j: (i, 0, j))],
        out_specs=pl.BlockSpec((1, rows, tn), lambda i, j: (i, 0, j)),
        out_shape=jax.ShapeDtypeStruct((depth, rows, n6), F32),
        compiler_params=_cparams(("arbitrary", "arbitrary")),
        name="ada",
    )(cc, ada_w, ada_b.reshape(depth, 1, n6))


_W1_COLS = MLA_Q_LORA + MLA_KV_LORA + 2 * HEAD_PAD
_QW = MLA_HEADS * HEAD_PAD


def _mla_proj_kernel(x_ref, mod_ref, g1_ref, w1_ref, qn_ref, wq_ref, kvn_ref, wk_ref, we_ref, wv_ref,
                     cq_ref, sq_ref, ck_ref, sk_ref, q_ref, k_ref, v_ref):
    h = _norm_mod_tile(x_ref, mod_ref, g1_ref, 0)
    d = jnp.dot(h, w1_ref[...], preferred_element_type=F32)
    dq = d[:, :MLA_Q_LORA]
    ckv = d[:, MLA_Q_LORA:MLA_Q_LORA + MLA_KV_LORA]
    kr = d[:, MLA_Q_LORA + MLA_KV_LORA:MLA_Q_LORA + MLA_KV_LORA + HEAD_PAD]
    kr_sw = d[:, MLA_Q_LORA + MLA_KV_LORA + HEAD_PAD:]
    qn = _rms(dq, qn_ref[...]).astype(BF16)
    qq = jnp.dot(qn, wq_ref[...], preferred_element_type=F32)
    cq = cq_ref[...]
    sq = sq_ref[...]
    for hd in range(MLA_HEADS):
        lo = hd * HEAD_PAD
        q_ref[:, lo:lo + HEAD_PAD] = (qq[:, lo:lo + HEAD_PAD] * cq + qq[:, _QW + lo:_QW + lo + HEAD_PAD] * sq).astype(BF16)
    c = _rms(ckv, kvn_ref[...]).astype(BF16)
    kr_roped = (kr * ck_ref[...] + kr_sw * sk_ref[...]).astype(BF16)
    k = jnp.dot(c, wk_ref[...], preferred_element_type=F32) + jnp.dot(kr_roped, we_ref[...], preferred_element_type=F32)
    k_ref[...] = k.astype(BF16)
    v_ref[...] = jnp.dot(c, wv_ref[...], preferred_element_type=F32).astype(BF16)


def _mla_proj(x, modtab, g1, pw, rope):
    n, d = x.shape
    tm = SEG
    full = lambda a: pl.BlockSpec(a.shape, lambda g: (0,) * a.ndim)
    pos = lambda g: (g % SEGS_PER_BATCH, 0)
    cq, sq, ck, sk = rope
    return pl.pallas_call(
        _mla_proj_kernel,
        grid=(n // tm,),
        in_specs=[pl.BlockSpec((tm, d), lambda g: (g, 0)),
                  pl.BlockSpec((tm // SEG, N_MOD, d), lambda g: (g, 0, 0)),
                  full(g1), full(pw["w1"]), full(pw["qn"]), full(pw["wq"]), full(pw["kvn"]),
                  full(pw["wk"]), full(pw["we"]), full(pw["wv"]),
                  pl.BlockSpec((tm, HEAD_PAD), pos), pl.BlockSpec((tm, HEAD_PAD), pos),
                  pl.BlockSpec((tm, HEAD_PAD), pos), pl.BlockSpec((tm, HEAD_PAD), pos)],
        out_specs=[pl.BlockSpec((tm, _QW), lambda g: (g, 0)),
                   pl.BlockSpec((tm, _QW), lambda g: (g, 0)),
                   pl.BlockSpec((tm, MLA_HEADS * MLA_V), lambda g: (g, 0))],
        out_shape=[jax.ShapeDtypeStruct((n, _QW), BF16),
                   jax.ShapeDtypeStruct((n, _QW), BF16),
                   jax.ShapeDtypeStruct((n, MLA_HEADS * MLA_V), BF16)],
        compiler_params=_cparams(("parallel",)),
        name="mla_proj",
    )(x, modtab, g1, pw["w1"], pw["qn"], pw["wq"], pw["kvn"], pw["wk"], pw["we"], pw["wv"], cq, sq, ck, sk)


def _mla_weights(w_dq, q_norm, w_uq, w_dkv, kv_norm, w_ukv):
    d = w_dq.shape[0]
    swap = np.arange(MLA_ROPE) ^ ROPE_AXIS_FREQS
    w_kr = w_dkv[:, MLA_KV_LORA:]
    padr = lambda w: jnp.pad(w, ((0, 0), (0, HEAD_PAD - MLA_ROPE)))
    w1 = jnp.concatenate([w_dq, w_dkv[:, :MLA_KV_LORA], padr(w_kr), padr(w_kr[:, swap])], axis=1)
    uq = w_uq.reshape(MLA_Q_LORA, MLA_HEADS, MLA_QK)
    zpad = jnp.zeros((MLA_Q_LORA, MLA_HEADS, HEAD_PAD - MLA_QK), w_uq.dtype)
    wq_main = jnp.concatenate([uq, zpad], axis=-1).reshape(MLA_Q_LORA, _QW)
    uq_sw = jnp.concatenate([jnp.zeros((MLA_Q_LORA, MLA_HEADS, MLA_NOPE), w_uq.dtype),
                             uq[:, :, MLA_NOPE:][:, :, swap], zpad], axis=-1).reshape(MLA_Q_LORA, _QW)
    wq = jnp.concatenate([wq_main, uq_sw], axis=1)
    ukv = w_ukv.reshape(MLA_KV_LORA, MLA_HEADS, MLA_NOPE + MLA_V)
    wk = jnp.concatenate([ukv[:, :, :MLA_NOPE],
                          jnp.zeros((MLA_KV_LORA, MLA_HEADS, HEAD_PAD - MLA_NOPE), w_ukv.dtype)], axis=-1)
    wk = wk.reshape(MLA_KV_LORA, _QW)
    wv = ukv[:, :, MLA_NOPE:].reshape(MLA_KV_LORA, MLA_HEADS * MLA_V)
    e = np.zeros((HEAD_PAD, MLA_HEADS, HEAD_PAD), np.float32)
    for r in range(MLA_ROPE):
        e[r, :, MLA_NOPE + r] = 1.0
    we = jnp.asarray(e.reshape(HEAD_PAD, _QW))
    return {"w1": w1.astype(BF16), "qn": q_norm.reshape(1, -1), "wq": wq.astype(BF16),
            "kvn": kv_norm.reshape(1, -1), "wk": wk.astype(BF16), "we": we.astype(BF16), "wv": wv.astype(BF16)}


def _rope_tables():
    rows = SEQ // GRID_W
    row = jnp.repeat(jnp.arange(rows, dtype=F32), GRID_W)
    col = jnp.tile(jnp.arange(GRID_W, dtype=F32), rows)
    inv_freq = 1.0 / (ROPE_THETA ** (jnp.arange(ROPE_AXIS_FREQS, dtype=F32) / ROPE_AXIS_FREQS))
    ang = jnp.stack([row[:, None] * inv_freq, col[:, None] * inv_freq], axis=1)
    ang = jnp.concatenate([ang, jnp.zeros((CTX_LEN, 2, ROPE_AXIS_FREQS), F32)], axis=0)
    cos = jnp.cos(ang)
    sin = jnp.sin(ang)
    c32 = jnp.stack([cos, cos], axis=2).reshape(T_TOK, MLA_ROPE)
    s32 = jnp.stack([-sin, sin], axis=2).reshape(T_TOK, MLA_ROPE)
    scale = MLA_QK ** -0.5
    zq = jnp.zeros((T_TOK, HEAD_PAD - MLA_QK), F32)
    cq = jnp.concatenate([jnp.full((T_TOK, MLA_NOPE), scale, F32), c32 * scale, zq], axis=1)
    sq = jnp.concatenate([jnp.zeros((T_TOK, MLA_NOPE), F32), s32 * scale, zq], axis=1)
    zk = jnp.zeros((T_TOK, HEAD_PAD - MLA_ROPE), F32)
    ck = jnp.concatenate([c32, zk], axis=1)
    sk = jnp.concatenate([s32, zk], axis=1)
    return cq, sq, ck, sk


def _attend_pair(q, k, v):
    outs = []
    for j in range(2):
        s = lax.dot_general(q[:, j * HEAD_PAD:(j + 1) * HEAD_PAD], k[:, j * HEAD_PAD:(j + 1) * HEAD_PAD],
                            (((1,), (1,)), ((), ())), preferred_element_type=F32)
        m = jnp.max(s, axis=-1, keepdims=True)
        p = jnp.exp(s - m)
        l = jnp.sum(p, axis=-1, keepdims=True)
        o = jnp.dot(p.astype(BF16), v, preferred_element_type=F32)
        outs.append(o / l)
    lane = lax.broadcasted_iota(jnp.int32, outs[0].shape, 1)
    return jnp.where(lane < MLA_V, outs[0], outs[1])


def _attn_kernel(q_ref, k_ref, v_ref, o_ref, *, lat_tiles):
    qi = pl.program_id(2)

    @pl.when(qi < lat_tiles)
    def _():
        o_ref[0] = _attend_pair(q_ref[0], k_ref[0], v_ref[0]).astype(BF16)

    @pl.when(qi >= lat_tiles)
    def _():
        o_ref[0] = _attend_pair(q_ref[0], k_ref[0, SEQ:, :], v_ref[0, SEQ:, :]).astype(BF16)


def _attention(q, k, v):
    b = q.shape[0]
    tq = SEG
    lat_tiles = SEQ // tq
    n_tiles = lat_tiles + 1
    return pl.pallas_call(
        functools.partial(_attn_kernel, lat_tiles=lat_tiles),
        grid=(b, MLA_HEADS // 2, n_tiles),
        in_specs=[pl.BlockSpec((1, tq, 2 * HEAD_PAD), lambda bi, hp, qi: (bi, qi, hp)),
                  pl.BlockSpec((1, T_TOK, 2 * HEAD_PAD), lambda bi, hp, qi: (bi, 0, hp)),
                  pl.BlockSpec((1, T_TOK, 2 * MLA_V), lambda bi, hp, qi: (bi, 0, hp))],
        out_specs=pl.BlockSpec((1, tq, 2 * MLA_V), lambda bi, hp, qi: (bi, qi, hp)),
        out_shape=jax.ShapeDtypeStruct((b, T_TOK, MLA_HEADS * MLA_V), BF16),
        compiler_params=_cparams(("parallel", "parallel", "arbitrary")),
        name="attention",
    )(q, k, v)


def _gated_add(x_ref, mod_ref, y, gate_idx, o_ref):
    for s in range(x_ref.shape[0] // SEG):
        rows = slice(s * SEG, (s + 1) * SEG)
        g = mod_ref[s][gate_idx:gate_idx + 1]
        o_ref[rows, :] = x_ref[rows, :] + g * y[rows, :]


def _proj_res_kernel(x_ref, mod_ref, y_ref, w_ref, o_ref):
    y = jnp.dot(y_ref[...], w_ref[...], preferred_element_type=F32)
    _gated_add(x_ref, mod_ref, y, 2, o_ref)


def _proj_res(x, modtab, y, w, tm=512):
    n, d = x.shape
    return pl.pallas_call(
        _proj_res_kernel,
        grid=(n // tm,),
        in_specs=[pl.BlockSpec((tm, d), lambda g: (g, 0)),
                  pl.BlockSpec((tm // SEG, N_MOD, d), lambda g: (g, 0, 0)),
                  pl.BlockSpec((tm, y.shape[1]), lambda g: (g, 0)),
                  pl.BlockSpec(w.shape, lambda g: (0, 0))],
        out_specs=pl.BlockSpec((tm, d), lambda g: (g, 0)),
        out_shape=jax.ShapeDtypeStruct((n, d), F32),
        compiler_params=_cparams(("parallel",)),
        name="proj_res",
    )(x, modtab, y, w)


def _glu_res_kernel(x_ref, mod_ref, y_ref, w_ref, o_ref):
    d = x_ref.shape[1]
    g = jax.nn.gelu(y_ref[...]).astype(BF16)
    z = jnp.dot(g, w_ref[...], preferred_element_type=F32)
    y = z[:, :d] * jax.nn.sigmoid(z[:, d:])
    _gated_add(x_ref, mod_ref, y, 2, o_ref)


def _glu_res(x, modtab, y, w, tm=512):
    n, d = x.shape
    return pl.pallas_call(
        _glu_res_kernel,
        grid=(n // tm,),
        in_specs=[pl.BlockSpec((tm, d), lambda g: (g, 0)),
                  pl.BlockSpec((tm // SEG, N_MOD, d), lambda g: (g, 0, 0)),
                  pl.BlockSpec((tm, d), lambda g: (g, 0)),
                  pl.BlockSpec(w.shape, lambda g: (0, 0))],
        out_specs=pl.BlockSpec((tm, d), lambda g: (g, 0)),
        out_shape=jax.ShapeDtypeStruct((n, d), F32),
        compiler_params=_cparams(("parallel",)),
        name="glu_res",
    )(x, modtab, y, w)


def _normmod_kernel(x_ref, mod_ref, g_ref, o_ref):
    o_ref[...] = _norm_mod_tile(x_ref, mod_ref, g_ref, 0)


def _normmod(x, modtab, g1, tm=512):
    n, d = x.shape
    return pl.pallas_call(
        _normmod_kernel,
        grid=(n // tm,),
        in_specs=[pl.BlockSpec((tm, d), lambda g: (g, 0)),
                  pl.BlockSpec((tm // SEG, N_MOD, d), lambda g: (g, 0, 0)),
                  pl.BlockSpec(g1.shape, lambda g: (0, 0))],
        out_specs=pl.BlockSpec((tm, d), lambda g: (g, 0)),
        out_shape=jax.ShapeDtypeStruct((n, d), BF16),
        compiler_params=_cparams(("parallel",)),
        name="normmod",
    )(x, modtab, g1)


_S5_LAT_CHUNKS = SEQ // S5_CHUNK
_S5_CHUNKS = T_TOK // S5_CHUNK
_S5_W = S5_CHUNK * S5_GROUP


def _s5_kernel(u_ref, m_ref, win_ref, wof_ref, wor_ref, ar_ref, ai_ref, y_ref, z_ref, sf_ref, sr_ref, *, nb):
    u = u_ref[0]
    z_ref[...] = jnp.dot(u, win_ref[0], preferred_element_type=F32)
    half = _S5_W // 2
    ar = ar_ref[0]
    ai = ai_ref[0]
    lane = lax.broadcasted_iota(jnp.int32, (nb, _S5_W), 1)
    is_fwd = (lane % half) < S5_STATE

    def step(i, s):
        cf = jnp.where(i < _S5_CHUNKS - _S5_LAT_CHUNKS, i + _S5_LAT_CHUNKS, i - (_S5_CHUNKS - _S5_LAT_CHUNKS))
        cr = _S5_CHUNKS - 1 - i
        rf = pl.multiple_of(cf * nb, nb)
        rr = pl.multiple_of(cr * nb, nb)
        sf_ref[pl.ds(rf, nb), :] = s
        sr_ref[pl.ds(rr, nb), :] = s
        z = jnp.where(is_fwd, z_ref[pl.ds(rf, nb), :], z_ref[pl.ds(rr, nb), :])
        re = s[:, :half]
        im = s[:, half:]
        return jnp.concatenate([ar * re - ai * im + z[:, :half], ar * im + ai * re + z[:, half:]], axis=1)

    lax.fori_loop(0, _S5_CHUNKS, step, jnp.zeros((nb, _S5_W), F32))
    y = jnp.dot(u, m_ref[0], preferred_element_type=F32)
    y = y + jnp.dot(sf_ref[...].astype(BF16), wof_ref[0], preferred_element_type=F32)
    y = y + jnp.dot(sr_ref[...].astype(BF16), wor_ref[0], preferred_element_type=F32)
    y_ref[0] = y


def _s5_core(u, sp, nb):
    g, rows, w = u.shape
    blk = lambda a: pl.BlockSpec((1,) + a.shape[1:], lambda i: (i,) + (0,) * (a.ndim - 1))
    return pl.pallas_call(
        functools.partial(_s5_kernel, nb=nb),
        grid=(g,),
        in_specs=[blk(u), blk(sp["m"]), blk(sp["win"]), blk(sp["wof"]), blk(sp["wor"]), blk(sp["ar"]), blk(sp["ai"])],
        out_specs=pl.BlockSpec((1, rows, w), lambda i: (i, 0, 0)),
        out_shape=jax.ShapeDtypeStruct((g, rows, w), F32),
        scratch_shapes=[pltpu.VMEM((rows, w), F32), pltpu.VMEM((rows, w), F32), pltpu.VMEM((rows, w), F32)],
        compiler_params=_cparams(("parallel",)),
        name="s5_core",
    )(u, sp["m"], sp["win"], sp["wof"], sp["wor"], sp["ar"], sp["ai"])


def _s5_params(a_re, a_im, log_dt, b_re, b_im, c_re, c_im, d):
    c64 = jnp.complex64
    lam = lax.complex(jnp.minimum(a_re, S5_MAX_RE), a_im)
    lam_dt = lam * jnp.exp(log_dt)[..., None]
    lam_bar = jnp.exp(lam_dt)
    b_bar = ((lam_bar - 1.0) / lam)[..., None] * lax.complex(b_re, b_im)
    c_mat = lax.complex(c_re, c_im)
    taus = jnp.arange(S5_CHUNK + 1, dtype=F32)
    pw = jnp.exp(lam_dt[None] * taus[:, None, None, None].astype(c64))
    hi = lax.Precision.HIGHEST
    kern = jnp.real(jnp.einsum('dgnp,tdgp,dgpm->dgtnm', c_mat, pw[:S5_CHUNK], b_bar, precision=hi))
    eye = jnp.eye(S5_GROUP, dtype=F32)
    k0 = kern[0, :, 0] + kern[1, :, 0] + d.reshape(S5_GROUPS, S5_GROUP)[:, :, None] * eye
    lags = jnp.concatenate([kern[1, :, :0:-1], k0[:, None], kern[0, :, 1:]], axis=1)
    idx = (np.arange(S5_CHUNK)[None, :] - np.arange(S5_CHUNK)[:, None]) + S5_CHUNK - 1
    m = lags[:, idx]
    m = m.transpose(0, 1, 4, 2, 3).reshape(S5_GROUPS, _S5_W, _S5_W)
    wf = pw[S5_CHUNK - 1::-1][:S5_CHUNK, 0][..., None] * b_bar[0][None]
    wr = pw[:S5_CHUNK, 1][..., None] * b_bar[1][None]
    to_rows = lambda w: w.transpose(1, 0, 3, 2).reshape(S5_GROUPS, _S5_W, S5_STATE)
    wf, wr = to_rows(wf), to_rows(wr)
    win = jnp.concatenate([jnp.real(wf), jnp.real(wr), jnp.imag(wf), jnp.imag(wr)], axis=-1)
    of = c_mat[0][None] * pw[1:, 0][:, :, None, :]
    orv = c_mat[1][None] * pw[S5_CHUNK:0:-1, 1][:, :, None, :]
    to_cols = lambda w: w.transpose(1, 3, 0, 2).reshape(S5_GROUPS, S5_STATE, _S5_W)
    of, orv = to_cols(of), to_cols(orv)
    zeros = jnp.zeros_like(jnp.real(of))
    wof = jnp.concatenate([jnp.real(of), zeros, -jnp.imag(of), zeros], axis=1)
    wor = jnp.concatenate([zeros, jnp.real(orv), zeros, -jnp.imag(orv)], axis=1)
    a16 = pw[S5_CHUNK]
    ar = jnp.concatenate([jnp.real(a16[0]), jnp.real(a16[1])], axis=-1)[:, None, :]
    ai = jnp.concatenate([jnp.imag(a16[0]), jnp.imag(a16[1])], axis=-1)[:, None, :]
    return {"m": m.astype(BF16), "win": win.astype(BF16), "wof": wof.astype(BF16), "wor": wor.astype(BF16),
            "ar": ar, "ai": ai}


def _pool_kernel(xp_ref, xc_ref, xn_ref, mod_ref, g_ref, wp_ref, sc_ref, o_ref):
    seg = pl.program_id(0) % SEGS_PER_BATCH
    is_ctx = seg >= LAT_SEGS
    p0 = jnp.where(is_ctx, 0, seg * SEG)
    lseq = jnp.where(is_ctx, CTX_LEN, SEQ)
    m = mod_ref[0]
    g = g_ref[...]
    hs = [_norm_mod(r[...], g, m[0:1], m[1:2]) for r in (xp_ref, xc_ref, xn_ref)]
    hcat = jnp.concatenate([h.astype(BF16) for h in hs], axis=0)
    r = lax.broadcasted_iota(jnp.int32, (SEG, 3 * SEG), 0)
    s = lax.broadcasted_iota(jnp.int32, (SEG, 3 * SEG), 1)
    pt = p0 + r
    ps = p0 - SEG + s
    rcol = lax.broadcasted_iota(jnp.int32, (SEG, 1), 0) + p0
    x = xc_ref[...]
    gate = m[2:3]
    for gi, w in enumerate(POOL_WINDOWS):
        cols = slice(gi * POOL_GROUP, (gi + 1) * POOL_GROUP)
        lo = jnp.maximum(pt - w // 2, 0)
        hi = jnp.minimum(pt + w - w // 2, lseq)
        band = jnp.where((ps >= lo) & (ps < hi), 1.0, 0.0).astype(BF16)
        tot = jnp.dot(band, hcat[:, cols], preferred_element_type=F32)
        cnt = (jnp.minimum(rcol + w - w // 2, lseq) - jnp.maximum(rcol - w // 2, 0)).astype(F32)
        resid = tot / cnt - hs[1][:, cols]
        y = jnp.dot(resid.astype(BF16), wp_ref[gi], preferred_element_type=F32) * sc_ref[:, cols]
        o_ref[:, cols] = x[:, cols] + gate[:, cols] * y


def _pool(x, modtab, g1, w_pool, scale):
    n, d = x.shape
    nseg = n // SEG

    def prev(g):
        return (jnp.maximum(g - 1, 0), 0)

    def nxt(g):
        return (jnp.minimum(g + 1, nseg - 1), 0)

    return pl.pallas_call(
        _pool_kernel,
        grid=(nseg,),
        in_specs=[pl.BlockSpec((SEG, d), prev), pl.BlockSpec((SEG, d), lambda g: (g, 0)), pl.BlockSpec((SEG, d), nxt),
                  pl.BlockSpec((1, N_MOD, d), lambda g: (g, 0, 0)),
                  pl.BlockSpec(g1.shape, lambda g: (0, 0)),
                  pl.BlockSpec(w_pool.shape, lambda g: (0, 0, 0)),
                  pl.BlockSpec(scale.shape, lambda g: (0, 0))],
        out_specs=pl.BlockSpec((SEG, d), lambda g: (g, 0)),
        out_shape=jax.ShapeDtypeStruct((n, d), F32),
        compiler_params=_cparams(("parallel",)),
        name="pool",
    )(x, x, x, modtab, g1, w_pool, scale)


def _swiglu_partial(h, wg, wu, wd):
    a = jax.nn.silu(jnp.dot(h, wg, preferred_element_type=F32)) * jnp.dot(h, wu, preferred_element_type=F32)
    return jnp.dot(a.astype(BF16), wd, preferred_element_type=F32)


def _ffn_kernel(x_ref, mod_ref, g_ref, wg_ref, wu_ref, wd_ref, o_ref, h_ref, acc_ref):
    f = pl.program_id(1)

    @pl.when(f == 0)
    def _():
        h_ref[...] = _norm_mod_tile(x_ref, mod_ref, g_ref, 3)
        acc_ref[...] = jnp.zeros_like(acc_ref)

    acc_ref[...] += _swiglu_partial(h_ref[...], wg_ref[...], wu_ref[...], wd_ref[...])

    @pl.when(f == pl.num_programs(1) - 1)
    def _():
        _gated_add(x_ref, mod_ref, acc_ref[...], 5, o_ref)


def _ffn(x, modtab, g2, w_gu, w_down, tm=512, tf=1408):
    n, d = x.shape
    nf = (w_gu.shape[1] // 2) // tf
    return pl.pallas_call(
        _ffn_kernel,
        grid=(n // tm, nf),
        in_specs=[pl.BlockSpec((tm, d), lambda g, f: (g, 0)),
                  pl.BlockSpec((tm // SEG, N_MOD, d), lambda g, f: (g, 0, 0)),
                  pl.BlockSpec(g2.shape, lambda g, f: (0, 0)),
                  pl.BlockSpec((d, tf), lambda g, f: (0, f)),
                  pl.BlockSpec((d, tf), lambda g, f: (0, nf + f)),
                  pl.BlockSpec((tf, d), lambda g, f: (f, 0))],
        out_specs=pl.BlockSpec((tm, d), lambda g, f: (g, 0)),
        out_shape=jax.ShapeDtypeStruct((n, d), F32),
        scratch_shapes=[pltpu.VMEM((tm, d), BF16), pltpu.VMEM((tm, d), F32)],
        compiler_params=_cparams(("parallel", "arbitrary")),
        name="ffn",
    )(x, modtab, g2, w_gu, w_gu, w_down)


MOE_TILE = 512
MOE_DMA_TILE = 512


def _router_kernel(x_ref, mod_ref, g_ref, w_ref, b_ref, h_ref, rw_ref, ri_ref):
    h = _norm_mod_tile(x_ref, mod_ref, g_ref, 3, out_dtype=F32)
    h_ref[...] = h
    logits = jnp.dot(h, w_ref[...], precision=lax.Precision.HIGHEST, preferred_element_type=F32) + b_ref[...]
    lane = lax.broadcasted_iota(jnp.int32, logits.shape, 1)
    neg = jnp.float32(-jnp.inf)
    lg = jnp.where(lane < N_EXPERTS, logits, neg)
    m1 = jnp.max(lg, axis=-1, keepdims=True)
    i1 = jnp.min(jnp.where(lg == m1, lane, HEAD_PAD), axis=-1, keepdims=True)
    lg2 = jnp.where(lane == i1, neg, lg)
    m2 = jnp.max(lg2, axis=-1, keepdims=True)
    i2 = jnp.min(jnp.where(lg2 == m2, lane, HEAD_PAD), axis=-1, keepdims=True)
    e2 = jnp.exp(m2 - m1)
    den = 1.0 + e2
    rw_ref[...] = jnp.where(lane == 0, 1.0 / den, jnp.where(lane == 1, e2 / den, 0.0))
    ri_ref[...] = jnp.where(lane == 0, i1, jnp.where(lane == 1, i2, 0))


def _router(x, modtab, g2, w_router, b_router, tm=512):
    n, d = x.shape
    wr = jnp.pad(w_router, ((0, 0), (0, HEAD_PAD - N_EXPERTS)))
    br = jnp.pad(b_router, (0, HEAD_PAD - N_EXPERTS)).reshape(1, HEAD_PAD)
    return pl.pallas_call(
        _router_kernel,
        grid=(n // tm,),
        in_specs=[pl.BlockSpec((tm, d), lambda g: (g, 0)),
                  pl.BlockSpec((tm // SEG, N_MOD, d), lambda g: (g, 0, 0)),
                  pl.BlockSpec(g2.shape, lambda g: (0, 0)),
                  pl.BlockSpec(wr.shape, lambda g: (0, 0)),
                  pl.BlockSpec(br.shape, lambda g: (0, 0))],
        out_specs=[pl.BlockSpec((tm, d), lambda g: (g, 0)),
                   pl.BlockSpec((tm, HEAD_PAD), lambda g: (g, 0)),
                   pl.BlockSpec((tm, HEAD_PAD), lambda g: (g, 0))],
        out_shape=[jax.ShapeDtypeStruct((n, d), F32),
                   jax.ShapeDtypeStruct((n, HEAD_PAD), F32),
                   jax.ShapeDtypeStruct((n, HEAD_PAD), jnp.int32)],
        compiler_params=_cparams(("parallel",)),
        name="router",
    )(x, modtab, g2, wr, br)


def _moe_plan(ri, n):
    e = ri[:, :TOP_K]
    mask = (e[:, :, None] == jnp.arange(N_EXPERTS, dtype=jnp.int32)).astype(jnp.int32).sum(axis=1)
    csum = jnp.cumsum(mask, axis=0)
    counts = csum[-1]
    padded = (counts + MOE_TILE - 1) // MOE_TILE * MOE_TILE
    ends = jnp.cumsum(padded)
    starts = ends - padded
    dest = starts[e] + jnp.take_along_axis(csum, e, axis=1) - 1
    n_tiles = (TOP_K * n) // MOE_TILE + N_EXPERTS
    tile_expert = jnp.searchsorted(ends // MOE_TILE, jnp.arange(n_tiles, dtype=jnp.int32), side="right")
    tile_expert = jnp.minimum(tile_expert, N_EXPERTS - 1).astype(jnp.int32)
    n_valid = (ends[-1] // MOE_TILE).astype(jnp.int32).reshape(1)
    return dest[:, 0].astype(jnp.int32), dest[:, 1].astype(jnp.int32), tile_expert, n_valid, n_tiles


def _row(ref, i):
    return ref.at[pl.ds(i, 1)]


def _scatter_rows_kernel(d0_ref, d1_ref, h_hbm, init_hbm, o_hbm, sem):
    del init_hbm
    base = pl.program_id(0) * MOE_DMA_TILE

    def issue(r, carry):
        t = base + r
        pltpu.make_async_copy(_row(h_hbm, t), _row(o_hbm, d0_ref[t]), sem).start()
        pltpu.make_async_copy(_row(h_hbm, t), _row(o_hbm, d1_ref[t]), sem).start()
        return carry

    lax.fori_loop(0, MOE_DMA_TILE, issue, 0)
    pltpu.make_async_copy(h_hbm.at[pl.ds(0, TOP_K * MOE_DMA_TILE)], o_hbm.at[pl.ds(0, TOP_K * MOE_DMA_TILE)], sem).wait()


def _scatter_rows(d0, d1, h, n_rows):
    n, d = h.shape
    init = jnp.zeros((n_rows, d), F32)
    return pl.pallas_call(
        _scatter_rows_kernel,
        grid_spec=pltpu.PrefetchScalarGridSpec(
            num_scalar_prefetch=2,
            grid=(n // MOE_DMA_TILE,),
            in_specs=[pl.BlockSpec(memory_space=pl.ANY), pl.BlockSpec(memory_space=pl.ANY)],
            out_specs=pl.BlockSpec(memory_space=pl.ANY),
            scratch_shapes=[pltpu.SemaphoreType.DMA(())]),
        out_shape=jax.ShapeDtypeStruct((n_rows, d), F32),
        input_output_aliases={3: 0},
        compiler_params=_cparams(("arbitrary",)),
        name="moe_scatter",
    )(d0, d1, h, init)


def _moe_ffn_kernel(te_ref, nv_ref, xs_ref, wg_ref, wu_ref, wd_ref, o_ref, acc_ref):
    del te_ref
    i = pl.program_id(0)
    f = pl.program_id(1)
    last = pl.num_programs(1) - 1

    @pl.when(i < nv_ref[0])
    def _():
        part = _swiglu_partial(xs_ref[...].astype(BF16), wg_ref[0], wu_ref[0], wd_ref[0])

        @pl.when(f == 0)
        def _():
            acc_ref[...] = part

        @pl.when(f > 0)
        def _():
            acc_ref[...] += part

        @pl.when(f == last)
        def _():
            o_ref[...] = acc_ref[...]

    @pl.when((i >= nv_ref[0]) & (f == last))
    def _():
        o_ref[...] = jnp.zeros_like(o_ref)


def _moe_ffn(xs, tile_expert, n_valid, w_gu, w_down, tf=1408):
    rows, d = xs.shape
    nf = (w_gu.shape[2] // 2) // tf
    fsel = lambda i, f, nv: jnp.where(i < nv[0], f, nf - 1)
    return pl.pallas_call(
        _moe_ffn_kernel,
        grid_spec=pltpu.PrefetchScalarGridSpec(
            num_scalar_prefetch=2,
            grid=(rows // MOE_TILE, nf),
            in_specs=[pl.BlockSpec((MOE_TILE, d), lambda i, f, te, nv: (i, 0)),
                      pl.BlockSpec((1, d, tf), lambda i, f, te, nv: (te[i], 0, fsel(i, f, nv))),
                      pl.BlockSpec((1, d, tf), lambda i, f, te, nv: (te[i], 0, nf + fsel(i, f, nv))),
                      pl.BlockSpec((1, tf, d), lambda i, f, te, nv: (te[i], fsel(i, f, nv), 0))],
            out_specs=pl.BlockSpec((MOE_TILE, d), lambda i, f, te, nv: (i, 0)),
            scratch_shapes=[pltpu.VMEM((MOE_TILE, d), F32)]),
        out_shape=jax.ShapeDtypeStruct((rows, d), F32),
        compiler_params=_cparams(("arbitrary", "arbitrary")),
        name="moe_ffn",
    )(tile_expert, n_valid, xs, w_gu, w_gu, w_down)


def _combine_kernel(d0_ref, d1_ref, x_ref, mod_ref, rw_ref, y_hbm, o_ref, ya_ref, yb_ref, sem):
    base = pl.program_id(0) * MOE_DMA_TILE

    def issue(r, carry):
        t = base + r
        pltpu.make_async_copy(_row(y_hbm, d0_ref[t]), _row(ya_ref, r), sem.at[0]).start()
        pltpu.make_async_copy(_row(y_hbm, d1_ref[t]), _row(yb_ref, r), sem.at[1]).start()
        return carry

    lax.fori_loop(0, MOE_DMA_TILE, issue, 0)
    rw = rw_ref[...]
    lane = lax.broadcasted_iota(jnp.int32, rw.shape, 1)
    w1 = jnp.sum(jnp.where(lane == 0, rw, 0.0), axis=-1, keepdims=True)
    w2 = jnp.sum(jnp.where(lane == 1, rw, 0.0), axis=-1, keepdims=True)
    pltpu.make_async_copy(y_hbm.at[pl.ds(0, MOE_DMA_TILE)], ya_ref, sem.at[0]).wait()
    pltpu.make_async_copy(y_hbm.at[pl.ds(0, MOE_DMA_TILE)], yb_ref, sem.at[1]).wait()
    _gated_add(x_ref, mod_ref, w1 * ya_ref[...] + w2 * yb_ref[...], 5, o_ref)


def _combine(d0, d1, x, modtab, rw, y):
    n, d = x.shape
    tm = MOE_DMA_TILE
    return pl.pallas_call(
        _combine_kernel,
        grid_spec=pltpu.PrefetchScalarGridSpec(
            num_scalar_prefetch=2,
            grid=(n // tm,),
            in_specs=[pl.BlockSpec((tm, d), lambda g, d0, d1: (g, 0)),
                      pl.BlockSpec((tm // SEG, N_MOD, d), lambda g, d0, d1: (g, 0, 0)),
                      pl.BlockSpec((tm, HEAD_PAD), lambda g, d0, d1: (g, 0)),
                      pl.BlockSpec(memory_space=pl.ANY)],
            out_specs=pl.BlockSpec((tm, d), lambda g, d0, d1: (g, 0)),
            scratch_shapes=[pltpu.VMEM((tm, d), F32), pltpu.VMEM((tm, d), F32), pltpu.SemaphoreType.DMA((2,))]),
        out_shape=jax.ShapeDtypeStruct((n, d), F32),
        compiler_params=_cparams(("arbitrary",)),
        name="moe_combine",
    )(d0, d1, x, modtab, rw, y)


def _moe(x, modtab, g2, w_router, b_router, w_gu, w_down):
    n = x.shape[0]
    h, rw, ri = _router(x, modtab, g2, w_router, b_router)
    d0, d1, tile_expert, n_valid, n_tiles = _moe_plan(ri, n)
    xs = _scatter_rows(d0, d1, h, n_tiles * MOE_TILE)
    ys = _moe_ffn(xs, tile_expert, n_valid, w_gu, w_down)
    return _combine(d0, d1, x, modtab, rw, ys)


def _final_kernel(x_ref, g_ref, o_ref):
    o_ref[0] = _rms(x_ref[0], g_ref[...])


def _final_norm(x3, g, tm=512):
    b, _, d = x3.shape
    return pl.pallas_call(
        _final_kernel,
        grid=(b, SEQ // tm),
        in_specs=[pl.BlockSpec((1, tm, d), lambda bi, t: (bi, t, 0)), pl.BlockSpec(g.shape, lambda bi, t: (0, 0))],
        out_specs=pl.BlockSpec((1, tm, d), lambda bi, t: (bi, t, 0)),
        out_shape=jax.ShapeDtypeStruct((b, SEQ, d), F32),
        compiler_params=_cparams(("parallel", "parallel")),
        name="final_norm",
    )(x3, g)


def kernel(x, c, ctx, c_ctx, norm1_g, norm2_g, ada_w, ada_b, final_norm_g, mla_w_dq, mla_q_norm, mla_w_uq, mla_w_dkv, mla_kv_norm, mla_w_ukv, mla_w_o, s5_a_re, s5_a_im, s5_log_dt, s5_b_re, s5_b_im, s5_c_re, s5_c_im, s5_d, s5_w_glu, pool_w, pool_scale, ffn_w_gu, ffn_w_down, moe_w_router, moe_b_router, moe_w_gu, moe_w_down):
    b, l, d = x.shape
    n = b * T_TOK
    xs = jnp.concatenate([x, ctx], axis=1).reshape(n, d)

    rows = -(-(b + 1) // 8) * 8
    cc = jnp.concatenate([c, c_ctx[None], jnp.zeros((rows - b - 1, d), F32)], axis=0)
    mods = _ada(cc, ada_w, ada_b)
    seg_src = np.array([bi if sj < LAT_SEGS else b for bi in range(b) for sj in range(SEGS_PER_BATCH)], np.int32)

    rope = _rope_tables()
    for i in range(DEPTH):
        last = i == DEPTH - 1
        modtab = mods[i].reshape(rows, N_MOD, d)[seg_src]
        g1 = norm1_g[i].reshape(1, d)
        g2 = norm2_g[i].reshape(1, d)
        kind = i % N_MIXERS
        j = i // N_MIXERS
        if kind == 0:
            pw = _mla_weights(mla_w_dq[j], mla_q_norm[j], mla_w_uq[j], mla_w_dkv[j], mla_kv_norm[j], mla_w_ukv[j])
            q, k, v = _mla_proj(xs, modtab, g1, pw, rope)
            o = _attention(q.reshape(b, T_TOK, -1), k.reshape(b, T_TOK, -1), v.reshape(b, T_TOK, -1))
            xs = _proj_res(xs, modtab, o.reshape(n, -1), mla_w_o[j].astype(BF16))
        elif kind == 1:
            h = _normmod(xs, modtab, g1)
            nch = T_TOK // S5_CHUNK
            u = h.reshape(b, nch, S5_CHUNK, S5_GROUPS, S5_GROUP).transpose(3, 1, 0, 2, 4).reshape(S5_GROUPS, nch * b, _S5_W)
            sp = _s5_params(s5_a_re[j], s5_a_im[j], s5_log_dt[j], s5_b_re[j], s5_b_im[j], s5_c_re[j], s5_c_im[j], s5_d[j])
            y = _s5_core(u, sp, b)
            y = y.reshape(S5_GROUPS, nch, b, S5_CHUNK, S5_GROUP).transpose(2, 1, 3, 0, 4).reshape(n, d)
            xs = _glu_res(xs, modtab, y, s5_w_glu[j].astype(BF16))
        else:
            xs = _pool(xs, modtab, g1, pool_w[j].astype(BF16), pool_scale[j].reshape(1, d))
        kk = i // 2
        if i % 2 == 0:
            xs = _ffn(xs, modtab, g2, ffn_w_gu[kk].astype(BF16), ffn_w_down[kk].astype(BF16))
        else:
            xs = _moe(xs, modtab, g2, moe_w_router[kk], moe_b_router[kk], moe_w_gu[kk].astype(BF16), moe_w_down[kk].astype(BF16))
    return _final_norm(xs.reshape(b, T_TOK, d), final_norm_g.reshape(1, d))
```

```python
import functools
import math

import jax
import jax.numpy as jnp
import numpy as np
from jax import lax
from jax.experimental import pallas as pl
from jax.experimental.pallas import tpu as pltpu

F32 = jnp.float32
BF16 = jnp.bfloat16

D_MODEL = 1024
SEQ = 4096
DEPTH = 4
GRID_W = 64
CTX_LEN = 256
N_MIXERS = 3
NORM_EPS = 1e-6

MLA_HEADS = 16
MLA_Q_LORA = 384
MLA_KV_LORA = 256
MLA_NOPE = 64
MLA_ROPE = 32
MLA_V = 64
MLA_QK = MLA_NOPE + MLA_ROPE
ROPE_AXIS_FREQS = MLA_ROPE // 4
ROPE_THETA = 10000.0
HEAD_PAD = 128

S5_GROUP = 16
S5_GROUPS = D_MODEL // S5_GROUP
S5_STATE = 64
S5_MAX_RE = -1e-4
S5_CHUNK = 16

POOL_WINDOWS = (2, 4, 8, 16)
POOL_GROUP = D_MODEL // len(POOL_WINDOWS)

FFN_DIM = 2816
N_EXPERTS = 8
TOP_K = 2

T_TOK = SEQ + CTX_LEN
SEG = CTX_LEN
SEGS_PER_BATCH = T_TOK // SEG
LAT_SEGS = SEQ // SEG
N_MOD = 6

VMEM_LIMIT = 56 * 1024 * 1024


def _cparams(sem, vmem=VMEM_LIMIT):
    return pltpu.CompilerParams(dimension_semantics=sem, vmem_limit_bytes=vmem)


def _rms(x, g):
    return x * lax.rsqrt(jnp.mean(x * x, axis=-1, keepdims=True) + NORM_EPS) * g


def _norm_mod(x, g, shift, scale):
    return _rms(x, g) * (1.0 + scale) + shift


def _norm_mod_tile(x_ref, mod_ref, g_ref, shift_idx, out_dtype=BF16):
    parts = []
    for s in range(x_ref.shape[0] // SEG):
        m = mod_ref[s]
        x = x_ref[s * SEG:(s + 1) * SEG, :]
        parts.append(_norm_mod(x, g_ref[...], m[shift_idx:shift_idx + 1], m[shift_idx + 1:shift_idx + 2]).astype(out_dtype))
    return parts[0] if len(parts) == 1 else jnp.concatenate(parts, axis=0)


def _ada_kernel(c_ref, w_ref, b_ref, o_ref):
    s = jax.nn.silu(c_ref[...])
    o_ref[0] = jnp.dot(s, w_ref[0], precision=lax.Precision.HIGHEST, preferred_element_type=F32) + b_ref[0]


def _ada(cc, ada_w, ada_b):
    depth, d, n6 = ada_w.shape
    rows = cc.shape[0]
    tn = 1024
    return pl.pallas_call(
        _ada_kernel,
        grid=(depth, n6 // tn),
        in_specs=[pl.BlockSpec((rows, d), lambda i, j: (0, 0)),
                  pl.BlockSpec((1, d, tn), lambda i, j: (i, 0, j)),
                  pl.BlockSpec((1, 1, tn), lambda i, j: (i, 0, j))],
        out_specs=pl.BlockSpec((1, rows, tn), lambda i, j: (i, 0, j)),
        out_shape=jax.ShapeDtypeStruct((depth, rows, n6), F32),
        compiler_params=_cparams(("arbitrary", "arbitrary")),
        name="ada",
    )(cc, ada_w, ada_b.reshape(depth, 1, n6))


_W1_COLS = MLA_Q_LORA + MLA_KV_LORA + 2 * HEAD_PAD
_QW = MLA_HEADS * HEAD_PAD


def _mla_proj_kernel(x_ref, mod_ref, g1_ref, w1_ref, qn_ref, wq_ref, kvn_ref, wk_ref, we_ref, wv_ref,
                     cq_ref, sq_ref, ck_ref, sk_ref, q_ref, k_ref, v_ref):
    h = _norm_mod_tile(x_ref, mod_ref, g1_ref, 0)
    d = jnp.dot(h, w1_ref[...], preferred_element_type=F32)
    dq = d[:, :MLA_Q_LORA]
    ckv = d[:, MLA_Q_LORA:MLA_Q_LORA + MLA_KV_LORA]
    kr = d[:, MLA_Q_LORA + MLA_KV_LORA:MLA_Q_LORA + MLA_KV_LORA + HEAD_PAD]
    kr_sw = d[:, MLA_Q_LORA + MLA_KV_LORA + HEAD_PAD:]
    qn = _rms(dq, qn_ref[...]).astype(BF16)
    qq = jnp.dot(qn, wq_ref[...], preferred_element_type=F32)
    cq = cq_ref[...]
    sq = sq_ref[...]
    for hd in range(MLA_HEADS):
        lo = hd * HEAD_PAD
        q_ref[:, lo:lo + HEAD_PAD] = (qq[:, lo:lo + HEAD_PAD] * cq + qq[:, _QW + lo:_QW + lo + HEAD_PAD] * sq).astype(BF16)
    c = _rms(ckv, kvn_ref[...]).astype(BF16)
    kr_roped = (kr * ck_ref[...] + kr_sw * sk_ref[...]).astype(BF16)
    k = jnp.dot(c, wk_ref[...], preferred_element_type=F32) + jnp.dot(kr_roped, we_ref[...], preferred_element_type=F32)
    k_ref[...] = k.astype(BF16)
    v = jnp.dot(c, wv_ref[...], preferred_element_type=F32)
    lane = lax.broadcasted_iota(jnp.int32, v.shape, 1)
    v_ref[...] = jnp.where(lane % HEAD_PAD == MLA_V, 1.0, v).astype(BF16)


def _mla_proj(x, modtab, g1, pw, rope):
    n, d = x.shape
    tm = SEG
    full = lambda a: pl.BlockSpec(a.shape, lambda g: (0,) * a.ndim)
    pos = lambda g: (g % SEGS_PER_BATCH, 0)
    cq, sq, ck, sk = rope
    return pl.pallas_call(
        _mla_proj_kernel,
        grid=(n // tm,),
        in_specs=[pl.BlockSpec((tm, d), lambda g: (g, 0)),
                  pl.BlockSpec((tm // SEG, N_MOD, d), lambda g: (g, 0, 0)),
                  full(g1), full(pw["w1"]), full(pw["qn"]), full(pw["wq"]), full(pw["kvn"]),
                  full(pw["wk"]), full(pw["we"]), full(pw["wv"]),
                  pl.BlockSpec((tm, HEAD_PAD), pos), pl.BlockSpec((tm, HEAD_PAD), pos),
                  pl.BlockSpec((tm, HEAD_PAD), pos), pl.BlockSpec((tm, HEAD_PAD), pos)],
        out_specs=[pl.BlockSpec((tm, _QW), lambda g: (g, 0)),
                   pl.BlockSpec((tm, _QW), lambda g: (g, 0)),
                   pl.BlockSpec((tm, _QW), lambda g: (g, 0))],
        out_shape=[jax.ShapeDtypeStruct((n, _QW), BF16)] * 3,
        compiler_params=_cparams(("parallel",)),
        name="mla_proj",
    )(x, modtab, g1, pw["w1"], pw["qn"], pw["wq"], pw["kvn"], pw["wk"], pw["we"], pw["wv"], cq, sq, ck, sk)


def _mla_weights(w_dq, q_norm, w_uq, w_dkv, kv_norm, w_ukv):
    d = w_dq.shape[0]
    swap = np.arange(MLA_ROPE) ^ ROPE_AXIS_FREQS
    w_kr = w_dkv[:, MLA_KV_LORA:]
    padr = lambda w: jnp.pad(w, ((0, 0), (0, HEAD_PAD - MLA_ROPE)))
    w1 = jnp.concatenate([w_dq, w_dkv[:, :MLA_KV_LORA], padr(w_kr), padr(w_kr[:, swap])], axis=1)
    uq = w_uq.reshape(MLA_Q_LORA, MLA_HEADS, MLA_QK)
    zpad = jnp.zeros((MLA_Q_LORA, MLA_HEADS, HEAD_PAD - MLA_QK), w_uq.dtype)
    wq_main = jnp.concatenate([uq, zpad], axis=-1).reshape(MLA_Q_LORA, _QW)
    uq_sw = jnp.concatenate([jnp.zeros((MLA_Q_LORA, MLA_HEADS, MLA_NOPE), w_uq.dtype),
                             uq[:, :, MLA_NOPE:][:, :, swap], zpad], axis=-1).reshape(MLA_Q_LORA, _QW)
    wq = jnp.concatenate([wq_main, uq_sw], axis=1)
    ukv = w_ukv.reshape(MLA_KV_LORA, MLA_HEADS, MLA_NOPE + MLA_V)
    wk = jnp.concatenate([ukv[:, :, :MLA_NOPE],
                          jnp.zeros((MLA_KV_LORA, MLA_HEADS, HEAD_PAD - MLA_NOPE), w_ukv.dtype)], axis=-1)
    wk = wk.reshape(MLA_KV_LORA, _QW)
    wv = jnp.concatenate([ukv[:, :, MLA_NOPE:],
                          jnp.zeros((MLA_KV_LORA, MLA_HEADS, HEAD_PAD - MLA_V), w_ukv.dtype)], axis=-1)
    wv = wv.reshape(MLA_KV_LORA, _QW)
    e = np.zeros((HEAD_PAD, MLA_HEADS, HEAD_PAD), np.float32)
    for r in range(MLA_ROPE):
        e[r, :, MLA_NOPE + r] = 1.0
    we = jnp.asarray(e.reshape(HEAD_PAD, _QW))
    return {"w1": w1.astype(BF16), "qn": q_norm.reshape(1, -1), "wq": wq.astype(BF16),
            "kvn": kv_norm.reshape(1, -1), "wk": wk.astype(BF16), "we": we.astype(BF16), "wv": wv.astype(BF16)}


def _rope_tables():
    rows = SEQ // GRID_W
    row = jnp.repeat(jnp.arange(rows, dtype=F32), GRID_W)
    col = jnp.tile(jnp.arange(GRID_W, dtype=F32), rows)
    inv_freq = 1.0 / (ROPE_THETA ** (jnp.arange(ROPE_AXIS_FREQS, dtype=F32) / ROPE_AXIS_FREQS))
    ang = jnp.stack([row[:, None] * inv_freq, col[:, None] * inv_freq], axis=1)
    ang = jnp.concatenate([ang, jnp.zeros((CTX_LEN, 2, ROPE_AXIS_FREQS), F32)], axis=0)
    cos = jnp.cos(ang)
    sin = jnp.sin(ang)
    c32 = jnp.stack([cos, cos], axis=2).reshape(T_TOK, MLA_ROPE)
    s32 = jnp.stack([-sin, sin], axis=2).reshape(T_TOK, MLA_ROPE)
    scale = MLA_QK ** -0.5 * math.log2(math.e)
    zq = jnp.zeros((T_TOK, HEAD_PAD - MLA_QK), F32)
    cq = jnp.concatenate([jnp.full((T_TOK, MLA_NOPE), scale, F32), c32 * scale, zq], axis=1)
    sq = jnp.concatenate([jnp.zeros((T_TOK, MLA_NOPE), F32), s32 * scale, zq], axis=1)
    zk = jnp.zeros((T_TOK, HEAD_PAD - MLA_ROPE), F32)
    ck = jnp.concatenate([c32, zk], axis=1)
    sk = jnp.concatenate([s32, zk], axis=1)
    return cq, sq, ck, sk


ATTN_TQ = 512
ATTN_KEY_CHUNK = 1024
_LAT_CHUNKS = tuple((lo, ATTN_KEY_CHUNK) for lo in range(0, SEQ, ATTN_KEY_CHUNK))
_CTX_CHUNK = ((SEQ, CTX_LEN),)


def _attend_pair(q, k_ref, v_ref, chunks):
    outs = []
    for j in range(2):
        cols = slice(j * HEAD_PAD, (j + 1) * HEAD_PAD)
        qj = q[:, cols]
        m = None
        acc = None
        for lo, size in chunks:
            s = lax.dot_general(qj, k_ref[0, lo:lo + size, cols], (((1,), (1,)), ((), ())), preferred_element_type=F32)
            mc = jnp.max(s, axis=-1, keepdims=True)
            m_new = mc if m is None else jnp.maximum(m, mc)
            pv = jnp.dot(jnp.exp2(s - m_new).astype(BF16), v_ref[0, lo:lo + size, cols], preferred_element_type=F32)
            acc = pv if m is None else acc * jnp.exp2(m - m_new) + pv
            m = m_new
        outs.append(acc / acc[:, MLA_V:MLA_V + 1])
    lane = lax.broadcasted_iota(jnp.int32, outs[0].shape, 1)
    return jnp.where(lane < MLA_V, outs[0], pltpu.roll(outs[1], MLA_V, 1))


def _attn_kernel(q_ref, k_ref, v_ref, o_ref, *, lat_tiles):
    qi = pl.program_id(2)

    @pl.when(qi < lat_tiles)
    def _():
        o_ref[0] = _attend_pair(q_ref[0], k_ref, v_ref, _LAT_CHUNKS + _CTX_CHUNK).astype(BF16)

    @pl.when(qi >= lat_tiles)
    def _():
        o_ref[0, :CTX_LEN, :] = _attend_pair(q_ref[0, :CTX_LEN, :], k_ref, v_ref, _CTX_CHUNK).astype(BF16)


def _attention(q, k, v):
    b = q.shape[0]
    tq = ATTN_TQ
    lat_tiles = SEQ // tq
    return pl.pallas_call(
        functools.partial(_attn_kernel, lat_tiles=lat_tiles),
        grid=(b, MLA_HEADS // 2, lat_tiles + 1),
        in_specs=[pl.BlockSpec((1, tq, 2 * HEAD_PAD), lambda bi, hp, qi: (bi, qi, hp)),
                  pl.BlockSpec((1, T_TOK, 2 * HEAD_PAD), lambda bi, hp, qi: (bi, 0, hp)),
                  pl.BlockSpec((1, T_TOK, 2 * HEAD_PAD), lambda bi, hp, qi: (bi, 0, hp))],
        out_specs=pl.BlockSpec((1, tq, 2 * MLA_V), lambda bi, hp, qi: (bi, qi, hp)),
        out_shape=jax.ShapeDtypeStruct((b, T_TOK, MLA_HEADS * MLA_V), BF16),
        compiler_params=_cparams(("parallel", "parallel", "arbitrary")),
        name="attention",
    )(q, k, v)


def _gated_add(x_ref, mod_ref, y, gate_idx, o_ref):
    for s in range(x_ref.shape[0] // SEG):
        rows = slice(s * SEG, (s + 1) * SEG)
        g = mod_ref[s][gate_idx:gate_idx + 1]
        o_ref[rows, :] = x_ref[rows, :] + g * y[rows, :]


def _proj_res_kernel(x_ref, mod_ref, y_ref, w_ref, o_ref):
    y = jnp.dot(y_ref[...], w_ref[...], preferred_element_type=F32)
    _gated_add(x_ref, mod_ref, y, 2, o_ref)


def _proj_res(x, modtab, y, w, tm=512):
    n, d = x.shape
    return pl.pallas_call(
        _proj_res_kernel,
        grid=(n // tm,),
        in_specs=[pl.BlockSpec((tm, d), lambda g: (g, 0)),
                  pl.BlockSpec((tm // SEG, N_MOD, d), lambda g: (g, 0, 0)),
                  pl.BlockSpec((tm, y.shape[1]), lambda g: (g, 0)),
                  pl.BlockSpec(w.shape, lambda g: (0, 0))],
        out_specs=pl.BlockSpec((tm, d), lambda g: (g, 0)),
        out_shape=jax.ShapeDtypeStruct((n, d), F32),
        compiler_params=_cparams(("parallel",)),
        name="proj_res",
    )(x, modtab, y, w)


def _glu_res_kernel(x_ref, mod_ref, y_ref, w_ref, o_ref):
    d = x_ref.shape[1]
    g = jax.nn.gelu(y_ref[...]).astype(BF16)
    z = jnp.dot(g, w_ref[...], preferred_element_type=F32)
    y = z[:, :d] * jax.nn.sigmoid(z[:, d:])
    _gated_add(x_ref, mod_ref, y, 2, o_ref)


def _glu_res(x, modtab, y, w, tm=512):
    n, d = x.shape
    return pl.pallas_call(
        _glu_res_kernel,
        grid=(n // tm,),
        in_specs=[pl.BlockSpec((tm, d), lambda g: (g, 0)),
                  pl.BlockSpec((tm // SEG, N_MOD, d), lambda g: (g, 0, 0)),
                  pl.BlockSpec((tm, d), lambda g: (g, 0)),
                  pl.BlockSpec(w.shape, lambda g: (0, 0))],
        out_specs=pl.BlockSpec((tm, d), lambda g: (g, 0)),
        out_shape=jax.ShapeDtypeStruct((n, d), F32),
        compiler_params=_cparams(("parallel",)),
        name="glu_res",
    )(x, modtab, y, w)


def _normmod_kernel(x_ref, mod_ref, g_ref, o_ref):
    o_ref[...] = _norm_mod_tile(x_ref, mod_ref, g_ref, 0)


def _normmod(x, modtab, g1, tm=512):
    n, d = x.shape
    return pl.pallas_call(
        _normmod_kernel,
        grid=(n // tm,),
        in_specs=[pl.BlockSpec((tm, d), lambda g: (g, 0)),
                  pl.BlockSpec((tm // SEG, N_MOD, d), lambda g: (g, 0, 0)),
                  pl.BlockSpec(g1.shape, lambda g: (0, 0))],
        out_specs=pl.BlockSpec((tm, d), lambda g: (g, 0)),
        out_shape=jax.ShapeDtypeStruct((n, d), BF16),
        compiler_params=_cparams(("parallel",)),
        name="normmod",
    )(x, modtab, g1)


_S5_LAT_CHUNKS = SEQ // S5_CHUNK
_S5_CHUNKS = T_TOK // S5_CHUNK
_S5_W = S5_CHUNK * S5_GROUP


def _s5_kernel(u_ref, m_ref, win_ref, wof_ref, wor_ref, ar_ref, ai_ref, y_ref, z_ref, sf_ref, sr_ref, *, nb):
    u = u_ref[0]
    z_ref[...] = jnp.dot(u, win_ref[0], preferred_element_type=F32)
    half = _S5_W // 2
    ar = ar_ref[0]
    ai = ai_ref[0]
    lane = lax.broadcasted_iota(jnp.int32, (nb, _S5_W), 1)
    is_fwd = (lane % half) < S5_STATE

    def step(i, s):
        cf = jnp.where(i < _S5_CHUNKS - _S5_LAT_CHUNKS, i + _S5_LAT_CHUNKS, i - (_S5_CHUNKS - _S5_LAT_CHUNKS))
        cr = _S5_CHUNKS - 1 - i
        rf = pl.multiple_of(cf * nb, nb)
        rr = pl.multiple_of(cr * nb, nb)
        sf_ref[pl.ds(rf, nb), :] = s
        sr_ref[pl.ds(rr, nb), :] = s
        z = jnp.where(is_fwd, z_ref[pl.ds(rf, nb), :], z_ref[pl.ds(rr, nb), :])
        re = s[:, :half]
        im = s[:, half:]
        return jnp.concatenate([ar * re - ai * im + z[:, :half], ar * im + ai * re + z[:, half:]], axis=1)

    lax.fori_loop(0, _S5_CHUNKS, step, jnp.zeros((nb, _S5_W), F32))
    y = jnp.dot(u, m_ref[0], preferred_element_type=F32)
    y = y + jnp.dot(sf_ref[...].astype(BF16), wof_ref[0], preferred_element_type=F32)
    y = y + jnp.dot(sr_ref[...].astype(BF16), wor_ref[0], preferred_element_type=F32)
    y_ref[0] = y


def _s5_core(u, sp, nb):
    g, rows, w = u.shape
    blk = lambda a: pl.BlockSpec((1,) + a.shape[1:], lambda i: (i,) + (0,) * (a.ndim - 1))
    return pl.pallas_call(
        functools.partial(_s5_kernel, nb=nb),
        grid=(g,),
        in_specs=[blk(u), blk(sp["m"]), blk(sp["win"]), blk(sp["wof"]), blk(sp["wor"]), blk(sp["ar"]), blk(sp["ai"])],
        out_specs=pl.BlockSpec((1, rows, w), lambda i: (i, 0, 0)),
        out_shape=jax.ShapeDtypeStruct((g, rows, w), F32),
        scratch_shapes=[pltpu.VMEM((rows, w), F32), pltpu.VMEM((rows, w), F32), pltpu.VMEM((rows, w), F32)],
        compiler_params=_cparams(("parallel",)),
        name="s5_core",
    )(u, sp["m"], sp["win"], sp["wof"], sp["wor"], sp["ar"], sp["ai"])


def _s5_params(a_re, a_im, log_dt, b_re, b_im, c_re, c_im, d):
    c64 = jnp.complex64
    lam = lax.complex(jnp.minimum(a_re, S5_MAX_RE), a_im)
    lam_dt = lam * jnp.exp(log_dt)[..., None]
    lam_bar = jnp.exp(lam_dt)
    b_bar = ((lam_bar - 1.0) / lam)[..., None] * lax.complex(b_re, b_im)
    c_mat = lax.complex(c_re, c_im)
    taus = jnp.arange(S5_CHUNK + 1, dtype=F32)
    pw = jnp.exp(lam_dt[None] * taus[:, None, None, None].astype(c64))
    hi = lax.Precision.HIGHEST
    kern = jnp.real(jnp.einsum('dgnp,tdgp,dgpm->dgtnm', c_mat, pw[:S5_CHUNK], b_bar, precision=hi))
    eye = jnp.eye(S5_GROUP, dtype=F32)
    k0 = kern[0, :, 0] + kern[1, :, 0] + d.reshape(S5_GROUPS, S5_GROUP)[:, :, None] * eye
    lags = jnp.concatenate([kern[1, :, :0:-1], k0[:, None], kern[0, :, 1:]], axis=1)
    idx = (np.arange(S5_CHUNK)[None, :] - np.arange(S5_CHUNK)[:, None]) + S5_CHUNK - 1
    m = lags[:, idx]
    m = m.transpose(0, 1, 4, 2, 3).reshape(S5_GROUPS, _S5_W, _S5_W)
    wf = pw[S5_CHUNK - 1::-1][:S5_CHUNK, 0][..., None] * b_bar[0][None]
    wr = pw[:S5_CHUNK, 1][..., None] * b_bar[1][None]
    to_rows = lambda w: w.transpose(1, 0, 3, 2).reshape(S5_GROUPS, _S5_W, S5_STATE)
    wf, wr = to_rows(wf), to_rows(wr)
    win = jnp.concatenate([jnp.real(wf), jnp.real(wr), jnp.imag(wf), jnp.imag(wr)], axis=-1)
    of = c_mat[0][None] * pw[1:, 0][:, :, None, :]
    orv = c_mat[1][None] * pw[S5_CHUNK:0:-1, 1][:, :, None, :]
    to_cols = lambda w: w.transpose(1, 3, 0, 2).reshape(S5_GROUPS, S5_STATE, _S5_W)
    of, orv = to_cols(of), to_cols(orv)
    zeros = jnp.zeros_like(jnp.real(of))
    wof = jnp.concatenate([jnp.real(of), zeros, -jnp.imag(of), zeros], axis=1)
    wor = jnp.concatenate([zeros, jnp.real(orv), zeros, -jnp.imag(orv)], axis=1)
    a16 = pw[S5_CHUNK]
    ar = jnp.concatenate([jnp.real(a16[0]), jnp.real(a16[1])], axis=-1)[:, None, :]
    ai = jnp.concatenate([jnp.imag(a16[0]), jnp.imag(a16[1])], axis=-1)[:, None, :]
    return {"m": m.astype(BF16), "win": win.astype(BF16), "wof": wof.astype(BF16), "wor": wor.astype(BF16),
            "ar": ar, "ai": ai}


def _pool_kernel(xp_ref, xc_ref, xn_ref, mod_ref, g_ref, wp_ref, sc_ref, o_ref):
    seg = pl.program_id(0) % SEGS_PER_BATCH
    is_ctx = seg >= LAT_SEGS
    p0 = jnp.where(is_ctx, 0, seg * SEG)
    lseq = jnp.where(is_ctx, CTX_LEN, SEQ)
    m = mod_ref[0]
    g = g_ref[...]
    hs = [_norm_mod(r[...], g, m[0:1], m[1:2]) for r in (xp_ref, xc_ref, xn_ref)]
    hcat = jnp.concatenate([h.astype(BF16) for h in hs], axis=0)
    r = lax.broadcasted_iota(jnp.int32, (SEG, 3 * SEG), 0)
    s = lax.broadcasted_iota(jnp.int32, (SEG, 3 * SEG), 1)
    pt = p0 + r
    ps = p0 - SEG + s
    rcol = lax.broadcasted_iota(jnp.int32, (SEG, 1), 0) + p0
    x = xc_ref[...]
    gate = m[2:3]
    for gi, w in enumerate(POOL_WINDOWS):
        cols = slice(gi * POOL_GROUP, (gi + 1) * POOL_GROUP)
        lo = jnp.maximum(pt - w // 2, 0)
        hi = jnp.minimum(pt + w - w // 2, lseq)
        band = jnp.where((ps >= lo) & (ps < hi), 1.0, 0.0).astype(BF16)
        tot = jnp.dot(band, hcat[:, cols], preferred_element_type=F32)
        cnt = (jnp.minimum(rcol + w - w // 2, lseq) - jnp.maximum(rcol - w // 2, 0)).astype(F32)
        resid = tot / cnt - hs[1][:, cols]
        y = jnp.dot(resid.astype(BF16), wp_ref[gi], preferred_element_type=F32) * sc_ref[:, cols]
        o_ref[:, cols] = x[:, cols] + gate[:, cols] * y


def _pool(x, modtab, g1, w_pool, scale):
    n, d = x.shape
    nseg = n // SEG

    def prev(g):
        return (jnp.maximum(g - 1, 0), 0)

    def nxt(g):
        return (jnp.minimum(g + 1, nseg - 1), 0)

    return pl.pallas_call(
        _pool_kernel,
        grid=(nseg,),
        in_specs=[pl.BlockSpec((SEG, d), prev), pl.BlockSpec((SEG, d), lambda g: (g, 0)), pl.BlockSpec((SEG, d), nxt),
                  pl.BlockSpec((1, N_MOD, d), lambda g: (g, 0, 0)),
                  pl.BlockSpec(g1.shape, lambda g: (0, 0)),
                  pl.BlockSpec(w_pool.shape, lambda g: (0, 0, 0)),
                  pl.BlockSpec(scale.shape, lambda g: (0, 0))],
        out_specs=pl.BlockSpec((SEG, d), lambda g: (g, 0)),
        out_shape=jax.ShapeDtypeStruct((n, d), F32),
        compiler_params=_cparams(("parallel",)),
        name="pool",
    )(x, x, x, modtab, g1, w_pool, scale)


def _swiglu_partial(h, wg, wu, wd):
    a = jax.nn.silu(jnp.dot(h, wg, preferred_element_type=F32)) * jnp.dot(h, wu, preferred_element_type=F32)
    return jnp.dot(a.astype(BF16), wd, preferred_element_type=F32)


def _ffn_kernel(x_ref, mod_ref, g_ref, wg_ref, wu_ref, wd_ref, o_ref, h_ref, acc_ref):
    f = pl.program_id(1)

    @pl.when(f == 0)
    def _():
        h_ref[...] = _norm_mod_tile(x_ref, mod_ref, g_ref, 3)
        acc_ref[...] = jnp.zeros_like(acc_ref)

    acc_ref[...] += _swiglu_partial(h_ref[...], wg_ref[...], wu_ref[...], wd_ref[...])

    @pl.when(f == pl.num_programs(1) - 1)
    def _():
        _gated_add(x_ref, mod_ref, acc_ref[...], 5, o_ref)


def _ffn(x, modtab, g2, w_gu, w_down, tm=512, tf=1408):
    n, d = x.shape
    nf = (w_gu.shape[1] // 2) // tf
    return pl.pallas_call(
        _ffn_kernel,
        grid=(n // tm, nf),
        in_specs=[pl.BlockSpec((tm, d), lambda g, f: (g, 0)),
                  pl.BlockSpec((tm // SEG, N_MOD, d), lambda g, f: (g, 0, 0)),
                  pl.BlockSpec(g2.shape, lambda g, f: (0, 0)),
                  pl.BlockSpec((d, tf), lambda g, f: (0, f)),
                  pl.BlockSpec((d, tf), lambda g, f: (0, nf + f)),
                  pl.BlockSpec((tf, d), lambda g, f: (f, 0))],
        out_specs=pl.BlockSpec((tm, d), lambda g, f: (g, 0)),
        out_shape=jax.ShapeDtypeStruct((n, d), F32),
        scratch_shapes=[pltpu.VMEM((tm, d), BF16), pltpu.VMEM((tm, d), F32)],
        compiler_params=_cparams(("parallel", "arbitrary")),
        name="ffn",
    )(x, modtab, g2, w_gu, w_gu, w_down)


MOE_TILE = 512
MOE_DMA_TILE = 512


def _router_kernel(x_ref, mod_ref, g_ref, w_ref, b_ref, h_ref, rw_ref, ri_ref):
    h = _norm_mod_tile(x_ref, mod_ref, g_ref, 3, out_dtype=F32)
    h_ref[...] = h
    logits = jnp.dot(h, w_ref[...], precision=lax.Precision.HIGHEST, preferred_element_type=F32) + b_ref[...]
    lane = lax.broadcasted_iota(jnp.int32, logits.shape, 1)
    neg = jnp.float32(-jnp.inf)
    lg = jnp.where(lane < N_EXPERTS, logits, neg)
    m1 = jnp.max(lg, axis=-1, keepdims=True)
    i1 = jnp.min(jnp.where(lg == m1, lane, HEAD_PAD), axis=-1, keepdims=True)
    lg2 = jnp.where(lane == i1, neg, lg)
    m2 = jnp.max(lg2, axis=-1, keepdims=True)
    i2 = jnp.min(jnp.where(lg2 == m2, lane, HEAD_PAD), axis=-1, keepdims=True)
    e2 = jnp.exp(m2 - m1)
    den = 1.0 + e2
    rw_ref[...] = jnp.where(lane == 0, 1.0 / den, jnp.where(lane == 1, e2 / den, 0.0))
    ri_ref[...] = jnp.where(lane == 0, i1, jnp.where(lane == 1, i2, 0))


def _router(x, modtab, g2, w_router, b_router, tm=512):
    n, d = x.shape
    wr = jnp.pad(w_router, ((0, 0), (0, HEAD_PAD - N_EXPERTS)))
    br = jnp.pad(b_router, (0, HEAD_PAD - N_EXPERTS)).reshape(1, HEAD_PAD)
    return pl.pallas_call(
        _router_kernel,
        grid=(n // tm,),
        in_specs=[pl.BlockSpec((tm, d), lambda g: (g, 0)),
                  pl.BlockSpec((tm // SEG, N_MOD, d), lambda g: (g, 0, 0)),
                  pl.BlockSpec(g2.shape, lambda g: (0, 0)),
                  pl.BlockSpec(wr.shape, lambda g: (0, 0)),
                  pl.BlockSpec(br.shape, lambda g: (0, 0))],
        out_specs=[pl.BlockSpec((tm, d), lambda g: (g, 0)),
                   pl.BlockSpec((tm, HEAD_PAD), lambda g: (g, 0)),
                   pl.BlockSpec((tm, HEAD_PAD), lambda g: (g, 0))],
        out_shape=[jax.ShapeDtypeStruct((n, d), F32),
                   jax.ShapeDtypeStruct((n, HEAD_PAD), F32),
                   jax.ShapeDtypeStruct((n, HEAD_PAD), jnp.int32)],
        compiler_params=_cparams(("parallel",)),
        name="router",
    )(x, modtab, g2, wr, br)


def _moe_plan(ri, n):
    e = ri[:, :TOP_K]
    mask = (e[:, :, None] == jnp.arange(N_EXPERTS, dtype=jnp.int32)).astype(jnp.int32).sum(axis=1)
    csum = jnp.cumsum(mask, axis=0)
    counts = csum[-1]
    padded = (counts + MOE_TILE - 1) // MOE_TILE * MOE_TILE
    ends = jnp.cumsum(padded)
    starts = ends - padded
    dest = starts[e] + jnp.take_along_axis(csum, e, axis=1) - 1
    n_tiles = (TOP_K * n) // MOE_TILE + N_EXPERTS
    tile_expert = jnp.searchsorted(ends // MOE_TILE, jnp.arange(n_tiles, dtype=jnp.int32), side="right")
    tile_expert = jnp.minimum(tile_expert, N_EXPERTS - 1).astype(jnp.int32)
    n_valid = (ends[-1] // MOE_TILE).astype(jnp.int32).reshape(1)
    return dest[:, 0].astype(jnp.int32), dest[:, 1].astype(jnp.int32), tile_expert, n_valid, n_tiles


def _row(ref, i):
    return ref.at[pl.ds(i, 1)]


def _scatter_rows_kernel(d0_ref, d1_ref, h_ref, init_hbm, o_hbm, sem):
    del init_hbm
    base = pl.program_id(0) * MOE_DMA_TILE

    def issue(r, carry):
        t = base + r
        pltpu.make_async_copy(_row(h_ref, r), _row(o_hbm, d0_ref[t]), sem.at[0]).start()
        pltpu.make_async_copy(_row(h_ref, r), _row(o_hbm, d1_ref[t]), sem.at[1]).start()
        return carry

    lax.fori_loop(0, MOE_DMA_TILE, issue, 0)
    pltpu.make_async_copy(h_ref, o_hbm.at[pl.ds(0, MOE_DMA_TILE)], sem.at[0]).wait()
    pltpu.make_async_copy(h_ref, o_hbm.at[pl.ds(0, MOE_DMA_TILE)], sem.at[1]).wait()


def _scatter_rows(d0, d1, h, n_rows):
    n, d = h.shape
    init = jnp.zeros((n_rows, d), F32)
    return pl.pallas_call(
        _scatter_rows_kernel,
        grid_spec=pltpu.PrefetchScalarGridSpec(
            num_scalar_prefetch=2,
            grid=(n // MOE_DMA_TILE,),
            in_specs=[pl.BlockSpec((MOE_DMA_TILE, d), lambda g, d0, d1: (g, 0)), pl.BlockSpec(memory_space=pl.ANY)],
            out_specs=pl.BlockSpec(memory_space=pl.ANY),
            scratch_shapes=[pltpu.SemaphoreType.DMA((2,))]),
        out_shape=jax.ShapeDtypeStruct((n_rows, d), F32),
        input_output_aliases={3: 0},
        compiler_params=_cparams(("arbitrary",)),
        name="moe_scatter",
    )(d0, d1, h, init)


def _moe_ffn_kernel(te_ref, nv_ref, xs_ref, wg_ref, wu_ref, wd_ref, o_ref, acc_ref):
    del te_ref
    i = pl.program_id(0)
    f = pl.program_id(1)
    last = pl.num_programs(1) - 1

    @pl.when(i < nv_ref[0])
    def _():
        part = _swiglu_partial(xs_ref[...].astype(BF16), wg_ref[0], wu_ref[0], wd_ref[0])

        @pl.when(f == 0)
        def _():
            acc_ref[...] = part

        @pl.when(f > 0)
        def _():
            acc_ref[...] += part

        @pl.when(f == last)
        def _():
            o_ref[...] = acc_ref[...]

    @pl.when((i >= nv_ref[0]) & (f == last))
    def _():
        o_ref[...] = jnp.zeros_like(o_ref)


def _moe_ffn(xs, tile_expert, n_valid, w_gu, w_down, tf=1408):
    rows, d = xs.shape
    nf = (w_gu.shape[2] // 2) // tf
    fsel = lambda i, f, nv: jnp.where(i < nv[0], f, nf - 1)
    return pl.pallas_call(
        _moe_ffn_kernel,
        grid_spec=pltpu.PrefetchScalarGridSpec(
            num_scalar_prefetch=2,
            grid=(rows // MOE_TILE, nf),
            in_specs=[pl.BlockSpec((MOE_TILE, d), lambda i, f, te, nv: (i, 0)),
                      pl.BlockSpec((1, d, tf), lambda i, f, te, nv: (te[i], 0, fsel(i, f, nv))),
                      pl.BlockSpec((1, d, tf), lambda i, f, te, nv: (te[i], 0, nf + fsel(i, f, nv))),
                      pl.BlockSpec((1, tf, d), lambda i, f, te, nv: (te[i], fsel(i, f, nv), 0))],
            out_specs=pl.BlockSpec((MOE_TILE, d), lambda i, f, te, nv: (i, 0)),
            scratch_shapes=[pltpu.VMEM((MOE_TILE, d), F32)]),
        out_shape=jax.ShapeDtypeStruct((rows, d), F32),
        compiler_params=_cparams(("arbitrary", "arbitrary")),
        name="moe_ffn",
    )(tile_expert, n_valid, xs, w_gu, w_gu, w_down)


def _combine_kernel(d0_ref, d1_ref, x_ref, mod_ref, rw_ref, y_hbm, o_ref, ya_ref, yb_ref, sem):
    base = pl.program_id(0) * MOE_DMA_TILE

    def issue(r, carry):
        t = base + r
        pltpu.make_async_copy(_row(y_hbm, d0_ref[t]), _row(ya_ref, r), sem.at[0]).start()
        pltpu.make_async_copy(_row(y_hbm, d1_ref[t]), _row(yb_ref, r), sem.at[1]).start()
        return carry

    lax.fori_loop(0, MOE_DMA_TILE, issue, 0)
    rw = rw_ref[...]
    lane = lax.broadcasted_iota(jnp.int32, rw.shape, 1)
    w1 = jnp.sum(jnp.where(lane == 0, rw, 0.0), axis=-1, keepdims=True)
    w2 = jnp.sum(jnp.where(lane == 1, rw, 0.0), axis=-1, keepdims=True)
    pltpu.make_async_copy(y_hbm.at[pl.ds(0, MOE_DMA_TILE)], ya_ref, sem.at[0]).wait()
    pltpu.make_async_copy(y_hbm.at[pl.ds(0, MOE_DMA_TILE)], yb_ref, sem.at[1]).wait()
    _gated_add(x_ref, mod_ref, w1 * ya_ref[...] + w2 * yb_ref[...], 5, o_ref)


def _combine(d0, d1, x, modtab, rw, y):
    n, d = x.shape
    tm = MOE_DMA_TILE
    return pl.pallas_call(
        _combine_kernel,
        grid_spec=pltpu.PrefetchScalarGridSpec(
            num_scalar_prefetch=2,
            grid=(n // tm,),
            in_specs=[pl.BlockSpec((tm, d), lambda g, d0, d1: (g, 0)),
                      pl.BlockSpec((tm // SEG, N_MOD, d), lambda g, d0, d1: (g, 0, 0)),
                      pl.BlockSpec((tm, HEAD_PAD), lambda g, d0, d1: (g, 0)),
                      pl.BlockSpec(memory_space=pl.ANY)],
            out_specs=pl.BlockSpec((tm, d), lambda g, d0, d1: (g, 0)),
            scratch_shapes=[pltpu.VMEM((tm, d), F32), pltpu.VMEM((tm, d), F32), pltpu.SemaphoreType.DMA((2,))]),
        out_shape=jax.ShapeDtypeStruct((n, d), F32),
        compiler_params=_cparams(("arbitrary",)),
        name="moe_combine",
    )(d0, d1, x, modtab, rw, y)


def _moe(x, modtab, g2, w_router, b_router, w_gu, w_down):
    n = x.shape[0]
    h, rw, ri = _router(x, modtab, g2, w_router, b_router)
    d0, d1, tile_expert, n_valid, n_tiles = _moe_plan(ri, n)
    xs = _scatter_rows(d0, d1, h, n_tiles * MOE_TILE)
    ys = _moe_ffn(xs, tile_expert, n_valid, w_gu, w_down)
    return _combine(d0, d1, x, modtab, rw, ys)


def _final_kernel(x_ref, g_ref, o_ref):
    o_ref[0] = _rms(x_ref[0], g_ref[...])


def _final_norm(x3, g, tm=512):
    b, _, d = x3.shape
    return pl.pallas_call(
        _final_kernel,
        grid=(b, SEQ // tm),
        in_specs=[pl.BlockSpec((1, tm, d), lambda bi, t: (bi, t, 0)), pl.BlockSpec(g.shape, lambda bi, t: (0, 0))],
        out_specs=pl.BlockSpec((1, tm, d), lambda bi, t: (bi, t, 0)),
        out_shape=jax.ShapeDtypeStruct((b, SEQ, d), F32),
        compiler_params=_cparams(("parallel", "parallel")),
        name="final_norm",
    )(x3, g)


def kernel(x, c, ctx, c_ctx, norm1_g, norm2_g, ada_w, ada_b, final_norm_g, mla_w_dq, mla_q_norm, mla_w_uq, mla_w_dkv, mla_kv_norm, mla_w_ukv, mla_w_o, s5_a_re, s5_a_im, s5_log_dt, s5_b_re, s5_b_im, s5_c_re, s5_c_im, s5_d, s5_w_glu, pool_w, pool_scale, ffn_w_gu, ffn_w_down, moe_w_router, moe_b_router, moe_w_gu, moe_w_down):
    b, l, d = x.shape
    n = b * T_TOK
    xs = jnp.concatenate([x, ctx], axis=1).reshape(n, d)

    rows = -(-(b + 1) // 8) * 8
    cc = jnp.concatenate([c, c_ctx[None], jnp.zeros((rows - b - 1, d), F32)], axis=0)
    mods = _ada(cc, ada_w, ada_b)
    seg_src = np.array([bi if sj < LAT_SEGS else b for bi in range(b) for sj in range(SEGS_PER_BATCH)], np.int32)

    rope = _rope_tables()
    for i in range(DEPTH):
        last = i == DEPTH - 1
        modtab = mods[i].reshape(rows, N_MOD, d)[seg_src]
        g1 = norm1_g[i].reshape(1, d)
        g2 = norm2_g[i].reshape(1, d)
        kind = i % N_MIXERS
        j = i // N_MIXERS
        if kind == 0:
            pw = _mla_weights(mla_w_dq[j], mla_q_norm[j], mla_w_uq[j], mla_w_dkv[j], mla_kv_norm[j], mla_w_ukv[j])
            q, k, v = _mla_proj(xs, modtab, g1, pw, rope)
            o = _attention(q.reshape(b, T_TOK, -1), k.reshape(b, T_TOK, -1), v.reshape(b, T_TOK, -1))
            xs = _proj_res(xs, modtab, o.reshape(n, -1), mla_w_o[j].astype(BF16))
        elif kind == 1:
            h = _normmod(xs, modtab, g1)
            nch = T_TOK // S5_CHUNK
            u = h.reshape(b, nch, S5_CHUNK, S5_GROUPS, S5_GROUP).transpose(3, 1, 0, 2, 4).reshape(S5_GROUPS, nch * b, _S5_W)
            sp = _s5_params(s5_a_re[j], s5_a_im[j], s5_log_dt[j], s5_b_re[j], s5_b_im[j], s5_c_re[j], s5_c_im[j], s5_d[j])
            y = _s5_core(u, sp, b)
            y = y.reshape(S5_GROUPS, nch, b, S5_CHUNK, S5_GROUP).transpose(2, 1, 3, 0, 4).reshape(n, d)
            xs = _glu_res(xs, modtab, y, s5_w_glu[j].astype(BF16))
        else:
            xs = _pool(xs, modtab, g1, pool_w[j].astype(BF16), pool_scale[j].reshape(1, d))
        kk = i // 2
        if i % 2 == 0:
            xs = _ffn(xs, modtab, g2, ffn_w_gu[kk].astype(BF16), ffn_w_down[kk].astype(BF16))
        else:
            xs = _moe(xs, modtab, g2, moe_w_router[kk], moe_b_router[kk], moe_w_gu[kk].astype(BF16), moe_w_down[kk].astype(BF16))
    return _final_norm(xs.reshape(b, T_TOK, d), final_norm_g.reshape(1, d))
```

```python
import functools
import math

import jax
import jax.numpy as jnp
import numpy as np
from jax import lax
from jax.experimental import pallas as pl
from jax.experimental.pallas import tpu as pltpu

F32 = jnp.float32
BF16 = jnp.bfloat16

D_MODEL = 1024
SEQ = 4096
DEPTH = 4
GRID_W = 64
CTX_LEN = 256
N_MIXERS = 3
NORM_EPS = 1e-6

MLA_HEADS = 16
MLA_Q_LORA = 384
MLA_KV_LORA = 256
MLA_NOPE = 64
MLA_ROPE = 32
MLA_V = 64
MLA_QK = MLA_NOPE + MLA_ROPE
ROPE_AXIS_FREQS = MLA_ROPE // 4
ROPE_THETA = 10000.0
HEAD_PAD = 128

S5_GROUP = 16
S5_GROUPS = D_MODEL // S5_GROUP
S5_STATE = 64
S5_MAX_RE = -1e-4
S5_CHUNK = 16

POOL_WINDOWS = (2, 4, 8, 16)
POOL_GROUP = D_MODEL // len(POOL_WINDOWS)

FFN_DIM = 2816
N_EXPERTS = 8
TOP_K = 2

T_TOK = SEQ + CTX_LEN
SEG = CTX_LEN
SEGS_PER_BATCH = T_TOK // SEG
LAT_SEGS = SEQ // SEG
N_MOD = 6

VMEM_LIMIT = 56 * 1024 * 1024


def _cparams(sem, vmem=VMEM_LIMIT):
    return pltpu.CompilerParams(dimension_semantics=sem, vmem_limit_bytes=vmem)


def _rms(x, g):
    return x * lax.rsqrt(jnp.mean(x * x, axis=-1, keepdims=True) + NORM_EPS) * g


def _norm_mod(x, g, shift, scale):
    return _rms(x, g) * (1.0 + scale) + shift


def _norm_mod_tile(x_ref, mod_ref, g_ref, shift_idx, out_dtype=BF16):
    parts = []
    for s in range(x_ref.shape[0] // SEG):
        m = mod_ref[s]
        x = x_ref[s * SEG:(s + 1) * SEG, :]
        parts.append(_norm_mod(x, g_ref[...], m[shift_idx:shift_idx + 1], m[shift_idx + 1:shift_idx + 2]).astype(out_dtype))
    return parts[0] if len(parts) == 1 else jnp.concatenate(parts, axis=0)


def _ada_kernel(c_ref, w_ref, b_ref, o_ref):
    s = jax.nn.silu(c_ref[...])
    o_ref[0] = jnp.dot(s, w_ref[0], precision=lax.Precision.HIGHEST, preferred_element_type=F32) + b_ref[0]


def _ada(cc, ada_w, ada_b):
    depth, d, n6 = ada_w.shape
    rows = cc.shape[0]
    tn = 1024
    return pl.pallas_call(
        _ada_kernel,
        grid=(depth, n6 // tn),
        in_specs=[pl.BlockSpec((rows, d), lambda i, j: (0, 0)),
                  pl.BlockSpec((1, d, tn), lambda i, j: (i, 0, j)),
                  pl.BlockSpec((1, 1, tn), lambda i, j: (i, 0, j))],
        out_specs=pl.BlockSpec((1, rows, tn), lambda i, j: (i, 0, j)),
        out_shape=jax.ShapeDtypeStruct((depth, rows, n6), F32),
        compiler_params=_cparams(("arbitrary", "arbitrary")),
        name="ada",
    )(cc, ada_w, ada_b.reshape(depth, 1, n6))


_W1_COLS = MLA_Q_LORA + MLA_KV_LORA + 2 * HEAD_PAD
_QW = MLA_HEADS * HEAD_PAD


def _mla_proj_kernel(x_ref, mod_ref, g1_ref, w1_ref, qn_ref, wq_ref, kvn_ref, wk_ref, we_ref, wv_ref,
                     cq_ref, sq_ref, ck_ref, sk_ref, q_ref, k_ref, v_ref):
    h = _norm_mod_tile(x_ref, mod_ref, g1_ref, 0)
    d = jnp.dot(h, w1_ref[...], preferred_element_type=F32)
    dq = d[:, :MLA_Q_LORA]
    ckv = d[:, MLA_Q_LORA:MLA_Q_LORA + MLA_KV_LORA]
    kr = d[:, MLA_Q_LORA + MLA_KV_LORA:MLA_Q_LORA + MLA_KV_LORA + HEAD_PAD]
    kr_sw = d[:, MLA_Q_LORA + MLA_KV_LORA + HEAD_PAD:]
    qn = _rms(dq, qn_ref[...]).astype(BF16)
    qq = jnp.dot(qn, wq_ref[...], preferred_element_type=F32)
    cq = cq_ref[...]
    sq = sq_ref[...]
    for hd in range(MLA_HEADS):
        lo = hd * HEAD_PAD
        q_ref[:, lo:lo + HEAD_PAD] = (qq[:, lo:lo + HEAD_PAD] * cq + qq[:, _QW + lo:_QW + lo + HEAD_PAD] * sq).astype(BF16)
    c = _rms(ckv, kvn_ref[...]).astype(BF16)
    kr_roped = (kr * ck_ref[...] + kr_sw * sk_ref[...]).astype(BF16)
    k = jnp.dot(c, wk_ref[...], preferred_element_type=F32) + jnp.dot(kr_roped, we_ref[...], preferred_element_type=F32)
    k_ref[...] = k.astype(BF16)
    v = jnp.dot(c, wv_ref[...], preferred_element_type=F32)
    lane = lax.broadcasted_iota(jnp.int32, v.shape, 1)
    v_ref[...] = jnp.where(lane % HEAD_PAD == MLA_V, 1.0, v).astype(BF16)


def _mla_proj(x, modtab, g1, pw, rope):
    n, d = x.shape
    tm = SEG
    full = lambda a: pl.BlockSpec(a.shape, lambda g: (0,) * a.ndim)
    pos = lambda g: (g % SEGS_PER_BATCH, 0)
    cq, sq, ck, sk = rope
    return pl.pallas_call(
        _mla_proj_kernel,
        grid=(n // tm,),
        in_specs=[pl.BlockSpec((tm, d), lambda g: (g, 0)),
                  pl.BlockSpec((tm // SEG, N_MOD, d), lambda g: (g, 0, 0)),
                  full(g1), full(pw["w1"]), full(pw["qn"]), full(pw["wq"]), full(pw["kvn"]),
                  full(pw["wk"]), full(pw["we"]), full(pw["wv"]),
                  pl.BlockSpec((tm, HEAD_PAD), pos), pl.BlockSpec((tm, HEAD_PAD), pos),
                  pl.BlockSpec((tm, HEAD_PAD), pos), pl.BlockSpec((tm, HEAD_PAD), pos)],
        out_specs=[pl.BlockSpec((tm, _QW), lambda g: (g, 0)),
                   pl.BlockSpec((tm, _QW), lambda g: (g, 0)),
                   pl.BlockSpec((tm, _QW), lambda g: (g, 0))],
        out_shape=[jax.ShapeDtypeStruct((n, _QW), BF16)] * 3,
        compiler_params=_cparams(("parallel",)),
        name="mla_proj",
    )(x, modtab, g1, pw["w1"], pw["qn"], pw["wq"], pw["kvn"], pw["wk"], pw["we"], pw["wv"], cq, sq, ck, sk)


def _mla_weights(w_dq, q_norm, w_uq, w_dkv, kv_norm, w_ukv):
    d = w_dq.shape[0]
    swap = np.arange(MLA_ROPE) ^ ROPE_AXIS_FREQS
    w_kr = w_dkv[:, MLA_KV_LORA:]
    padr = lambda w: jnp.pad(w, ((0, 0), (0, HEAD_PAD - MLA_ROPE)))
    w1 = jnp.concatenate([w_dq, w_dkv[:, :MLA_KV_LORA], padr(w_kr), padr(w_kr[:, swap])], axis=1)
    uq = w_uq.reshape(MLA_Q_LORA, MLA_HEADS, MLA_QK)
    zpad = jnp.zeros((MLA_Q_LORA, MLA_HEADS, HEAD_PAD - MLA_QK), w_uq.dtype)
    wq_main = jnp.concatenate([uq, zpad], axis=-1).reshape(MLA_Q_LORA, _QW)
    uq_sw = jnp.concatenate([jnp.zeros((MLA_Q_LORA, MLA_HEADS, MLA_NOPE), w_uq.dtype),
                             uq[:, :, MLA_NOPE:][:, :, swap], zpad], axis=-1).reshape(MLA_Q_LORA, _QW)
    wq = jnp.concatenate([wq_main, uq_sw], axis=1)
    ukv = w_ukv.reshape(MLA_KV_LORA, MLA_HEADS, MLA_NOPE + MLA_V)
    wk = jnp.concatenate([ukv[:, :, :MLA_NOPE],
                          jnp.zeros((MLA_KV_LORA, MLA_HEADS, HEAD_PAD - MLA_NOPE), w_ukv.dtype)], axis=-1)
    wk = wk.reshape(MLA_KV_LORA, _QW)
    wv = jnp.concatenate([ukv[:, :, MLA_NOPE:],
                          jnp.zeros((MLA_KV_LORA, MLA_HEADS, HEAD_PAD - MLA_V), w_ukv.dtype)], axis=-1)
    wv = wv.reshape(MLA_KV_LORA, _QW)
    e = np.zeros((HEAD_PAD, MLA_HEADS, HEAD_PAD), np.float32)
    for r in range(MLA_ROPE):
        e[r, :, MLA_NOPE + r] = 1.0
    we = jnp.asarray(e.reshape(HEAD_PAD, _QW))
    return {"w1": w1.astype(BF16), "qn": q_norm.reshape(1, -1), "wq": wq.astype(BF16),
            "kvn": kv_norm.reshape(1, -1), "wk": wk.astype(BF16), "we": we.astype(BF16), "wv": wv.astype(BF16)}


def _rope_tables():
    rows = SEQ // GRID_W
    row = jnp.repeat(jnp.arange(rows, dtype=F32), GRID_W)
    col = jnp.tile(jnp.arange(GRID_W, dtype=F32), rows)
    inv_freq = 1.0 / (ROPE_THETA ** (jnp.arange(ROPE_AXIS_FREQS, dtype=F32) / ROPE_AXIS_FREQS))
    ang = jnp.stack([row[:, None] * inv_freq, col[:, None] * inv_freq], axis=1)
    ang = jnp.concatenate([ang, jnp.zeros((CTX_LEN, 2, ROPE_AXIS_FREQS), F32)], axis=0)
    cos = jnp.cos(ang)
    sin = jnp.sin(ang)
    c32 = jnp.stack([cos, cos], axis=2).reshape(T_TOK, MLA_ROPE)
    s32 = jnp.stack([-sin, sin], axis=2).reshape(T_TOK, MLA_ROPE)
    scale = MLA_QK ** -0.5 * math.log2(math.e)
    zq = jnp.zeros((T_TOK, HEAD_PAD - MLA_QK), F32)
    cq = jnp.concatenate([jnp.full((T_TOK, MLA_NOPE), scale, F32), c32 * scale, zq], axis=1)
    sq = jnp.concatenate([jnp.zeros((T_TOK, MLA_NOPE), F32), s32 * scale, zq], axis=1)
    zk = jnp.zeros((T_TOK, HEAD_PAD - MLA_ROPE), F32)
    ck = jnp.concatenate([c32, zk], axis=1)
    sk = jnp.concatenate([s32, zk], axis=1)
    return cq, sq, ck, sk


ATTN_TQ = 512
ATTN_KEY_CHUNK = 1024
_LAT_CHUNKS = tuple((lo, ATTN_KEY_CHUNK) for lo in range(0, SEQ, ATTN_KEY_CHUNK))
_CTX_CHUNK = ((SEQ, CTX_LEN),)


def _attend_pair(q, k_ref, v_ref, chunks):
    outs = []
    for j in range(2):
        cols = slice(j * HEAD_PAD, (j + 1) * HEAD_PAD)
        qj = q[:, cols]
        m = None
        acc = None
        for lo, size in chunks:
            s = lax.dot_general(qj, k_ref[0, lo:lo + size, cols], (((1,), (1,)), ((), ())), preferred_element_type=F32)
            mc = jnp.max(s, axis=-1, keepdims=True)
            m_new = mc if m is None else jnp.maximum(m, mc)
            pv = jnp.dot(jnp.exp2(s - m_new).astype(BF16), v_ref[0, lo:lo + size, cols], preferred_element_type=F32)
            acc = pv if m is None else acc * jnp.exp2(m - m_new) + pv
            m = m_new
        outs.append(acc / acc[:, MLA_V:MLA_V + 1])
    lane = lax.broadcasted_iota(jnp.int32, outs[0].shape, 1)
    return jnp.where(lane < MLA_V, outs[0], pltpu.roll(outs[1], MLA_V, 1))


def _attn_kernel(q_ref, k_ref, v_ref, o_ref, *, lat_tiles):
    qi = pl.program_id(2)

    @pl.when(qi < lat_tiles)
    def _():
        o_ref[0] = _attend_pair(q_ref[0], k_ref, v_ref, _LAT_CHUNKS + _CTX_CHUNK).astype(BF16)

    @pl.when(qi >= lat_tiles)
    def _():
        o_ref[0, :CTX_LEN, :] = _attend_pair(q_ref[0, :CTX_LEN, :], k_ref, v_ref, _CTX_CHUNK).astype(BF16)


def _attention(q, k, v):
    b = q.shape[0]
    tq = ATTN_TQ
    lat_tiles = SEQ // tq
    return pl.pallas_call(
        functools.partial(_attn_kernel, lat_tiles=lat_tiles),
        grid=(b, MLA_HEADS // 2, lat_tiles + 1),
        in_specs=[pl.BlockSpec((1, tq, 2 * HEAD_PAD), lambda bi, hp, qi: (bi, qi, hp)),
                  pl.BlockSpec((1, T_TOK, 2 * HEAD_PAD), lambda bi, hp, qi: (bi, 0, hp)),
                  pl.BlockSpec((1, T_TOK, 2 * HEAD_PAD), lambda bi, hp, qi: (bi, 0, hp))],
        out_specs=pl.BlockSpec((1, tq, 2 * MLA_V), lambda bi, hp, qi: (bi, qi, hp)),
        out_shape=jax.ShapeDtypeStruct((b, T_TOK, MLA_HEADS * MLA_V), BF16),
        compiler_params=_cparams(("parallel", "parallel", "arbitrary")),
        name="attention",
    )(q, k, v)


def _gated_add(x_ref, mod_ref, y, gate_idx, o_ref):
    for s in range(x_ref.shape[0] // SEG):
        rows = slice(s * SEG, (s + 1) * SEG)
        g = mod_ref[s][gate_idx:gate_idx + 1]
        o_ref[rows, :] = x_ref[rows, :] + g * y[rows, :]


def _proj_res_kernel(x_ref, mod_ref, y_ref, w_ref, o_ref):
    y = jnp.dot(y_ref[...], w_ref[...], preferred_element_type=F32)
    _gated_add(x_ref, mod_ref, y, 2, o_ref)


def _proj_res(x, modtab, y, w, tm=512):
    n, d = x.shape
    return pl.pallas_call(
        _proj_res_kernel,
        grid=(n // tm,),
        in_specs=[pl.BlockSpec((tm, d), lambda g: (g, 0)),
                  pl.BlockSpec((tm // SEG, N_MOD, d), lambda g: (g, 0, 0)),
                  pl.BlockSpec((tm, y.shape[1]), lambda g: (g, 0)),
                  pl.BlockSpec(w.shape, lambda g: (0, 0))],
        out_specs=pl.BlockSpec((tm, d), lambda g: (g, 0)),
        out_shape=jax.ShapeDtypeStruct((n, d), F32),
        compiler_params=_cparams(("parallel",)),
        name="proj_res",
    )(x, modtab, y, w)


def _glu_res_kernel(x_ref, mod_ref, y_ref, w_ref, o_ref):
    d = x_ref.shape[1]
    g = jax.nn.gelu(y_ref[...]).astype(BF16)
    z = jnp.dot(g, w_ref[...], preferred_element_type=F32)
    y = z[:, :d] * jax.nn.sigmoid(z[:, d:])
    _gated_add(x_ref, mod_ref, y, 2, o_ref)


def _glu_res(x, modtab, y, w, tm=512):
    n, d = x.shape
    return pl.pallas_call(
        _glu_res_kernel,
        grid=(n // tm,),
        in_specs=[pl.BlockSpec((tm, d), lambda g: (g, 0)),
                  pl.BlockSpec((tm // SEG, N_MOD, d), lambda g: (g, 0, 0)),
                  pl.BlockSpec((tm, d), lambda g: (g, 0)),
                  pl.BlockSpec(w.shape, lambda g: (0, 0))],
        out_specs=pl.BlockSpec((tm, d), lambda g: (g, 0)),
        out_shape=jax.ShapeDtypeStruct((n, d), F32),
        compiler_params=_cparams(("parallel",)),
        name="glu_res",
    )(x, modtab, y, w)


def _normmod_kernel(x_ref, mod_ref, g_ref, o_ref):
    o_ref[...] = _norm_mod_tile(x_ref, mod_ref, g_ref, 0)


def _normmod(x, modtab, g1, tm=512):
    n, d = x.shape
    return pl.pallas_call(
        _normmod_kernel,
        grid=(n // tm,),
        in_specs=[pl.BlockSpec((tm, d), lambda g: (g, 0)),
                  pl.BlockSpec((tm // SEG, N_MOD, d), lambda g: (g, 0, 0)),
                  pl.BlockSpec(g1.shape, lambda g: (0, 0))],
        out_specs=pl.BlockSpec((tm, d), lambda g: (g, 0)),
        out_shape=jax.ShapeDtypeStruct((n, d), BF16),
        compiler_params=_cparams(("parallel",)),
        name="normmod",
    )(x, modtab, g1)


_S5_LAT_CHUNKS = SEQ // S5_CHUNK
_S5_CHUNKS = T_TOK // S5_CHUNK
_S5_W = S5_CHUNK * S5_GROUP


def _s5_kernel(u_ref, m_ref, win_ref, wof_ref, wor_ref, ar_ref, ai_ref, y_ref, z_ref, sf_ref, sr_ref, *, nb):
    u = u_ref[0]
    z_ref[...] = jnp.dot(u, win_ref[0], preferred_element_type=F32)
    half = _S5_W // 2
    ar = ar_ref[0]
    ai = ai_ref[0]
    lane = lax.broadcasted_iota(jnp.int32, (nb, _S5_W), 1)
    is_fwd = (lane % half) < S5_STATE

    def step(i, s):
        cf = jnp.where(i < _S5_CHUNKS - _S5_LAT_CHUNKS, i + _S5_LAT_CHUNKS, i - (_S5_CHUNKS - _S5_LAT_CHUNKS))
        cr = _S5_CHUNKS - 1 - i
        rf = pl.multiple_of(cf * nb, nb)
        rr = pl.multiple_of(cr * nb, nb)
        sf_ref[pl.ds(rf, nb), :] = s
        sr_ref[pl.ds(rr, nb), :] = s
        z = jnp.where(is_fwd, z_ref[pl.ds(rf, nb), :], z_ref[pl.ds(rr, nb), :])
        re = s[:, :half]
        im = s[:, half:]
        return jnp.concatenate([ar * re - ai * im + z[:, :half], ar * im + ai * re + z[:, half:]], axis=1)

    lax.fori_loop(0, _S5_CHUNKS, step, jnp.zeros((nb, _S5_W), F32))
    y = jnp.dot(u, m_ref[0], preferred_element_type=F32)
    y = y + jnp.dot(sf_ref[...].astype(BF16), wof_ref[0], preferred_element_type=F32)
    y = y + jnp.dot(sr_ref[...].astype(BF16), wor_ref[0], preferred_element_type=F32)
    y_ref[0] = y


def _s5_core(u, sp, nb):
    g, rows, w = u.shape
    blk = lambda a: pl.BlockSpec((1,) + a.shape[1:], lambda i: (i,) + (0,) * (a.ndim - 1))
    return pl.pallas_call(
        functools.partial(_s5_kernel, nb=nb),
        grid=(g,),
        in_specs=[blk(u), blk(sp["m"]), blk(sp["win"]), blk(sp["wof"]), blk(sp["wor"]), blk(sp["ar"]), blk(sp["ai"])],
        out_specs=pl.BlockSpec((1, rows, w), lambda i: (i, 0, 0)),
        out_shape=jax.ShapeDtypeStruct((g, rows, w), F32),
        scratch_shapes=[pltpu.VMEM((rows, w), F32), pltpu.VMEM((rows, w), F32), pltpu.VMEM((rows, w), F32)],
        compiler_params=_cparams(("parallel",)),
        name="s5_core",
    )(u, sp["m"], sp["win"], sp["wof"], sp["wor"], sp["ar"], sp["ai"])


def _s5_params(a_re, a_im, log_dt, b_re, b_im, c_re, c_im, d):
    c64 = jnp.complex64
    lam = lax.complex(jnp.minimum(a_re, S5_MAX_RE), a_im)
    lam_dt = lam * jnp.exp(log_dt)[..., None]
    lam_bar = jnp.exp(lam_dt)
    b_bar = ((lam_bar - 1.0) / lam)[..., None] * lax.complex(b_re, b_im)
    c_mat = lax.complex(c_re, c_im)
    taus = jnp.arange(S5_CHUNK + 1, dtype=F32)
    pw = jnp.exp(lam_dt[None] * taus[:, None, None, None].astype(c64))
    hi = lax.Precision.HIGHEST
    kern = jnp.real(jnp.einsum('dgnp,tdgp,dgpm->dgtnm', c_mat, pw[:S5_CHUNK], b_bar, precision=hi))
    eye = jnp.eye(S5_GROUP, dtype=F32)
    k0 = kern[0, :, 0] + kern[1, :, 0] + d.reshape(S5_GROUPS, S5_GROUP)[:, :, None] * eye
    lags = jnp.concatenate([kern[1, :, :0:-1], k0[:, None], kern[0, :, 1:]], axis=1)
    idx = (np.arange(S5_CHUNK)[None, :] - np.arange(S5_CHUNK)[:, None]) + S5_CHUNK - 1
    m = lags[:, idx]
    m = m.transpose(0, 1, 4, 2, 3).reshape(S5_GROUPS, _S5_W, _S5_W)
    wf = pw[S5_CHUNK - 1::-1][:S5_CHUNK, 0][..., None] * b_bar[0][None]
    wr = pw[:S5_CHUNK, 1][..., None] * b_bar[1][None]
    to_rows = lambda w: w.transpose(1, 0, 3, 2).reshape(S5_GROUPS, _S5_W, S5_STATE)
    wf, wr = to_rows(wf), to_rows(wr)
    win = jnp.concatenate([jnp.real(wf), jnp.real(wr), jnp.imag(wf), jnp.imag(wr)], axis=-1)
    of = c_mat[0][None] * pw[1:, 0][:, :, None, :]
    orv = c_mat[1][None] * pw[S5_CHUNK:0:-1, 1][:, :, None, :]
    to_cols = lambda w: w.transpose(1, 3, 0, 2).reshape(S5_GROUPS, S5_STATE, _S5_W)
    of, orv = to_cols(of), to_cols(orv)
    zeros = jnp.zeros_like(jnp.real(of))
    wof = jnp.concatenate([jnp.real(of), zeros, -jnp.imag(of), zeros], axis=1)
    wor = jnp.concatenate([zeros, jnp.real(orv), zeros, -jnp.imag(orv)], axis=1)
    a16 = pw[S5_CHUNK]
    ar = jnp.concatenate([jnp.real(a16[0]), jnp.real(a16[1])], axis=-1)[:, None, :]
    ai = jnp.concatenate([jnp.imag(a16[0]), jnp.imag(a16[1])], axis=-1)[:, None, :]
    return {"m": m.astype(BF16), "win": win.astype(BF16), "wof": wof.astype(BF16), "wor": wor.astype(BF16),
            "ar": ar, "ai": ai}


def _pool_kernel(xp_ref, xc_ref, xn_ref, mod_ref, g_ref, wp_ref, sc_ref, o_ref):
    seg = pl.program_id(0) % SEGS_PER_BATCH
    is_ctx = seg >= LAT_SEGS
    p0 = jnp.where(is_ctx, 0, seg * SEG)
    lseq = jnp.where(is_ctx, CTX_LEN, SEQ)
    m = mod_ref[0]
    g = g_ref[...]
    hs = [_norm_mod(r[...], g, m[0:1], m[1:2]) for r in (xp_ref, xc_ref, xn_ref)]
    hcat = jnp.concatenate([h.astype(BF16) for h in hs], axis=0)
    r = lax.broadcasted_iota(jnp.int32, (SEG, 3 * SEG), 0)
    s = lax.broadcasted_iota(jnp.int32, (SEG, 3 * SEG), 1)
    pt = p0 + r
    ps = p0 - SEG + s
    rcol = lax.broadcasted_iota(jnp.int32, (SEG, 1), 0) + p0
    x = xc_ref[...]
    gate = m[2:3]
    for gi, w in enumerate(POOL_WINDOWS):
        cols = slice(gi * POOL_GROUP, (gi + 1) * POOL_GROUP)
        lo = jnp.maximum(pt - w // 2, 0)
        hi = jnp.minimum(pt + w - w // 2, lseq)
        band = jnp.where((ps >= lo) & (ps < hi), 1.0, 0.0).astype(BF16)
        tot = jnp.dot(band, hcat[:, cols], preferred_element_type=F32)
        cnt = (jnp.minimum(rcol + w - w // 2, lseq) - jnp.maximum(rcol - w // 2, 0)).astype(F32)
        resid = tot / cnt - hs[1][:, cols]
        y = jnp.dot(resid.astype(BF16), wp_ref[gi], preferred_element_type=F32) * sc_ref[:, cols]
        o_ref[:, cols] = x[:, cols] + gate[:, cols] * y


def _pool(x, modtab, g1, w_pool, scale):
    n, d = x.shape
    nseg = n // SEG

    def prev(g):
        return (jnp.maximum(g - 1, 0), 0)

    def nxt(g):
        return (jnp.minimum(g + 1, nseg - 1), 0)

    return pl.pallas_call(
        _pool_kernel,
        grid=(nseg,),
        in_specs=[pl.BlockSpec((SEG, d), prev), pl.BlockSpec((SEG, d), lambda g: (g, 0)), pl.BlockSpec((SEG, d), nxt),
                  pl.BlockSpec((1, N_MOD, d), lambda g: (g, 0, 0)),
                  pl.BlockSpec(g1.shape, lambda g: (0, 0)),
                  pl.BlockSpec(w_pool.shape, lambda g: (0, 0, 0)),
                  pl.BlockSpec(scale.shape, lambda g: (0, 0))],
        out_specs=pl.BlockSpec((SEG, d), lambda g: (g, 0)),
        out_shape=jax.ShapeDtypeStruct((n, d), F32),
        compiler_params=_cparams(("parallel",)),
        name="pool",
    )(x, x, x, modtab, g1, w_pool, scale)


def _swiglu_partial(h, wg, wu, wd):
    a = jax.nn.silu(jnp.dot(h, wg, preferred_element_type=F32)) * jnp.dot(h, wu, preferred_element_type=F32)
    return jnp.dot(a.astype(BF16), wd, preferred_element_type=F32)


def _ffn_kernel(x_ref, mod_ref, g_ref, wg_ref, wu_ref, wd_ref, o_ref):
    h = _norm_mod_tile(x_ref, mod_ref, g_ref, 3)
    _gated_add(x_ref, mod_ref, _swiglu_partial(h, wg_ref[...], wu_ref[...], wd_ref[...]), 5, o_ref)


_RESIDENT = pl.Buffered(1)


def _ffn(x, modtab, g2, w_gu, w_down, tm=512):
    n, d = x.shape
    f = w_down.shape[0]
    return pl.pallas_call(
        _ffn_kernel,
        grid=(n // tm,),
        in_specs=[pl.BlockSpec((tm, d), lambda g: (g, 0)),
                  pl.BlockSpec((tm // SEG, N_MOD, d), lambda g: (g, 0, 0)),
                  pl.BlockSpec(g2.shape, lambda g: (0, 0)),
                  pl.BlockSpec((d, f), lambda g: (0, 0), pipeline_mode=_RESIDENT),
                  pl.BlockSpec((d, f), lambda g: (0, 1), pipeline_mode=_RESIDENT),
                  pl.BlockSpec((f, d), lambda g: (0, 0), pipeline_mode=_RESIDENT)],
        out_specs=pl.BlockSpec((tm, d), lambda g: (g, 0)),
        out_shape=jax.ShapeDtypeStruct((n, d), F32),
        compiler_params=_cparams(("parallel",)),
        name="ffn",
    )(x, modtab, g2, w_gu, w_gu, w_down)


MOE_TILE = 512
MOE_DMA_TILE = 512


def _router_kernel(x_ref, mod_ref, g_ref, w_ref, b_ref, h_ref, rw_ref, ri_ref):
    h = _norm_mod_tile(x_ref, mod_ref, g_ref, 3, out_dtype=F32)
    h_ref[...] = h
    logits = jnp.dot(h, w_ref[...], precision=lax.Precision.HIGHEST, preferred_element_type=F32) + b_ref[...]
    lane = lax.broadcasted_iota(jnp.int32, logits.shape, 1)
    neg = jnp.float32(-jnp.inf)
    lg = jnp.where(lane < N_EXPERTS, logits, neg)
    m1 = jnp.max(lg, axis=-1, keepdims=True)
    i1 = jnp.min(jnp.where(lg == m1, lane, HEAD_PAD), axis=-1, keepdims=True)
    lg2 = jnp.where(lane == i1, neg, lg)
    m2 = jnp.max(lg2, axis=-1, keepdims=True)
    i2 = jnp.min(jnp.where(lg2 == m2, lane, HEAD_PAD), axis=-1, keepdims=True)
    e2 = jnp.exp(m2 - m1)
    den = 1.0 + e2
    rw_ref[...] = jnp.where(lane == 0, 1.0 / den, jnp.where(lane == 1, e2 / den, 0.0))
    ri_ref[...] = jnp.where(lane == 0, i1, jnp.where(lane == 1, i2, 0))


def _router(x, modtab, g2, w_router, b_router, tm=512):
    n, d = x.shape
    wr = jnp.pad(w_router, ((0, 0), (0, HEAD_PAD - N_EXPERTS)))
    br = jnp.pad(b_router, (0, HEAD_PAD - N_EXPERTS)).reshape(1, HEAD_PAD)
    return pl.pallas_call(
        _router_kernel,
        grid=(n // tm,),
        in_specs=[pl.BlockSpec((tm, d), lambda g: (g, 0)),
                  pl.BlockSpec((tm // SEG, N_MOD, d), lambda g: (g, 0, 0)),
                  pl.BlockSpec(g2.shape, lambda g: (0, 0)),
                  pl.BlockSpec(wr.shape, lambda g: (0, 0)),
                  pl.BlockSpec(br.shape, lambda g: (0, 0))],
        out_specs=[pl.BlockSpec((tm, d), lambda g: (g, 0)),
                   pl.BlockSpec((tm, HEAD_PAD), lambda g: (g, 0)),
                   pl.BlockSpec((tm, HEAD_PAD), lambda g: (g, 0))],
        out_shape=[jax.ShapeDtypeStruct((n, d), F32),
                   jax.ShapeDtypeStruct((n, HEAD_PAD), F32),
                   jax.ShapeDtypeStruct((n, HEAD_PAD), jnp.int32)],
        compiler_params=_cparams(("parallel",)),
        name="router",
    )(x, modtab, g2, wr, br)


def _moe_plan(ri, n):
    e = ri[:, :TOP_K]
    mask = (e[:, :, None] == jnp.arange(N_EXPERTS, dtype=jnp.int32)).astype(jnp.int32).sum(axis=1)
    csum = jnp.cumsum(mask, axis=0)
    counts = csum[-1]
    padded = (counts + MOE_TILE - 1) // MOE_TILE * MOE_TILE
    ends = jnp.cumsum(padded)
    starts = ends - padded
    dest = starts[e] + jnp.take_along_axis(csum, e, axis=1) - 1
    n_tiles = (TOP_K * n) // MOE_TILE + N_EXPERTS
    n_valid = (ends[-1] // MOE_TILE).astype(jnp.int32).reshape(1)
    tile_ids = jnp.arange(n_tiles, dtype=jnp.int32)
    tile_expert = jnp.searchsorted(ends // MOE_TILE, jnp.minimum(tile_ids, n_valid - 1), side="right").astype(jnp.int32)
    return dest[:, 0].astype(jnp.int32), dest[:, 1].astype(jnp.int32), tile_expert, n_valid, n_tiles


def _row(ref, i):
    return ref.at[pl.ds(i, 1)]


def _scatter_rows_kernel(d0_ref, d1_ref, h_ref, init_hbm, o_hbm, sem):
    del init_hbm
    base = pl.program_id(0) * MOE_DMA_TILE

    def issue(r, carry):
        t = base + r
        pltpu.make_async_copy(_row(h_ref, r), _row(o_hbm, d0_ref[t]), sem.at[0]).start()
        pltpu.make_async_copy(_row(h_ref, r), _row(o_hbm, d1_ref[t]), sem.at[1]).start()
        return carry

    lax.fori_loop(0, MOE_DMA_TILE, issue, 0, unroll=8)
    pltpu.make_async_copy(h_ref, o_hbm.at[pl.ds(0, MOE_DMA_TILE)], sem.at[0]).wait()
    pltpu.make_async_copy(h_ref, o_hbm.at[pl.ds(0, MOE_DMA_TILE)], sem.at[1]).wait()


def _scatter_rows(d0, d1, h, n_rows):
    n, d = h.shape
    init = jnp.zeros((n_rows, d), F32)
    return pl.pallas_call(
        _scatter_rows_kernel,
        grid_spec=pltpu.PrefetchScalarGridSpec(
            num_scalar_prefetch=2,
            grid=(n // MOE_DMA_TILE,),
            in_specs=[pl.BlockSpec((MOE_DMA_TILE, d), lambda g, d0, d1: (g, 0)), pl.BlockSpec(memory_space=pl.ANY)],
            out_specs=pl.BlockSpec(memory_space=pl.ANY),
            scratch_shapes=[pltpu.SemaphoreType.DMA((2,))]),
        out_shape=jax.ShapeDtypeStruct((n_rows, d), F32),
        input_output_aliases={3: 0},
        compiler_params=_cparams(("arbitrary",)),
        name="moe_scatter",
    )(d0, d1, h, init)


def _moe_ffn_kernel(te_ref, nv_ref, xs_ref, wg_ref, wu_ref, wd_ref, o_ref):
    del te_ref
    i = pl.program_id(0)

    @pl.when(i < nv_ref[0])
    def _():
        o_ref[...] = _swiglu_partial(xs_ref[...].astype(BF16), wg_ref[0], wu_ref[0], wd_ref[0])

    @pl.when(i >= nv_ref[0])
    def _():
        o_ref[...] = jnp.zeros_like(o_ref)


def _moe_ffn(xs, tile_expert, n_valid, w_gu, w_down):
    rows, d = xs.shape
    f = w_down.shape[1]
    return pl.pallas_call(
        _moe_ffn_kernel,
        grid_spec=pltpu.PrefetchScalarGridSpec(
            num_scalar_prefetch=2,
            grid=(rows // MOE_TILE,),
            in_specs=[pl.BlockSpec((MOE_TILE, d), lambda i, te, nv: (i, 0)),
                      pl.BlockSpec((1, d, f), lambda i, te, nv: (te[i], 0, 0), pipeline_mode=_RESIDENT),
                      pl.BlockSpec((1, d, f), lambda i, te, nv: (te[i], 0, 1), pipeline_mode=_RESIDENT),
                      pl.BlockSpec((1, f, d), lambda i, te, nv: (te[i], 0, 0), pipeline_mode=_RESIDENT)],
            out_specs=pl.BlockSpec((MOE_TILE, d), lambda i, te, nv: (i, 0))),
        out_shape=jax.ShapeDtypeStruct((rows, d), F32),
        compiler_params=_cparams(("arbitrary",)),
        name="moe_ffn",
    )(tile_expert, n_valid, xs, w_gu, w_gu, w_down)


def _combine_kernel(d0_ref, d1_ref, x_ref, mod_ref, rw_ref, y_hbm, o_ref, ya_ref, yb_ref, sem):
    base = pl.program_id(0) * MOE_DMA_TILE

    def issue(r, carry):
        t = base + r
        pltpu.make_async_copy(_row(y_hbm, d0_ref[t]), _row(ya_ref, r), sem.at[0]).start()
        pltpu.make_async_copy(_row(y_hbm, d1_ref[t]), _row(yb_ref, r), sem.at[1]).start()
        return carry

    lax.fori_loop(0, MOE_DMA_TILE, issue, 0, unroll=8)
    rw = rw_ref[...]
    lane = lax.broadcasted_iota(jnp.int32, rw.shape, 1)
    w1 = jnp.sum(jnp.where(lane == 0, rw, 0.0), axis=-1, keepdims=True)
    w2 = jnp.sum(jnp.where(lane == 1, rw, 0.0), axis=-1, keepdims=True)
    pltpu.make_async_copy(y_hbm.at[pl.ds(0, MOE_DMA_TILE)], ya_ref, sem.at[0]).wait()
    pltpu.make_async_copy(y_hbm.at[pl.ds(0, MOE_DMA_TILE)], yb_ref, sem.at[1]).wait()
    _gated_add(x_ref, mod_ref, w1 * ya_ref[...] + w2 * yb_ref[...], 5, o_ref)


def _combine(d0, d1, x, modtab, rw, y):
    n, d = x.shape
    tm = MOE_DMA_TILE
    return pl.pallas_call(
        _combine_kernel,
        grid_spec=pltpu.PrefetchScalarGridSpec(
            num_scalar_prefetch=2,
            grid=(n // tm,),
            in_specs=[pl.BlockSpec((tm, d), lambda g, d0, d1: (g, 0)),
                      pl.BlockSpec((tm // SEG, N_MOD, d), lambda g, d0, d1: (g, 0, 0)),
                      pl.BlockSpec((tm, HEAD_PAD), lambda g, d0, d1: (g, 0)),
                      pl.BlockSpec(memory_space=pl.ANY)],
            out_specs=pl.BlockSpec((tm, d), lambda g, d0, d1: (g, 0)),
            scratch_shapes=[pltpu.VMEM((tm, d), F32), pltpu.VMEM((tm, d), F32), pltpu.SemaphoreType.DMA((2,))]),
        out_shape=jax.ShapeDtypeStruct((n, d), F32),
        compiler_params=_cparams(("arbitrary",)),
        name="moe_combine",
    )(d0, d1, x, modtab, rw, y)


def _moe(x, modtab, g2, w_router, b_router, w_gu, w_down):
    n = x.shape[0]
    h, rw, ri = _router(x, modtab, g2, w_router, b_router)
    d0, d1, tile_expert, n_valid, n_tiles = _moe_plan(ri, n)
    xs = _scatter_rows(d0, d1, h, n_tiles * MOE_TILE)
    ys = _moe_ffn(xs, tile_expert, n_valid, w_gu, w_down)
    return _combine(d0, d1, x, modtab, rw, ys)


def _final_kernel(x_ref, g_ref, o_ref):
    o_ref[0] = _rms(x_ref[0], g_ref[...])


def _final_norm(x3, g, tm=512):
    b, _, d = x3.shape
    return pl.pallas_call(
        _final_kernel,
        grid=(b, SEQ // tm),
        in_specs=[pl.BlockSpec((1, tm, d), lambda bi, t: (bi, t, 0)), pl.BlockSpec(g.shape, lambda bi, t: (0, 0))],
        out_specs=pl.BlockSpec((1, tm, d), lambda bi, t: (bi, t, 0)),
        out_shape=jax.ShapeDtypeStruct((b, SEQ, d), F32),
        compiler_params=_cparams(("parallel", "parallel")),
        name="final_norm",
    )(x3, g)


def kernel(x, c, ctx, c_ctx, norm1_g, norm2_g, ada_w, ada_b, final_norm_g, mla_w_dq, mla_q_norm, mla_w_uq, mla_w_dkv, mla_kv_norm, mla_w_ukv, mla_w_o, s5_a_re, s5_a_im, s5_log_dt, s5_b_re, s5_b_im, s5_c_re, s5_c_im, s5_d, s5_w_glu, pool_w, pool_scale, ffn_w_gu, ffn_w_down, moe_w_router, moe_b_router, moe_w_gu, moe_w_down):
    b, l, d = x.shape
    n = b * T_TOK
    xs = jnp.concatenate([x, ctx], axis=1).reshape(n, d)

    rows = -(-(b + 1) // 8) * 8
    cc = jnp.concatenate([c, c_ctx[None], jnp.zeros((rows - b - 1, d), F32)], axis=0)
    mods = _ada(cc, ada_w, ada_b)
    seg_src = np.array([bi if sj < LAT_SEGS else b for bi in range(b) for sj in range(SEGS_PER_BATCH)], np.int32)

    rope = _rope_tables()
    for i in range(DEPTH):
        last = i == DEPTH - 1
        modtab = mods[i].reshape(rows, N_MOD, d)[seg_src]
        g1 = norm1_g[i].reshape(1, d)
        g2 = norm2_g[i].reshape(1, d)
        kind = i % N_MIXERS
        j = i // N_MIXERS
        if kind == 0:
            pw = _mla_weights(mla_w_dq[j], mla_q_norm[j], mla_w_uq[j], mla_w_dkv[j], mla_kv_norm[j], mla_w_ukv[j])
            q, k, v = _mla_proj(xs, modtab, g1, pw, rope)
            o = _attention(q.reshape(b, T_TOK, -1), k.reshape(b, T_TOK, -1), v.reshape(b, T_TOK, -1))
            xs = _proj_res(xs, modtab, o.reshape(n, -1), mla_w_o[j].astype(BF16))
        elif kind == 1:
            h = _normmod(xs, modtab, g1)
            nch = T_TOK // S5_CHUNK
            u = h.reshape(b, nch, S5_CHUNK, S5_GROUPS, S5_GROUP).transpose(3, 1, 0, 2, 4).reshape(S5_GROUPS, nch * b, _S5_W)
            sp = _s5_params(s5_a_re[j], s5_a_im[j], s5_log_dt[j], s5_b_re[j], s5_b_im[j], s5_c_re[j], s5_c_im[j], s5_d[j])
            y = _s5_core(u, sp, b)
            y = y.reshape(S5_GROUPS, nch, b, S5_CHUNK, S5_GROUP).transpose(2, 1, 3, 0, 4).reshape(n, d)
            xs = _glu_res(xs, modtab, y, s5_w_glu[j].astype(BF16))
        else:
            xs = _pool(xs, modtab, g1, pool_w[j].astype(BF16), pool_scale[j].reshape(1, d))
        kk = i // 2
        if i % 2 == 0:
            xs = _ffn(xs, modtab, g2, ffn_w_gu[kk].astype(BF16), ffn_w_down[kk].astype(BF16))
        else:
            xs = _moe(xs, modtab, g2, moe_w_router[kk], moe_b_router[kk], moe_w_gu[kk].astype(BF16), moe_w_down[kk].astype(BF16))
    return _final_norm(xs.reshape(b, T_TOK, d), final_norm_g.reshape(1, d))
```

```python
import functools
import math

import jax
import jax.numpy as jnp
import numpy as np
from jax import lax
from jax.experimental import pallas as pl
from jax.experimental.pallas import tpu as pltpu

F32 = jnp.float32
BF16 = jnp.bfloat16

D_MODEL = 1024
SEQ = 4096
DEPTH = 4
GRID_W = 64
CTX_LEN = 256
N_MIXERS = 3
NORM_EPS = 1e-6

MLA_HEADS = 16
MLA_Q_LORA = 384
MLA_KV_LORA = 256
MLA_NOPE = 64
MLA_ROPE = 32
MLA_V = 64
MLA_QK = MLA_NOPE + MLA_ROPE
ROPE_AXIS_FREQS = MLA_ROPE // 4
ROPE_THETA = 10000.0
HEAD_PAD = 128

S5_GROUP = 16
S5_GROUPS = D_MODEL // S5_GROUP
S5_STATE = 64
S5_MAX_RE = -1e-4
S5_CHUNK = 16

POOL_WINDOWS = (2, 4, 8, 16)
POOL_GROUP = D_MODEL // len(POOL_WINDOWS)

FFN_DIM = 2816
N_EXPERTS = 8
TOP_K = 2

T_TOK = SEQ + CTX_LEN
SEG = CTX_LEN
SEGS_PER_BATCH = T_TOK // SEG
LAT_SEGS = SEQ // SEG
N_MOD = 6

VMEM_LIMIT = 56 * 1024 * 1024


def _cparams(sem, vmem=VMEM_LIMIT):
    return pltpu.CompilerParams(dimension_semantics=sem, vmem_limit_bytes=vmem)


def _rms(x, g):
    return x * lax.rsqrt(jnp.mean(x * x, axis=-1, keepdims=True) + NORM_EPS) * g


def _norm_mod(x, g, shift, scale):
    return _rms(x, g) * (1.0 + scale) + shift


def _norm_mod_tile(x_ref, mod_ref, g_ref, shift_idx, out_dtype=BF16):
    parts = []
    for s in range(x_ref.shape[0] // SEG):
        m = mod_ref[s]
        x = x_ref[s * SEG:(s + 1) * SEG, :]
        parts.append(_norm_mod(x, g_ref[...], m[shift_idx:shift_idx + 1], m[shift_idx + 1:shift_idx + 2]).astype(out_dtype))
    return parts[0] if len(parts) == 1 else jnp.concatenate(parts, axis=0)


def _ada_kernel(c_ref, w_ref, b_ref, o_ref):
    s = jax.nn.silu(c_ref[...])
    o_ref[0] = jnp.dot(s, w_ref[0], precision=lax.Precision.HIGHEST, preferred_element_type=F32) + b_ref[0]


def _ada(cc, ada_w, ada_b):
    depth, d, n6 = ada_w.shape
    rows = cc.shape[0]
    tn = 1024
    return pl.pallas_call(
        _ada_kernel,
        grid=(depth, n6 // tn),
        in_specs=[pl.BlockSpec((rows, d), lambda i, j: (0, 0)),
                  pl.BlockSpec((1, d, tn), lambda i, j: (i, 0, j)),
                  pl.BlockSpec((1, 1, tn), lambda i, j: (i, 0, j))],
        out_specs=pl.BlockSpec((1, rows, tn), lambda i, j: (i, 0, j)),
        out_shape=jax.ShapeDtypeStruct((depth, rows, n6), F32),
        compiler_params=_cparams(("arbitrary", "arbitrary")),
        name="ada",
    )(cc, ada_w, ada_b.reshape(depth, 1, n6))


_W1_COLS = MLA_Q_LORA + MLA_KV_LORA + 2 * HEAD_PAD
_QW = MLA_HEADS * HEAD_PAD


def _mla_proj_kernel(x_ref, mod_ref, g1_ref, w1_ref, qn_ref, wq_ref, kvn_ref, wk_ref, we_ref, wv_ref,
                     cq_ref, sq_ref, ck_ref, sk_ref, q_ref, k_ref, v_ref):
    h = _norm_mod_tile(x_ref, mod_ref, g1_ref, 0)
    d = jnp.dot(h, w1_ref[...], preferred_element_type=F32)
    dq = d[:, :MLA_Q_LORA]
    ckv = d[:, MLA_Q_LORA:MLA_Q_LORA + MLA_KV_LORA]
    kr = d[:, MLA_Q_LORA + MLA_KV_LORA:MLA_Q_LORA + MLA_KV_LORA + HEAD_PAD]
    kr_sw = d[:, MLA_Q_LORA + MLA_KV_LORA + HEAD_PAD:]
    qn = _rms(dq, qn_ref[...]).astype(BF16)
    qq = jnp.dot(qn, wq_ref[...], preferred_element_type=F32)
    cq = cq_ref[...]
    sq = sq_ref[...]
    for hd in range(MLA_HEADS):
        lo = hd * HEAD_PAD
        q_ref[:, lo:lo + HEAD_PAD] = (qq[:, lo:lo + HEAD_PAD] * cq + qq[:, _QW + lo:_QW + lo + HEAD_PAD] * sq).astype(BF16)
    c = _rms(ckv, kvn_ref[...]).astype(BF16)
    kr_roped = (kr * ck_ref[...] + kr_sw * sk_ref[...]).astype(BF16)
    k = jnp.dot(c, wk_ref[...], preferred_element_type=F32) + jnp.dot(kr_roped, we_ref[...], preferred_element_type=F32)
    k_ref[...] = k.astype(BF16)
    v = jnp.dot(c, wv_ref[...], preferred_element_type=F32)
    lane = lax.broadcasted_iota(jnp.int32, v.shape, 1)
    v_ref[...] = jnp.where(lane % HEAD_PAD == MLA_V, 1.0, v).astype(BF16)


def _mla_proj(x, modtab, g1, pw, rope):
    n, d = x.shape
    tm = SEG
    full = lambda a: pl.BlockSpec(a.shape, lambda g: (0,) * a.ndim)
    pos = lambda g: (g % SEGS_PER_BATCH, 0)
    cq, sq, ck, sk = rope
    return pl.pallas_call(
        _mla_proj_kernel,
        grid=(n // tm,),
        in_specs=[pl.BlockSpec((tm, d), lambda g: (g, 0)),
                  pl.BlockSpec((tm // SEG, N_MOD, d), lambda g: (g, 0, 0)),
                  full(g1), full(pw["w1"]), full(pw["qn"]), full(pw["wq"]), full(pw["kvn"]),
                  full(pw["wk"]), full(pw["we"]), full(pw["wv"]),
                  pl.BlockSpec((tm, HEAD_PAD), pos), pl.BlockSpec((tm, HEAD_PAD), pos),
                  pl.BlockSpec((tm, HEAD_PAD), pos), pl.BlockSpec((tm, HEAD_PAD), pos)],
        out_specs=[pl.BlockSpec((tm, _QW), lambda g: (g, 0)),
                   pl.BlockSpec((tm, _QW), lambda g: (g, 0)),
                   pl.BlockSpec((tm, _QW), lambda g: (g, 0))],
        out_shape=[jax.ShapeDtypeStruct((n, _QW), BF16)] * 3,
        compiler_params=_cparams(("parallel",)),
        name="mla_proj",
    )(x, modtab, g1, pw["w1"], pw["qn"], pw["wq"], pw["kvn"], pw["wk"], pw["we"], pw["wv"], cq, sq, ck, sk)


def _mla_weights(w_dq, q_norm, w_uq, w_dkv, kv_norm, w_ukv):
    d = w_dq.shape[0]
    swap = np.arange(MLA_ROPE) ^ ROPE_AXIS_FREQS
    w_kr = w_dkv[:, MLA_KV_LORA:]
    padr = lambda w: jnp.pad(w, ((0, 0), (0, HEAD_PAD - MLA_ROPE)))
    w1 = jnp.concatenate([w_dq, w_dkv[:, :MLA_KV_LORA], padr(w_kr), padr(w_kr[:, swap])], axis=1)
    uq = w_uq.reshape(MLA_Q_LORA, MLA_HEADS, MLA_QK)
    zpad = jnp.zeros((MLA_Q_LORA, MLA_HEADS, HEAD_PAD - MLA_QK), w_uq.dtype)
    wq_main = jnp.concatenate([uq, zpad], axis=-1).reshape(MLA_Q_LORA, _QW)
    uq_sw = jnp.concatenate([jnp.zeros((MLA_Q_LORA, MLA_HEADS, MLA_NOPE), w_uq.dtype),
                             uq[:, :, MLA_NOPE:][:, :, swap], zpad], axis=-1).reshape(MLA_Q_LORA, _QW)
    wq = jnp.concatenate([wq_main, uq_sw], axis=1)
    ukv = w_ukv.reshape(MLA_KV_LORA, MLA_HEADS, MLA_NOPE + MLA_V)
    wk = jnp.concatenate([ukv[:, :, :MLA_NOPE],
                          jnp.zeros((MLA_KV_LORA, MLA_HEADS, HEAD_PAD - MLA_NOPE), w_ukv.dtype)], axis=-1)
    wk = wk.reshape(MLA_KV_LORA, _QW)
    wv = jnp.concatenate([ukv[:, :, MLA_NOPE:],
                          jnp.zeros((MLA_KV_LORA, MLA_HEADS, HEAD_PAD - MLA_V), w_ukv.dtype)], axis=-1)
    wv = wv.reshape(MLA_KV_LORA, _QW)
    e = np.zeros((HEAD_PAD, MLA_HEADS, HEAD_PAD), np.float32)
    for r in range(MLA_ROPE):
        e[r, :, MLA_NOPE + r] = 1.0
    we = jnp.asarray(e.reshape(HEAD_PAD, _QW))
    return {"w1": w1.astype(BF16), "qn": q_norm.reshape(1, -1), "wq": wq.astype(BF16),
            "kvn": kv_norm.reshape(1, -1), "wk": wk.astype(BF16), "we": we.astype(BF16), "wv": wv.astype(BF16)}


def _rope_tables():
    rows = SEQ // GRID_W
    row = jnp.repeat(jnp.arange(rows, dtype=F32), GRID_W)
    col = jnp.tile(jnp.arange(GRID_W, dtype=F32), rows)
    inv_freq = 1.0 / (ROPE_THETA ** (jnp.arange(ROPE_AXIS_FREQS, dtype=F32) / ROPE_AXIS_FREQS))
    ang = jnp.stack([row[:, None] * inv_freq, col[:, None] * inv_freq], axis=1)
    ang = jnp.concatenate([ang, jnp.zeros((CTX_LEN, 2, ROPE_AXIS_FREQS), F32)], axis=0)
    cos = jnp.cos(ang)
    sin = jnp.sin(ang)
    c32 = jnp.stack([cos, cos], axis=2).reshape(T_TOK, MLA_ROPE)
    s32 = jnp.stack([-sin, sin], axis=2).reshape(T_TOK, MLA_ROPE)
    scale = MLA_QK ** -0.5 * math.log2(math.e)
    zq = jnp.zeros((T_TOK, HEAD_PAD - MLA_QK), F32)
    cq = jnp.concatenate([jnp.full((T_TOK, MLA_NOPE), scale, F32), c32 * scale, zq], axis=1)
    sq = jnp.concatenate([jnp.zeros((T_TOK, MLA_NOPE), F32), s32 * scale, zq], axis=1)
    zk = jnp.zeros((T_TOK, HEAD_PAD - MLA_ROPE), F32)
    ck = jnp.concatenate([c32, zk], axis=1)
    sk = jnp.concatenate([s32, zk], axis=1)
    return cq, sq, ck, sk


ATTN_TQ = 512
ATTN_KEY_CHUNK = 1024
_LAT_CHUNKS = tuple((lo, ATTN_KEY_CHUNK) for lo in range(0, SEQ, ATTN_KEY_CHUNK))
_CTX_CHUNK = ((SEQ, CTX_LEN),)


def _attend_pair(q, k_ref, v_ref, chunks):
    outs = []
    for j in range(2):
        cols = slice(j * HEAD_PAD, (j + 1) * HEAD_PAD)
        qj = q[:, cols]
        m = None
        acc = None
        for lo, size in chunks:
            s = lax.dot_general(qj, k_ref[0, lo:lo + size, cols], (((1,), (1,)), ((), ())), preferred_element_type=F32)
            mc = jnp.max(s, axis=-1, keepdims=True)
            m_new = mc if m is None else jnp.maximum(m, mc)
            pv = jnp.dot(jnp.exp2(s - m_new).astype(BF16), v_ref[0, lo:lo + size, cols], preferred_element_type=F32)
            acc = pv if m is None else acc * jnp.exp2(m - m_new) + pv
            m = m_new
        outs.append(acc / acc[:, MLA_V:MLA_V + 1])
    lane = lax.broadcasted_iota(jnp.int32, outs[0].shape, 1)
    return jnp.where(lane < MLA_V, outs[0], pltpu.roll(outs[1], MLA_V, 1))


def _attn_kernel(q_ref, k_ref, v_ref, o_ref, *, lat_tiles):
    qi = pl.program_id(2)

    @pl.when(qi < lat_tiles)
    def _():
        o_ref[0] = _attend_pair(q_ref[0], k_ref, v_ref, _LAT_CHUNKS + _CTX_CHUNK).astype(BF16)

    @pl.when(qi >= lat_tiles)
    def _():
        o_ref[0, :CTX_LEN, :] = _attend_pair(q_ref[0, :CTX_LEN, :], k_ref, v_ref, _CTX_CHUNK).astype(BF16)


def _attention(q, k, v):
    b = q.shape[0]
    tq = ATTN_TQ
    lat_tiles = SEQ // tq
    return pl.pallas_call(
        functools.partial(_attn_kernel, lat_tiles=lat_tiles),
        grid=(b, MLA_HEADS // 2, lat_tiles + 1),
        in_specs=[pl.BlockSpec((1, tq, 2 * HEAD_PAD), lambda bi, hp, qi: (bi, qi, hp)),
                  pl.BlockSpec((1, T_TOK, 2 * HEAD_PAD), lambda bi, hp, qi: (bi, 0, hp)),
                  pl.BlockSpec((1, T_TOK, 2 * HEAD_PAD), lambda bi, hp, qi: (bi, 0, hp))],
        out_specs=pl.BlockSpec((1, tq, 2 * MLA_V), lambda bi, hp, qi: (bi, qi, hp)),
        out_shape=jax.ShapeDtypeStruct((b, T_TOK, MLA_HEADS * MLA_V), BF16),
        compiler_params=_cparams(("parallel", "parallel", "arbitrary")),
        name="attention",
    )(q, k, v)


def _gated_add(x_ref, mod_ref, y, gate_idx, o_ref):
    for s in range(x_ref.shape[0] // SEG):
        rows = slice(s * SEG, (s + 1) * SEG)
        g = mod_ref[s][gate_idx:gate_idx + 1]
        o_ref[rows, :] = x_ref[rows, :] + g * y[rows, :]


def _proj_res_kernel(x_ref, mod_ref, y_ref, w_ref, o_ref):
    y = jnp.dot(y_ref[...], w_ref[...], preferred_element_type=F32)
    _gated_add(x_ref, mod_ref, y, 2, o_ref)


def _proj_res(x, modtab, y, w, tm=512):
    n, d = x.shape
    return pl.pallas_call(
        _proj_res_kernel,
        grid=(n // tm,),
        in_specs=[pl.BlockSpec((tm, d), lambda g: (g, 0)),
                  pl.BlockSpec((tm // SEG, N_MOD, d), lambda g: (g, 0, 0)),
                  pl.BlockSpec((tm, y.shape[1]), lambda g: (g, 0)),
                  pl.BlockSpec(w.shape, lambda g: (0, 0))],
        out_specs=pl.BlockSpec((tm, d), lambda g: (g, 0)),
        out_shape=jax.ShapeDtypeStruct((n, d), F32),
        compiler_params=_cparams(("parallel",)),
        name="proj_res",
    )(x, modtab, y, w)


_S5_LAT_CHUNKS = SEQ // S5_CHUNK
_S5_CHUNKS = T_TOK // S5_CHUNK
_S5_W = S5_CHUNK * S5_GROUP
_S5_BLK_CHUNKS = 2
_S5_BLK_TOK = _S5_BLK_CHUNKS * S5_CHUNK
LANES = 128
_SLOTS = LANES // S5_GROUP


def _s5_in_kernel(x_ref, mod_ref, g_ref, u_ref, h_scr, *, nb):
    for b in range(nb):
        m = mod_ref[0, b]
        h = _norm_mod(x_ref[b], g_ref[...], m[0:1], m[1:2])
        for j in range(D_MODEL // LANES):
            h_scr[j, b * _S5_BLK_TOK:(b + 1) * _S5_BLK_TOK, :] = h[:, j * LANES:(j + 1) * LANES]
    slot = lax.broadcasted_iota(jnp.int32, (_S5_BLK_CHUNKS * nb, LANES), 1) // S5_GROUP
    for j in range(D_MODEL // LANES):
        for half in range(_S5_W // LANES):
            srcs = []
            for p in range(_SLOTS):
                t = _SLOTS * half + p
                rows = [h_scr[j, pl.ds(c * S5_CHUNK + t, nb, stride=_S5_BLK_TOK), :] for c in range(_S5_BLK_CHUNKS)]
                srcs.append(jnp.concatenate(rows, axis=0))
            for gs in range(_SLOTS):
                acc = None
                for p in range(_SLOTS):
                    k = (p - gs) % _SLOTS
                    r = pltpu.roll(srcs[p], k * S5_GROUP, 1) if k else srcs[p]
                    acc = r if acc is None else jnp.where(slot == p, r, acc)
                u_ref[_SLOTS * j + gs, :, half * LANES:(half + 1) * LANES] = acc.astype(BF16)


def _s5_in(x3, modsel, g1):
    nb, t, d = x3.shape
    nblk = t // _S5_BLK_TOK
    rows = _S5_BLK_CHUNKS * nb
    return pl.pallas_call(
        functools.partial(_s5_in_kernel, nb=nb),
        grid=(nblk,),
        in_specs=[pl.BlockSpec((nb, _S5_BLK_TOK, d), lambda k: (0, k, 0)),
                  pl.BlockSpec((1, nb, N_MOD, d), lambda k: (k // (SEQ // _S5_BLK_TOK), 0, 0, 0)),
                  pl.BlockSpec(g1.shape, lambda k: (0, 0))],
        out_specs=pl.BlockSpec((S5_GROUPS, rows, _S5_W), lambda k: (0, k, 0)),
        out_shape=jax.ShapeDtypeStruct((S5_GROUPS, nblk * rows, _S5_W), BF16),
        scratch_shapes=[pltpu.VMEM((d // LANES, nb * _S5_BLK_TOK, LANES), F32)],
        compiler_params=_cparams(("parallel",)),
        name="s5_in",
    )(x3, modsel, g1)


def _s5_out_kernel(x_ref, mod_ref, y_ref, w_ref, o_ref, nat_scr, *, nb):
    d = x_ref.shape[2]
    slot = lax.broadcasted_iota(jnp.int32, (nb, LANES), 1) // S5_GROUP
    for j in range(d // LANES):
        for half in range(_S5_W // LANES):
            for c in range(_S5_BLK_CHUNKS):
                srcs = [y_ref[_SLOTS * j + gs, c * nb:(c + 1) * nb, half * LANES:(half + 1) * LANES] for gs in range(_SLOTS)]
                for p in range(_SLOTS):
                    acc = None
                    for gs in range(_SLOTS):
                        k = (gs - p) % _SLOTS
                        r = pltpu.roll(srcs[gs], k * S5_GROUP, 1) if k else srcs[gs]
                        acc = r if acc is None else jnp.where(slot == gs, r, acc)
                    nat_scr[j, pl.ds(c * S5_CHUNK + _SLOTS * half + p, nb, stride=_S5_BLK_TOK), :] = acc
    y_nat = jnp.concatenate([nat_scr[j] for j in range(d // LANES)], axis=1)
    g = jax.nn.gelu(y_nat).astype(BF16)
    z = jnp.dot(g, w_ref[...], preferred_element_type=F32)
    y = z[:, :d] * jax.nn.sigmoid(z[:, d:])
    for b in range(nb):
        gate = mod_ref[0, b][2:3]
        o_ref[b] = x_ref[b] + gate * y[b * _S5_BLK_TOK:(b + 1) * _S5_BLK_TOK, :]


def _s5_out(x3, modsel, y, w):
    nb, t, d = x3.shape
    nblk = t // _S5_BLK_TOK
    rows = _S5_BLK_CHUNKS * nb
    return pl.pallas_call(
        functools.partial(_s5_out_kernel, nb=nb),
        grid=(nblk,),
        in_specs=[pl.BlockSpec((nb, _S5_BLK_TOK, d), lambda k: (0, k, 0)),
                  pl.BlockSpec((1, nb, N_MOD, d), lambda k: (k // (SEQ // _S5_BLK_TOK), 0, 0, 0)),
                  pl.BlockSpec((S5_GROUPS, rows, _S5_W), lambda k: (0, k, 0)),
                  pl.BlockSpec(w.shape, lambda k: (0, 0))],
        out_specs=pl.BlockSpec((nb, _S5_BLK_TOK, d), lambda k: (0, k, 0)),
        out_shape=jax.ShapeDtypeStruct((nb, t, d), F32),
        scratch_shapes=[pltpu.VMEM((d // LANES, nb * _S5_BLK_TOK, LANES), F32)],
        compiler_params=_cparams(("parallel",)),
        name="s5_out",
    )(x3, modsel, y, w)


def _s5_kernel(u_ref, m_ref, win_ref, wof_ref, wor_ref, ar_ref, ai_ref, y_ref, z_ref, sf_ref, sr_ref, *, nb):
    u = u_ref[0]
    z_ref[...] = jnp.dot(u, win_ref[0], preferred_element_type=F32)
    half = _S5_W // 2
    ar = ar_ref[0]
    ai = ai_ref[0]
    lane = lax.broadcasted_iota(jnp.int32, (nb, _S5_W), 1)
    is_fwd = (lane % half) < S5_STATE

    def step(i, s):
        cf = jnp.where(i < _S5_CHUNKS - _S5_LAT_CHUNKS, i + _S5_LAT_CHUNKS, i - (_S5_CHUNKS - _S5_LAT_CHUNKS))
        cr = _S5_CHUNKS - 1 - i
        rf = pl.multiple_of(cf * nb, nb)
        rr = pl.multiple_of(cr * nb, nb)
        sf_ref[pl.ds(rf, nb), :] = s
        sr_ref[pl.ds(rr, nb), :] = s
        z = jnp.where(is_fwd, z_ref[pl.ds(rf, nb), :], z_ref[pl.ds(rr, nb), :])
        re = s[:, :half]
        im = s[:, half:]
        return jnp.concatenate([ar * re - ai * im + z[:, :half], ar * im + ai * re + z[:, half:]], axis=1)

    lax.fori_loop(0, _S5_CHUNKS, step, jnp.zeros((nb, _S5_W), F32))
    y = jnp.dot(u, m_ref[0], preferred_element_type=F32)
    y = y + jnp.dot(sf_ref[...].astype(BF16), wof_ref[0], preferred_element_type=F32)
    y = y + jnp.dot(sr_ref[...].astype(BF16), wor_ref[0], preferred_element_type=F32)
    y_ref[0] = y


def _s5_core(u, sp, nb):
    g, rows, w = u.shape
    blk = lambda a: pl.BlockSpec((1,) + a.shape[1:], lambda i: (i,) + (0,) * (a.ndim - 1))
    return pl.pallas_call(
        functools.partial(_s5_kernel, nb=nb),
        grid=(g,),
        in_specs=[blk(u), blk(sp["m"]), blk(sp["win"]), blk(sp["wof"]), blk(sp["wor"]), blk(sp["ar"]), blk(sp["ai"])],
        out_specs=pl.BlockSpec((1, rows, w), lambda i: (i, 0, 0)),
        out_shape=jax.ShapeDtypeStruct((g, rows, w), F32),
        scratch_shapes=[pltpu.VMEM((rows, w), F32), pltpu.VMEM((rows, w), F32), pltpu.VMEM((rows, w), F32)],
        compiler_params=_cparams(("parallel",)),
        name="s5_core",
    )(u, sp["m"], sp["win"], sp["wof"], sp["wor"], sp["ar"], sp["ai"])


def _s5_params(a_re, a_im, log_dt, b_re, b_im, c_re, c_im, d):
    c64 = jnp.complex64
    lam = lax.complex(jnp.minimum(a_re, S5_MAX_RE), a_im)
    lam_dt = lam * jnp.exp(log_dt)[..., None]
    lam_bar = jnp.exp(lam_dt)
    b_bar = ((lam_bar - 1.0) / lam)[..., None] * lax.complex(b_re, b_im)
    c_mat = lax.complex(c_re, c_im)
    taus = jnp.arange(S5_CHUNK + 1, dtype=F32)
    pw = jnp.exp(lam_dt[None] * taus[:, None, None, None].astype(c64))
    hi = lax.Precision.HIGHEST
    kern = jnp.real(jnp.einsum('dgnp,tdgp,dgpm->dgtnm', c_mat, pw[:S5_CHUNK], b_bar, precision=hi))
    eye = jnp.eye(S5_GROUP, dtype=F32)
    k0 = kern[0, :, 0] + kern[1, :, 0] + d.reshape(S5_GROUPS, S5_GROUP)[:, :, None] * eye
    lags = jnp.concatenate([kern[1, :, :0:-1], k0[:, None], kern[0, :, 1:]], axis=1)
    idx = (np.arange(S5_CHUNK)[None, :] - np.arange(S5_CHUNK)[:, None]) + S5_CHUNK - 1
    m = lags[:, idx]
    m = m.transpose(0, 1, 4, 2, 3).reshape(S5_GROUPS, _S5_W, _S5_W)
    wf = pw[S5_CHUNK - 1::-1][:S5_CHUNK, 0][..., None] * b_bar[0][None]
    wr = pw[:S5_CHUNK, 1][..., None] * b_bar[1][None]
    to_rows = lambda w: w.transpose(1, 0, 3, 2).reshape(S5_GROUPS, _S5_W, S5_STATE)
    wf, wr = to_rows(wf), to_rows(wr)
    win = jnp.concatenate([jnp.real(wf), jnp.real(wr), jnp.imag(wf), jnp.imag(wr)], axis=-1)
    of = c_mat[0][None] * pw[1:, 0][:, :, None, :]
    orv = c_mat[1][None] * pw[S5_CHUNK:0:-1, 1][:, :, None, :]
    to_cols = lambda w: w.transpose(1, 3, 0, 2).reshape(S5_GROUPS, S5_STATE, _S5_W)
    of, orv = to_cols(of), to_cols(orv)
    zeros = jnp.zeros_like(jnp.real(of))
    wof = jnp.concatenate([jnp.real(of), zeros, -jnp.imag(of), zeros], axis=1)
    wor = jnp.concatenate([zeros, jnp.real(orv), zeros, -jnp.imag(orv)], axis=1)
    a16 = pw[S5_CHUNK]
    ar = jnp.concatenate([jnp.real(a16[0]), jnp.real(a16[1])], axis=-1)[:, None, :]
    ai = jnp.concatenate([jnp.imag(a16[0]), jnp.imag(a16[1])], axis=-1)[:, None, :]
    return {"m": m.astype(BF16), "win": win.astype(BF16), "wof": wof.astype(BF16), "wor": wor.astype(BF16),
            "ar": ar, "ai": ai}


def _pool_kernel(xp_ref, xc_ref, xn_ref, mod_ref, g_ref, wp_ref, sc_ref, o_ref):
    seg = pl.program_id(0) % SEGS_PER_BATCH
    is_ctx = seg >= LAT_SEGS
    p0 = jnp.where(is_ctx, 0, seg * SEG)
    lseq = jnp.where(is_ctx, CTX_LEN, SEQ)
    m = mod_ref[0]
    g = g_ref[...]
    hs = [_norm_mod(r[...], g, m[0:1], m[1:2]) for r in (xp_ref, xc_ref, xn_ref)]
    hcat = jnp.concatenate([h.astype(BF16) for h in hs], axis=0)
    r = lax.broadcasted_iota(jnp.int32, (SEG, 3 * SEG), 0)
    s = lax.broadcasted_iota(jnp.int32, (SEG, 3 * SEG), 1)
    pt = p0 + r
    ps = p0 - SEG + s
    rcol = lax.broadcasted_iota(jnp.int32, (SEG, 1), 0) + p0
    x = xc_ref[...]
    gate = m[2:3]
    for gi, w in enumerate(POOL_WINDOWS):
        cols = slice(gi * POOL_GROUP, (gi + 1) * POOL_GROUP)
        lo = jnp.maximum(pt - w // 2, 0)
        hi = jnp.minimum(pt + w - w // 2, lseq)
        band = jnp.where((ps >= lo) & (ps < hi), 1.0, 0.0).astype(BF16)
        tot = jnp.dot(band, hcat[:, cols], preferred_element_type=F32)
        cnt = (jnp.minimum(rcol + w - w // 2, lseq) - jnp.maximum(rcol - w // 2, 0)).astype(F32)
        resid = tot / cnt - hs[1][:, cols]
        y = jnp.dot(resid.astype(BF16), wp_ref[gi], preferred_element_type=F32) * sc_ref[:, cols]
        o_ref[:, cols] = x[:, cols] + gate[:, cols] * y


def _pool(x, modtab, g1, w_pool, scale):
    n, d = x.shape
    nseg = n // SEG

    def prev(g):
        return (jnp.maximum(g - 1, 0), 0)

    def nxt(g):
        return (jnp.minimum(g + 1, nseg - 1), 0)

    return pl.pallas_call(
        _pool_kernel,
        grid=(nseg,),
        in_specs=[pl.BlockSpec((SEG, d), prev), pl.BlockSpec((SEG, d), lambda g: (g, 0)), pl.BlockSpec((SEG, d), nxt),
                  pl.BlockSpec((1, N_MOD, d), lambda g: (g, 0, 0)),
                  pl.BlockSpec(g1.shape, lambda g: (0, 0)),
                  pl.BlockSpec(w_pool.shape, lambda g: (0, 0, 0)),
                  pl.BlockSpec(scale.shape, lambda g: (0, 0))],
        out_specs=pl.BlockSpec((SEG, d), lambda g: (g, 0)),
        out_shape=jax.ShapeDtypeStruct((n, d), F32),
        compiler_params=_cparams(("parallel",)),
        name="pool",
    )(x, x, x, modtab, g1, w_pool, scale)


def _swiglu_partial(h, wg, wu, wd):
    a = jax.nn.silu(jnp.dot(h, wg, preferred_element_type=F32)) * jnp.dot(h, wu, preferred_element_type=F32)
    return jnp.dot(a.astype(BF16), wd, preferred_element_type=F32)


def _ffn_kernel(x_ref, mod_ref, g_ref, wg_ref, wu_ref, wd_ref, o_ref):
    h = _norm_mod_tile(x_ref, mod_ref, g_ref, 3)
    _gated_add(x_ref, mod_ref, _swiglu_partial(h, wg_ref[...], wu_ref[...], wd_ref[...]), 5, o_ref)


_RESIDENT = pl.Buffered(1)


def _ffn(x, modtab, g2, w_gu, w_down, tm=512):
    n, d = x.shape
    f = w_down.shape[0]
    return pl.pallas_call(
        _ffn_kernel,
        grid=(n // tm,),
        in_specs=[pl.BlockSpec((tm, d), lambda g: (g, 0)),
                  pl.BlockSpec((tm // SEG, N_MOD, d), lambda g: (g, 0, 0)),
                  pl.BlockSpec(g2.shape, lambda g: (0, 0)),
                  pl.BlockSpec((d, f), lambda g: (0, 0), pipeline_mode=_RESIDENT),
                  pl.BlockSpec((d, f), lambda g: (0, 1), pipeline_mode=_RESIDENT),
                  pl.BlockSpec((f, d), lambda g: (0, 0), pipeline_mode=_RESIDENT)],
        out_specs=pl.BlockSpec((tm, d), lambda g: (g, 0)),
        out_shape=jax.ShapeDtypeStruct((n, d), F32),
        compiler_params=_cparams(("parallel",)),
        name="ffn",
    )(x, modtab, g2, w_gu, w_gu, w_down)


MOE_TILE = 512
MOE_DMA_TILE = 512


def _router_kernel(x_ref, mod_ref, g_ref, w_ref, b_ref, h_ref, rw_ref, ri_ref):
    h = _norm_mod_tile(x_ref, mod_ref, g_ref, 3, out_dtype=F32)
    h_ref[...] = h
    logits = jnp.dot(h, w_ref[...], precision=lax.Precision.HIGHEST, preferred_element_type=F32) + b_ref[...]
    lane = lax.broadcasted_iota(jnp.int32, logits.shape, 1)
    neg = jnp.float32(-jnp.inf)
    lg = jnp.where(lane < N_EXPERTS, logits, neg)
    m1 = jnp.max(lg, axis=-1, keepdims=True)
    i1 = jnp.min(jnp.where(lg == m1, lane, HEAD_PAD), axis=-1, keepdims=True)
    lg2 = jnp.where(lane == i1, neg, lg)
    m2 = jnp.max(lg2, axis=-1, keepdims=True)
    i2 = jnp.min(jnp.where(lg2 == m2, lane, HEAD_PAD), axis=-1, keepdims=True)
    e2 = jnp.exp(m2 - m1)
    den = 1.0 + e2
    rw_ref[...] = jnp.where(lane == 0, 1.0 / den, jnp.where(lane == 1, e2 / den, 0.0))
    ri_ref[...] = jnp.where(lane == 0, i1, jnp.where(lane == 1, i2, 0))


def _router(x, modtab, g2, w_router, b_router, tm=512):
    n, d = x.shape
    wr = jnp.pad(w_router, ((0, 0), (0, HEAD_PAD - N_EXPERTS)))
    br = jnp.pad(b_router, (0, HEAD_PAD - N_EXPERTS)).reshape(1, HEAD_PAD)
    return pl.pallas_call(
        _router_kernel,
        grid=(n // tm,),
        in_specs=[pl.BlockSpec((tm, d), lambda g: (g, 0)),
                  pl.BlockSpec((tm // SEG, N_MOD, d), lambda g: (g, 0, 0)),
                  pl.BlockSpec(g2.shape, lambda g: (0, 0)),
                  pl.BlockSpec(wr.shape, lambda g: (0, 0)),
                  pl.BlockSpec(br.shape, lambda g: (0, 0))],
        out_specs=[pl.BlockSpec((tm, d), lambda g: (g, 0)),
                   pl.BlockSpec((tm, HEAD_PAD), lambda g: (g, 0)),
                   pl.BlockSpec((tm, HEAD_PAD), lambda g: (g, 0))],
        out_shape=[jax.ShapeDtypeStruct((n, d), F32),
                   jax.ShapeDtypeStruct((n, HEAD_PAD), F32),
                   jax.ShapeDtypeStruct((n, HEAD_PAD), jnp.int32)],
        compiler_params=_cparams(("parallel",)),
        name="router",
    )(x, modtab, g2, wr, br)


def _moe_plan(ri, n):
    e = ri[:, :TOP_K]
    mask = (e[:, :, None] == jnp.arange(N_EXPERTS, dtype=jnp.int32)).astype(jnp.int32).sum(axis=1)
    csum = jnp.cumsum(mask, axis=0)
    counts = csum[-1]
    padded = (counts + MOE_TILE - 1) // MOE_TILE * MOE_TILE
    ends = jnp.cumsum(padded)
    starts = ends - padded
    dest = starts[e] + jnp.take_along_axis(csum, e, axis=1) - 1
    n_tiles = (TOP_K * n) // MOE_TILE + N_EXPERTS
    n_valid = (ends[-1] // MOE_TILE).astype(jnp.int32).reshape(1)
    tile_ids = jnp.arange(n_tiles, dtype=jnp.int32)
    tile_expert = jnp.searchsorted(ends // MOE_TILE, jnp.minimum(tile_ids, n_valid - 1), side="right").astype(jnp.int32)
    return dest[:, 0].astype(jnp.int32), dest[:, 1].astype(jnp.int32), tile_expert, n_valid, n_tiles


def _row(ref, i):
    return ref.at[pl.ds(i, 1)]


def _scatter_rows_kernel(d0_ref, d1_ref, h_ref, init_hbm, o_hbm, sem):
    del init_hbm
    base = pl.program_id(0) * MOE_DMA_TILE

    def issue(r, carry):
        t = base + r
        pltpu.make_async_copy(_row(h_ref, r), _row(o_hbm, d0_ref[t]), sem.at[0]).start()
        pltpu.make_async_copy(_row(h_ref, r), _row(o_hbm, d1_ref[t]), sem.at[1]).start()
        return carry

    lax.fori_loop(0, MOE_DMA_TILE, issue, 0, unroll=8)
    pltpu.make_async_copy(h_ref, o_hbm.at[pl.ds(0, MOE_DMA_TILE)], sem.at[0]).wait()
    pltpu.make_async_copy(h_ref, o_hbm.at[pl.ds(0, MOE_DMA_TILE)], sem.at[1]).wait()


def _scatter_rows(d0, d1, h, n_rows):
    n, d = h.shape
    init = jnp.zeros((n_rows, d), F32)
    return pl.pallas_call(
        _scatter_rows_kernel,
        grid_spec=pltpu.PrefetchScalarGridSpec(
            num_scalar_prefetch=2,
            grid=(n // MOE_DMA_TILE,),
            in_specs=[pl.BlockSpec((MOE_DMA_TILE, d), lambda g, d0, d1: (g, 0)), pl.BlockSpec(memory_space=pl.ANY)],
            out_specs=pl.BlockSpec(memory_space=pl.ANY),
            scratch_shapes=[pltpu.SemaphoreType.DMA((2,))]),
        out_shape=jax.ShapeDtypeStruct((n_rows, d), F32),
        input_output_aliases={3: 0},
        compiler_params=_cparams(("arbitrary",)),
        name="moe_scatter",
    )(d0, d1, h, init)


def _moe_ffn_kernel(te_ref, nv_ref, xs_ref, wg_ref, wu_ref, wd_ref, o_ref):
    del te_ref
    i = pl.program_id(0)

    @pl.when(i < nv_ref[0])
    def _():
        o_ref[...] = _swiglu_partial(xs_ref[...].astype(BF16), wg_ref[0], wu_ref[0], wd_ref[0])

    @pl.when(i >= nv_ref[0])
    def _():
        o_ref[...] = jnp.zeros_like(o_ref)


def _moe_ffn(xs, tile_expert, n_valid, w_gu, w_down):
    rows, d = xs.shape
    f = w_down.shape[1]
    return pl.pallas_call(
        _moe_ffn_kernel,
        grid_spec=pltpu.PrefetchScalarGridSpec(
            num_scalar_prefetch=2,
            grid=(rows // MOE_TILE,),
            in_specs=[pl.BlockSpec((MOE_TILE, d), lambda i, te, nv: (i, 0)),
                      pl.BlockSpec((1, d, f), lambda i, te, nv: (te[i], 0, 0), pipeline_mode=_RESIDENT),
                      pl.BlockSpec((1, d, f), lambda i, te, nv: (te[i], 0, 1), pipeline_mode=_RESIDENT),
                      pl.BlockSpec((1, f, d), lambda i, te, nv: (te[i], 0, 0), pipeline_mode=_RESIDENT)],
            out_specs=pl.BlockSpec((MOE_TILE, d), lambda i, te, nv: (i, 0))),
        out_shape=jax.ShapeDtypeStruct((rows, d), F32),
        compiler_params=_cparams(("arbitrary",)),
        name="moe_ffn",
    )(tile_expert, n_valid, xs, w_gu, w_gu, w_down)


def _combine_kernel(d0_ref, d1_ref, x_ref, mod_ref, rw_ref, y_hbm, o_ref, ya_ref, yb_ref, sem):
    base = pl.program_id(0) * MOE_DMA_TILE

    def issue(r, carry):
        t = base + r
        pltpu.make_async_copy(_row(y_hbm, d0_ref[t]), _row(ya_ref, r), sem.at[0]).start()
        pltpu.make_async_copy(_row(y_hbm, d1_ref[t]), _row(yb_ref, r), sem.at[1]).start()
        return carry

    lax.fori_loop(0, MOE_DMA_TILE, issue, 0, unroll=8)
    rw = rw_ref[...]
    lane = lax.broadcasted_iota(jnp.int32, rw.shape, 1)
    w1 = jnp.sum(jnp.where(lane == 0, rw, 0.0), axis=-1, keepdims=True)
    w2 = jnp.sum(jnp.where(lane == 1, rw, 0.0), axis=-1, keepdims=True)
    pltpu.make_async_copy(y_hbm.at[pl.ds(0, MOE_DMA_TILE)], ya_ref, sem.at[0]).wait()
    pltpu.make_async_copy(y_hbm.at[pl.ds(0, MOE_DMA_TILE)], yb_ref, sem.at[1]).wait()
    _gated_add(x_ref, mod_ref, w1 * ya_ref[...] + w2 * yb_ref[...], 5, o_ref)


def _combine(d0, d1, x, modtab, rw, y):
    n, d = x.shape
    tm = MOE_DMA_TILE
    return pl.pallas_call(
        _combine_kernel,
        grid_spec=pltpu.PrefetchScalarGridSpec(
            num_scalar_prefetch=2,
            grid=(n // tm,),
            in_specs=[pl.BlockSpec((tm, d), lambda g, d0, d1: (g, 0)),
                      pl.BlockSpec((tm // SEG, N_MOD, d), lambda g, d0, d1: (g, 0, 0)),
                      pl.BlockSpec((tm, HEAD_PAD), lambda g, d0, d1: (g, 0)),
                      pl.BlockSpec(memory_space=pl.ANY)],
            out_specs=pl.BlockSpec((tm, d), lambda g, d0, d1: (g, 0)),
            scratch_shapes=[pltpu.VMEM((tm, d), F32), pltpu.VMEM((tm, d), F32), pltpu.SemaphoreType.DMA((2,))]),
        out_shape=jax.ShapeDtypeStruct((n, d), F32),
        compiler_params=_cparams(("arbitrary",)),
        name="moe_combine",
    )(d0, d1, x, modtab, rw, y)


def _moe(x, modtab, g2, w_router, b_router, w_gu, w_down):
    n = x.shape[0]
    h, rw, ri = _router(x, modtab, g2, w_router, b_router)
    d0, d1, tile_expert, n_valid, n_tiles = _moe_plan(ri, n)
    xs = _scatter_rows(d0, d1, h, n_tiles * MOE_TILE)
    ys = _moe_ffn(xs, tile_expert, n_valid, w_gu, w_down)
    return _combine(d0, d1, x, modtab, rw, ys)


def _final_kernel(x_ref, g_ref, o_ref):
    o_ref[0] = _rms(x_ref[0], g_ref[...])


def _final_norm(x3, g, tm=512):
    b, _, d = x3.shape
    return pl.pallas_call(
        _final_kernel,
        grid=(b, SEQ // tm),
        in_specs=[pl.BlockSpec((1, tm, d), lambda bi, t: (bi, t, 0)), pl.BlockSpec(g.shape, lambda bi, t: (0, 0))],
        out_specs=pl.BlockSpec((1, tm, d), lambda bi, t: (bi, t, 0)),
        out_shape=jax.ShapeDtypeStruct((b, SEQ, d), F32),
        compiler_params=_cparams(("parallel", "parallel")),
        name="final_norm",
    )(x3, g)


def kernel(x, c, ctx, c_ctx, norm1_g, norm2_g, ada_w, ada_b, final_norm_g, mla_w_dq, mla_q_norm, mla_w_uq, mla_w_dkv, mla_kv_norm, mla_w_ukv, mla_w_o, s5_a_re, s5_a_im, s5_log_dt, s5_b_re, s5_b_im, s5_c_re, s5_c_im, s5_d, s5_w_glu, pool_w, pool_scale, ffn_w_gu, ffn_w_down, moe_w_router, moe_b_router, moe_w_gu, moe_w_down):
    b, l, d = x.shape
    n = b * T_TOK
    xs = jnp.concatenate([x, ctx], axis=1).reshape(n, d)

    rows = -(-(b + 1) // 8) * 8
    cc = jnp.concatenate([c, c_ctx[None], jnp.zeros((rows - b - 1, d), F32)], axis=0)
    mods = _ada(cc, ada_w, ada_b)
    seg_src = np.array([bi if sj < LAT_SEGS else b for bi in range(b) for sj in range(SEGS_PER_BATCH)], np.int32)

    rope = _rope_tables()
    for i in range(DEPTH):
        last = i == DEPTH - 1
        modtab = mods[i].reshape(rows, N_MOD, d)[seg_src]
        g1 = norm1_g[i].reshape(1, d)
        g2 = norm2_g[i].reshape(1, d)
        kind = i % N_MIXERS
        j = i // N_MIXERS
        if kind == 0:
            pw = _mla_weights(mla_w_dq[j], mla_q_norm[j], mla_w_uq[j], mla_w_dkv[j], mla_kv_norm[j], mla_w_ukv[j])
            q, k, v = _mla_proj(xs, modtab, g1, pw, rope)
            o = _attention(q.reshape(b, T_TOK, -1), k.reshape(b, T_TOK, -1), v.reshape(b, T_TOK, -1))
            xs = _proj_res(xs, modtab, o.reshape(n, -1), mla_w_o[j].astype(BF16))
        elif kind == 1:
            mod9 = mods[i].reshape(rows, N_MOD, d)
            modsel = jnp.stack([mod9[:b], jnp.broadcast_to(mod9[b], (b, N_MOD, d))])
            x3 = xs.reshape(b, T_TOK, d)
            sp = _s5_params(s5_a_re[j], s5_a_im[j], s5_log_dt[j], s5_b_re[j], s5_b_im[j], s5_c_re[j], s5_c_im[j], s5_d[j])
            y = _s5_core(_s5_in(x3, modsel, g1), sp, b)
            xs = _s5_out(x3, modsel, y, s5_w_glu[j].astype(BF16)).reshape(n, d)
        else:
            xs = _pool(xs, modtab, g1, pool_w[j].astype(BF16), pool_scale[j].reshape(1, d))
        kk = i // 2
        if i % 2 == 0:
            xs = _ffn(xs, modtab, g2, ffn_w_gu[kk].astype(BF16), ffn_w_down[kk].astype(BF16))
        else:
            xs = _moe(xs, modtab, g2, moe_w_router[kk], moe_b_router[kk], moe_w_gu[kk].astype(BF16), moe_w_down[kk].astype(BF16))
    return _final_norm(xs.reshape(b, T_TOK, d), final_norm_g.reshape(1, d))
```

```python
import functools
import math

import jax
import jax.numpy as jnp
import numpy as np
from jax import lax
from jax.experimental import pallas as pl
from jax.experimental.pallas import tpu as pltpu

F32 = jnp.float32
BF16 = jnp.bfloat16

D_MODEL = 1024
SEQ = 4096
DEPTH = 4
GRID_W = 64
CTX_LEN = 256
N_MIXERS = 3
NORM_EPS = 1e-6

MLA_HEADS = 16
MLA_Q_LORA = 384
MLA_KV_LORA = 256
MLA_NOPE = 64
MLA_ROPE = 32
MLA_V = 64
MLA_QK = MLA_NOPE + MLA_ROPE
ROPE_AXIS_FREQS = MLA_ROPE // 4
ROPE_THETA = 10000.0
HEAD_PAD = 128

S5_GROUP = 16
S5_GROUPS = D_MODEL // S5_GROUP
S5_STATE = 64
S5_MAX_RE = -1e-4
S5_CHUNK = 16

POOL_WINDOWS = (2, 4, 8, 16)
POOL_GROUP = D_MODEL // len(POOL_WINDOWS)

FFN_DIM = 2816
N_EXPERTS = 8
TOP_K = 2

assert DEPTH % 2 == 0
T_TOK = SEQ + CTX_LEN
SEG = CTX_LEN
SEGS_PER_BATCH = T_TOK // SEG
LAT_SEGS = SEQ // SEG
N_MOD = 6

VMEM_LIMIT = 56 * 1024 * 1024


def _cparams(sem, vmem=VMEM_LIMIT):
    return pltpu.CompilerParams(dimension_semantics=sem, vmem_limit_bytes=vmem)


def _rms(x, g):
    return x * lax.rsqrt(jnp.mean(x * x, axis=-1, keepdims=True) + NORM_EPS) * g


def _norm_mod(x, g, shift, scale):
    return _rms(x, g) * (1.0 + scale) + shift


def _norm_mod_tile(x_ref, mod_ref, g_ref, shift_idx, out_dtype=BF16):
    parts = []
    for s in range(x_ref.shape[0] // SEG):
        m = mod_ref[s]
        x = x_ref[s * SEG:(s + 1) * SEG, :]
        parts.append(_norm_mod(x, g_ref[...], m[shift_idx:shift_idx + 1], m[shift_idx + 1:shift_idx + 2]).astype(out_dtype))
    return parts[0] if len(parts) == 1 else jnp.concatenate(parts, axis=0)


def _ada_kernel(c_ref, w_ref, b_ref, o_ref):
    s = jax.nn.silu(c_ref[...])
    o_ref[0] = jnp.dot(s, w_ref[0], precision=lax.Precision.HIGHEST, preferred_element_type=F32) + b_ref[0]


def _ada(cc, ada_w, ada_b):
    depth, d, n6 = ada_w.shape
    rows = cc.shape[0]
    tn = 1024
    return pl.pallas_call(
        _ada_kernel,
        grid=(depth, n6 // tn),
        in_specs=[pl.BlockSpec((rows, d), lambda i, j: (0, 0)),
                  pl.BlockSpec((1, d, tn), lambda i, j: (i, 0, j)),
                  pl.BlockSpec((1, 1, tn), lambda i, j: (i, 0, j))],
        out_specs=pl.BlockSpec((1, rows, tn), lambda i, j: (i, 0, j)),
        out_shape=jax.ShapeDtypeStruct((depth, rows, n6), F32),
        compiler_params=_cparams(("arbitrary", "arbitrary")),
        name="ada",
    )(cc, ada_w, ada_b.reshape(depth, 1, n6))


_W1_COLS = MLA_Q_LORA + MLA_KV_LORA + 2 * HEAD_PAD
_QW = MLA_HEADS * HEAD_PAD


def _mla_proj_kernel(x_ref, mod_ref, g1_ref, w1_ref, qn_ref, wq_ref, kvn_ref, wk_ref, we_ref, wv_ref,
                     cq_ref, sq_ref, ck_ref, sk_ref, q_ref, k_ref, v_ref):
    h = _norm_mod_tile(x_ref, mod_ref, g1_ref, 0)
    d = jnp.dot(h, w1_ref[...], preferred_element_type=F32)
    dq = d[:, :MLA_Q_LORA]
    ckv = d[:, MLA_Q_LORA:MLA_Q_LORA + MLA_KV_LORA]
    kr = d[:, MLA_Q_LORA + MLA_KV_LORA:MLA_Q_LORA + MLA_KV_LORA + HEAD_PAD]
    kr_sw = d[:, MLA_Q_LORA + MLA_KV_LORA + HEAD_PAD:]
    qn = _rms(dq, qn_ref[...]).astype(BF16)
    qq = jnp.dot(qn, wq_ref[...], preferred_element_type=F32)
    cq = cq_ref[...]
    sq = sq_ref[...]
    for hd in range(MLA_HEADS):
        lo = hd * HEAD_PAD
        q_ref[:, lo:lo + HEAD_PAD] = (qq[:, lo:lo + HEAD_PAD] * cq + qq[:, _QW + lo:_QW + lo + HEAD_PAD] * sq).astype(BF16)
    c = _rms(ckv, kvn_ref[...]).astype(BF16)
    kr_roped = (kr * ck_ref[...] + kr_sw * sk_ref[...]).astype(BF16)
    k = jnp.dot(c, wk_ref[...], preferred_element_type=F32) + jnp.dot(kr_roped, we_ref[...], preferred_element_type=F32)
    k_ref[...] = k.astype(BF16)
    v = jnp.dot(c, wv_ref[...], preferred_element_type=F32)
    lane = lax.broadcasted_iota(jnp.int32, v.shape, 1)
    v_ref[...] = jnp.where(lane % HEAD_PAD == MLA_V, 1.0, v).astype(BF16)


def _mla_proj(x, modtab, g1, pw, rope):
    n, d = x.shape
    tm = SEG
    full = lambda a: pl.BlockSpec(a.shape, lambda g: (0,) * a.ndim)
    pos = lambda g: (g % SEGS_PER_BATCH, 0)
    cq, sq, ck, sk = rope
    return pl.pallas_call(
        _mla_proj_kernel,
        grid=(n // tm,),
        in_specs=[pl.BlockSpec((tm, d), lambda g: (g, 0)),
                  pl.BlockSpec((tm // SEG, N_MOD, d), lambda g: (g, 0, 0)),
                  full(g1), full(pw["w1"]), full(pw["qn"]), full(pw["wq"]), full(pw["kvn"]),
                  full(pw["wk"]), full(pw["we"]), full(pw["wv"]),
                  pl.BlockSpec((tm, HEAD_PAD), pos), pl.BlockSpec((tm, HEAD_PAD), pos),
                  pl.BlockSpec((tm, HEAD_PAD), pos), pl.BlockSpec((tm, HEAD_PAD), pos)],
        out_specs=[pl.BlockSpec((tm, _QW), lambda g: (g, 0)),
                   pl.BlockSpec((tm, _QW), lambda g: (g, 0)),
                   pl.BlockSpec((tm, _QW), lambda g: (g, 0))],
        out_shape=[jax.ShapeDtypeStruct((n, _QW), BF16)] * 3,
        compiler_params=_cparams(("parallel",)),
        name="mla_proj",
    )(x, modtab, g1, pw["w1"], pw["qn"], pw["wq"], pw["kvn"], pw["wk"], pw["we"], pw["wv"], cq, sq, ck, sk)


def _mla_weights(w_dq, q_norm, w_uq, w_dkv, kv_norm, w_ukv):
    d = w_dq.shape[0]
    swap = np.arange(MLA_ROPE) ^ ROPE_AXIS_FREQS
    w_kr = w_dkv[:, MLA_KV_LORA:]
    padr = lambda w: jnp.pad(w, ((0, 0), (0, HEAD_PAD - MLA_ROPE)))
    w1 = jnp.concatenate([w_dq, w_dkv[:, :MLA_KV_LORA], padr(w_kr), padr(w_kr[:, swap])], axis=1)
    uq = w_uq.reshape(MLA_Q_LORA, MLA_HEADS, MLA_QK)
    zpad = jnp.zeros((MLA_Q_LORA, MLA_HEADS, HEAD_PAD - MLA_QK), w_uq.dtype)
    wq_main = jnp.concatenate([uq, zpad], axis=-1).reshape(MLA_Q_LORA, _QW)
    uq_sw = jnp.concatenate([jnp.zeros((MLA_Q_LORA, MLA_HEADS, MLA_NOPE), w_uq.dtype),
                             uq[:, :, MLA_NOPE:][:, :, swap], zpad], axis=-1).reshape(MLA_Q_LORA, _QW)
    wq = jnp.concatenate([wq_main, uq_sw], axis=1)
    ukv = w_ukv.reshape(MLA_KV_LORA, MLA_HEADS, MLA_NOPE + MLA_V)
    wk = jnp.concatenate([ukv[:, :, :MLA_NOPE],
                          jnp.zeros((MLA_KV_LORA, MLA_HEADS, HEAD_PAD - MLA_NOPE), w_ukv.dtype)], axis=-1)
    wk = wk.reshape(MLA_KV_LORA, _QW)
    wv = jnp.concatenate([ukv[:, :, MLA_NOPE:],
                          jnp.zeros((MLA_KV_LORA, MLA_HEADS, HEAD_PAD - MLA_V), w_ukv.dtype)], axis=-1)
    wv = wv.reshape(MLA_KV_LORA, _QW)
    e = np.zeros((HEAD_PAD, MLA_HEADS, HEAD_PAD), np.float32)
    for r in range(MLA_ROPE):
        e[r, :, MLA_NOPE + r] = 1.0
    we = jnp.asarray(e.reshape(HEAD_PAD, _QW))
    return {"w1": w1.astype(BF16), "qn": q_norm.reshape(1, -1), "wq": wq.astype(BF16),
            "kvn": kv_norm.reshape(1, -1), "wk": wk.astype(BF16), "we": we.astype(BF16), "wv": wv.astype(BF16)}


def _rope_tables():
    rows = SEQ // GRID_W
    row = jnp.repeat(jnp.arange(rows, dtype=F32), GRID_W)
    col = jnp.tile(jnp.arange(GRID_W, dtype=F32), rows)
    inv_freq = 1.0 / (ROPE_THETA ** (jnp.arange(ROPE_AXIS_FREQS, dtype=F32) / ROPE_AXIS_FREQS))
    ang = jnp.stack([row[:, None] * inv_freq, col[:, None] * inv_freq], axis=1)
    ang = jnp.concatenate([ang, jnp.zeros((CTX_LEN, 2, ROPE_AXIS_FREQS), F32)], axis=0)
    cos = jnp.cos(ang)
    sin = jnp.sin(ang)
    c32 = jnp.stack([cos, cos], axis=2).reshape(T_TOK, MLA_ROPE)
    s32 = jnp.stack([-sin, sin], axis=2).reshape(T_TOK, MLA_ROPE)
    scale = MLA_QK ** -0.5 * math.log2(math.e)
    zq = jnp.zeros((T_TOK, HEAD_PAD - MLA_QK), F32)
    cq = jnp.concatenate([jnp.full((T_TOK, MLA_NOPE), scale, F32), c32 * scale, zq], axis=1)
    sq = jnp.concatenate([jnp.zeros((T_TOK, MLA_NOPE), F32), s32 * scale, zq], axis=1)
    zk = jnp.zeros((T_TOK, HEAD_PAD - MLA_ROPE), F32)
    ck = jnp.concatenate([c32, zk], axis=1)
    sk = jnp.concatenate([s32, zk], axis=1)
    return cq, sq, ck, sk


ATTN_TQ = 1024
ATTN_KEY_CHUNK = 2048
_LAT_CHUNKS = tuple((lo, ATTN_KEY_CHUNK) for lo in range(0, SEQ, ATTN_KEY_CHUNK))
_CTX_CHUNK = ((SEQ, CTX_LEN),)


def _attend_pair(q, k_ref, v_ref, chunks):
    m = [None, None]
    acc = [None, None]
    for lo, size in chunks:
        for j in range(2):
            cols = slice(j * HEAD_PAD, (j + 1) * HEAD_PAD)
            s = lax.dot_general(q[:, cols], k_ref[0, lo:lo + size, cols], (((1,), (1,)), ((), ())), preferred_element_type=F32)
            mc = jnp.max(s, axis=-1, keepdims=True)
            m_new = mc if m[j] is None else jnp.maximum(m[j], mc)
            pv = jnp.dot(jnp.exp2(s - m_new).astype(BF16), v_ref[0, lo:lo + size, cols], preferred_element_type=F32)
            acc[j] = pv if m[j] is None else acc[j] * jnp.exp2(m[j] - m_new) + pv
            m[j] = m_new
    outs = [a / a[:, MLA_V:MLA_V + 1] for a in acc]
    lane = lax.broadcasted_iota(jnp.int32, outs[0].shape, 1)
    return jnp.where(lane < MLA_V, outs[0], pltpu.roll(outs[1], MLA_V, 1))


def _attn_kernel(q_ref, k_ref, v_ref, o_ref, *, lat_tiles):
    qi = pl.program_id(2)

    @pl.when(qi < lat_tiles)
    def _():
        o_ref[0] = _attend_pair(q_ref[0], k_ref, v_ref, _LAT_CHUNKS + _CTX_CHUNK).astype(BF16)

    @pl.when(qi >= lat_tiles)
    def _():
        o_ref[0, :CTX_LEN, :] = _attend_pair(q_ref[0, :CTX_LEN, :], k_ref, v_ref, _CTX_CHUNK).astype(BF16)


def _attention(q, k, v):
    b = q.shape[0]
    tq = ATTN_TQ
    lat_tiles = SEQ // tq
    return pl.pallas_call(
        functools.partial(_attn_kernel, lat_tiles=lat_tiles),
        grid=(b, MLA_HEADS // 2, lat_tiles + 1),
        in_specs=[pl.BlockSpec((1, tq, 2 * HEAD_PAD), lambda bi, hp, qi: (bi, qi, hp)),
                  pl.BlockSpec((1, T_TOK, 2 * HEAD_PAD), lambda bi, hp, qi: (bi, 0, hp)),
                  pl.BlockSpec((1, T_TOK, 2 * HEAD_PAD), lambda bi, hp, qi: (bi, 0, hp))],
        out_specs=pl.BlockSpec((1, tq, 2 * MLA_V), lambda bi, hp, qi: (bi, qi, hp)),
        out_shape=jax.ShapeDtypeStruct((b, T_TOK, MLA_HEADS * MLA_V), BF16),
        compiler_params=_cparams(("parallel", "parallel", "arbitrary")),
        name="attention",
    )(q, k, v)


def _gated_add(x_ref, mod_ref, y, gate_idx, o_ref):
    for s in range(x_ref.shape[0] // SEG):
        rows = slice(s * SEG, (s + 1) * SEG)
        g = mod_ref[s][gate_idx:gate_idx + 1]
        o_ref[rows, :] = x_ref[rows, :] + g * y[rows, :]


def _proj_res_kernel(x_ref, mod_ref, y_ref, w_ref, o_ref):
    y = jnp.dot(y_ref[...], w_ref[...], preferred_element_type=F32)
    _gated_add(x_ref, mod_ref, y, 2, o_ref)


def _proj_res(x, modtab, y, w, tm=512):
    n, d = x.shape
    return pl.pallas_call(
        _proj_res_kernel,
        grid=(n // tm,),
        in_specs=[pl.BlockSpec((tm, d), lambda g: (g, 0)),
                  pl.BlockSpec((tm // SEG, N_MOD, d), lambda g: (g, 0, 0)),
                  pl.BlockSpec((tm, y.shape[1]), lambda g: (g, 0)),
                  pl.BlockSpec(w.shape, lambda g: (0, 0))],
        out_specs=pl.BlockSpec((tm, d), lambda g: (g, 0)),
        out_shape=jax.ShapeDtypeStruct((n, d), F32),
        compiler_params=_cparams(("parallel",)),
        name="proj_res",
    )(x, modtab, y, w)


_S5_LAT_CHUNKS = SEQ // S5_CHUNK
_S5_CHUNKS = T_TOK // S5_CHUNK
_S5_W = S5_CHUNK * S5_GROUP
_S5_BLK_CHUNKS = 2
_S5_BLK_TOK = _S5_BLK_CHUNKS * S5_CHUNK
LANES = 128
_SLOTS = LANES // S5_GROUP


def _s5_in_kernel(x_ref, mod_ref, g_ref, u_ref, h_scr, *, nb):
    for b in range(nb):
        m = mod_ref[0, b]
        h = _norm_mod(x_ref[b], g_ref[...], m[0:1], m[1:2])
        for j in range(D_MODEL // LANES):
            h_scr[j, b * _S5_BLK_TOK:(b + 1) * _S5_BLK_TOK, :] = h[:, j * LANES:(j + 1) * LANES]
    slot = lax.broadcasted_iota(jnp.int32, (_S5_BLK_CHUNKS * nb, LANES), 1) // S5_GROUP
    for j in range(D_MODEL // LANES):
        for half in range(_S5_W // LANES):
            srcs = []
            for p in range(_SLOTS):
                t = _SLOTS * half + p
                rows = [h_scr[j, pl.ds(c * S5_CHUNK + t, nb, stride=_S5_BLK_TOK), :] for c in range(_S5_BLK_CHUNKS)]
                srcs.append(jnp.concatenate(rows, axis=0))
            for gs in range(_SLOTS):
                acc = None
                for p in range(_SLOTS):
                    k = (p - gs) % _SLOTS
                    r = pltpu.roll(srcs[p], k * S5_GROUP, 1) if k else srcs[p]
                    acc = r if acc is None else jnp.where(slot == p, r, acc)
                u_ref[_SLOTS * j + gs, :, half * LANES:(half + 1) * LANES] = acc.astype(BF16)


def _s5_in(x3, modsel, g1):
    nb, t, d = x3.shape
    nblk = t // _S5_BLK_TOK
    rows = _S5_BLK_CHUNKS * nb
    return pl.pallas_call(
        functools.partial(_s5_in_kernel, nb=nb),
        grid=(nblk,),
        in_specs=[pl.BlockSpec((nb, _S5_BLK_TOK, d), lambda k: (0, k, 0)),
                  pl.BlockSpec((1, nb, N_MOD, d), lambda k: (k // (SEQ // _S5_BLK_TOK), 0, 0, 0)),
                  pl.BlockSpec(g1.shape, lambda k: (0, 0))],
        out_specs=pl.BlockSpec((S5_GROUPS, rows, _S5_W), lambda k: (0, k, 0)),
        out_shape=jax.ShapeDtypeStruct((S5_GROUPS, nblk * rows, _S5_W), BF16),
        scratch_shapes=[pltpu.VMEM((d // LANES, nb * _S5_BLK_TOK, LANES), F32)],
        compiler_params=_cparams(("parallel",)),
        name="s5_in",
    )(x3, modsel, g1)


def _s5_out_kernel(x_ref, mod_ref, y_ref, w_ref, o_ref, nat_scr, *, nb):
    d = x_ref.shape[2]
    slot = lax.broadcasted_iota(jnp.int32, (nb, LANES), 1) // S5_GROUP
    for j in range(d // LANES):
        for half in range(_S5_W // LANES):
            for c in range(_S5_BLK_CHUNKS):
                srcs = [y_ref[_SLOTS * j + gs, c * nb:(c + 1) * nb, half * LANES:(half + 1) * LANES] for gs in range(_SLOTS)]
                for p in range(_SLOTS):
                    acc = None
                    for gs in range(_SLOTS):
                        k = (gs - p) % _SLOTS
                        r = pltpu.roll(srcs[gs], k * S5_GROUP, 1) if k else srcs[gs]
                        acc = r if acc is None else jnp.where(slot == gs, r, acc)
                    nat_scr[j, pl.ds(c * S5_CHUNK + _SLOTS * half + p, nb, stride=_S5_BLK_TOK), :] = acc
    y_nat = jnp.concatenate([nat_scr[j] for j in range(d // LANES)], axis=1)
    g = jax.nn.gelu(y_nat).astype(BF16)
    z = jnp.dot(g, w_ref[...], preferred_element_type=F32)
    y = z[:, :d] * jax.nn.sigmoid(z[:, d:])
    for b in range(nb):
        gate = mod_ref[0, b][2:3]
        o_ref[b] = x_ref[b] + gate * y[b * _S5_BLK_TOK:(b + 1) * _S5_BLK_TOK, :]


def _s5_out(x3, modsel, y, w):
    nb, t, d = x3.shape
    nblk = t // _S5_BLK_TOK
    rows = _S5_BLK_CHUNKS * nb
    return pl.pallas_call(
        functools.partial(_s5_out_kernel, nb=nb),
        grid=(nblk,),
        in_specs=[pl.BlockSpec((nb, _S5_BLK_TOK, d), lambda k: (0, k, 0)),
                  pl.BlockSpec((1, nb, N_MOD, d), lambda k: (k // (SEQ // _S5_BLK_TOK), 0, 0, 0)),
                  pl.BlockSpec((S5_GROUPS, rows, _S5_W), lambda k: (0, k, 0)),
                  pl.BlockSpec(w.shape, lambda k: (0, 0))],
        out_specs=pl.BlockSpec((nb, _S5_BLK_TOK, d), lambda k: (0, k, 0)),
        out_shape=jax.ShapeDtypeStruct((nb, t, d), F32),
        scratch_shapes=[pltpu.VMEM((d // LANES, nb * _S5_BLK_TOK, LANES), F32)],
        compiler_params=_cparams(("parallel",)),
        name="s5_out",
    )(x3, modsel, y, w)


def _s5_kernel(u_ref, m_ref, win_ref, wof_ref, wor_ref, ar_ref, ai_ref, y_ref, z_ref, sf_ref, sr_ref, *, nb):
    u = u_ref[0]
    z_ref[...] = jnp.dot(u, win_ref[0], preferred_element_type=F32)
    half = _S5_W // 2
    ar = ar_ref[0]
    ai = ai_ref[0]
    lane = lax.broadcasted_iota(jnp.int32, (nb, _S5_W), 1)
    is_fwd = (lane % half) < S5_STATE

    def step(i, s):
        cf = jnp.where(i < _S5_CHUNKS - _S5_LAT_CHUNKS, i + _S5_LAT_CHUNKS, i - (_S5_CHUNKS - _S5_LAT_CHUNKS))
        cr = _S5_CHUNKS - 1 - i
        rf = pl.multiple_of(cf * nb, nb)
        rr = pl.multiple_of(cr * nb, nb)
        sf_ref[pl.ds(rf, nb), :] = s
        sr_ref[pl.ds(rr, nb), :] = s
        z = jnp.where(is_fwd, z_ref[pl.ds(rf, nb), :], z_ref[pl.ds(rr, nb), :])
        re = s[:, :half]
        im = s[:, half:]
        return jnp.concatenate([ar * re - ai * im + z[:, :half], ar * im + ai * re + z[:, half:]], axis=1)

    lax.fori_loop(0, _S5_CHUNKS, step, jnp.zeros((nb, _S5_W), F32))
    y = jnp.dot(u, m_ref[0], preferred_element_type=F32)
    y = y + jnp.dot(sf_ref[...].astype(BF16), wof_ref[0], preferred_element_type=F32)
    y = y + jnp.dot(sr_ref[...].astype(BF16), wor_ref[0], preferred_element_type=F32)
    y_ref[0] = y


def _s5_core(u, sp, nb):
    g, rows, w = u.shape
    blk = lambda a: pl.BlockSpec((1,) + a.shape[1:], lambda i: (i,) + (0,) * (a.ndim - 1))
    return pl.pallas_call(
        functools.partial(_s5_kernel, nb=nb),
        grid=(g,),
        in_specs=[blk(u), blk(sp["m"]), blk(sp["win"]), blk(sp["wof"]), blk(sp["wor"]), blk(sp["ar"]), blk(sp["ai"])],
        out_specs=pl.BlockSpec((1, rows, w), lambda i: (i, 0, 0)),
        out_shape=jax.ShapeDtypeStruct((g, rows, w), F32),
        scratch_shapes=[pltpu.VMEM((rows, w), F32), pltpu.VMEM((rows, w), F32), pltpu.VMEM((rows, w), F32)],
        compiler_params=_cparams(("parallel",)),
        name="s5_core",
    )(u, sp["m"], sp["win"], sp["wof"], sp["wor"], sp["ar"], sp["ai"])


def _s5_params(a_re, a_im, log_dt, b_re, b_im, c_re, c_im, d):
    c64 = jnp.complex64
    lam = lax.complex(jnp.minimum(a_re, S5_MAX_RE), a_im)
    lam_dt = lam * jnp.exp(log_dt)[..., None]
    lam_bar = jnp.exp(lam_dt)
    b_bar = ((lam_bar - 1.0) / lam)[..., None] * lax.complex(b_re, b_im)
    c_mat = lax.complex(c_re, c_im)
    taus = jnp.arange(S5_CHUNK + 1, dtype=F32)
    pw = jnp.exp(lam_dt[None] * taus[:, None, None, None].astype(c64))
    hi = lax.Precision.HIGHEST
    kern = jnp.real(jnp.einsum('dgnp,tdgp,dgpm->dgtnm', c_mat, pw[:S5_CHUNK], b_bar, precision=hi))
    eye = jnp.eye(S5_GROUP, dtype=F32)
    k0 = kern[0, :, 0] + kern[1, :, 0] + d.reshape(S5_GROUPS, S5_GROUP)[:, :, None] * eye
    lags = jnp.concatenate([kern[1, :, :0:-1], k0[:, None], kern[0, :, 1:]], axis=1)
    idx = (np.arange(S5_CHUNK)[None, :] - np.arange(S5_CHUNK)[:, None]) + S5_CHUNK - 1
    m = lags[:, idx]
    m = m.transpose(0, 1, 4, 2, 3).reshape(S5_GROUPS, _S5_W, _S5_W)
    wf = pw[S5_CHUNK - 1::-1][:S5_CHUNK, 0][..., None] * b_bar[0][None]
    wr = pw[:S5_CHUNK, 1][..., None] * b_bar[1][None]
    to_rows = lambda w: w.transpose(1, 0, 3, 2).reshape(S5_GROUPS, _S5_W, S5_STATE)
    wf, wr = to_rows(wf), to_rows(wr)
    win = jnp.concatenate([jnp.real(wf), jnp.real(wr), jnp.imag(wf), jnp.imag(wr)], axis=-1)
    of = c_mat[0][None] * pw[1:, 0][:, :, None, :]
    orv = c_mat[1][None] * pw[S5_CHUNK:0:-1, 1][:, :, None, :]
    to_cols = lambda w: w.transpose(1, 3, 0, 2).reshape(S5_GROUPS, S5_STATE, _S5_W)
    of, orv = to_cols(of), to_cols(orv)
    zeros = jnp.zeros_like(jnp.real(of))
    wof = jnp.concatenate([jnp.real(of), zeros, -jnp.imag(of), zeros], axis=1)
    wor = jnp.concatenate([zeros, jnp.real(orv), zeros, -jnp.imag(orv)], axis=1)
    a16 = pw[S5_CHUNK]
    ar = jnp.concatenate([jnp.real(a16[0]), jnp.real(a16[1])], axis=-1)[:, None, :]
    ai = jnp.concatenate([jnp.imag(a16[0]), jnp.imag(a16[1])], axis=-1)[:, None, :]
    return {"m": m.astype(BF16), "win": win.astype(BF16), "wof": wof.astype(BF16), "wor": wor.astype(BF16),
            "ar": ar, "ai": ai}


def _pool_kernel(xp_ref, xc_ref, xn_ref, mod_ref, g_ref, wp_ref, sc_ref, o_ref):
    seg = pl.program_id(0) % SEGS_PER_BATCH
    is_ctx = seg >= LAT_SEGS
    p0 = jnp.where(is_ctx, 0, seg * SEG)
    lseq = jnp.where(is_ctx, CTX_LEN, SEQ)
    m = mod_ref[0]
    g = g_ref[...]
    hs = [_norm_mod(r[...], g, m[0:1], m[1:2]) for r in (xp_ref, xc_ref, xn_ref)]
    hcat = jnp.concatenate([h.astype(BF16) for h in hs], axis=0)
    r = lax.broadcasted_iota(jnp.int32, (SEG, 3 * SEG), 0)
    s = lax.broadcasted_iota(jnp.int32, (SEG, 3 * SEG), 1)
    pt = p0 + r
    ps = p0 - SEG + s
    rcol = lax.broadcasted_iota(jnp.int32, (SEG, 1), 0) + p0
    x = xc_ref[...]
    gate = m[2:3]
    for gi, w in enumerate(POOL_WINDOWS):
        cols = slice(gi * POOL_GROUP, (gi + 1) * POOL_GROUP)
        lo = jnp.maximum(pt - w // 2, 0)
        hi = jnp.minimum(pt + w - w // 2, lseq)
        band = jnp.where((ps >= lo) & (ps < hi), 1.0, 0.0).astype(BF16)
        tot = jnp.dot(band, hcat[:, cols], preferred_element_type=F32)
        cnt = (jnp.minimum(rcol + w - w // 2, lseq) - jnp.maximum(rcol - w // 2, 0)).astype(F32)
        resid = tot / cnt - hs[1][:, cols]
        y = jnp.dot(resid.astype(BF16), wp_ref[gi], preferred_element_type=F32) * sc_ref[:, cols]
        o_ref[:, cols] = x[:, cols] + gate[:, cols] * y


def _pool(x, modtab, g1, w_pool, scale):
    n, d = x.shape
    nseg = n // SEG

    def prev(g):
        return (jnp.maximum(g - 1, 0), 0)

    def nxt(g):
        return (jnp.minimum(g + 1, nseg - 1), 0)

    return pl.pallas_call(
        _pool_kernel,
        grid=(nseg,),
        in_specs=[pl.BlockSpec((SEG, d), prev), pl.BlockSpec((SEG, d), lambda g: (g, 0)), pl.BlockSpec((SEG, d), nxt),
                  pl.BlockSpec((1, N_MOD, d), lambda g: (g, 0, 0)),
                  pl.BlockSpec(g1.shape, lambda g: (0, 0)),
                  pl.BlockSpec(w_pool.shape, lambda g: (0, 0, 0)),
                  pl.BlockSpec(scale.shape, lambda g: (0, 0))],
        out_specs=pl.BlockSpec((SEG, d), lambda g: (g, 0)),
        out_shape=jax.ShapeDtypeStruct((n, d), F32),
        compiler_params=_cparams(("parallel",)),
        name="pool",
    )(x, x, x, modtab, g1, w_pool, scale)


def _swiglu_partial(h, wg, wu, wd):
    a = jax.nn.silu(jnp.dot(h, wg, preferred_element_type=F32)) * jnp.dot(h, wu, preferred_element_type=F32)
    return jnp.dot(a.astype(BF16), wd, preferred_element_type=F32)


def _ffn_kernel(x_ref, mod_ref, g_ref, wg_ref, wu_ref, wd_ref, o_ref):
    h = _norm_mod_tile(x_ref, mod_ref, g_ref, 3)
    _gated_add(x_ref, mod_ref, _swiglu_partial(h, wg_ref[...], wu_ref[...], wd_ref[...]), 5, o_ref)


_RESIDENT = pl.Buffered(1)


def _ffn(x, modtab, g2, w_gu, w_down, tm=512):
    n, d = x.shape
    f = w_down.shape[0]
    return pl.pallas_call(
        _ffn_kernel,
        grid=(n // tm,),
        in_specs=[pl.BlockSpec((tm, d), lambda g: (g, 0)),
                  pl.BlockSpec((tm // SEG, N_MOD, d), lambda g: (g, 0, 0)),
                  pl.BlockSpec(g2.shape, lambda g: (0, 0)),
                  pl.BlockSpec((d, f), lambda g: (0, 0), pipeline_mode=_RESIDENT),
                  pl.BlockSpec((d, f), lambda g: (0, 1), pipeline_mode=_RESIDENT),
                  pl.BlockSpec((f, d), lambda g: (0, 0), pipeline_mode=_RESIDENT)],
        out_specs=pl.BlockSpec((tm, d), lambda g: (g, 0)),
        out_shape=jax.ShapeDtypeStruct((n, d), F32),
        compiler_params=_cparams(("parallel",)),
        name="ffn",
    )(x, modtab, g2, w_gu, w_gu, w_down)


MOE_TILE = 512
MOE_DMA_TILE = 512


def _router_kernel(x_ref, mod_ref, g_ref, w_ref, b_ref, h_ref, rw_ref, ri_ref, cnt_ref, run_ref):
    @pl.when(pl.program_id(0) == 0)
    def _():
        run_ref[...] = jnp.zeros_like(run_ref)

    h = _norm_mod_tile(x_ref, mod_ref, g_ref, 3, out_dtype=F32)
    h_ref[...] = h
    logits = jnp.dot(h, w_ref[...], precision=lax.Precision.HIGHEST, preferred_element_type=F32) + b_ref[...]
    lane = lax.broadcasted_iota(jnp.int32, logits.shape, 1)
    neg = jnp.float32(-jnp.inf)
    lg = jnp.where(lane < N_EXPERTS, logits, neg)
    m1 = jnp.max(lg, axis=-1, keepdims=True)
    i1 = jnp.min(jnp.where(lg == m1, lane, HEAD_PAD), axis=-1, keepdims=True)
    lg2 = jnp.where(lane == i1, neg, lg)
    m2 = jnp.max(lg2, axis=-1, keepdims=True)
    i2 = jnp.min(jnp.where(lg2 == m2, lane, HEAD_PAD), axis=-1, keepdims=True)
    e2 = jnp.exp(m2 - m1)
    den = 1.0 + e2
    tm = logits.shape[0]
    sel = jnp.where((lane == i1) | (lane == i2), 1.0, 0.0).astype(BF16)
    tri = jnp.where(lax.broadcasted_iota(jnp.int32, (tm, tm), 1) <= lax.broadcasted_iota(jnp.int32, (tm, tm), 0), 1.0, 0.0)
    within = jnp.dot(tri.astype(BF16), sel, preferred_element_type=F32)
    rank = run_ref[0:1, :] + within - 1.0
    r1 = jnp.sum(jnp.where(lane == i1, rank, 0.0), axis=-1, keepdims=True).astype(jnp.int32)
    r2 = jnp.sum(jnp.where(lane == i2, rank, 0.0), axis=-1, keepdims=True).astype(jnp.int32)
    total = run_ref[0:1, :] + within[tm - 1:tm, :]
    run_ref[...] = jnp.broadcast_to(total, run_ref.shape)
    cnt_ref[...] = jnp.broadcast_to(total, cnt_ref.shape).astype(jnp.int32)
    rw_ref[...] = jnp.where(lane == 0, 1.0 / den, jnp.where(lane == 1, e2 / den, 0.0))
    ri_ref[...] = jnp.where(lane == 0, i1, jnp.where(lane == 1, i2, jnp.where(lane == 2, r1, jnp.where(lane == 3, r2, 0))))


def _router(x, modtab, g2, w_router, b_router, tm=512):
    n, d = x.shape
    wr = jnp.pad(w_router, ((0, 0), (0, HEAD_PAD - N_EXPERTS)))
    br = jnp.pad(b_router, (0, HEAD_PAD - N_EXPERTS)).reshape(1, HEAD_PAD)
    return pl.pallas_call(
        _router_kernel,
        grid=(n // tm,),
        in_specs=[pl.BlockSpec((tm, d), lambda g: (g, 0)),
                  pl.BlockSpec((tm // SEG, N_MOD, d), lambda g: (g, 0, 0)),
                  pl.BlockSpec(g2.shape, lambda g: (0, 0)),
                  pl.BlockSpec(wr.shape, lambda g: (0, 0)),
                  pl.BlockSpec(br.shape, lambda g: (0, 0))],
        out_specs=[pl.BlockSpec((tm, d), lambda g: (g, 0)),
                   pl.BlockSpec((tm, HEAD_PAD), lambda g: (g, 0)),
                   pl.BlockSpec((tm, HEAD_PAD), lambda g: (g, 0)),
                   pl.BlockSpec((8, HEAD_PAD), lambda g: (0, 0))],
        out_shape=[jax.ShapeDtypeStruct((n, d), F32),
                   jax.ShapeDtypeStruct((n, HEAD_PAD), F32),
                   jax.ShapeDtypeStruct((n, HEAD_PAD), jnp.int32),
                   jax.ShapeDtypeStruct((8, HEAD_PAD), jnp.int32)],
        scratch_shapes=[pltpu.VMEM((8, HEAD_PAD), F32)],
        compiler_params=_cparams(("arbitrary",)),
        name="router",
    )(x, modtab, g2, wr, br)


def _moe_plan(ri, cnt, n):
    e = ri[:, :TOP_K]
    counts = cnt[0, :N_EXPERTS]
    padded = (counts + MOE_TILE - 1) // MOE_TILE * MOE_TILE
    ends = jnp.cumsum(padded)
    starts = ends - padded
    dest = starts[e] + ri[:, TOP_K:2 * TOP_K]
    n_tiles = (TOP_K * n) // MOE_TILE + N_EXPERTS
    n_valid = (ends[-1] // MOE_TILE).astype(jnp.int32).reshape(1)
    tile_ids = jnp.arange(n_tiles, dtype=jnp.int32)
    tile_expert = jnp.searchsorted(ends // MOE_TILE, jnp.minimum(tile_ids, n_valid - 1), side="right").astype(jnp.int32)
    return dest[:, 0].astype(jnp.int32), dest[:, 1].astype(jnp.int32), tile_expert, n_valid, n_tiles


def _row(ref, i):
    return ref.at[pl.ds(i, 1)]


def _scatter_rows_kernel(d0_ref, d1_ref, h_ref, init_hbm, o_hbm, sem):
    del init_hbm
    base = pl.program_id(0) * MOE_DMA_TILE

    def issue(r, carry):
        t = base + r
        pltpu.make_async_copy(_row(h_ref, r), _row(o_hbm, d0_ref[t]), sem.at[0]).start()
        pltpu.make_async_copy(_row(h_ref, r), _row(o_hbm, d1_ref[t]), sem.at[1]).start()
        return carry

    lax.fori_loop(0, MOE_DMA_TILE, issue, 0, unroll=8)
    pltpu.make_async_copy(h_ref, o_hbm.at[pl.ds(0, MOE_DMA_TILE)], sem.at[0]).wait()
    pltpu.make_async_copy(h_ref, o_hbm.at[pl.ds(0, MOE_DMA_TILE)], sem.at[1]).wait()


def _scatter_rows(d0, d1, h, n_rows):
    n, d = h.shape
    init = jnp.zeros((n_rows, d), F32)
    return pl.pallas_call(
        _scatter_rows_kernel,
        grid_spec=pltpu.PrefetchScalarGridSpec(
            num_scalar_prefetch=2,
            grid=(n // MOE_DMA_TILE,),
            in_specs=[pl.BlockSpec((MOE_DMA_TILE, d), lambda g, d0, d1: (g, 0)), pl.BlockSpec(memory_space=pl.ANY)],
            out_specs=pl.BlockSpec(memory_space=pl.ANY),
            scratch_shapes=[pltpu.SemaphoreType.DMA((2,))]),
        out_shape=jax.ShapeDtypeStruct((n_rows, d), F32),
        input_output_aliases={3: 0},
        compiler_params=_cparams(("arbitrary",)),
        name="moe_scatter",
    )(d0, d1, h, init)


def _moe_ffn_kernel(te_ref, nv_ref, xs_ref, wg_ref, wu_ref, wd_ref, o_ref):
    del te_ref
    i = pl.program_id(0)

    @pl.when(i < nv_ref[0])
    def _():
        o_ref[...] = _swiglu_partial(xs_ref[...].astype(BF16), wg_ref[0], wu_ref[0], wd_ref[0])

    @pl.when(i >= nv_ref[0])
    def _():
        o_ref[...] = jnp.zeros_like(o_ref)


def _moe_ffn(xs, tile_expert, n_valid, w_gu, w_down):
    rows, d = xs.shape
    f = w_down.shape[1]
    return pl.pallas_call(
        _moe_ffn_kernel,
        grid_spec=pltpu.PrefetchScalarGridSpec(
            num_scalar_prefetch=2,
            grid=(rows // MOE_TILE,),
            in_specs=[pl.BlockSpec((MOE_TILE, d), lambda i, te, nv: (i, 0)),
                      pl.BlockSpec((1, d, f), lambda i, te, nv: (te[i], 0, 0), pipeline_mode=_RESIDENT),
                      pl.BlockSpec((1, d, f), lambda i, te, nv: (te[i], 0, 1), pipeline_mode=_RESIDENT),
                      pl.BlockSpec((1, f, d), lambda i, te, nv: (te[i], 0, 0), pipeline_mode=_RESIDENT)],
            out_specs=pl.BlockSpec((MOE_TILE, d), lambda i, te, nv: (i, 0))),
        out_shape=jax.ShapeDtypeStruct((rows, d), F32),
        compiler_params=_cparams(("arbitrary",)),
        name="moe_ffn",
    )(tile_expert, n_valid, xs, w_gu, w_gu, w_down)


def _combine_kernel(d0_ref, d1_ref, x_ref, mod_ref, rw_ref, y_hbm, *rest, tm, final):
    if final:
        fg_ref, o_ref, ya_ref, yb_ref, sem = rest
        tile = pl.program_id(0) * SEGS_PER_BATCH + pl.program_id(1)
    else:
        o_ref, ya_ref, yb_ref, sem = rest
        tile = pl.program_id(0)
    base = tile * tm

    def issue(r, carry):
        t = base + r
        pltpu.make_async_copy(_row(y_hbm, d0_ref[t]), _row(ya_ref, r), sem.at[0]).start()
        pltpu.make_async_copy(_row(y_hbm, d1_ref[t]), _row(yb_ref, r), sem.at[1]).start()
        return carry

    lax.fori_loop(0, tm, issue, 0, unroll=8)
    rw = rw_ref[...]
    lane = lax.broadcasted_iota(jnp.int32, rw.shape, 1)
    w1 = jnp.sum(jnp.where(lane == 0, rw, 0.0), axis=-1, keepdims=True)
    w2 = jnp.sum(jnp.where(lane == 1, rw, 0.0), axis=-1, keepdims=True)
    pltpu.make_async_copy(y_hbm.at[pl.ds(0, tm)], ya_ref, sem.at[0]).wait()
    pltpu.make_async_copy(y_hbm.at[pl.ds(0, tm)], yb_ref, sem.at[1]).wait()
    y = w1 * ya_ref[...] + w2 * yb_ref[...]
    if final:
        o_ref[0] = _rms(x_ref[...] + mod_ref[0][5:6] * y, fg_ref[...])
    else:
        _gated_add(x_ref, mod_ref, y, 5, o_ref)


def _combine(d0, d1, x, modtab, rw, y, final_g=None):
    n, d = x.shape
    final = final_g is not None
    tm = SEG if final else MOE_DMA_TILE
    if final:
        nb = n // T_TOK
        grid = (nb, LAT_SEGS)
        row = lambda bi, j, d0, d1: (bi * SEGS_PER_BATCH + j, 0)
        mrow = lambda bi, j, d0, d1: (bi * SEGS_PER_BATCH + j, 0, 0)
        extra = [pl.BlockSpec(final_g.shape, lambda bi, j, d0, d1: (0, 0))]
        out_spec = pl.BlockSpec((1, tm, d), lambda bi, j, d0, d1: (bi, j, 0))
        out_shape = jax.ShapeDtypeStruct((nb, SEQ, d), F32)
        args = (final_g,)
    else:
        grid = (n // tm,)
        row = lambda g, d0, d1: (g, 0)
        mrow = lambda g, d0, d1: (g, 0, 0)
        extra = []
        out_spec = pl.BlockSpec((tm, d), row)
        out_shape = jax.ShapeDtypeStruct((n, d), F32)
        args = ()
    return pl.pallas_call(
        functools.partial(_combine_kernel, tm=tm, final=final),
        grid_spec=pltpu.PrefetchScalarGridSpec(
            num_scalar_prefetch=2,
            grid=grid,
            in_specs=[pl.BlockSpec((tm, d), row),
                      pl.BlockSpec((tm // SEG, N_MOD, d), mrow),
                      pl.BlockSpec((tm, HEAD_PAD), row),
                      pl.BlockSpec(memory_space=pl.ANY)] + extra,
            out_specs=out_spec,
            scratch_shapes=[pltpu.VMEM((tm, d), F32), pltpu.VMEM((tm, d), F32), pltpu.SemaphoreType.DMA((2,))]),
        out_shape=out_shape,
        compiler_params=_cparams(("arbitrary",) * len(grid)),
        name="moe_combine",
    )(d0, d1, x, modtab, rw, y, *args)


def _moe(x, modtab, g2, w_router, b_router, w_gu, w_down, final_g=None):
    n = x.shape[0]
    h, rw, ri, cnt = _router(x, modtab, g2, w_router, b_router)
    d0, d1, tile_expert, n_valid, n_tiles = _moe_plan(ri, cnt, n)
    xs = _scatter_rows(d0, d1, h, n_tiles * MOE_TILE)
    ys = _moe_ffn(xs, tile_expert, n_valid, w_gu, w_down)
    return _combine(d0, d1, x, modtab, rw, ys, final_g)


def kernel(x, c, ctx, c_ctx, norm1_g, norm2_g, ada_w, ada_b, final_norm_g, mla_w_dq, mla_q_norm, mla_w_uq, mla_w_dkv, mla_kv_norm, mla_w_ukv, mla_w_o, s5_a_re, s5_a_im, s5_log_dt, s5_b_re, s5_b_im, s5_c_re, s5_c_im, s5_d, s5_w_glu, pool_w, pool_scale, ffn_w_gu, ffn_w_down, moe_w_router, moe_b_router, moe_w_gu, moe_w_down):
    b, l, d = x.shape
    n = b * T_TOK
    xs = jnp.concatenate([x, ctx], axis=1).reshape(n, d)

    rows = -(-(b + 1) // 8) * 8
    cc = jnp.concatenate([c, c_ctx[None], jnp.zeros((rows - b - 1, d), F32)], axis=0)
    mods = _ada(cc, ada_w, ada_b)
    seg_src = np.array([bi if sj < LAT_SEGS else b for bi in range(b) for sj in range(SEGS_PER_BATCH)], np.int32)

    rope = _rope_tables()
    for i in range(DEPTH):
        last = i == DEPTH - 1
        modtab = mods[i].reshape(rows, N_MOD, d)[seg_src]
        g1 = norm1_g[i].reshape(1, d)
        g2 = norm2_g[i].reshape(1, d)
        kind = i % N_MIXERS
        j = i // N_MIXERS
        if kind == 0:
            pw = _mla_weights(mla_w_dq[j], mla_q_norm[j], mla_w_uq[j], mla_w_dkv[j], mla_kv_norm[j], mla_w_ukv[j])
            q, k, v = _mla_proj(xs, modtab, g1, pw, rope)
            o = _attention(q.reshape(b, T_TOK, -1), k.reshape(b, T_TOK, -1), v.reshape(b, T_TOK, -1))
            xs = _proj_res(xs, modtab, o.reshape(n, -1), mla_w_o[j].astype(BF16))
        elif kind == 1:
            mod9 = mods[i].reshape(rows, N_MOD, d)
            modsel = jnp.stack([mod9[:b], jnp.broadcast_to(mod9[b], (b, N_MOD, d))])
            x3 = xs.reshape(b, T_TOK, d)
            sp = _s5_params(s5_a_re[j], s5_a_im[j], s5_log_dt[j], s5_b_re[j], s5_b_im[j], s5_c_re[j], s5_c_im[j], s5_d[j])
            y = _s5_core(_s5_in(x3, modsel, g1), sp, b)
            xs = _s5_out(x3, modsel, y, s5_w_glu[j].astype(BF16)).reshape(n, d)
        else:
            xs = _pool(xs, modtab, g1, pool_w[j].astype(BF16), pool_scale[j].reshape(1, d))
        kk = i // 2
        if i % 2 == 0:
            xs = _ffn(xs, modtab, g2, ffn_w_gu[kk].astype(BF16), ffn_w_down[kk].astype(BF16))
        else:
            xs = _moe(xs, modtab, g2, moe_w_router[kk], moe_b_router[kk], moe_w_gu[kk].astype(BF16), moe_w_down[kk].astype(BF16),
                      final_norm_g.reshape(1, d) if last else None)
    return xs
```

```python
import functools
import math

import jax
import jax.numpy as jnp
import numpy as np
from jax import lax
from jax.experimental import pallas as pl
from jax.experimental.pallas import tpu as pltpu

F32 = jnp.float32
BF16 = jnp.bfloat16

D_MODEL = 1024
SEQ = 4096
DEPTH = 4
GRID_W = 64
CTX_LEN = 256
N_MIXERS = 3
NORM_EPS = 1e-6

MLA_HEADS = 16
MLA_Q_LORA = 384
MLA_KV_LORA = 256
MLA_NOPE = 64
MLA_ROPE = 32
MLA_V = 64
MLA_QK = MLA_NOPE + MLA_ROPE
ROPE_AXIS_FREQS = MLA_ROPE // 4
ROPE_THETA = 10000.0
HEAD_PAD = 128

S5_GROUP = 16
S5_GROUPS = D_MODEL // S5_GROUP
S5_STATE = 64
S5_MAX_RE = -1e-4
S5_CHUNK = 16

POOL_WINDOWS = (2, 4, 8, 16)
POOL_GROUP = D_MODEL // len(POOL_WINDOWS)

FFN_DIM = 2816
N_EXPERTS = 8
TOP_K = 2

assert DEPTH % 2 == 0
T_TOK = SEQ + CTX_LEN
SEG = CTX_LEN
SEGS_PER_BATCH = T_TOK // SEG
LAT_SEGS = SEQ // SEG
N_MOD = 6

VMEM_LIMIT = 56 * 1024 * 1024


def _cparams(sem, vmem=VMEM_LIMIT):
    return pltpu.CompilerParams(dimension_semantics=sem, vmem_limit_bytes=vmem)


def _rms(x, g):
    return x * lax.rsqrt(jnp.mean(x * x, axis=-1, keepdims=True) + NORM_EPS) * g


def _norm_mod(x, g, shift, scale):
    return _rms(x, g) * (1.0 + scale) + shift


def _norm_mod_tile(x_ref, mod_ref, g_ref, shift_idx, out_dtype=BF16):
    parts = []
    for s in range(x_ref.shape[0] // SEG):
        m = mod_ref[s]
        x = x_ref[s * SEG:(s + 1) * SEG, :]
        parts.append(_norm_mod(x, g_ref[...], m[shift_idx:shift_idx + 1], m[shift_idx + 1:shift_idx + 2]).astype(out_dtype))
    return parts[0] if len(parts) == 1 else jnp.concatenate(parts, axis=0)


def _ada_kernel(c_ref, w_ref, b_ref, o_ref):
    s = jax.nn.silu(c_ref[...])
    o_ref[0] = jnp.dot(s, w_ref[0], precision=lax.Precision.HIGHEST, preferred_element_type=F32) + b_ref[0]


def _ada(cc, ada_w, ada_b):
    depth, d, n6 = ada_w.shape
    rows = cc.shape[0]
    tn = 1024
    return pl.pallas_call(
        _ada_kernel,
        grid=(depth, n6 // tn),
        in_specs=[pl.BlockSpec((rows, d), lambda i, j: (0, 0)),
                  pl.BlockSpec((1, d, tn), lambda i, j: (i, 0, j)),
                  pl.BlockSpec((1, 1, tn), lambda i, j: (i, 0, j))],
        out_specs=pl.BlockSpec((1, rows, tn), lambda i, j: (i, 0, j)),
        out_shape=jax.ShapeDtypeStruct((depth, rows, n6), F32),
        compiler_params=_cparams(("arbitrary", "arbitrary")),
        name="ada",
    )(cc, ada_w, ada_b.reshape(depth, 1, n6))


_W1_COLS = MLA_Q_LORA + MLA_KV_LORA + 2 * HEAD_PAD
_QW = MLA_HEADS * HEAD_PAD


def _mla_proj_kernel(x_ref, mod_ref, g1_ref, w1_ref, qn_ref, wq_ref, kvn_ref, wk_ref, we_ref, wv_ref,
                     cq_ref, sq_ref, ck_ref, sk_ref, q_ref, k_ref, v_ref):
    h = _norm_mod_tile(x_ref, mod_ref, g1_ref, 0)
    d = jnp.dot(h, w1_ref[...], preferred_element_type=F32)
    dq = d[:, :MLA_Q_LORA]
    ckv = d[:, MLA_Q_LORA:MLA_Q_LORA + MLA_KV_LORA]
    kr = d[:, MLA_Q_LORA + MLA_KV_LORA:MLA_Q_LORA + MLA_KV_LORA + HEAD_PAD]
    kr_sw = d[:, MLA_Q_LORA + MLA_KV_LORA + HEAD_PAD:]
    qn = _rms(dq, qn_ref[...]).astype(BF16)
    qq = jnp.dot(qn, wq_ref[...], preferred_element_type=F32)
    cq = cq_ref[...]
    sq = sq_ref[...]
    for hd in range(MLA_HEADS):
        lo = hd * HEAD_PAD
        q_ref[:, lo:lo + HEAD_PAD] = (qq[:, lo:lo + HEAD_PAD] * cq + qq[:, _QW + lo:_QW + lo + HEAD_PAD] * sq).astype(BF16)
    c = _rms(ckv, kvn_ref[...]).astype(BF16)
    kr_roped = (kr * ck_ref[...] + kr_sw * sk_ref[...]).astype(BF16)
    k = jnp.dot(c, wk_ref[...], preferred_element_type=F32) + jnp.dot(kr_roped, we_ref[...], preferred_element_type=F32)
    k_ref[...] = k.astype(BF16)
    v = jnp.dot(c, wv_ref[...], preferred_element_type=F32)
    lane = lax.broadcasted_iota(jnp.int32, v.shape, 1)
    v_ref[...] = jnp.where(lane % HEAD_PAD == MLA_V, 1.0, v).astype(BF16)


def _mla_proj(x, modtab, g1, pw, rope):
    n, d = x.shape
    tm = SEG
    full = lambda a: pl.BlockSpec(a.shape, lambda g: (0,) * a.ndim)
    pos = lambda g: (g % SEGS_PER_BATCH, 0)
    cq, sq, ck, sk = rope
    return pl.pallas_call(
        _mla_proj_kernel,
        grid=(n // tm,),
        in_specs=[pl.BlockSpec((tm, d), lambda g: (g, 0)),
                  pl.BlockSpec((tm // SEG, N_MOD, d), lambda g: (g, 0, 0)),
                  full(g1), full(pw["w1"]), full(pw["qn"]), full(pw["wq"]), full(pw["kvn"]),
                  full(pw["wk"]), full(pw["we"]), full(pw["wv"]),
                  pl.BlockSpec((tm, HEAD_PAD), pos), pl.BlockSpec((tm, HEAD_PAD), pos),
                  pl.BlockSpec((tm, HEAD_PAD), pos), pl.BlockSpec((tm, HEAD_PAD), pos)],
        out_specs=[pl.BlockSpec((tm, _QW), lambda g: (g, 0)),
                   pl.BlockSpec((tm, _QW), lambda g: (g, 0)),
                   pl.BlockSpec((tm, _QW), lambda g: (g, 0))],
        out_shape=[jax.ShapeDtypeStruct((n, _QW), BF16)] * 3,
        compiler_params=_cparams(("parallel",)),
        name="mla_proj",
    )(x, modtab, g1, pw["w1"], pw["qn"], pw["wq"], pw["kvn"], pw["wk"], pw["we"], pw["wv"], cq, sq, ck, sk)


def _mla_weights(w_dq, q_norm, w_uq, w_dkv, kv_norm, w_ukv):
    d = w_dq.shape[0]
    swap = np.arange(MLA_ROPE) ^ ROPE_AXIS_FREQS
    w_kr = w_dkv[:, MLA_KV_LORA:]
    padr = lambda w: jnp.pad(w, ((0, 0), (0, HEAD_PAD - MLA_ROPE)))
    w1 = jnp.concatenate([w_dq, w_dkv[:, :MLA_KV_LORA], padr(w_kr), padr(w_kr[:, swap])], axis=1)
    uq = w_uq.reshape(MLA_Q_LORA, MLA_HEADS, MLA_QK)
    zpad = jnp.zeros((MLA_Q_LORA, MLA_HEADS, HEAD_PAD - MLA_QK), w_uq.dtype)
    wq_main = jnp.concatenate([uq, zpad], axis=-1).reshape(MLA_Q_LORA, _QW)
    uq_sw = jnp.concatenate([jnp.zeros((MLA_Q_LORA, MLA_HEADS, MLA_NOPE), w_uq.dtype),
                             uq[:, :, MLA_NOPE:][:, :, swap], zpad], axis=-1).reshape(MLA_Q_LORA, _QW)
    wq = jnp.concatenate([wq_main, uq_sw], axis=1)
    ukv = w_ukv.reshape(MLA_KV_LORA, MLA_HEADS, MLA_NOPE + MLA_V)
    wk = jnp.concatenate([ukv[:, :, :MLA_NOPE],
                          jnp.zeros((MLA_KV_LORA, MLA_HEADS, HEAD_PAD - MLA_NOPE), w_ukv.dtype)], axis=-1)
    wk = wk.reshape(MLA_KV_LORA, _QW)
    wv = jnp.concatenate([ukv[:, :, MLA_NOPE:],
                          jnp.zeros((MLA_KV_LORA, MLA_HEADS, HEAD_PAD - MLA_V), w_ukv.dtype)], axis=-1)
    wv = wv.reshape(MLA_KV_LORA, _QW)
    e = np.zeros((HEAD_PAD, MLA_HEADS, HEAD_PAD), np.float32)
    for r in range(MLA_ROPE):
        e[r, :, MLA_NOPE + r] = 1.0
    we = jnp.asarray(e.reshape(HEAD_PAD, _QW))
    return {"w1": w1.astype(BF16), "qn": q_norm.reshape(1, -1), "wq": wq.astype(BF16),
            "kvn": kv_norm.reshape(1, -1), "wk": wk.astype(BF16), "we": we.astype(BF16), "wv": wv.astype(BF16)}


def _rope_tables():
    rows = SEQ // GRID_W
    row = jnp.repeat(jnp.arange(rows, dtype=F32), GRID_W)
    col = jnp.tile(jnp.arange(GRID_W, dtype=F32), rows)
    inv_freq = 1.0 / (ROPE_THETA ** (jnp.arange(ROPE_AXIS_FREQS, dtype=F32) / ROPE_AXIS_FREQS))
    ang = jnp.stack([row[:, None] * inv_freq, col[:, None] * inv_freq], axis=1)
    ang = jnp.concatenate([ang, jnp.zeros((CTX_LEN, 2, ROPE_AXIS_FREQS), F32)], axis=0)
    cos = jnp.cos(ang)
    sin = jnp.sin(ang)
    c32 = jnp.stack([cos, cos], axis=2).reshape(T_TOK, MLA_ROPE)
    s32 = jnp.stack([-sin, sin], axis=2).reshape(T_TOK, MLA_ROPE)
    scale = MLA_QK ** -0.5 * math.log2(math.e)
    zq = jnp.zeros((T_TOK, HEAD_PAD - MLA_QK), F32)
    cq = jnp.concatenate([jnp.full((T_TOK, MLA_NOPE), scale, F32), c32 * scale, zq], axis=1)
    sq = jnp.concatenate([jnp.zeros((T_TOK, MLA_NOPE), F32), s32 * scale, zq], axis=1)
    zk = jnp.zeros((T_TOK, HEAD_PAD - MLA_ROPE), F32)
    ck = jnp.concatenate([c32, zk], axis=1)
    sk = jnp.concatenate([s32, zk], axis=1)
    return cq, sq, ck, sk


ATTN_TQ = 1024
ATTN_KEY_CHUNK = 2048
_LAT_CHUNKS = tuple((lo, ATTN_KEY_CHUNK) for lo in range(0, SEQ, ATTN_KEY_CHUNK))
_CTX_CHUNK = ((SEQ, CTX_LEN),)


def _attend_pair(q, k_ref, v_ref, chunks):
    m = [None, None]
    acc = [None, None]
    for lo, size in chunks:
        for j in range(2):
            cols = slice(j * HEAD_PAD, (j + 1) * HEAD_PAD)
            s = lax.dot_general(q[:, cols], k_ref[0, lo:lo + size, cols], (((1,), (1,)), ((), ())), preferred_element_type=F32)
            mc = jnp.max(s, axis=-1, keepdims=True)
            m_new = mc if m[j] is None else jnp.maximum(m[j], mc)
            pv = jnp.dot(jnp.exp2(s - m_new).astype(BF16), v_ref[0, lo:lo + size, cols], preferred_element_type=F32)
            acc[j] = pv if m[j] is None else acc[j] * jnp.exp2(m[j] - m_new) + pv
            m[j] = m_new
    outs = [a / a[:, MLA_V:MLA_V + 1] for a in acc]
    lane = lax.broadcasted_iota(jnp.int32, outs[0].shape, 1)
    return jnp.where(lane < MLA_V, outs[0], pltpu.roll(outs[1], MLA_V, 1))


def _attn_kernel(q_ref, k_ref, v_ref, o_ref, *, lat_tiles):
    qi = pl.program_id(2)

    @pl.when(qi < lat_tiles)
    def _():
        o_ref[0] = _attend_pair(q_ref[0], k_ref, v_ref, _LAT_CHUNKS + _CTX_CHUNK).astype(BF16)

    @pl.when(qi >= lat_tiles)
    def _():
        o_ref[0, :CTX_LEN, :] = _attend_pair(q_ref[0, :CTX_LEN, :], k_ref, v_ref, _CTX_CHUNK).astype(BF16)


def _attention(q, k, v):
    b = q.shape[0]
    tq = ATTN_TQ
    lat_tiles = SEQ // tq
    return pl.pallas_call(
        functools.partial(_attn_kernel, lat_tiles=lat_tiles),
        grid=(b, MLA_HEADS // 2, lat_tiles + 1),
        in_specs=[pl.BlockSpec((1, tq, 2 * HEAD_PAD), lambda bi, hp, qi: (bi, qi, hp)),
                  pl.BlockSpec((1, T_TOK, 2 * HEAD_PAD), lambda bi, hp, qi: (bi, 0, hp)),
                  pl.BlockSpec((1, T_TOK, 2 * HEAD_PAD), lambda bi, hp, qi: (bi, 0, hp))],
        out_specs=pl.BlockSpec((1, tq, 2 * MLA_V), lambda bi, hp, qi: (bi, qi, hp)),
        out_shape=jax.ShapeDtypeStruct((b, T_TOK, MLA_HEADS * MLA_V), BF16),
        compiler_params=_cparams(("parallel", "parallel", "arbitrary")),
        name="attention",
    )(q, k, v)


def _gated_add(x_ref, mod_ref, y, gate_idx, o_ref):
    for s in range(x_ref.shape[0] // SEG):
        rows = slice(s * SEG, (s + 1) * SEG)
        g = mod_ref[s][gate_idx:gate_idx + 1]
        o_ref[rows, :] = x_ref[rows, :] + g * y[rows, :]


def _proj_res_kernel(x_ref, mod_ref, y_ref, w_ref, o_ref):
    y = jnp.dot(y_ref[...], w_ref[...], preferred_element_type=F32)
    _gated_add(x_ref, mod_ref, y, 2, o_ref)


def _proj_res(x, modtab, y, w, tm=512):
    n, d = x.shape
    return pl.pallas_call(
        _proj_res_kernel,
        grid=(n // tm,),
        in_specs=[pl.BlockSpec((tm, d), lambda g: (g, 0)),
                  pl.BlockSpec((tm // SEG, N_MOD, d), lambda g: (g, 0, 0)),
                  pl.BlockSpec((tm, y.shape[1]), lambda g: (g, 0)),
                  pl.BlockSpec(w.shape, lambda g: (0, 0))],
        out_specs=pl.BlockSpec((tm, d), lambda g: (g, 0)),
        out_shape=jax.ShapeDtypeStruct((n, d), F32),
        compiler_params=_cparams(("parallel",)),
        name="proj_res",
    )(x, modtab, y, w)


_S5_LAT_CHUNKS = SEQ // S5_CHUNK
_S5_CHUNKS = T_TOK // S5_CHUNK
_S5_W = S5_CHUNK * S5_GROUP
_S5_BLK_CHUNKS = 2
_S5_BLK_TOK = _S5_BLK_CHUNKS * S5_CHUNK
LANES = 128
_SLOTS = LANES // S5_GROUP


def _s5_in_kernel(x_ref, mod_ref, g_ref, u_ref, h_scr, *, nb):
    for b in range(nb):
        m = mod_ref[0, b]
        h = _norm_mod(x_ref[b], g_ref[...], m[0:1], m[1:2])
        for j in range(D_MODEL // LANES):
            h_scr[j, b * _S5_BLK_TOK:(b + 1) * _S5_BLK_TOK, :] = h[:, j * LANES:(j + 1) * LANES]
    slot = lax.broadcasted_iota(jnp.int32, (_S5_BLK_CHUNKS * nb, LANES), 1) // S5_GROUP
    for j in range(D_MODEL // LANES):
        for half in range(_S5_W // LANES):
            srcs = []
            for p in range(_SLOTS):
                t = _SLOTS * half + p
                rows = [h_scr[j, pl.ds(c * S5_CHUNK + t, nb, stride=_S5_BLK_TOK), :] for c in range(_S5_BLK_CHUNKS)]
                srcs.append(jnp.concatenate(rows, axis=0))
            for gs in range(_SLOTS):
                acc = None
                for p in range(_SLOTS):
                    k = (p - gs) % _SLOTS
                    r = pltpu.roll(srcs[p], k * S5_GROUP, 1) if k else srcs[p]
                    acc = r if acc is None else jnp.where(slot == p, r, acc)
                u_ref[_SLOTS * j + gs, :, half * LANES:(half + 1) * LANES] = acc.astype(BF16)


def _s5_in(x3, modsel, g1):
    nb, t, d = x3.shape
    nblk = t // _S5_BLK_TOK
    rows = _S5_BLK_CHUNKS * nb
    return pl.pallas_call(
        functools.partial(_s5_in_kernel, nb=nb),
        grid=(nblk,),
        in_specs=[pl.BlockSpec((nb, _S5_BLK_TOK, d), lambda k: (0, k, 0)),
                  pl.BlockSpec((1, nb, N_MOD, d), lambda k: (k // (SEQ // _S5_BLK_TOK), 0, 0, 0)),
                  pl.BlockSpec(g1.shape, lambda k: (0, 0))],
        out_specs=pl.BlockSpec((S5_GROUPS, rows, _S5_W), lambda k: (0, k, 0)),
        out_shape=jax.ShapeDtypeStruct((S5_GROUPS, nblk * rows, _S5_W), BF16),
        scratch_shapes=[pltpu.VMEM((d // LANES, nb * _S5_BLK_TOK, LANES), F32)],
        compiler_params=_cparams(("parallel",)),
        name="s5_in",
    )(x3, modsel, g1)


def _s5_out_kernel(x_ref, mod_ref, y_ref, w_ref, o_ref, nat_scr, *, nb):
    d = x_ref.shape[2]
    slot = lax.broadcasted_iota(jnp.int32, (nb, LANES), 1) // S5_GROUP
    for j in range(d // LANES):
        for half in range(_S5_W // LANES):
            for c in range(_S5_BLK_CHUNKS):
                srcs = [y_ref[_SLOTS * j + gs, c * nb:(c + 1) * nb, half * LANES:(half + 1) * LANES] for gs in range(_SLOTS)]
                for p in range(_SLOTS):
                    acc = None
                    for gs in range(_SLOTS):
                        k = (gs - p) % _SLOTS
                        r = pltpu.roll(srcs[gs], k * S5_GROUP, 1) if k else srcs[gs]
                        acc = r if acc is None else jnp.where(slot == gs, r, acc)
                    nat_scr[j, pl.ds(c * S5_CHUNK + _SLOTS * half + p, nb, stride=_S5_BLK_TOK), :] = acc
    y_nat = jnp.concatenate([nat_scr[j] for j in range(d // LANES)], axis=1)
    g = jax.nn.gelu(y_nat).astype(BF16)
    z = jnp.dot(g, w_ref[...], preferred_element_type=F32)
    y = z[:, :d] * jax.nn.sigmoid(z[:, d:])
    for b in range(nb):
        gate = mod_ref[0, b][2:3]
        o_ref[b] = x_ref[b] + gate * y[b * _S5_BLK_TOK:(b + 1) * _S5_BLK_TOK, :]


def _s5_out(x3, modsel, y, w):
    nb, t, d = x3.shape
    nblk = t // _S5_BLK_TOK
    rows = _S5_BLK_CHUNKS * nb
    return pl.pallas_call(
        functools.partial(_s5_out_kernel, nb=nb),
        grid=(nblk,),
        in_specs=[pl.BlockSpec((nb, _S5_BLK_TOK, d), lambda k: (0, k, 0)),
                  pl.BlockSpec((1, nb, N_MOD, d), lambda k: (k // (SEQ // _S5_BLK_TOK), 0, 0, 0)),
                  pl.BlockSpec((S5_GROUPS, rows, _S5_W), lambda k: (0, k, 0)),
                  pl.BlockSpec(w.shape, lambda k: (0, 0))],
        out_specs=pl.BlockSpec((nb, _S5_BLK_TOK, d), lambda k: (0, k, 0)),
        out_shape=jax.ShapeDtypeStruct((nb, t, d), F32),
        scratch_shapes=[pltpu.VMEM((d // LANES, nb * _S5_BLK_TOK, LANES), F32)],
        compiler_params=_cparams(("parallel",)),
        name="s5_out",
    )(x3, modsel, y, w)


def _s5_kernel(u_ref, m_ref, win_ref, wof_ref, wor_ref, ar_ref, ai_ref, y_ref, z_ref, sf_ref, sr_ref, *, nb):
    u = u_ref[0]
    z_ref[...] = jnp.dot(u, win_ref[0], preferred_element_type=F32)
    half = _S5_W // 2
    ar = ar_ref[0]
    ai = ai_ref[0]
    lane = lax.broadcasted_iota(jnp.int32, (nb, _S5_W), 1)
    is_fwd = (lane % half) < S5_STATE

    def step(i, s):
        cf = jnp.where(i < _S5_CHUNKS - _S5_LAT_CHUNKS, i + _S5_LAT_CHUNKS, i - (_S5_CHUNKS - _S5_LAT_CHUNKS))
        cr = _S5_CHUNKS - 1 - i
        rf = pl.multiple_of(cf * nb, nb)
        rr = pl.multiple_of(cr * nb, nb)
        sf_ref[pl.ds(rf, nb), :] = s
        sr_ref[pl.ds(rr, nb), :] = s
        z = jnp.where(is_fwd, z_ref[pl.ds(rf, nb), :], z_ref[pl.ds(rr, nb), :])
        re = s[:, :half]
        im = s[:, half:]
        return jnp.concatenate([ar * re - ai * im + z[:, :half], ar * im + ai * re + z[:, half:]], axis=1)

    lax.fori_loop(0, _S5_CHUNKS, step, jnp.zeros((nb, _S5_W), F32))
    y = jnp.dot(u, m_ref[0], preferred_element_type=F32)
    y = y + jnp.dot(sf_ref[...].astype(BF16), wof_ref[0], preferred_element_type=F32)
    y = y + jnp.dot(sr_ref[...].astype(BF16), wor_ref[0], preferred_element_type=F32)
    y_ref[0] = y


def _s5_core(u, sp, nb):
    g, rows, w = u.shape
    blk = lambda a: pl.BlockSpec((1,) + a.shape[1:], lambda i: (i,) + (0,) * (a.ndim - 1))
    return pl.pallas_call(
        functools.partial(_s5_kernel, nb=nb),
        grid=(g,),
        in_specs=[blk(u), blk(sp["m"]), blk(sp["win"]), blk(sp["wof"]), blk(sp["wor"]), blk(sp["ar"]), blk(sp["ai"])],
        out_specs=pl.BlockSpec((1, rows, w), lambda i: (i, 0, 0)),
        out_shape=jax.ShapeDtypeStruct((g, rows, w), F32),
        scratch_shapes=[pltpu.VMEM((rows, w), F32), pltpu.VMEM((rows, w), F32), pltpu.VMEM((rows, w), F32)],
        compiler_params=_cparams(("parallel",)),
        name="s5_core",
    )(u, sp["m"], sp["win"], sp["wof"], sp["wor"], sp["ar"], sp["ai"])


def _s5_params(a_re, a_im, log_dt, b_re, b_im, c_re, c_im, d):
    c64 = jnp.complex64
    lam = lax.complex(jnp.minimum(a_re, S5_MAX_RE), a_im)
    lam_dt = lam * jnp.exp(log_dt)[..., None]
    lam_bar = jnp.exp(lam_dt)
    b_bar = ((lam_bar - 1.0) / lam)[..., None] * lax.complex(b_re, b_im)
    c_mat = lax.complex(c_re, c_im)
    taus = jnp.arange(S5_CHUNK + 1, dtype=F32)
    pw = jnp.exp(lam_dt[None] * taus[:, None, None, None].astype(c64))
    hi = lax.Precision.HIGHEST
    kern = jnp.real(jnp.einsum('dgnp,tdgp,dgpm->dgtnm', c_mat, pw[:S5_CHUNK], b_bar, precision=hi))
    eye = jnp.eye(S5_GROUP, dtype=F32)
    k0 = kern[0, :, 0] + kern[1, :, 0] + d.reshape(S5_GROUPS, S5_GROUP)[:, :, None] * eye
    lags = jnp.concatenate([kern[1, :, :0:-1], k0[:, None], kern[0, :, 1:]], axis=1)
    idx = (np.arange(S5_CHUNK)[None, :] - np.arange(S5_CHUNK)[:, None]) + S5_CHUNK - 1
    m = lags[:, idx]
    m = m.transpose(0, 1, 4, 2, 3).reshape(S5_GROUPS, _S5_W, _S5_W)
    wf = pw[S5_CHUNK - 1::-1][:S5_CHUNK, 0][..., None] * b_bar[0][None]
    wr = pw[:S5_CHUNK, 1][..., None] * b_bar[1][None]
    to_rows = lambda w: w.transpose(1, 0, 3, 2).reshape(S5_GROUPS, _S5_W, S5_STATE)
    wf, wr = to_rows(wf), to_rows(wr)
    win = jnp.concatenate([jnp.real(wf), jnp.real(wr), jnp.imag(wf), jnp.imag(wr)], axis=-1)
    of = c_mat[0][None] * pw[1:, 0][:, :, None, :]
    orv = c_mat[1][None] * pw[S5_CHUNK:0:-1, 1][:, :, None, :]
    to_cols = lambda w: w.transpose(1, 3, 0, 2).reshape(S5_GROUPS, S5_STATE, _S5_W)
    of, orv = to_cols(of), to_cols(orv)
    zeros = jnp.zeros_like(jnp.real(of))
    wof = jnp.concatenate([jnp.real(of), zeros, -jnp.imag(of), zeros], axis=1)
    wor = jnp.concatenate([zeros, jnp.real(orv), zeros, -jnp.imag(orv)], axis=1)
    a16 = pw[S5_CHUNK]
    ar = jnp.concatenate([jnp.real(a16[0]), jnp.real(a16[1])], axis=-1)[:, None, :]
    ai = jnp.concatenate([jnp.imag(a16[0]), jnp.imag(a16[1])], axis=-1)[:, None, :]
    return {"m": m.astype(BF16), "win": win.astype(BF16), "wof": wof.astype(BF16), "wor": wor.astype(BF16),
            "ar": ar, "ai": ai}


def _pool_kernel(xp_ref, xc_ref, xn_ref, mod_ref, g_ref, wp_ref, sc_ref, o_ref):
    seg = pl.program_id(0) % SEGS_PER_BATCH
    is_ctx = seg >= LAT_SEGS
    p0 = jnp.where(is_ctx, 0, seg * SEG)
    lseq = jnp.where(is_ctx, CTX_LEN, SEQ)
    m = mod_ref[0]
    g = g_ref[...]
    hs = [_norm_mod(r[...], g, m[0:1], m[1:2]) for r in (xp_ref, xc_ref, xn_ref)]
    hcat = jnp.concatenate([h.astype(BF16) for h in hs], axis=0)
    r = lax.broadcasted_iota(jnp.int32, (SEG, 3 * SEG), 0)
    s = lax.broadcasted_iota(jnp.int32, (SEG, 3 * SEG), 1)
    pt = p0 + r
    ps = p0 - SEG + s
    rcol = lax.broadcasted_iota(jnp.int32, (SEG, 1), 0) + p0
    x = xc_ref[...]
    gate = m[2:3]
    for gi, w in enumerate(POOL_WINDOWS):
        cols = slice(gi * POOL_GROUP, (gi + 1) * POOL_GROUP)
        lo = jnp.maximum(pt - w // 2, 0)
        hi = jnp.minimum(pt + w - w // 2, lseq)
        band = jnp.where((ps >= lo) & (ps < hi), 1.0, 0.0).astype(BF16)
        tot = jnp.dot(band, hcat[:, cols], preferred_element_type=F32)
        cnt = (jnp.minimum(rcol + w - w // 2, lseq) - jnp.maximum(rcol - w // 2, 0)).astype(F32)
        resid = tot / cnt - hs[1][:, cols]
        y = jnp.dot(resid.astype(BF16), wp_ref[gi], preferred_element_type=F32) * sc_ref[:, cols]
        o_ref[:, cols] = x[:, cols] + gate[:, cols] * y


def _pool(x, modtab, g1, w_pool, scale):
    n, d = x.shape
    nseg = n // SEG

    def prev(g):
        return (jnp.maximum(g - 1, 0), 0)

    def nxt(g):
        return (jnp.minimum(g + 1, nseg - 1), 0)

    return pl.pallas_call(
        _pool_kernel,
        grid=(nseg,),
        in_specs=[pl.BlockSpec((SEG, d), prev), pl.BlockSpec((SEG, d), lambda g: (g, 0)), pl.BlockSpec((SEG, d), nxt),
                  pl.BlockSpec((1, N_MOD, d), lambda g: (g, 0, 0)),
                  pl.BlockSpec(g1.shape, lambda g: (0, 0)),
                  pl.BlockSpec(w_pool.shape, lambda g: (0, 0, 0)),
                  pl.BlockSpec(scale.shape, lambda g: (0, 0))],
        out_specs=pl.BlockSpec((SEG, d), lambda g: (g, 0)),
        out_shape=jax.ShapeDtypeStruct((n, d), F32),
        compiler_params=_cparams(("parallel",)),
        name="pool",
    )(x, x, x, modtab, g1, w_pool, scale)


def _swiglu_partial(h, wg, wu, wd):
    a = jax.nn.silu(jnp.dot(h, wg, preferred_element_type=F32)) * jnp.dot(h, wu, preferred_element_type=F32)
    return jnp.dot(a.astype(BF16), wd, preferred_element_type=F32)


def _ffn_kernel(x_ref, mod_ref, g_ref, wg_ref, wu_ref, wd_ref, o_ref):
    h = _norm_mod_tile(x_ref, mod_ref, g_ref, 3)
    _gated_add(x_ref, mod_ref, _swiglu_partial(h, wg_ref[...], wu_ref[...], wd_ref[...]), 5, o_ref)


_RESIDENT = pl.Buffered(1)


def _ffn(x, modtab, g2, w_gu, w_down, layer, tm=512):
    n, d = x.shape
    f = w_down.shape[1]
    return pl.pallas_call(
        _ffn_kernel,
        grid=(n // tm,),
        in_specs=[pl.BlockSpec((tm, d), lambda g: (g, 0)),
                  pl.BlockSpec((tm // SEG, N_MOD, d), lambda g: (g, 0, 0)),
                  pl.BlockSpec(g2.shape, lambda g: (0, 0)),
                  pl.BlockSpec((None, d, f), lambda g: (layer, 0, 0), pipeline_mode=_RESIDENT),
                  pl.BlockSpec((None, d, f), lambda g: (layer, 0, 1), pipeline_mode=_RESIDENT),
                  pl.BlockSpec((None, f, d), lambda g: (layer, 0, 0), pipeline_mode=_RESIDENT)],
        out_specs=pl.BlockSpec((tm, d), lambda g: (g, 0)),
        out_shape=jax.ShapeDtypeStruct((n, d), F32),
        compiler_params=_cparams(("parallel",)),
        name="ffn",
    )(x, modtab, g2, w_gu, w_gu, w_down)


MOE_TILE = 512
MOE_DMA_TILE = 512


def _router_kernel(x_ref, mod_ref, g_ref, w_ref, b_ref, h_ref, rw_ref, ri_ref, cnt_ref, run_ref):
    @pl.when(pl.program_id(0) == 0)
    def _():
        run_ref[...] = jnp.zeros_like(run_ref)

    h = _norm_mod_tile(x_ref, mod_ref, g_ref, 3, out_dtype=F32)
    h_ref[...] = h
    logits = jnp.dot(h, w_ref[...], precision=lax.Precision.HIGHEST, preferred_element_type=F32) + b_ref[...]
    lane = lax.broadcasted_iota(jnp.int32, logits.shape, 1)
    neg = jnp.float32(-jnp.inf)
    lg = jnp.where(lane < N_EXPERTS, logits, neg)
    m1 = jnp.max(lg, axis=-1, keepdims=True)
    i1 = jnp.min(jnp.where(lg == m1, lane, HEAD_PAD), axis=-1, keepdims=True)
    lg2 = jnp.where(lane == i1, neg, lg)
    m2 = jnp.max(lg2, axis=-1, keepdims=True)
    i2 = jnp.min(jnp.where(lg2 == m2, lane, HEAD_PAD), axis=-1, keepdims=True)
    e2 = jnp.exp(m2 - m1)
    den = 1.0 + e2
    tm = logits.shape[0]
    sel = jnp.where((lane == i1) | (lane == i2), 1.0, 0.0).astype(BF16)
    tri = jnp.where(lax.broadcasted_iota(jnp.int32, (tm, tm), 1) <= lax.broadcasted_iota(jnp.int32, (tm, tm), 0), 1.0, 0.0)
    within = jnp.dot(tri.astype(BF16), sel, preferred_element_type=F32)
    rank = run_ref[0:1, :] + within - 1.0
    r1 = jnp.sum(jnp.where(lane == i1, rank, 0.0), axis=-1, keepdims=True).astype(jnp.int32)
    r2 = jnp.sum(jnp.where(lane == i2, rank, 0.0), axis=-1, keepdims=True).astype(jnp.int32)
    total = run_ref[0:1, :] + within[tm - 1:tm, :]
    run_ref[...] = jnp.broadcast_to(total, run_ref.shape)
    cnt_ref[...] = jnp.broadcast_to(total, cnt_ref.shape).astype(jnp.int32)
    rw_ref[...] = jnp.where(lane == 0, 1.0 / den, jnp.where(lane == 1, e2 / den, 0.0))
    ri_ref[...] = jnp.where(lane == 0, i1, jnp.where(lane == 1, i2, jnp.where(lane == 2, r1, jnp.where(lane == 3, r2, 0))))


def _router(x, modtab, g2, w_router, b_router, tm=512):
    n, d = x.shape
    wr = jnp.pad(w_router, ((0, 0), (0, HEAD_PAD - N_EXPERTS)))
    br = jnp.pad(b_router, (0, HEAD_PAD - N_EXPERTS)).reshape(1, HEAD_PAD)
    return pl.pallas_call(
        _router_kernel,
        grid=(n // tm,),
        in_specs=[pl.BlockSpec((tm, d), lambda g: (g, 0)),
                  pl.BlockSpec((tm // SEG, N_MOD, d), lambda g: (g, 0, 0)),
                  pl.BlockSpec(g2.shape, lambda g: (0, 0)),
                  pl.BlockSpec(wr.shape, lambda g: (0, 0)),
                  pl.BlockSpec(br.shape, lambda g: (0, 0))],
        out_specs=[pl.BlockSpec((tm, d), lambda g: (g, 0)),
                   pl.BlockSpec((tm, HEAD_PAD), lambda g: (g, 0)),
                   pl.BlockSpec((tm, HEAD_PAD), lambda g: (g, 0)),
                   pl.BlockSpec((8, HEAD_PAD), lambda g: (0, 0))],
        out_shape=[jax.ShapeDtypeStruct((n, d), F32),
                   jax.ShapeDtypeStruct((n, HEAD_PAD), F32),
                   jax.ShapeDtypeStruct((n, HEAD_PAD), jnp.int32),
                   jax.ShapeDtypeStruct((8, HEAD_PAD), jnp.int32)],
        scratch_shapes=[pltpu.VMEM((8, HEAD_PAD), F32)],
        compiler_params=_cparams(("arbitrary",)),
        name="router",
    )(x, modtab, g2, wr, br)


def _moe_plan(ri, cnt, n):
    e = ri[:, :TOP_K]
    counts = cnt[0, :N_EXPERTS]
    padded = (counts + MOE_TILE - 1) // MOE_TILE * MOE_TILE
    ends = jnp.cumsum(padded)
    starts = ends - padded
    dest = starts[e] + ri[:, TOP_K:2 * TOP_K]
    n_tiles = (TOP_K * n) // MOE_TILE + N_EXPERTS
    n_valid = (ends[-1] // MOE_TILE).astype(jnp.int32).reshape(1)
    tile_ids = jnp.arange(n_tiles, dtype=jnp.int32)
    tile_expert = jnp.searchsorted(ends // MOE_TILE, jnp.minimum(tile_ids, n_valid - 1), side="right").astype(jnp.int32)
    seg_len = jnp.concatenate([padded - counts, (n_tiles * MOE_TILE - ends[-1])[None]])
    seg_base = jnp.concatenate([starts + counts, ends[-1:]])
    seg_end = jnp.cumsum(seg_len)
    k = jnp.arange(N_EXPERTS * MOE_TILE, dtype=jnp.int32)
    seg = jnp.searchsorted(seg_end, k, side="right")
    pad_dest = seg_base[seg] + k - (seg_end - seg_len)[seg]
    return dest[:, 0].astype(jnp.int32), dest[:, 1].astype(jnp.int32), pad_dest.astype(jnp.int32), tile_expert, n_valid, n_tiles


def _row(ref, i):
    return ref.at[pl.ds(i, 1)]


def _scatter_rows_kernel(d0_ref, d1_ref, pad_ref, h_ref, o_hbm, zero_ref, sem):
    @pl.when(pl.program_id(0) == 0)
    def _():
        zero_ref[...] = jnp.zeros_like(zero_ref)

        def issue_zero(k, carry):
            pltpu.make_async_copy(_row(zero_ref, 0), _row(o_hbm, pad_ref[k]), sem.at[2]).start()
            return carry

        lax.fori_loop(0, N_EXPERTS * MOE_TILE, issue_zero, 0, unroll=8)
        for _ in range(N_EXPERTS * MOE_TILE // MOE_DMA_TILE):
            pltpu.make_async_copy(h_ref, o_hbm.at[pl.ds(0, MOE_DMA_TILE)], sem.at[2]).wait()

    base = pl.program_id(0) * MOE_DMA_TILE

    def issue(r, carry):
        t = base + r
        pltpu.make_async_copy(_row(h_ref, r), _row(o_hbm, d0_ref[t]), sem.at[0]).start()
        pltpu.make_async_copy(_row(h_ref, r), _row(o_hbm, d1_ref[t]), sem.at[1]).start()
        return carry

    lax.fori_loop(0, MOE_DMA_TILE, issue, 0, unroll=8)
    pltpu.make_async_copy(h_ref, o_hbm.at[pl.ds(0, MOE_DMA_TILE)], sem.at[0]).wait()
    pltpu.make_async_copy(h_ref, o_hbm.at[pl.ds(0, MOE_DMA_TILE)], sem.at[1]).wait()


def _scatter_rows(d0, d1, pad_dest, h, n_rows):
    n, d = h.shape
    return pl.pallas_call(
        _scatter_rows_kernel,
        grid_spec=pltpu.PrefetchScalarGridSpec(
            num_scalar_prefetch=3,
            grid=(n // MOE_DMA_TILE,),
            in_specs=[pl.BlockSpec((MOE_DMA_TILE, d), lambda g, d0, d1, pd: (g, 0))],
            out_specs=pl.BlockSpec(memory_space=pl.ANY),
            scratch_shapes=[pltpu.VMEM((8, d), F32), pltpu.SemaphoreType.DMA((3,))]),
        out_shape=jax.ShapeDtypeStruct((n_rows, d), F32),
        compiler_params=_cparams(("arbitrary",)),
        name="moe_scatter",
    )(d0, d1, pad_dest, h)


def _moe_ffn_kernel(te_ref, nv_ref, xs_ref, wg_ref, wu_ref, wd_ref, o_ref):
    del te_ref
    i = pl.program_id(0)

    @pl.when(i < nv_ref[0])
    def _():
        o_ref[...] = _swiglu_partial(xs_ref[...].astype(BF16), wg_ref[...], wu_ref[...], wd_ref[...])

    @pl.when(i >= nv_ref[0])
    def _():
        o_ref[...] = jnp.zeros_like(o_ref)


def _moe_ffn(xs, n_rows, tile_expert, n_valid, w_gu, w_down, layer):
    d = xs.shape[1]
    rows = n_rows
    f = w_down.shape[2]
    return pl.pallas_call(
        _moe_ffn_kernel,
        grid_spec=pltpu.PrefetchScalarGridSpec(
            num_scalar_prefetch=2,
            grid=(rows // MOE_TILE,),
            in_specs=[pl.BlockSpec((MOE_TILE, d), lambda i, te, nv: (i, 0)),
                      pl.BlockSpec((None, None, d, f), lambda i, te, nv: (layer, te[i], 0, 0), pipeline_mode=_RESIDENT),
                      pl.BlockSpec((None, None, d, f), lambda i, te, nv: (layer, te[i], 0, 1), pipeline_mode=_RESIDENT),
                      pl.BlockSpec((None, None, f, d), lambda i, te, nv: (layer, te[i], 0, 0), pipeline_mode=_RESIDENT)],
            out_specs=pl.BlockSpec((MOE_TILE, d), lambda i, te, nv: (i, 0))),
        out_shape=jax.ShapeDtypeStruct((rows, d), F32),
        compiler_params=_cparams(("arbitrary",)),
        name="moe_ffn",
    )(tile_expert, n_valid, xs, w_gu, w_gu, w_down)


def _combine_kernel(d0_ref, d1_ref, x_ref, mod_ref, rw_ref, y_hbm, *rest, tm, final):
    if final:
        fg_ref, o_ref, ya_ref, yb_ref, sem = rest
        tile = pl.program_id(0) * SEGS_PER_BATCH + pl.program_id(1)
    else:
        o_ref, ya_ref, yb_ref, sem = rest
        tile = pl.program_id(0)
    base = tile * tm

    def issue(r, carry):
        t = base + r
        pltpu.make_async_copy(_row(y_hbm, d0_ref[t]), _row(ya_ref, r), sem.at[0]).start()
        pltpu.make_async_copy(_row(y_hbm, d1_ref[t]), _row(yb_ref, r), sem.at[1]).start()
        return carry

    lax.fori_loop(0, tm, issue, 0, unroll=8)
    rw = rw_ref[...]
    lane = lax.broadcasted_iota(jnp.int32, rw.shape, 1)
    w1 = jnp.sum(jnp.where(lane == 0, rw, 0.0), axis=-1, keepdims=True)
    w2 = jnp.sum(jnp.where(lane == 1, rw, 0.0), axis=-1, keepdims=True)
    pltpu.make_async_copy(y_hbm.at[pl.ds(0, tm)], ya_ref, sem.at[0]).wait()
    pltpu.make_async_copy(y_hbm.at[pl.ds(0, tm)], yb_ref, sem.at[1]).wait()
    y = w1 * ya_ref[...] + w2 * yb_ref[...]
    if final:
        o_ref[0] = _rms(x_ref[...] + mod_ref[0][5:6] * y, fg_ref[...])
    else:
        _gated_add(x_ref, mod_ref, y, 5, o_ref)


def _combine(d0, d1, x, modtab, rw, y, final_g=None):
    n, d = x.shape
    final = final_g is not None
    tm = SEG if final else MOE_DMA_TILE
    if final:
        nb = n // T_TOK
        grid = (nb, LAT_SEGS)
        row = lambda bi, j, d0, d1: (bi * SEGS_PER_BATCH + j, 0)
        mrow = lambda bi, j, d0, d1: (bi * SEGS_PER_BATCH + j, 0, 0)
        extra = [pl.BlockSpec(final_g.shape, lambda bi, j, d0, d1: (0, 0))]
        out_spec = pl.BlockSpec((1, tm, d), lambda bi, j, d0, d1: (bi, j, 0))
        out_shape = jax.ShapeDtypeStruct((nb, SEQ, d), F32)
        args = (final_g,)
    else:
        grid = (n // tm,)
        row = lambda g, d0, d1: (g, 0)
        mrow = lambda g, d0, d1: (g, 0, 0)
        extra = []
        out_spec = pl.BlockSpec((tm, d), row)
        out_shape = jax.ShapeDtypeStruct((n, d), F32)
        args = ()
    return pl.pallas_call(
        functools.partial(_combine_kernel, tm=tm, final=final),
        grid_spec=pltpu.PrefetchScalarGridSpec(
            num_scalar_prefetch=2,
            grid=grid,
            in_specs=[pl.BlockSpec((tm, d), row),
                      pl.BlockSpec((tm // SEG, N_MOD, d), mrow),
                      pl.BlockSpec((tm, HEAD_PAD), row),
                      pl.BlockSpec(memory_space=pl.ANY)] + extra,
            out_specs=out_spec,
            scratch_shapes=[pltpu.VMEM((tm, d), F32), pltpu.VMEM((tm, d), F32), pltpu.SemaphoreType.DMA((2,))]),
        out_shape=out_shape,
        compiler_params=_cparams(("arbitrary",) * len(grid)),
        name="moe_combine",
    )(d0, d1, x, modtab, rw, y, *args)


def _moe(x, modtab, g2, w_router, b_router, w_gu, w_down, layer, final_g=None):
    n = x.shape[0]
    h, rw, ri, cnt = _router(x, modtab, g2, w_router, b_router)
    d0, d1, pad_dest, tile_expert, n_valid, n_tiles = _moe_plan(ri, cnt, n)
    xs = _scatter_rows(d0, d1, pad_dest, h, n_tiles * MOE_TILE)
    ys = _moe_ffn(xs, n_tiles * MOE_TILE, tile_expert, n_valid, w_gu, w_down, layer)
    return _combine(d0, d1, x, modtab, rw, ys, final_g)


def kernel(x, c, ctx, c_ctx, norm1_g, norm2_g, ada_w, ada_b, final_norm_g, mla_w_dq, mla_q_norm, mla_w_uq, mla_w_dkv, mla_kv_norm, mla_w_ukv, mla_w_o, s5_a_re, s5_a_im, s5_log_dt, s5_b_re, s5_b_im, s5_c_re, s5_c_im, s5_d, s5_w_glu, pool_w, pool_scale, ffn_w_gu, ffn_w_down, moe_w_router, moe_b_router, moe_w_gu, moe_w_down):
    b, l, d = x.shape
    n = b * T_TOK
    xs = jnp.concatenate([x, ctx], axis=1).reshape(n, d)

    rows = -(-(b + 1) // 8) * 8
    cc = jnp.concatenate([c, c_ctx[None], jnp.zeros((rows - b - 1, d), F32)], axis=0)
    mods = _ada(cc, ada_w, ada_b)
    seg_src = np.array([bi if sj < LAT_SEGS else b for bi in range(b) for sj in range(SEGS_PER_BATCH)], np.int32)

    rope = _rope_tables()
    ffn_w_gu_bf, ffn_w_down_bf = ffn_w_gu.astype(BF16), ffn_w_down.astype(BF16)
    moe_w_gu_bf, moe_w_down_bf = moe_w_gu.astype(BF16), moe_w_down.astype(BF16)
    for i in range(DEPTH):
        last = i == DEPTH - 1
        modtab = mods[i].reshape(rows, N_MOD, d)[seg_src]
        g1 = norm1_g[i].reshape(1, d)
        g2 = norm2_g[i].reshape(1, d)
        kind = i % N_MIXERS
        j = i // N_MIXERS
        if kind == 0:
            pw = _mla_weights(mla_w_dq[j], mla_q_norm[j], mla_w_uq[j], mla_w_dkv[j], mla_kv_norm[j], mla_w_ukv[j])
            q, k, v = _mla_proj(xs, modtab, g1, pw, rope)
            o = _attention(q.reshape(b, T_TOK, -1), k.reshape(b, T_TOK, -1), v.reshape(b, T_TOK, -1))
            xs = _proj_res(xs, modtab, o.reshape(n, -1), mla_w_o[j].astype(BF16))
        elif kind == 1:
            mod9 = mods[i].reshape(rows, N_MOD, d)
            modsel = jnp.stack([mod9[:b], jnp.broadcast_to(mod9[b], (b, N_MOD, d))])
            x3 = xs.reshape(b, T_TOK, d)
            sp = _s5_params(s5_a_re[j], s5_a_im[j], s5_log_dt[j], s5_b_re[j], s5_b_im[j], s5_c_re[j], s5_c_im[j], s5_d[j])
            y = _s5_core(_s5_in(x3, modsel, g1), sp, b)
            xs = _s5_out(x3, modsel, y, s5_w_glu[j].astype(BF16)).reshape(n, d)
        else:
            xs = _pool(xs, modtab, g1, pool_w[j].astype(BF16), pool_scale[j].reshape(1, d))
        kk = i // 2
        if i % 2 == 0:
            xs = _ffn(xs, modtab, g2, ffn_w_gu_bf, ffn_w_down_bf, kk)
        else:
            xs = _moe(xs, modtab, g2, moe_w_router[kk], moe_b_router[kk], moe_w_gu_bf, moe_w_down_bf, kk,
                      final_norm_g.reshape(1, d) if last else None)
    return xs
```

```python
import functools
import math

import jax
import jax.numpy as jnp
import numpy as np
from jax import lax
from jax.experimental import pallas as pl
from jax.experimental.pallas import tpu as pltpu

F32 = jnp.float32
BF16 = jnp.bfloat16

D_MODEL = 1024
SEQ = 4096
DEPTH = 4
GRID_W = 64
CTX_LEN = 256
N_MIXERS = 3
NORM_EPS = 1e-6

MLA_HEADS = 16
MLA_Q_LORA = 384
MLA_KV_LORA = 256
MLA_NOPE = 64
MLA_ROPE = 32
MLA_V = 64
MLA_QK = MLA_NOPE + MLA_ROPE
ROPE_AXIS_FREQS = MLA_ROPE // 4
ROPE_THETA = 10000.0
HEAD_PAD = 128

S5_GROUP = 16
S5_GROUPS = D_MODEL // S5_GROUP
S5_STATE = 64
S5_MAX_RE = -1e-4
S5_CHUNK = 16

POOL_WINDOWS = (2, 4, 8, 16)
POOL_GROUP = D_MODEL // len(POOL_WINDOWS)

FFN_DIM = 2816
N_EXPERTS = 8
TOP_K = 2

assert DEPTH % 2 == 0
T_TOK = SEQ + CTX_LEN
SEG = CTX_LEN
SEGS_PER_BATCH = T_TOK // SEG
LAT_SEGS = SEQ // SEG
N_MOD = 6

VMEM_LIMIT = 56 * 1024 * 1024


def _cparams(sem, vmem=VMEM_LIMIT):
    return pltpu.CompilerParams(dimension_semantics=sem, vmem_limit_bytes=vmem)


def _rms(x, g):
    return x * lax.rsqrt(jnp.mean(x * x, axis=-1, keepdims=True) + NORM_EPS) * g


def _norm_mod(x, g, shift, scale):
    return _rms(x, g) * (1.0 + scale) + shift


def _norm_mod_tile(x_ref, mod_ref, g_ref, shift_idx, out_dtype=BF16):
    parts = []
    for s in range(x_ref.shape[0] // SEG):
        m = mod_ref[s]
        x = x_ref[s * SEG:(s + 1) * SEG, :]
        parts.append(_norm_mod(x, g_ref[...], m[shift_idx:shift_idx + 1], m[shift_idx + 1:shift_idx + 2]).astype(out_dtype))
    return parts[0] if len(parts) == 1 else jnp.concatenate(parts, axis=0)


def _ada_kernel(c_ref, w_ref, b_ref, o_ref):
    s = jax.nn.silu(c_ref[...])
    o_ref[0] = jnp.dot(s, w_ref[0], precision=lax.Precision.HIGHEST, preferred_element_type=F32) + b_ref[0]


def _ada(cc, ada_w, ada_b):
    depth, d, n6 = ada_w.shape
    rows = cc.shape[0]
    tn = 1024
    return pl.pallas_call(
        _ada_kernel,
        grid=(depth, n6 // tn),
        in_specs=[pl.BlockSpec((rows, d), lambda i, j: (0, 0)),
                  pl.BlockSpec((1, d, tn), lambda i, j: (i, 0, j)),
                  pl.BlockSpec((1, 1, tn), lambda i, j: (i, 0, j))],
        out_specs=pl.BlockSpec((1, rows, tn), lambda i, j: (i, 0, j)),
        out_shape=jax.ShapeDtypeStruct((depth, rows, n6), F32),
        compiler_params=_cparams(("arbitrary", "arbitrary")),
        name="ada",
    )(cc, ada_w, ada_b.reshape(depth, 1, n6))


_W1_COLS = MLA_Q_LORA + MLA_KV_LORA + 2 * HEAD_PAD
_QW = MLA_HEADS * HEAD_PAD


def _mla_proj_kernel(x_ref, mod_ref, g1_ref, w1_ref, qn_ref, wq_ref, kvn_ref, wk_ref, we_ref, wv_ref,
                     cq_ref, sq_ref, ck_ref, sk_ref, q_ref, k_ref, vt_ref):
    h = _norm_mod_tile(x_ref, mod_ref, g1_ref, 0)
    d = jnp.dot(h, w1_ref[...], preferred_element_type=F32)
    dq = d[:, :MLA_Q_LORA]
    ckv = d[:, MLA_Q_LORA:MLA_Q_LORA + MLA_KV_LORA]
    kr = d[:, MLA_Q_LORA + MLA_KV_LORA:MLA_Q_LORA + MLA_KV_LORA + HEAD_PAD]
    kr_sw = d[:, MLA_Q_LORA + MLA_KV_LORA + HEAD_PAD:]
    qn = _rms(dq, qn_ref[...]).astype(BF16)
    qq = jnp.dot(qn, wq_ref[...], preferred_element_type=F32)
    cq = cq_ref[...]
    sq = sq_ref[...]
    for hd in range(MLA_HEADS):
        lo = hd * HEAD_PAD
        q_ref[:, lo:lo + HEAD_PAD] = (qq[:, lo:lo + HEAD_PAD] * cq + qq[:, _QW + lo:_QW + lo + HEAD_PAD] * sq).astype(BF16)
    c = _rms(ckv, kvn_ref[...]).astype(BF16)
    kr_roped = (kr * ck_ref[...] + kr_sw * sk_ref[...]).astype(BF16)
    k = jnp.dot(c, wk_ref[...], preferred_element_type=F32) + jnp.dot(kr_roped, we_ref[...], preferred_element_type=F32)
    k_ref[...] = k.astype(BF16)
    v = jnp.dot(c, wv_ref[...], preferred_element_type=F32)
    lane = lax.broadcasted_iota(jnp.int32, v.shape, 1)
    vt_ref[...] = jnp.where(lane % HEAD_PAD == MLA_V, 1.0, v).T.astype(BF16)


def _mla_proj(x, modtab, g1, pw, rope):
    n, d = x.shape
    tm = SEG
    full = lambda a: pl.BlockSpec(a.shape, lambda g: (0,) * a.ndim)
    pos = lambda g: (g % SEGS_PER_BATCH, 0)
    cq, sq, ck, sk = rope
    return pl.pallas_call(
        _mla_proj_kernel,
        grid=(n // tm,),
        in_specs=[pl.BlockSpec((tm, d), lambda g: (g, 0)),
                  pl.BlockSpec((tm // SEG, N_MOD, d), lambda g: (g, 0, 0)),
                  full(g1), full(pw["w1"]), full(pw["qn"]), full(pw["wq"]), full(pw["kvn"]),
                  full(pw["wk"]), full(pw["we"]), full(pw["wv"]),
                  pl.BlockSpec((tm, HEAD_PAD), pos), pl.BlockSpec((tm, HEAD_PAD), pos),
                  pl.BlockSpec((tm, HEAD_PAD), pos), pl.BlockSpec((tm, HEAD_PAD), pos)],
        out_specs=[pl.BlockSpec((tm, _QW), lambda g: (g, 0)),
                   pl.BlockSpec((tm, _QW), lambda g: (g, 0)),
                   pl.BlockSpec((_QW, tm), lambda g: (0, g))],
        out_shape=[jax.ShapeDtypeStruct((n, _QW), BF16)] * 2 + [jax.ShapeDtypeStruct((_QW, n), BF16)],
        compiler_params=_cparams(("parallel",)),
        name="mla_proj",
    )(x, modtab, g1, pw["w1"], pw["qn"], pw["wq"], pw["kvn"], pw["wk"], pw["we"], pw["wv"], cq, sq, ck, sk)


def _mla_weights(w_dq, q_norm, w_uq, w_dkv, kv_norm, w_ukv):
    d = w_dq.shape[0]
    swap = np.arange(MLA_ROPE) ^ ROPE_AXIS_FREQS
    w_kr = w_dkv[:, MLA_KV_LORA:]
    padr = lambda w: jnp.pad(w, ((0, 0), (0, HEAD_PAD - MLA_ROPE)))
    w1 = jnp.concatenate([w_dq, w_dkv[:, :MLA_KV_LORA], padr(w_kr), padr(w_kr[:, swap])], axis=1)
    uq = w_uq.reshape(MLA_Q_LORA, MLA_HEADS, MLA_QK)
    zpad = jnp.zeros((MLA_Q_LORA, MLA_HEADS, HEAD_PAD - MLA_QK), w_uq.dtype)
    wq_main = jnp.concatenate([uq, zpad], axis=-1).reshape(MLA_Q_LORA, _QW)
    uq_sw = jnp.concatenate([jnp.zeros((MLA_Q_LORA, MLA_HEADS, MLA_NOPE), w_uq.dtype),
                             uq[:, :, MLA_NOPE:][:, :, swap], zpad], axis=-1).reshape(MLA_Q_LORA, _QW)
    wq = jnp.concatenate([wq_main, uq_sw], axis=1)
    ukv = w_ukv.reshape(MLA_KV_LORA, MLA_HEADS, MLA_NOPE + MLA_V)
    wk = jnp.concatenate([ukv[:, :, :MLA_NOPE],
                          jnp.zeros((MLA_KV_LORA, MLA_HEADS, HEAD_PAD - MLA_NOPE), w_ukv.dtype)], axis=-1)
    wk = wk.reshape(MLA_KV_LORA, _QW)
    wv = jnp.concatenate([ukv[:, :, MLA_NOPE:],
                          jnp.zeros((MLA_KV_LORA, MLA_HEADS, HEAD_PAD - MLA_V), w_ukv.dtype)], axis=-1)
    wv = wv.reshape(MLA_KV_LORA, _QW)
    e = np.zeros((HEAD_PAD, MLA_HEADS, HEAD_PAD), np.float32)
    for r in range(MLA_ROPE):
        e[r, :, MLA_NOPE + r] = 1.0
    we = jnp.asarray(e.reshape(HEAD_PAD, _QW))
    return {"w1": w1.astype(BF16), "qn": q_norm.reshape(1, -1), "wq": wq.astype(BF16),
            "kvn": kv_norm.reshape(1, -1), "wk": wk.astype(BF16), "we": we.astype(BF16), "wv": wv.astype(BF16)}


def _rope_tables():
    rows = SEQ // GRID_W
    row = jnp.repeat(jnp.arange(rows, dtype=F32), GRID_W)
    col = jnp.tile(jnp.arange(GRID_W, dtype=F32), rows)
    inv_freq = 1.0 / (ROPE_THETA ** (jnp.arange(ROPE_AXIS_FREQS, dtype=F32) / ROPE_AXIS_FREQS))
    ang = jnp.stack([row[:, None] * inv_freq, col[:, None] * inv_freq], axis=1)
    ang = jnp.concatenate([ang, jnp.zeros((CTX_LEN, 2, ROPE_AXIS_FREQS), F32)], axis=0)
    cos = jnp.cos(ang)
    sin = jnp.sin(ang)
    c32 = jnp.stack([cos, cos], axis=2).reshape(T_TOK, MLA_ROPE)
    s32 = jnp.stack([-sin, sin], axis=2).reshape(T_TOK, MLA_ROPE)
    scale = MLA_QK ** -0.5 * math.log2(math.e)
    zq = jnp.zeros((T_TOK, HEAD_PAD - MLA_QK), F32)
    cq = jnp.concatenate([jnp.full((T_TOK, MLA_NOPE), scale, F32), c32 * scale, zq], axis=1)
    sq = jnp.concatenate([jnp.zeros((T_TOK, MLA_NOPE), F32), s32 * scale, zq], axis=1)
    zk = jnp.zeros((T_TOK, HEAD_PAD - MLA_ROPE), F32)
    ck = jnp.concatenate([c32, zk], axis=1)
    sk = jnp.concatenate([s32, zk], axis=1)
    return cq, sq, ck, sk


ATTN_TQ = 1024
ATTN_KEY_CHUNK = 2048
_LAT_CHUNKS = tuple((lo, ATTN_KEY_CHUNK) for lo in range(0, SEQ, ATTN_KEY_CHUNK))
_CTX_CHUNK = ((SEQ, CTX_LEN),)


def _attend_pair(q, k_ref, vt_ref, chunks):
    m = [None, None]
    acc = [None, None]
    for lo, size in chunks:
        for j in range(2):
            cols = slice(j * HEAD_PAD, (j + 1) * HEAD_PAD)
            st = lax.dot_general(k_ref[0, lo:lo + size, cols], q[:, cols], (((1,), (1,)), ((), ())), preferred_element_type=F32)
            mc = jnp.max(st, axis=0, keepdims=True)
            m_new = mc if m[j] is None else jnp.maximum(m[j], mc)
            pv = jnp.dot(vt_ref[cols, lo:lo + size], jnp.exp2(st - m_new).astype(BF16), preferred_element_type=F32)
            acc[j] = pv if m[j] is None else acc[j] * jnp.exp2(m[j] - m_new) + pv
            m[j] = m_new
    outs = [a / a[MLA_V:MLA_V + 1, :] for a in acc]
    return jnp.concatenate([outs[0][:MLA_V], outs[1][:MLA_V]], axis=0).T


def _attn_kernel(q_ref, k_ref, vt_ref, o_ref, *, lat_tiles):
    qi = pl.program_id(2)

    @pl.when(qi < lat_tiles)
    def _():
        o_ref[0] = _attend_pair(q_ref[0], k_ref, vt_ref, _LAT_CHUNKS + _CTX_CHUNK).astype(BF16)

    @pl.when(qi >= lat_tiles)
    def _():
        o_ref[0, :CTX_LEN, :] = _attend_pair(q_ref[0, :CTX_LEN, :], k_ref, vt_ref, _CTX_CHUNK).astype(BF16)


def _attention(q, k, vt):
    b = q.shape[0]
    tq = ATTN_TQ
    lat_tiles = SEQ // tq
    return pl.pallas_call(
        functools.partial(_attn_kernel, lat_tiles=lat_tiles),
        grid=(b, MLA_HEADS // 2, lat_tiles + 1),
        in_specs=[pl.BlockSpec((1, tq, 2 * HEAD_PAD), lambda bi, hp, qi: (bi, qi, hp)),
                  pl.BlockSpec((1, T_TOK, 2 * HEAD_PAD), lambda bi, hp, qi: (bi, 0, hp)),
                  pl.BlockSpec((2 * HEAD_PAD, T_TOK), lambda bi, hp, qi: (hp, bi))],
        out_specs=pl.BlockSpec((1, tq, 2 * MLA_V), lambda bi, hp, qi: (bi, qi, hp)),
        out_shape=jax.ShapeDtypeStruct((b, T_TOK, MLA_HEADS * MLA_V), BF16),
        compiler_params=_cparams(("parallel", "parallel", "arbitrary")),
        name="attention",
    )(q, k, vt)


def _gated_add(x_ref, mod_ref, y, gate_idx, o_ref):
    for s in range(x_ref.shape[0] // SEG):
        rows = slice(s * SEG, (s + 1) * SEG)
        g = mod_ref[s][gate_idx:gate_idx + 1]
        o_ref[rows, :] = x_ref[rows, :] + g * y[rows, :]


def _proj_res_kernel(x_ref, mod_ref, y_ref, w_ref, o_ref):
    y = jnp.dot(y_ref[...], w_ref[...], preferred_element_type=F32)
    _gated_add(x_ref, mod_ref, y, 2, o_ref)


def _proj_res(x, modtab, y, w, tm=512):
    n, d = x.shape
    return pl.pallas_call(
        _proj_res_kernel,
        grid=(n // tm,),
        in_specs=[pl.BlockSpec((tm, d), lambda g: (g, 0)),
                  pl.BlockSpec((tm // SEG, N_MOD, d), lambda g: (g, 0, 0)),
                  pl.BlockSpec((tm, y.shape[1]), lambda g: (g, 0)),
                  pl.BlockSpec(w.shape, lambda g: (0, 0))],
        out_specs=pl.BlockSpec((tm, d), lambda g: (g, 0)),
        out_shape=jax.ShapeDtypeStruct((n, d), F32),
        compiler_params=_cparams(("parallel",)),
        name="proj_res",
    )(x, modtab, y, w)


_S5_LAT_CHUNKS = SEQ // S5_CHUNK
_S5_CHUNKS = T_TOK // S5_CHUNK
_S5_W = S5_CHUNK * S5_GROUP
_S5_BLK_CHUNKS = 2
_S5_BLK_TOK = _S5_BLK_CHUNKS * S5_CHUNK
LANES = 128
_SLOTS = LANES // S5_GROUP


def _s5_in_kernel(x_ref, mod_ref, g_ref, u_ref, h_scr, *, nb):
    for b in range(nb):
        m = mod_ref[0, b]
        h = _norm_mod(x_ref[b], g_ref[...], m[0:1], m[1:2])
        for j in range(D_MODEL // LANES):
            h_scr[j, b * _S5_BLK_TOK:(b + 1) * _S5_BLK_TOK, :] = h[:, j * LANES:(j + 1) * LANES]
    slot = lax.broadcasted_iota(jnp.int32, (_S5_BLK_CHUNKS * nb, LANES), 1) // S5_GROUP
    for j in range(D_MODEL // LANES):
        for half in range(_S5_W // LANES):
            srcs = []
            for p in range(_SLOTS):
                t = _SLOTS * half + p
                rows = [h_scr[j, pl.ds(c * S5_CHUNK + t, nb, stride=_S5_BLK_TOK), :] for c in range(_S5_BLK_CHUNKS)]
                srcs.append(jnp.concatenate(rows, axis=0))
            for gs in range(_SLOTS):
                acc = None
                for p in range(_SLOTS):
                    k = (p - gs) % _SLOTS
                    r = pltpu.roll(srcs[p], k * S5_GROUP, 1) if k else srcs[p]
                    acc = r if acc is None else jnp.where(slot == p, r, acc)
                u_ref[_SLOTS * j + gs, :, half * LANES:(half + 1) * LANES] = acc.astype(BF16)


def _s5_in(x3, modsel, g1):
    nb, t, d = x3.shape
    nblk = t // _S5_BLK_TOK
    rows = _S5_BLK_CHUNKS * nb
    return pl.pallas_call(
        functools.partial(_s5_in_kernel, nb=nb),
        grid=(nblk,),
        in_specs=[pl.BlockSpec((nb, _S5_BLK_TOK, d), lambda k: (0, k, 0)),
                  pl.BlockSpec((1, nb, N_MOD, d), lambda k: (k // (SEQ // _S5_BLK_TOK), 0, 0, 0)),
                  pl.BlockSpec(g1.shape, lambda k: (0, 0))],
        out_specs=pl.BlockSpec((S5_GROUPS, rows, _S5_W), lambda k: (0, k, 0)),
        out_shape=jax.ShapeDtypeStruct((S5_GROUPS, nblk * rows, _S5_W), BF16),
        scratch_shapes=[pltpu.VMEM((d // LANES, nb * _S5_BLK_TOK, LANES), F32)],
        compiler_params=_cparams(("parallel",)),
        name="s5_in",
    )(x3, modsel, g1)


def _s5_out_kernel(x_ref, mod_ref, y_ref, w_ref, o_ref, nat_scr, *, nb):
    d = x_ref.shape[2]
    slot = lax.broadcasted_iota(jnp.int32, (nb, LANES), 1) // S5_GROUP
    for j in range(d // LANES):
        for half in range(_S5_W // LANES):
            for c in range(_S5_BLK_CHUNKS):
                srcs = [y_ref[_SLOTS * j + gs, c * nb:(c + 1) * nb, half * LANES:(half + 1) * LANES] for gs in range(_SLOTS)]
                for p in range(_SLOTS):
                    acc = None
                    for gs in range(_SLOTS):
                        k = (gs - p) % _SLOTS
                        r = pltpu.roll(srcs[gs], k * S5_GROUP, 1) if k else srcs[gs]
                        acc = r if acc is None else jnp.where(slot == gs, r, acc)
                    nat_scr[j, pl.ds(c * S5_CHUNK + _SLOTS * half + p, nb, stride=_S5_BLK_TOK), :] = acc
    y_nat = jnp.concatenate([nat_scr[j] for j in range(d // LANES)], axis=1)
    g = jax.nn.gelu(y_nat).astype(BF16)
    z = jnp.dot(g, w_ref[...], preferred_element_type=F32)
    y = z[:, :d] * jax.nn.sigmoid(z[:, d:])
    for b in range(nb):
        gate = mod_ref[0, b][2:3]
        o_ref[b] = x_ref[b] + gate * y[b * _S5_BLK_TOK:(b + 1) * _S5_BLK_TOK, :]


def _s5_out(x3, modsel, y, w):
    nb, t, d = x3.shape
    nblk = t // _S5_BLK_TOK
    rows = _S5_BLK_CHUNKS * nb
    return pl.pallas_call(
        functools.partial(_s5_out_kernel, nb=nb),
        grid=(nblk,),
        in_specs=[pl.BlockSpec((nb, _S5_BLK_TOK, d), lambda k: (0, k, 0)),
                  pl.BlockSpec((1, nb, N_MOD, d), lambda k: (k // (SEQ // _S5_BLK_TOK), 0, 0, 0)),
                  pl.BlockSpec((S5_GROUPS, rows, _S5_W), lambda k: (0, k, 0)),
                  pl.BlockSpec(w.shape, lambda k: (0, 0))],
        out_specs=pl.BlockSpec((nb, _S5_BLK_TOK, d), lambda k: (0, k, 0)),
        out_shape=jax.ShapeDtypeStruct((nb, t, d), F32),
        scratch_shapes=[pltpu.VMEM((d // LANES, nb * _S5_BLK_TOK, LANES), F32)],
        compiler_params=_cparams(("parallel",)),
        name="s5_out",
    )(x3, modsel, y, w)


def _s5_kernel(u_ref, m_ref, win_ref, wof_ref, wor_ref, ar_ref, ai_ref, y_ref, z_ref, sf_ref, sr_ref, *, nb):
    u = u_ref[0]
    z_ref[...] = jnp.dot(u, win_ref[0], preferred_element_type=F32)
    half = _S5_W // 2
    ar = ar_ref[0]
    ai = ai_ref[0]
    lane = lax.broadcasted_iota(jnp.int32, (nb, _S5_W), 1)
    is_fwd = (lane % half) < S5_STATE

    def step(i, s):
        cf = jnp.where(i < _S5_CHUNKS - _S5_LAT_CHUNKS, i + _S5_LAT_CHUNKS, i - (_S5_CHUNKS - _S5_LAT_CHUNKS))
        cr = _S5_CHUNKS - 1 - i
        rf = pl.multiple_of(cf * nb, nb)
        rr = pl.multiple_of(cr * nb, nb)
        sf_ref[pl.ds(rf, nb), :] = s
        sr_ref[pl.ds(rr, nb), :] = s
        z = jnp.where(is_fwd, z_ref[pl.ds(rf, nb), :], z_ref[pl.ds(rr, nb), :])
        re = s[:, :half]
        im = s[:, half:]
        return jnp.concatenate([ar * re - ai * im + z[:, :half], ar * im + ai * re + z[:, half:]], axis=1)

    lax.fori_loop(0, _S5_CHUNKS, step, jnp.zeros((nb, _S5_W), F32))
    y = jnp.dot(u, m_ref[0], preferred_element_type=F32)
    y = y + jnp.dot(sf_ref[...].astype(BF16), wof_ref[0], preferred_element_type=F32)
    y = y + jnp.dot(sr_ref[...].astype(BF16), wor_ref[0], preferred_element_type=F32)
    y_ref[0] = y


def _s5_core(u, sp, nb):
    g, rows, w = u.shape
    blk = lambda a: pl.BlockSpec((1,) + a.shape[1:], lambda i: (i,) + (0,) * (a.ndim - 1))
    return pl.pallas_call(
        functools.partial(_s5_kernel, nb=nb),
        grid=(g,),
        in_specs=[blk(u), blk(sp["m"]), blk(sp["win"]), blk(sp["wof"]), blk(sp["wor"]), blk(sp["ar"]), blk(sp["ai"])],
        out_specs=pl.BlockSpec((1, rows, w), lambda i: (i, 0, 0)),
        out_shape=jax.ShapeDtypeStruct((g, rows, w), F32),
        scratch_shapes=[pltpu.VMEM((rows, w), F32), pltpu.VMEM((rows, w), F32), pltpu.VMEM((rows, w), F32)],
        compiler_params=_cparams(("parallel",)),
        name="s5_core",
    )(u, sp["m"], sp["win"], sp["wof"], sp["wor"], sp["ar"], sp["ai"])


def _s5_params(a_re, a_im, log_dt, b_re, b_im, c_re, c_im, d):
    c64 = jnp.complex64
    lam = lax.complex(jnp.minimum(a_re, S5_MAX_RE), a_im)
    lam_dt = lam * jnp.exp(log_dt)[..., None]
    lam_bar = jnp.exp(lam_dt)
    b_bar = ((lam_bar - 1.0) / lam)[..., None] * lax.complex(b_re, b_im)
    c_mat = lax.complex(c_re, c_im)
    taus = jnp.arange(S5_CHUNK + 1, dtype=F32)
    pw = jnp.exp(lam_dt[None] * taus[:, None, None, None].astype(c64))
    hi = lax.Precision.HIGHEST
    kern = jnp.real(jnp.einsum('dgnp,tdgp,dgpm->dgtnm', c_mat, pw[:S5_CHUNK], b_bar, precision=hi))
    eye = jnp.eye(S5_GROUP, dtype=F32)
    k0 = kern[0, :, 0] + kern[1, :, 0] + d.reshape(S5_GROUPS, S5_GROUP)[:, :, None] * eye
    lags = jnp.concatenate([kern[1, :, :0:-1], k0[:, None], kern[0, :, 1:]], axis=1)
    idx = (np.arange(S5_CHUNK)[None, :] - np.arange(S5_CHUNK)[:, None]) + S5_CHUNK - 1
    m = lags[:, idx]
    m = m.transpose(0, 1, 4, 2, 3).reshape(S5_GROUPS, _S5_W, _S5_W)
    wf = pw[S5_CHUNK - 1::-1][:S5_CHUNK, 0][..., None] * b_bar[0][None]
    wr = pw[:S5_CHUNK, 1][..., None] * b_bar[1][None]
    to_rows = lambda w: w.transpose(1, 0, 3, 2).reshape(S5_GROUPS, _S5_W, S5_STATE)
    wf, wr = to_rows(wf), to_rows(wr)
    win = jnp.concatenate([jnp.real(wf), jnp.real(wr), jnp.imag(wf), jnp.imag(wr)], axis=-1)
    of = c_mat[0][None] * pw[1:, 0][:, :, None, :]
    orv = c_mat[1][None] * pw[S5_CHUNK:0:-1, 1][:, :, None, :]
    to_cols = lambda w: w.transpose(1, 3, 0, 2).reshape(S5_GROUPS, S5_STATE, _S5_W)
    of, orv = to_cols(of), to_cols(orv)
    zeros = jnp.zeros_like(jnp.real(of))
    wof = jnp.concatenate([jnp.real(of), zeros, -jnp.imag(of), zeros], axis=1)
    wor = jnp.concatenate([zeros, jnp.real(orv), zeros, -jnp.imag(orv)], axis=1)
    a16 = pw[S5_CHUNK]
    ar = jnp.concatenate([jnp.real(a16[0]), jnp.real(a16[1])], axis=-1)[:, None, :]
    ai = jnp.concatenate([jnp.imag(a16[0]), jnp.imag(a16[1])], axis=-1)[:, None, :]
    return {"m": m.astype(BF16), "win": win.astype(BF16), "wof": wof.astype(BF16), "wor": wor.astype(BF16),
            "ar": ar, "ai": ai}


def _pool_kernel(xp_ref, xc_ref, xn_ref, mod_ref, g_ref, wp_ref, sc_ref, o_ref):
    seg = pl.program_id(0) % SEGS_PER_BATCH
    is_ctx = seg >= LAT_SEGS
    p0 = jnp.where(is_ctx, 0, seg * SEG)
    lseq = jnp.where(is_ctx, CTX_LEN, SEQ)
    m = mod_ref[0]
    g = g_ref[...]
    hs = [_norm_mod(r[...], g, m[0:1], m[1:2]) for r in (xp_ref, xc_ref, xn_ref)]
    hcat = jnp.concatenate([h.astype(BF16) for h in hs], axis=0)
    r = lax.broadcasted_iota(jnp.int32, (SEG, 3 * SEG), 0)
    s = lax.broadcasted_iota(jnp.int32, (SEG, 3 * SEG), 1)
    pt = p0 + r
    ps = p0 - SEG + s
    rcol = lax.broadcasted_iota(jnp.int32, (SEG, 1), 0) + p0
    x = xc_ref[...]
    gate = m[2:3]
    for gi, w in enumerate(POOL_WINDOWS):
        cols = slice(gi * POOL_GROUP, (gi + 1) * POOL_GROUP)
        lo = jnp.maximum(pt - w // 2, 0)
        hi = jnp.minimum(pt + w - w // 2, lseq)
        band = jnp.where((ps >= lo) & (ps < hi), 1.0, 0.0).astype(BF16)
        tot = jnp.dot(band, hcat[:, cols], preferred_element_type=F32)
        cnt = (jnp.minimum(rcol + w - w // 2, lseq) - jnp.maximum(rcol - w // 2, 0)).astype(F32)
        resid = tot / cnt - hs[1][:, cols]
        y = jnp.dot(resid.astype(BF16), wp_ref[gi], preferred_element_type=F32) * sc_ref[:, cols]
        o_ref[:, cols] = x[:, cols] + gate[:, cols] * y


def _pool(x, modtab, g1, w_pool, scale):
    n, d = x.shape
    nseg = n // SEG

    def prev(g):
        return (jnp.maximum(g - 1, 0), 0)

    def nxt(g):
        return (jnp.minimum(g + 1, nseg - 1), 0)

    return pl.pallas_call(
        _pool_kernel,
        grid=(nseg,),
        in_specs=[pl.BlockSpec((SEG, d), prev), pl.BlockSpec((SEG, d), lambda g: (g, 0)), pl.BlockSpec((SEG, d), nxt),
                  pl.BlockSpec((1, N_MOD, d), lambda g: (g, 0, 0)),
                  pl.BlockSpec(g1.shape, lambda g: (0, 0)),
                  pl.BlockSpec(w_pool.shape, lambda g: (0, 0, 0)),
                  pl.BlockSpec(scale.shape, lambda g: (0, 0))],
        out_specs=pl.BlockSpec((SEG, d), lambda g: (g, 0)),
        out_shape=jax.ShapeDtypeStruct((n, d), F32),
        compiler_params=_cparams(("parallel",)),
        name="pool",
    )(x, x, x, modtab, g1, w_pool, scale)


def _swiglu_partial(h, wg, wu, wd):
    a = jax.nn.silu(jnp.dot(h, wg, preferred_element_type=F32)) * jnp.dot(h, wu, preferred_element_type=F32)
    return jnp.dot(a.astype(BF16), wd, preferred_element_type=F32)


def _ffn_kernel(x_ref, mod_ref, g_ref, wg_ref, wu_ref, wd_ref, o_ref):
    h = _norm_mod_tile(x_ref, mod_ref, g_ref, 3)
    _gated_add(x_ref, mod_ref, _swiglu_partial(h, wg_ref[...], wu_ref[...], wd_ref[...]), 5, o_ref)


_RESIDENT = pl.Buffered(1)


def _ffn(x, modtab, g2, w_gu, w_down, layer, tm=512):
    n, d = x.shape
    f = w_down.shape[1]
    return pl.pallas_call(
        _ffn_kernel,
        grid=(n // tm,),
        in_specs=[pl.BlockSpec((tm, d), lambda g: (g, 0)),
                  pl.BlockSpec((tm // SEG, N_MOD, d), lambda g: (g, 0, 0)),
                  pl.BlockSpec(g2.shape, lambda g: (0, 0)),
                  pl.BlockSpec((None, d, f), lambda g: (layer, 0, 0), pipeline_mode=_RESIDENT),
                  pl.BlockSpec((None, d, f), lambda g: (layer, 0, 1), pipeline_mode=_RESIDENT),
                  pl.BlockSpec((None, f, d), lambda g: (layer, 0, 0), pipeline_mode=_RESIDENT)],
        out_specs=pl.BlockSpec((tm, d), lambda g: (g, 0)),
        out_shape=jax.ShapeDtypeStruct((n, d), F32),
        compiler_params=_cparams(("parallel",)),
        name="ffn",
    )(x, modtab, g2, w_gu, w_gu, w_down)


MOE_TILE = 512
MOE_DMA_TILE = 512


def _router_kernel(x_ref, mod_ref, g_ref, w_ref, b_ref, h_ref, rw_ref, ri_ref, cnt_ref, run_ref):
    @pl.when(pl.program_id(0) == 0)
    def _():
        run_ref[...] = jnp.zeros_like(run_ref)

    h = _norm_mod_tile(x_ref, mod_ref, g_ref, 3, out_dtype=F32)
    h_ref[...] = h
    logits = jnp.dot(h, w_ref[...], precision=lax.Precision.HIGHEST, preferred_element_type=F32) + b_ref[...]
    lane = lax.broadcasted_iota(jnp.int32, logits.shape, 1)
    neg = jnp.float32(-jnp.inf)
    lg = jnp.where(lane < N_EXPERTS, logits, neg)
    m1 = jnp.max(lg, axis=-1, keepdims=True)
    i1 = jnp.min(jnp.where(lg == m1, lane, HEAD_PAD), axis=-1, keepdims=True)
    lg2 = jnp.where(lane == i1, neg, lg)
    m2 = jnp.max(lg2, axis=-1, keepdims=True)
    i2 = jnp.min(jnp.where(lg2 == m2, lane, HEAD_PAD), axis=-1, keepdims=True)
    e2 = jnp.exp(m2 - m1)
    den = 1.0 + e2
    tm = logits.shape[0]
    sel = jnp.where((lane == i1) | (lane == i2), 1.0, 0.0).astype(BF16)
    tri = jnp.where(lax.broadcasted_iota(jnp.int32, (tm, tm), 1) <= lax.broadcasted_iota(jnp.int32, (tm, tm), 0), 1.0, 0.0)
    within = jnp.dot(tri.astype(BF16), sel, preferred_element_type=F32)
    rank = run_ref[0:1, :] + within - 1.0
    r1 = jnp.sum(jnp.where(lane == i1, rank, 0.0), axis=-1, keepdims=True).astype(jnp.int32)
    r2 = jnp.sum(jnp.where(lane == i2, rank, 0.0), axis=-1, keepdims=True).astype(jnp.int32)
    total = run_ref[0:1, :] + within[tm - 1:tm, :]
    run_ref[...] = jnp.broadcast_to(total, run_ref.shape)
    cnt_ref[...] = jnp.broadcast_to(total, cnt_ref.shape).astype(jnp.int32)
    rw_ref[...] = jnp.where(lane == 0, 1.0 / den, jnp.where(lane == 1, e2 / den, 0.0))
    ri_ref[...] = jnp.where(lane == 0, i1, jnp.where(lane == 1, i2, jnp.where(lane == 2, r1, jnp.where(lane == 3, r2, 0))))


def _router(x, modtab, g2, w_router, b_router, tm=512):
    n, d = x.shape
    wr = jnp.pad(w_router, ((0, 0), (0, HEAD_PAD - N_EXPERTS)))
    br = jnp.pad(b_router, (0, HEAD_PAD - N_EXPERTS)).reshape(1, HEAD_PAD)
    return pl.pallas_call(
        _router_kernel,
        grid=(n // tm,),
        in_specs=[pl.BlockSpec((tm, d), lambda g: (g, 0)),
                  pl.BlockSpec((tm // SEG, N_MOD, d), lambda g: (g, 0, 0)),
                  pl.BlockSpec(g2.shape, lambda g: (0, 0)),
                  pl.BlockSpec(wr.shape, lambda g: (0, 0)),
                  pl.BlockSpec(br.shape, lambda g: (0, 0))],
        out_specs=[pl.BlockSpec((tm, d), lambda g: (g, 0)),
                   pl.BlockSpec((tm, HEAD_PAD), lambda g: (g, 0)),
                   pl.BlockSpec((tm, HEAD_PAD), lambda g: (g, 0)),
                   pl.BlockSpec((8, HEAD_PAD), lambda g: (0, 0))],
        out_shape=[jax.ShapeDtypeStruct((n, d), F32),
                   jax.ShapeDtypeStruct((n, HEAD_PAD), F32),
                   jax.ShapeDtypeStruct((n, HEAD_PAD), jnp.int32),
                   jax.ShapeDtypeStruct((8, HEAD_PAD), jnp.int32)],
        scratch_shapes=[pltpu.VMEM((8, HEAD_PAD), F32)],
        compiler_params=_cparams(("arbitrary",)),
        name="router",
    )(x, modtab, g2, wr, br)


def _moe_plan(ri, cnt, n):
    e = ri[:, :TOP_K]
    counts = cnt[0, :N_EXPERTS]
    padded = (counts + MOE_TILE - 1) // MOE_TILE * MOE_TILE
    ends = jnp.cumsum(padded)
    starts = ends - padded
    dest = starts[e] + ri[:, TOP_K:2 * TOP_K]
    n_tiles = (TOP_K * n) // MOE_TILE + N_EXPERTS
    n_valid = (ends[-1] // MOE_TILE).astype(jnp.int32).reshape(1)
    tile_ids = jnp.arange(n_tiles, dtype=jnp.int32)
    tile_expert = jnp.searchsorted(ends // MOE_TILE, jnp.minimum(tile_ids, n_valid - 1), side="right").astype(jnp.int32)
    seg_len = jnp.concatenate([padded - counts, (n_tiles * MOE_TILE - ends[-1])[None]])
    seg_base = jnp.concatenate([starts + counts, ends[-1:]])
    seg_end = jnp.cumsum(seg_len)
    k = jnp.arange(N_EXPERTS * MOE_TILE, dtype=jnp.int32)
    seg = jnp.searchsorted(seg_end, k, side="right")
    pad_dest = seg_base[seg] + k - (seg_end - seg_len)[seg]
    return dest[:, 0].astype(jnp.int32), dest[:, 1].astype(jnp.int32), pad_dest.astype(jnp.int32), tile_expert, n_valid, n_tiles


def _row(ref, i):
    return ref.at[pl.ds(i, 1)]


def _scatter_rows_kernel(d0_ref, d1_ref, pad_ref, h_ref, o_hbm, zero_ref, sem):
    @pl.when(pl.program_id(0) == 0)
    def _():
        zero_ref[...] = jnp.zeros_like(zero_ref)

        def issue_zero(k, carry):
            pltpu.make_async_copy(_row(zero_ref, 0), _row(o_hbm, pad_ref[k]), sem.at[2]).start()
            return carry

        lax.fori_loop(0, N_EXPERTS * MOE_TILE, issue_zero, 0, unroll=8)
        for _ in range(N_EXPERTS * MOE_TILE // MOE_DMA_TILE):
            pltpu.make_async_copy(h_ref, o_hbm.at[pl.ds(0, MOE_DMA_TILE)], sem.at[2]).wait()

    base = pl.program_id(0) * MOE_DMA_TILE

    def issue(r, carry):
        t = base + r
        pltpu.make_async_copy(_row(h_ref, r), _row(o_hbm, d0_ref[t]), sem.at[0]).start()
        pltpu.make_async_copy(_row(h_ref, r), _row(o_hbm, d1_ref[t]), sem.at[1]).start()
        return carry

    lax.fori_loop(0, MOE_DMA_TILE, issue, 0, unroll=8)
    pltpu.make_async_copy(h_ref, o_hbm.at[pl.ds(0, MOE_DMA_TILE)], sem.at[0]).wait()
    pltpu.make_async_copy(h_ref, o_hbm.at[pl.ds(0, MOE_DMA_TILE)], sem.at[1]).wait()


def _scatter_rows(d0, d1, pad_dest, h, n_rows):
    n, d = h.shape
    return pl.pallas_call(
        _scatter_rows_kernel,
        grid_spec=pltpu.PrefetchScalarGridSpec(
            num_scalar_prefetch=3,
            grid=(n // MOE_DMA_TILE,),
            in_specs=[pl.BlockSpec((MOE_DMA_TILE, d), lambda g, d0, d1, pd: (g, 0))],
            out_specs=pl.BlockSpec(memory_space=pl.ANY),
            scratch_shapes=[pltpu.VMEM((8, d), F32), pltpu.SemaphoreType.DMA((3,))]),
        out_shape=jax.ShapeDtypeStruct((n_rows, d), F32),
        compiler_params=_cparams(("arbitrary",)),
        name="moe_scatter",
    )(d0, d1, pad_dest, h)


def _moe_ffn_kernel(te_ref, nv_ref, xs_ref, wg_ref, wu_ref, wd_ref, o_ref):
    del te_ref
    i = pl.program_id(0)

    @pl.when(i < nv_ref[0])
    def _():
        o_ref[...] = _swiglu_partial(xs_ref[...].astype(BF16), wg_ref[...], wu_ref[...], wd_ref[...])

    @pl.when(i >= nv_ref[0])
    def _():
        o_ref[...] = jnp.zeros_like(o_ref)


def _moe_ffn(xs, n_rows, tile_expert, n_valid, w_gu, w_down, layer):
    d = xs.shape[1]
    rows = n_rows
    f = w_down.shape[2]
    return pl.pallas_call(
        _moe_ffn_kernel,
        grid_spec=pltpu.PrefetchScalarGridSpec(
            num_scalar_prefetch=2,
            grid=(rows // MOE_TILE,),
            in_specs=[pl.BlockSpec((MOE_TILE, d), lambda i, te, nv: (i, 0)),
                      pl.BlockSpec((None, None, d, f), lambda i, te, nv: (layer, te[i], 0, 0), pipeline_mode=_RESIDENT),
                      pl.BlockSpec((None, None, d, f), lambda i, te, nv: (layer, te[i], 0, 1), pipeline_mode=_RESIDENT),
                      pl.BlockSpec((None, None, f, d), lambda i, te, nv: (layer, te[i], 0, 0), pipeline_mode=_RESIDENT)],
            out_specs=pl.BlockSpec((MOE_TILE, d), lambda i, te, nv: (i, 0))),
        out_shape=jax.ShapeDtypeStruct((rows, d), F32),
        compiler_params=_cparams(("arbitrary",)),
        name="moe_ffn",
    )(tile_expert, n_valid, xs, w_gu, w_gu, w_down)


def _combine_kernel(d0_ref, d1_ref, x_ref, mod_ref, rw_ref, y_hbm, *rest, tm, final):
    if final:
        fg_ref, o_ref, ya_ref, yb_ref, sem = rest
        tile = pl.program_id(0) * SEGS_PER_BATCH + pl.program_id(1)
    else:
        o_ref, ya_ref, yb_ref, sem = rest
        tile = pl.program_id(0)
    base = tile * tm

    def issue(r, carry):
        t = base + r
        pltpu.make_async_copy(_row(y_hbm, d0_ref[t]), _row(ya_ref, r), sem.at[0]).start()
        pltpu.make_async_copy(_row(y_hbm, d1_ref[t]), _row(yb_ref, r), sem.at[1]).start()
        return carry

    lax.fori_loop(0, tm, issue, 0, unroll=8)
    rw = rw_ref[...]
    lane = lax.broadcasted_iota(jnp.int32, rw.shape, 1)
    w1 = jnp.sum(jnp.where(lane == 0, rw, 0.0), axis=-1, keepdims=True)
    w2 = jnp.sum(jnp.where(lane == 1, rw, 0.0), axis=-1, keepdims=True)
    pltpu.make_async_copy(y_hbm.at[pl.ds(0, tm)], ya_ref, sem.at[0]).wait()
    pltpu.make_async_copy(y_hbm.at[pl.ds(0, tm)], yb_ref, sem.at[1]).wait()
    y = w1 * ya_ref[...] + w2 * yb_ref[...]
    if final:
        o_ref[0] = _rms(x_ref[...] + mod_ref[0][5:6] * y, fg_ref[...])
    else:
        _gated_add(x_ref, mod_ref, y, 5, o_ref)


def _combine(d0, d1, x, modtab, rw, y, final_g=None):
    n, d = x.shape
    final = final_g is not None
    tm = SEG if final else MOE_DMA_TILE
    if final:
        nb = n // T_TOK
        grid = (nb, LAT_SEGS)
        row = lambda bi, j, d0, d1: (bi * SEGS_PER_BATCH + j, 0)
        mrow = lambda bi, j, d0, d1: (bi * SEGS_PER_BATCH + j, 0, 0)
        extra = [pl.BlockSpec(final_g.shape, lambda bi, j, d0, d1: (0, 0))]
        out_spec = pl.BlockSpec((1, tm, d), lambda bi, j, d0, d1: (bi, j, 0))
        out_shape = jax.ShapeDtypeStruct((nb, SEQ, d), F32)
        args = (final_g,)
    else:
        grid = (n // tm,)
        row = lambda g, d0, d1: (g, 0)
        mrow = lambda g, d0, d1: (g, 0, 0)
        extra = []
        out_spec = pl.BlockSpec((tm, d), row)
        out_shape = jax.ShapeDtypeStruct((n, d), F32)
        args = ()
    return pl.pallas_call(
        functools.partial(_combine_kernel, tm=tm, final=final),
        grid_spec=pltpu.PrefetchScalarGridSpec(
            num_scalar_prefetch=2,
            grid=grid,
            in_specs=[pl.BlockSpec((tm, d), row),
                      pl.BlockSpec((tm // SEG, N_MOD, d), mrow),
                      pl.BlockSpec((tm, HEAD_PAD), row),
                      pl.BlockSpec(memory_space=pl.ANY)] + extra,
            out_specs=out_spec,
            scratch_shapes=[pltpu.VMEM((tm, d), F32), pltpu.VMEM((tm, d), F32), pltpu.SemaphoreType.DMA((2,))]),
        out_shape=out_shape,
        compiler_params=_cparams(("arbitrary",) * len(grid)),
        name="moe_combine",
    )(d0, d1, x, modtab, rw, y, *args)


def _moe(x, modtab, g2, w_router, b_router, w_gu, w_down, layer, final_g=None):
    n = x.shape[0]
    h, rw, ri, cnt = _router(x, modtab, g2, w_router, b_router)
    d0, d1, pad_dest, tile_expert, n_valid, n_tiles = _moe_plan(ri, cnt, n)
    xs = _scatter_rows(d0, d1, pad_dest, h, n_tiles * MOE_TILE)
    ys = _moe_ffn(xs, n_tiles * MOE_TILE, tile_expert, n_valid, w_gu, w_down, layer)
    return _combine(d0, d1, x, modtab, rw, ys, final_g)


def kernel(x, c, ctx, c_ctx, norm1_g, norm2_g, ada_w, ada_b, final_norm_g, mla_w_dq, mla_q_norm, mla_w_uq, mla_w_dkv, mla_kv_norm, mla_w_ukv, mla_w_o, s5_a_re, s5_a_im, s5_log_dt, s5_b_re, s5_b_im, s5_c_re, s5_c_im, s5_d, s5_w_glu, pool_w, pool_scale, ffn_w_gu, ffn_w_down, moe_w_router, moe_b_router, moe_w_gu, moe_w_down):
    b, l, d = x.shape
    n = b * T_TOK
    xs = jnp.concatenate([x, ctx], axis=1).reshape(n, d)

    rows = -(-(b + 1) // 8) * 8
    cc = jnp.concatenate([c, c_ctx[None], jnp.zeros((rows - b - 1, d), F32)], axis=0)
    mods = _ada(cc, ada_w, ada_b)
    seg_src = np.array([bi if sj < LAT_SEGS else b for bi in range(b) for sj in range(SEGS_PER_BATCH)], np.int32)

    rope = _rope_tables()
    ffn_w_gu_bf, ffn_w_down_bf = ffn_w_gu.astype(BF16), ffn_w_down.astype(BF16)
    moe_w_gu_bf, moe_w_down_bf = moe_w_gu.astype(BF16), moe_w_down.astype(BF16)
    for i in range(DEPTH):
        last = i == DEPTH - 1
        modtab = mods[i].reshape(rows, N_MOD, d)[seg_src]
        g1 = norm1_g[i].reshape(1, d)
        g2 = norm2_g[i].reshape(1, d)
        kind = i % N_MIXERS
        j = i // N_MIXERS
        if kind == 0:
            pw = _mla_weights(mla_w_dq[j], mla_q_norm[j], mla_w_uq[j], mla_w_dkv[j], mla_kv_norm[j], mla_w_ukv[j])
            q, k, vt = _mla_proj(xs, modtab, g1, pw, rope)
            o = _attention(q.reshape(b, T_TOK, -1), k.reshape(b, T_TOK, -1), vt)
            xs = _proj_res(xs, modtab, o.reshape(n, -1), mla_w_o[j].astype(BF16))
        elif kind == 1:
            mod9 = mods[i].reshape(rows, N_MOD, d)
            modsel = jnp.stack([mod9[:b], jnp.broadcast_to(mod9[b], (b, N_MOD, d))])
            x3 = xs.reshape(b, T_TOK, d)
            sp = _s5_params(s5_a_re[j], s5_a_im[j], s5_log_dt[j], s5_b_re[j], s5_b_im[j], s5_c_re[j], s5_c_im[j], s5_d[j])
            y = _s5_core(_s5_in(x3, modsel, g1), sp, b)
            xs = _s5_out(x3, modsel, y, s5_w_glu[j].astype(BF16)).reshape(n, d)
        else:
            xs = _pool(xs, modtab, g1, pool_w[j].astype(BF16), pool_scale[j].reshape(1, d))
        kk = i // 2
        if i % 2 == 0:
            xs = _ffn(xs, modtab, g2, ffn_w_gu_bf, ffn_w_down_bf, kk)
        else:
            xs = _moe(xs, modtab, g2, moe_w_router[kk], moe_b_router[kk], moe_w_gu_bf, moe_w_down_bf, kk,
                      final_norm_g.reshape(1, d) if last else None)
    return xs
```

```python
import functools
import math

import jax
import jax.numpy as jnp
import numpy as np
from jax import lax
from jax.experimental import pallas as pl
from jax.experimental.pallas import tpu as pltpu

F32 = jnp.float32
BF16 = jnp.bfloat16

D_MODEL = 1024
SEQ = 4096
DEPTH = 4
GRID_W = 64
CTX_LEN = 256
N_MIXERS = 3
NORM_EPS = 1e-6

MLA_HEADS = 16
MLA_Q_LORA = 384
MLA_KV_LORA = 256
MLA_NOPE = 64
MLA_ROPE = 32
MLA_V = 64
MLA_QK = MLA_NOPE + MLA_ROPE
ROPE_AXIS_FREQS = MLA_ROPE // 4
ROPE_THETA = 10000.0
HEAD_PAD = 128

S5_GROUP = 16
S5_GROUPS = D_MODEL // S5_GROUP
S5_STATE = 64
S5_MAX_RE = -1e-4
S5_CHUNK = 16

POOL_WINDOWS = (2, 4, 8, 16)
POOL_GROUP = D_MODEL // len(POOL_WINDOWS)

FFN_DIM = 2816
N_EXPERTS = 8
TOP_K = 2

assert DEPTH % 2 == 0
T_TOK = SEQ + CTX_LEN
SEG = CTX_LEN
SEGS_PER_BATCH = T_TOK // SEG
LAT_SEGS = SEQ // SEG
N_MOD = 6

VMEM_LIMIT = 56 * 1024 * 1024


def _cparams(sem, vmem=VMEM_LIMIT):
    return pltpu.CompilerParams(dimension_semantics=sem, vmem_limit_bytes=vmem)


def _rms(x, g):
    return x * lax.rsqrt(jnp.mean(x * x, axis=-1, keepdims=True) + NORM_EPS) * g


def _norm_mod(x, g, shift, scale):
    return _rms(x, g) * (1.0 + scale) + shift


def _norm_mod_tile(x_ref, mod_ref, g_ref, shift_idx, out_dtype=BF16):
    parts = []
    for s in range(x_ref.shape[0] // SEG):
        m = mod_ref[s]
        x = x_ref[s * SEG:(s + 1) * SEG, :]
        parts.append(_norm_mod(x, g_ref[...], m[shift_idx:shift_idx + 1], m[shift_idx + 1:shift_idx + 2]).astype(out_dtype))
    return parts[0] if len(parts) == 1 else jnp.concatenate(parts, axis=0)


def _ada_kernel(c_ref, w_ref, b_ref, o_ref):
    s = jax.nn.silu(c_ref[...])
    o_ref[0] = jnp.dot(s, w_ref[0], precision=lax.Precision.HIGHEST, preferred_element_type=F32) + b_ref[0]


def _ada(cc, ada_w, ada_b):
    depth, d, n6 = ada_w.shape
    rows = cc.shape[0]
    tn = 1024
    return pl.pallas_call(
        _ada_kernel,
        grid=(depth, n6 // tn),
        in_specs=[pl.BlockSpec((rows, d), lambda i, j: (0, 0)),
                  pl.BlockSpec((1, d, tn), lambda i, j: (i, 0, j)),
                  pl.BlockSpec((1, 1, tn), lambda i, j: (i, 0, j))],
        out_specs=pl.BlockSpec((1, rows, tn), lambda i, j: (i, 0, j)),
        out_shape=jax.ShapeDtypeStruct((depth, rows, n6), F32),
        compiler_params=_cparams(("arbitrary", "arbitrary")),
        name="ada",
    )(cc, ada_w, ada_b.reshape(depth, 1, n6))


_QW = MLA_HEADS * HEAD_PAD


def _rope(x, c, s):
    lane = lax.broadcasted_iota(jnp.int32, x.shape, 1)
    first_half = ((lane - MLA_NOPE) & ROPE_AXIS_FREQS) == 0
    partner = jnp.where(first_half, pltpu.roll(x, HEAD_PAD - ROPE_AXIS_FREQS, 1), pltpu.roll(x, ROPE_AXIS_FREQS, 1))
    return x * c + partner * s


def _mla_proj_kernel(x_ref, mod_ref, g1_ref, w1_ref, qn_ref, wq_ref, kvn_ref, wk_ref, wv_ref,
                     cq_ref, sq_ref, ck_ref, sk_ref, q_ref, k_ref, vt_ref):
    h = _norm_mod_tile(x_ref, mod_ref, g1_ref, 0)
    d = jnp.dot(h, w1_ref[...], preferred_element_type=F32)
    dq = d[:, :MLA_Q_LORA]
    ckv = d[:, MLA_Q_LORA:MLA_Q_LORA + MLA_KV_LORA]
    kr = d[:, MLA_Q_LORA + MLA_KV_LORA:]
    qn = _rms(dq, qn_ref[...]).astype(BF16)
    qq = jnp.dot(qn, wq_ref[...], preferred_element_type=F32)
    cq = cq_ref[...]
    sq = sq_ref[...]
    c = _rms(ckv, kvn_ref[...]).astype(BF16)
    kr_roped = _rope(kr, ck_ref[...], sk_ref[...])
    kn = jnp.dot(c, wk_ref[...], preferred_element_type=F32)
    for hd in range(MLA_HEADS):
        slab = slice(hd * HEAD_PAD, (hd + 1) * HEAD_PAD)
        q_ref[:, slab] = _rope(qq[:, slab], cq, sq).astype(BF16)
        k_ref[:, slab] = (kn[:, slab] + kr_roped).astype(BF16)
    v = jnp.dot(c, wv_ref[...], preferred_element_type=F32)
    lane = lax.broadcasted_iota(jnp.int32, v.shape, 1)
    vt_ref[...] = jnp.where(lane % HEAD_PAD == MLA_V, 1.0, v).T.astype(BF16)


def _mla_proj(x, modtab, g1, pw, rope):
    n, d = x.shape
    tm = SEG
    full = lambda a: pl.BlockSpec(a.shape, lambda g: (0,) * a.ndim)
    pos = lambda g: (g % SEGS_PER_BATCH, 0)
    cq, sq, ck, sk = rope
    return pl.pallas_call(
        _mla_proj_kernel,
        grid=(n // tm,),
        in_specs=[pl.BlockSpec((tm, d), lambda g: (g, 0)),
                  pl.BlockSpec((tm // SEG, N_MOD, d), lambda g: (g, 0, 0)),
                  full(g1), full(pw["w1"]), full(pw["qn"]), full(pw["wq"]), full(pw["kvn"]),
                  full(pw["wk"]), full(pw["wv"]),
                  pl.BlockSpec((tm, HEAD_PAD), pos), pl.BlockSpec((tm, HEAD_PAD), pos),
                  pl.BlockSpec((tm, HEAD_PAD), pos), pl.BlockSpec((tm, HEAD_PAD), pos)],
        out_specs=[pl.BlockSpec((tm, _QW), lambda g: (g, 0)),
                   pl.BlockSpec((tm, _QW), lambda g: (g, 0)),
                   pl.BlockSpec((_QW, tm), lambda g: (0, g))],
        out_shape=[jax.ShapeDtypeStruct((n, _QW), BF16)] * 2 + [jax.ShapeDtypeStruct((_QW, n), BF16)],
        compiler_params=_cparams(("parallel",)),
        name="mla_proj",
    )(x, modtab, g1, pw["w1"], pw["qn"], pw["wq"], pw["kvn"], pw["wk"], pw["wv"], cq, sq, ck, sk)


def _mla_weights(w_dq, q_norm, w_uq, w_dkv, kv_norm, w_ukv):
    w_kr = jnp.pad(w_dkv[:, MLA_KV_LORA:], ((0, 0), (MLA_NOPE, HEAD_PAD - MLA_QK)))
    w1 = jnp.concatenate([w_dq, w_dkv[:, :MLA_KV_LORA], w_kr], axis=1)
    uq = w_uq.reshape(MLA_Q_LORA, MLA_HEADS, MLA_QK)
    zpad = jnp.zeros((MLA_Q_LORA, MLA_HEADS, HEAD_PAD - MLA_QK), w_uq.dtype)
    wq = jnp.concatenate([uq, zpad], axis=-1).reshape(MLA_Q_LORA, _QW)
    ukv = w_ukv.reshape(MLA_KV_LORA, MLA_HEADS, MLA_NOPE + MLA_V)
    wk = jnp.concatenate([ukv[:, :, :MLA_NOPE],
                          jnp.zeros((MLA_KV_LORA, MLA_HEADS, HEAD_PAD - MLA_NOPE), w_ukv.dtype)], axis=-1)
    wk = wk.reshape(MLA_KV_LORA, _QW)
    wv = jnp.concatenate([ukv[:, :, MLA_NOPE:],
                          jnp.zeros((MLA_KV_LORA, MLA_HEADS, HEAD_PAD - MLA_V), w_ukv.dtype)], axis=-1)
    wv = wv.reshape(MLA_KV_LORA, _QW)
    return {"w1": w1.astype(BF16), "qn": q_norm.reshape(1, -1), "wq": wq.astype(BF16),
            "kvn": kv_norm.reshape(1, -1), "wk": wk.astype(BF16), "wv": wv.astype(BF16)}


def _rope_tables():
    rows = SEQ // GRID_W
    row = jnp.repeat(jnp.arange(rows, dtype=F32), GRID_W)
    col = jnp.tile(jnp.arange(GRID_W, dtype=F32), rows)
    inv_freq = 1.0 / (ROPE_THETA ** (jnp.arange(ROPE_AXIS_FREQS, dtype=F32) / ROPE_AXIS_FREQS))
    ang = jnp.stack([row[:, None] * inv_freq, col[:, None] * inv_freq], axis=1)
    ang = jnp.concatenate([ang, jnp.zeros((CTX_LEN, 2, ROPE_AXIS_FREQS), F32)], axis=0)
    cos = jnp.cos(ang)
    sin = jnp.sin(ang)
    c32 = jnp.stack([cos, cos], axis=2).reshape(T_TOK, MLA_ROPE)
    s32 = jnp.stack([-sin, sin], axis=2).reshape(T_TOK, MLA_ROPE)
    scale = MLA_QK ** -0.5 * math.log2(math.e)
    zq = jnp.zeros((T_TOK, HEAD_PAD - MLA_QK), F32)
    cq = jnp.concatenate([jnp.full((T_TOK, MLA_NOPE), scale, F32), c32 * scale, zq], axis=1)
    sq = jnp.concatenate([jnp.zeros((T_TOK, MLA_NOPE), F32), s32 * scale, zq], axis=1)
    zn = jnp.zeros((T_TOK, MLA_NOPE), F32)
    ck = jnp.concatenate([zn, c32, zq], axis=1)
    sk = jnp.concatenate([zn, s32, zq], axis=1)
    return cq, sq, ck, sk


ATTN_TQ = 1024
ATTN_KEY_CHUNK = 2048
_LAT_CHUNKS = tuple((lo, ATTN_KEY_CHUNK) for lo in range(0, SEQ, ATTN_KEY_CHUNK))
_CTX_CHUNK = ((SEQ, CTX_LEN),)


def _attend_pair(q, k_ref, vt_ref, chunks):
    m = [None, None]
    acc = [None, None]
    for lo, size in chunks:
        for j in range(2):
            cols = slice(j * HEAD_PAD, (j + 1) * HEAD_PAD)
            st = lax.dot_general(k_ref[0, lo:lo + size, cols], q[:, cols], (((1,), (1,)), ((), ())), preferred_element_type=F32)
            mc = jnp.max(st, axis=0, keepdims=True)
            m_new = mc if m[j] is None else jnp.maximum(m[j], mc)
            pv = jnp.dot(vt_ref[cols, lo:lo + size], jnp.exp2(st - m_new).astype(BF16), preferred_element_type=F32)
            acc[j] = pv if m[j] is None else acc[j] * jnp.exp2(m[j] - m_new) + pv
            m[j] = m_new
    outs = [a / a[MLA_V:MLA_V + 1, :] for a in acc]
    return jnp.concatenate([outs[0][:MLA_V], outs[1][:MLA_V]], axis=0).T


def _attn_kernel(q_ref, k_ref, vt_ref, o_ref, *, lat_tiles):
    qi = pl.program_id(2)

    @pl.when(qi < lat_tiles)
    def _():
        o_ref[0] = _attend_pair(q_ref[0], k_ref, vt_ref, _LAT_CHUNKS + _CTX_CHUNK).astype(BF16)

    @pl.when(qi >= lat_tiles)
    def _():
        o_ref[0, :CTX_LEN, :] = _attend_pair(q_ref[0, :CTX_LEN, :], k_ref, vt_ref, _CTX_CHUNK).astype(BF16)


def _attention(q, k, vt):
    b = q.shape[0]
    tq = ATTN_TQ
    lat_tiles = SEQ // tq
    return pl.pallas_call(
        functools.partial(_attn_kernel, lat_tiles=lat_tiles),
        grid=(b, MLA_HEADS // 2, lat_tiles + 1),
        in_specs=[pl.BlockSpec((1, tq, 2 * HEAD_PAD), lambda bi, hp, qi: (bi, qi, hp)),
                  pl.BlockSpec((1, T_TOK, 2 * HEAD_PAD), lambda bi, hp, qi: (bi, 0, hp)),
                  pl.BlockSpec((2 * HEAD_PAD, T_TOK), lambda bi, hp, qi: (hp, bi))],
        out_specs=pl.BlockSpec((1, tq, 2 * MLA_V), lambda bi, hp, qi: (bi, qi, hp)),
        out_shape=jax.ShapeDtypeStruct((b, T_TOK, MLA_HEADS * MLA_V), BF16),
        compiler_params=_cparams(("parallel", "parallel", "arbitrary")),
        name="attention",
    )(q, k, vt)


def _gated_add(x_ref, mod_ref, y, gate_idx, o_ref):
    for s in range(x_ref.shape[0] // SEG):
        rows = slice(s * SEG, (s + 1) * SEG)
        g = mod_ref[s][gate_idx:gate_idx + 1]
        o_ref[rows, :] = x_ref[rows, :] + g * y[rows, :]


def _proj_res_kernel(x_ref, mod_ref, y_ref, w_ref, o_ref):
    y = jnp.dot(y_ref[...], w_ref[...], preferred_element_type=F32)
    _gated_add(x_ref, mod_ref, y, 2, o_ref)


def _proj_res(x, modtab, y, w, tm=512):
    n, d = x.shape
    return pl.pallas_call(
        _proj_res_kernel,
        grid=(n // tm,),
        in_specs=[pl.BlockSpec((tm, d), lambda g: (g, 0)),
                  pl.BlockSpec((tm // SEG, N_MOD, d), lambda g: (g, 0, 0)),
                  pl.BlockSpec((tm, y.shape[1]), lambda g: (g, 0)),
                  pl.BlockSpec(w.shape, lambda g: (0, 0))],
        out_specs=pl.BlockSpec((tm, d), lambda g: (g, 0)),
        out_shape=jax.ShapeDtypeStruct((n, d), F32),
        compiler_params=_cparams(("parallel",)),
        name="proj_res",
    )(x, modtab, y, w)


_S5_LAT_CHUNKS = SEQ // S5_CHUNK
_S5_CHUNKS = T_TOK // S5_CHUNK
_S5_W = S5_CHUNK * S5_GROUP
_S5_BLK_CHUNKS = 2
_S5_BLK_TOK = _S5_BLK_CHUNKS * S5_CHUNK
LANES = 128
_SLOTS = LANES // S5_GROUP


def _s5_in_kernel(x_ref, mod_ref, g_ref, u_ref, h_scr, *, nb):
    for b in range(nb):
        m = mod_ref[0, b]
        h = _norm_mod(x_ref[b], g_ref[...], m[0:1], m[1:2])
        for j in range(D_MODEL // LANES):
            h_scr[j, b * _S5_BLK_TOK:(b + 1) * _S5_BLK_TOK, :] = h[:, j * LANES:(j + 1) * LANES]
    slot = lax.broadcasted_iota(jnp.int32, (_S5_BLK_CHUNKS * nb, LANES), 1) // S5_GROUP
    for j in range(D_MODEL // LANES):
        for half in range(_S5_W // LANES):
            srcs = []
            for p in range(_SLOTS):
                t = _SLOTS * half + p
                rows = [h_scr[j, pl.ds(c * S5_CHUNK + t, nb, stride=_S5_BLK_TOK), :] for c in range(_S5_BLK_CHUNKS)]
                srcs.append(jnp.concatenate(rows, axis=0))
            for gs in range(_SLOTS):
                acc = None
                for p in range(_SLOTS):
                    k = (p - gs) % _SLOTS
                    r = pltpu.roll(srcs[p], k * S5_GROUP, 1) if k else srcs[p]
                    acc = r if acc is None else jnp.where(slot == p, r, acc)
                u_ref[_SLOTS * j + gs, :, half * LANES:(half + 1) * LANES] = acc.astype(BF16)


def _s5_in(x3, modsel, g1):
    nb, t, d = x3.shape
    nblk = t // _S5_BLK_TOK
    rows = _S5_BLK_CHUNKS * nb
    return pl.pallas_call(
        functools.partial(_s5_in_kernel, nb=nb),
        grid=(nblk,),
        in_specs=[pl.BlockSpec((nb, _S5_BLK_TOK, d), lambda k: (0, k, 0)),
                  pl.BlockSpec((1, nb, N_MOD, d), lambda k: (k // (SEQ // _S5_BLK_TOK), 0, 0, 0)),
                  pl.BlockSpec(g1.shape, lambda k: (0, 0))],
        out_specs=pl.BlockSpec((S5_GROUPS, rows, _S5_W), lambda k: (0, k, 0)),
        out_shape=jax.ShapeDtypeStruct((S5_GROUPS, nblk * rows, _S5_W), BF16),
        scratch_shapes=[pltpu.VMEM((d // LANES, nb * _S5_BLK_TOK, LANES), F32)],
        compiler_params=_cparams(("parallel",)),
        name="s5_in",
    )(x3, modsel, g1)


def _s5_out_kernel(x_ref, mod_ref, y_ref, w_ref, o_ref, nat_scr, *, nb):
    d = x_ref.shape[2]
    slot = lax.broadcasted_iota(jnp.int32, (nb, LANES), 1) // S5_GROUP
    for j in range(d // LANES):
        for half in range(_S5_W // LANES):
            for c in range(_S5_BLK_CHUNKS):
                srcs = [y_ref[_SLOTS * j + gs, c * nb:(c + 1) * nb, half * LANES:(half + 1) * LANES] for gs in range(_SLOTS)]
                for p in range(_SLOTS):
                    acc = None
                    for gs in range(_SLOTS):
                        k = (gs - p) % _SLOTS
                        r = pltpu.roll(srcs[gs], k * S5_GROUP, 1) if k else srcs[gs]
                        acc = r if acc is None else jnp.where(slot == gs, r, acc)
                    nat_scr[j, pl.ds(c * S5_CHUNK + _SLOTS * half + p, nb, stride=_S5_BLK_TOK), :] = acc
    y_nat = jnp.concatenate([nat_scr[j] for j in range(d // LANES)], axis=1)
    g = jax.nn.gelu(y_nat).astype(BF16)
    z = jnp.dot(g, w_ref[...], preferred_element_type=F32)
    y = z[:, :d] * jax.nn.sigmoid(z[:, d:])
    for b in range(nb):
        gate = mod_ref[0, b][2:3]
        o_ref[b] = x_ref[b] + gate * y[b * _S5_BLK_TOK:(b + 1) * _S5_BLK_TOK, :]


def _s5_out(x3, modsel, y, w):
    nb, t, d = x3.shape
    nblk = t // _S5_BLK_TOK
    rows = _S5_BLK_CHUNKS * nb
    return pl.pallas_call(
        functools.partial(_s5_out_kernel, nb=nb),
        grid=(nblk,),
        in_specs=[pl.BlockSpec((nb, _S5_BLK_TOK, d), lambda k: (0, k, 0)),
                  pl.BlockSpec((1, nb, N_MOD, d), lambda k: (k // (SEQ // _S5_BLK_TOK), 0, 0, 0)),
                  pl.BlockSpec((S5_GROUPS, rows, _S5_W), lambda k: (0, k, 0)),
                  pl.BlockSpec(w.shape, lambda k: (0, 0))],
        out_specs=pl.BlockSpec((nb, _S5_BLK_TOK, d), lambda k: (0, k, 0)),
        out_shape=jax.ShapeDtypeStruct((nb, t, d), F32),
        scratch_shapes=[pltpu.VMEM((d // LANES, nb * _S5_BLK_TOK, LANES), F32)],
        compiler_params=_cparams(("parallel",)),
        name="s5_out",
    )(x3, modsel, y, w)


def _s5_kernel(u_ref, m_ref, win_ref, wof_ref, wor_ref, ar_ref, ai_ref, y_ref, z_ref, sf_ref, sr_ref, *, nb):
    u = u_ref[0]
    z_ref[...] = jnp.dot(u, win_ref[0], preferred_element_type=F32)
    half = _S5_W // 2
    ar = ar_ref[0]
    ai = ai_ref[0]
    lane = lax.broadcasted_iota(jnp.int32, (nb, _S5_W), 1)
    is_fwd = (lane % half) < S5_STATE

    def step(i, s):
        cf = jnp.where(i < _S5_CHUNKS - _S5_LAT_CHUNKS, i + _S5_LAT_CHUNKS, i - (_S5_CHUNKS - _S5_LAT_CHUNKS))
        cr = _S5_CHUNKS - 1 - i
        rf = pl.multiple_of(cf * nb, nb)
        rr = pl.multiple_of(cr * nb, nb)
        sf_ref[pl.ds(rf, nb), :] = s
        sr_ref[pl.ds(rr, nb), :] = s
        z = jnp.where(is_fwd, z_ref[pl.ds(rf, nb), :], z_ref[pl.ds(rr, nb), :])
        re = s[:, :half]
        im = s[:, half:]
        return jnp.concatenate([ar * re - ai * im + z[:, :half], ar * im + ai * re + z[:, half:]], axis=1)

    lax.fori_loop(0, _S5_CHUNKS, step, jnp.zeros((nb, _S5_W), F32))
    y = jnp.dot(u, m_ref[0], preferred_element_type=F32)
    y = y + jnp.dot(sf_ref[...].astype(BF16), wof_ref[0], preferred_element_type=F32)
    y = y + jnp.dot(sr_ref[...].astype(BF16), wor_ref[0], preferred_element_type=F32)
    y_ref[0] = y


def _s5_core(u, sp, nb):
    g, rows, w = u.shape
    blk = lambda a: pl.BlockSpec((1,) + a.shape[1:], lambda i: (i,) + (0,) * (a.ndim - 1))
    return pl.pallas_call(
        functools.partial(_s5_kernel, nb=nb),
        grid=(g,),
        in_specs=[blk(u), blk(sp["m"]), blk(sp["win"]), blk(sp["wof"]), blk(sp["wor"]), blk(sp["ar"]), blk(sp["ai"])],
        out_specs=pl.BlockSpec((1, rows, w), lambda i: (i, 0, 0)),
        out_shape=jax.ShapeDtypeStruct((g, rows, w), F32),
        scratch_shapes=[pltpu.VMEM((rows, w), F32), pltpu.VMEM((rows, w), F32), pltpu.VMEM((rows, w), F32)],
        compiler_params=_cparams(("parallel",)),
        name="s5_core",
    )(u, sp["m"], sp["win"], sp["wof"], sp["wor"], sp["ar"], sp["ai"])


def _s5_params(a_re, a_im, log_dt, b_re, b_im, c_re, c_im, d):
    c64 = jnp.complex64
    lam = lax.complex(jnp.minimum(a_re, S5_MAX_RE), a_im)
    lam_dt = lam * jnp.exp(log_dt)[..., None]
    lam_bar = jnp.exp(lam_dt)
    b_bar = ((lam_bar - 1.0) / lam)[..., None] * lax.complex(b_re, b_im)
    c_mat = lax.complex(c_re, c_im)
    taus = jnp.arange(S5_CHUNK + 1, dtype=F32)
    pw = jnp.exp(lam_dt[None] * taus[:, None, None, None].astype(c64))
    hi = lax.Precision.HIGHEST
    kern = jnp.real(jnp.einsum('dgnp,tdgp,dgpm->dgtnm', c_mat, pw[:S5_CHUNK], b_bar, precision=hi))
    eye = jnp.eye(S5_GROUP, dtype=F32)
    k0 = kern[0, :, 0] + kern[1, :, 0] + d.reshape(S5_GROUPS, S5_GROUP)[:, :, None] * eye
    lags = jnp.concatenate([kern[1, :, :0:-1], k0[:, None], kern[0, :, 1:]], axis=1)
    idx = (np.arange(S5_CHUNK)[None, :] - np.arange(S5_CHUNK)[:, None]) + S5_CHUNK - 1
    m = lags[:, idx]
    m = m.transpose(0, 1, 4, 2, 3).reshape(S5_GROUPS, _S5_W, _S5_W)
    wf = pw[S5_CHUNK - 1::-1][:S5_CHUNK, 0][..., None] * b_bar[0][None]
    wr = pw[:S5_CHUNK, 1][..., None] * b_bar[1][None]
    to_rows = lambda w: w.transpose(1, 0, 3, 2).reshape(S5_GROUPS, _S5_W, S5_STATE)
    wf, wr = to_rows(wf), to_rows(wr)
    win = jnp.concatenate([jnp.real(wf), jnp.real(wr), jnp.imag(wf), jnp.imag(wr)], axis=-1)
    of = c_mat[0][None] * pw[1:, 0][:, :, None, :]
    orv = c_mat[1][None] * pw[S5_CHUNK:0:-1, 1][:, :, None, :]
    to_cols = lambda w: w.transpose(1, 3, 0, 2).reshape(S5_GROUPS, S5_STATE, _S5_W)
    of, orv = to_cols(of), to_cols(orv)
    zeros = jnp.zeros_like(jnp.real(of))
    wof = jnp.concatenate([jnp.real(of), zeros, -jnp.imag(of), zeros], axis=1)
    wor = jnp.concatenate([zeros, jnp.real(orv), zeros, -jnp.imag(orv)], axis=1)
    a16 = pw[S5_CHUNK]
    ar = jnp.concatenate([jnp.real(a16[0]), jnp.real(a16[1])], axis=-1)[:, None, :]
    ai = jnp.concatenate([jnp.imag(a16[0]), jnp.imag(a16[1])], axis=-1)[:, None, :]
    return {"m": m.astype(BF16), "win": win.astype(BF16), "wof": wof.astype(BF16), "wor": wor.astype(BF16),
            "ar": ar, "ai": ai}


POOL_HALO = 16


def _pool_kernel(xp_ref, xc_ref, xn_ref, mod_ref, g_ref, wp_ref, sc_ref, o_ref):
    seg = pl.program_id(0) % SEGS_PER_BATCH
    is_ctx = seg >= LAT_SEGS
    p0 = jnp.where(is_ctx, 0, seg * SEG)
    lseq = jnp.where(is_ctx, CTX_LEN, SEQ)
    m = mod_ref[0]
    g = g_ref[...]
    hs = [_norm_mod(r[...], g, m[0:1], m[1:2]) for r in (xp_ref, xc_ref, xn_ref)]
    hcat = jnp.concatenate([h.astype(BF16) for h in hs], axis=0)
    r = lax.broadcasted_iota(jnp.int32, (SEG, SEG + 2 * POOL_HALO), 0)
    s = lax.broadcasted_iota(jnp.int32, (SEG, SEG + 2 * POOL_HALO), 1)
    pt = p0 + r
    ps = p0 - POOL_HALO + s
    rcol = lax.broadcasted_iota(jnp.int32, (SEG, 1), 0) + p0
    x = xc_ref[...]
    gate = m[2:3]
    for gi, w in enumerate(POOL_WINDOWS):
        cols = slice(gi * POOL_GROUP, (gi + 1) * POOL_GROUP)
        lo = jnp.maximum(pt - w // 2, 0)
        hi = jnp.minimum(pt + w - w // 2, lseq)
        band = jnp.where((ps >= lo) & (ps < hi), 1.0, 0.0).astype(BF16)
        tot = jnp.dot(band, hcat[:, cols], preferred_element_type=F32)
        cnt = (jnp.minimum(rcol + w - w // 2, lseq) - jnp.maximum(rcol - w // 2, 0)).astype(F32)
        resid = tot / cnt - hs[1][:, cols]
        y = jnp.dot(resid.astype(BF16), wp_ref[gi], preferred_element_type=F32) * sc_ref[:, cols]
        o_ref[:, cols] = x[:, cols] + gate[:, cols] * y


def _pool(x, modtab, g1, w_pool, scale):
    n, d = x.shape
    nseg = n // SEG

    halo_per_seg = SEG // POOL_HALO

    def prev(g):
        return (jnp.maximum(g * halo_per_seg - 1, 0), 0)

    def nxt(g):
        return (jnp.minimum((g + 1) * halo_per_seg, nseg * halo_per_seg - 1), 0)

    return pl.pallas_call(
        _pool_kernel,
        grid=(nseg,),
        in_specs=[pl.BlockSpec((POOL_HALO, d), prev), pl.BlockSpec((SEG, d), lambda g: (g, 0)), pl.BlockSpec((POOL_HALO, d), nxt),
                  pl.BlockSpec((1, N_MOD, d), lambda g: (g, 0, 0)),
                  pl.BlockSpec(g1.shape, lambda g: (0, 0)),
                  pl.BlockSpec(w_pool.shape, lambda g: (0, 0, 0)),
                  pl.BlockSpec(scale.shape, lambda g: (0, 0))],
        out_specs=pl.BlockSpec((SEG, d), lambda g: (g, 0)),
        out_shape=jax.ShapeDtypeStruct((n, d), F32),
        compiler_params=_cparams(("parallel",)),
        name="pool",
    )(x, x, x, modtab, g1, w_pool, scale)


def _swiglu_partial(h, wg, wu, wd):
    a = jax.nn.silu(jnp.dot(h, wg, preferred_element_type=F32)) * jnp.dot(h, wu, preferred_element_type=F32)
    return jnp.dot(a.astype(BF16), wd, preferred_element_type=F32)


def _ffn_kernel(x_ref, mod_ref, g_ref, wg_ref, wu_ref, wd_ref, o_ref):
    h = _norm_mod_tile(x_ref, mod_ref, g_ref, 3)
    _gated_add(x_ref, mod_ref, _swiglu_partial(h, wg_ref[...], wu_ref[...], wd_ref[...]), 5, o_ref)


_RESIDENT = pl.Buffered(1)


def _ffn(x, modtab, g2, w_gu, w_down, layer, tm=512):
    n, d = x.shape
    f = w_down.shape[1]
    return pl.pallas_call(
        _ffn_kernel,
        grid=(n // tm,),
        in_specs=[pl.BlockSpec((tm, d), lambda g: (g, 0)),
                  pl.BlockSpec((tm // SEG, N_MOD, d), lambda g: (g, 0, 0)),
                  pl.BlockSpec(g2.shape, lambda g: (0, 0)),
                  pl.BlockSpec((None, d, f), lambda g: (layer, 0, 0), pipeline_mode=_RESIDENT),
                  pl.BlockSpec((None, d, f), lambda g: (layer, 0, 1), pipeline_mode=_RESIDENT),
                  pl.BlockSpec((None, f, d), lambda g: (layer, 0, 0), pipeline_mode=_RESIDENT)],
        out_specs=pl.BlockSpec((tm, d), lambda g: (g, 0)),
        out_shape=jax.ShapeDtypeStruct((n, d), F32),
        compiler_params=_cparams(("parallel",)),
        name="ffn",
    )(x, modtab, g2, w_gu, w_gu, w_down)


MOE_TILE = 512
MOE_DMA_TILE = 512


def _router_kernel(x_ref, mod_ref, g_ref, w_ref, b_ref, h_ref, rw_ref, ri_ref, cnt_ref, run_ref):
    @pl.when(pl.program_id(0) == 0)
    def _():
        run_ref[...] = jnp.zeros_like(run_ref)

    h = _norm_mod_tile(x_ref, mod_ref, g_ref, 3, out_dtype=F32)
    h_ref[...] = h
    logits = jnp.dot(h, w_ref[...], precision=lax.Precision.HIGHEST, preferred_element_type=F32) + b_ref[...]
    lane = lax.broadcasted_iota(jnp.int32, logits.shape, 1)
    neg = jnp.float32(-jnp.inf)
    lg = jnp.where(lane < N_EXPERTS, logits, neg)
    m1 = jnp.max(lg, axis=-1, keepdims=True)
    i1 = jnp.min(jnp.where(lg == m1, lane, HEAD_PAD), axis=-1, keepdims=True)
    lg2 = jnp.where(lane == i1, neg, lg)
    m2 = jnp.max(lg2, axis=-1, keepdims=True)
    i2 = jnp.min(jnp.where(lg2 == m2, lane, HEAD_PAD), axis=-1, keepdims=True)
    e2 = jnp.exp(m2 - m1)
    den = 1.0 + e2
    tm = logits.shape[0]
    sel = jnp.where((lane == i1) | (lane == i2), 1.0, 0.0).astype(BF16)
    tri = jnp.where(lax.broadcasted_iota(jnp.int32, (tm, tm), 1) <= lax.broadcasted_iota(jnp.int32, (tm, tm), 0), 1.0, 0.0)
    within = jnp.dot(tri.astype(BF16), sel, preferred_element_type=F32)
    rank = run_ref[0:1, :] + within - 1.0
    r1 = jnp.sum(jnp.where(lane == i1, rank, 0.0), axis=-1, keepdims=True).astype(jnp.int32)
    r2 = jnp.sum(jnp.where(lane == i2, rank, 0.0), axis=-1, keepdims=True).astype(jnp.int32)
    total = run_ref[0:1, :] + within[tm - 1:tm, :]
    run_ref[...] = jnp.broadcast_to(total, run_ref.shape)
    cnt_ref[...] = jnp.broadcast_to(total, cnt_ref.shape).astype(jnp.int32)
    rw_ref[...] = jnp.where(lane == 0, 1.0 / den, jnp.where(lane == 1, e2 / den, 0.0))
    ri_ref[...] = jnp.where(lane == 0, i1, jnp.where(lane == 1, i2, jnp.where(lane == 2, r1, jnp.where(lane == 3, r2, 0))))


def _router(x, modtab, g2, w_router, b_router, tm=512):
    n, d = x.shape
    wr = jnp.pad(w_router, ((0, 0), (0, HEAD_PAD - N_EXPERTS)))
    br = jnp.pad(b_router, (0, HEAD_PAD - N_EXPERTS)).reshape(1, HEAD_PAD)
    return pl.pallas_call(
        _router_kernel,
        grid=(n // tm,),
        in_specs=[pl.BlockSpec((tm, d), lambda g: (g, 0)),
                  pl.BlockSpec((tm // SEG, N_MOD, d), lambda g: (g, 0, 0)),
                  pl.BlockSpec(g2.shape, lambda g: (0, 0)),
                  pl.BlockSpec(wr.shape, lambda g: (0, 0)),
                  pl.BlockSpec(br.shape, lambda g: (0, 0))],
        out_specs=[pl.BlockSpec((tm, d), lambda g: (g, 0)),
                   pl.BlockSpec((tm, HEAD_PAD), lambda g: (g, 0)),
                   pl.BlockSpec((tm, HEAD_PAD), lambda g: (g, 0)),
                   pl.BlockSpec((8, HEAD_PAD), lambda g: (0, 0))],
        out_shape=[jax.ShapeDtypeStruct((n, d), F32),
                   jax.ShapeDtypeStruct((n, HEAD_PAD), F32),
                   jax.ShapeDtypeStruct((n, HEAD_PAD), jnp.int32),
                   jax.ShapeDtypeStruct((8, HEAD_PAD), jnp.int32)],
        scratch_shapes=[pltpu.VMEM((8, HEAD_PAD), F32)],
        compiler_params=_cparams(("arbitrary",)),
        name="router",
    )(x, modtab, g2, wr, br)


def _moe_plan(ri, cnt, n):
    e = ri[:, :TOP_K]
    counts = cnt[0, :N_EXPERTS]
    padded = (counts + MOE_TILE - 1) // MOE_TILE * MOE_TILE
    ends = jnp.cumsum(padded)
    starts = ends - padded
    dest = starts[e] + ri[:, TOP_K:2 * TOP_K]
    n_tiles = (TOP_K * n) // MOE_TILE + N_EXPERTS
    n_valid = (ends[-1] // MOE_TILE).astype(jnp.int32).reshape(1)
    tile_ids = jnp.arange(n_tiles, dtype=jnp.int32)
    tile_expert = jnp.searchsorted(ends // MOE_TILE, jnp.minimum(tile_ids, n_valid - 1), side="right").astype(jnp.int32)
    seg_len = jnp.concatenate([padded - counts, (n_tiles * MOE_TILE - ends[-1])[None]])
    seg_base = jnp.concatenate([starts + counts, ends[-1:]])
    seg_end = jnp.cumsum(seg_len)
    k = jnp.arange(N_EXPERTS * MOE_TILE, dtype=jnp.int32)
    seg = jnp.searchsorted(seg_end, k, side="right")
    pad_dest = seg_base[seg] + k - (seg_end - seg_len)[seg]
    return dest[:, 0].astype(jnp.int32), dest[:, 1].astype(jnp.int32), pad_dest.astype(jnp.int32), tile_expert, n_valid, n_tiles


def _row(ref, i):
    return ref.at[pl.ds(i, 1)]


def _scatter_rows_kernel(d0_ref, d1_ref, pad_ref, h_ref, o_hbm, zero_ref, sem):
    @pl.when(pl.program_id(0) == 0)
    def _():
        zero_ref[...] = jnp.zeros_like(zero_ref)

        def issue_zero(k, carry):
            pltpu.make_async_copy(_row(zero_ref, 0), _row(o_hbm, pad_ref[k]), sem.at[2]).start()
            return carry

        lax.fori_loop(0, N_EXPERTS * MOE_TILE, issue_zero, 0, unroll=8)
        for _ in range(N_EXPERTS * MOE_TILE // MOE_DMA_TILE):
            pltpu.make_async_copy(h_ref, o_hbm.at[pl.ds(0, MOE_DMA_TILE)], sem.at[2]).wait()

    base = pl.program_id(0) * MOE_DMA_TILE

    def issue(r, carry):
        t = base + r
        pltpu.make_async_copy(_row(h_ref, r), _row(o_hbm, d0_ref[t]), sem.at[0]).start()
        pltpu.make_async_copy(_row(h_ref, r), _row(o_hbm, d1_ref[t]), sem.at[1]).start()
        return carry

    lax.fori_loop(0, MOE_DMA_TILE, issue, 0, unroll=8)
    pltpu.make_async_copy(h_ref, o_hbm.at[pl.ds(0, MOE_DMA_TILE)], sem.at[0]).wait()
    pltpu.make_async_copy(h_ref, o_hbm.at[pl.ds(0, MOE_DMA_TILE)], sem.at[1]).wait()


def _scatter_rows(d0, d1, pad_dest, h, n_rows):
    n, d = h.shape
    return pl.pallas_call(
        _scatter_rows_kernel,
        grid_spec=pltpu.PrefetchScalarGridSpec(
            num_scalar_prefetch=3,
            grid=(n // MOE_DMA_TILE,),
            in_specs=[pl.BlockSpec((MOE_DMA_TILE, d), lambda g, d0, d1, pd: (g, 0))],
            out_specs=pl.BlockSpec(memory_space=pl.ANY),
            scratch_shapes=[pltpu.VMEM((8, d), F32), pltpu.SemaphoreType.DMA((3,))]),
        out_shape=jax.ShapeDtypeStruct((n_rows, d), F32),
        compiler_params=_cparams(("arbitrary",)),
        name="moe_scatter",
    )(d0, d1, pad_dest, h)


def _moe_ffn_kernel(te_ref, nv_ref, xs_ref, wg_ref, wu_ref, wd_ref, o_ref):
    del te_ref
    i = pl.program_id(0)

    @pl.when(i < nv_ref[0])
    def _():
        o_ref[...] = _swiglu_partial(xs_ref[...].astype(BF16), wg_ref[...], wu_ref[...], wd_ref[...])

    @pl.when(i >= nv_ref[0])
    def _():
        o_ref[...] = jnp.zeros_like(o_ref)


def _moe_ffn(xs, n_rows, tile_expert, n_valid, w_gu, w_down, layer):
    d = xs.shape[1]
    rows = n_rows
    f = w_down.shape[2]
    return pl.pallas_call(
        _moe_ffn_kernel,
        grid_spec=pltpu.PrefetchScalarGridSpec(
            num_scalar_prefetch=2,
            grid=(rows // MOE_TILE,),
            in_specs=[pl.BlockSpec((MOE_TILE, d), lambda i, te, nv: (i, 0)),
                      pl.BlockSpec((None, None, d, f), lambda i, te, nv: (layer, te[i], 0, 0), pipeline_mode=_RESIDENT),
                      pl.BlockSpec((None, None, d, f), lambda i, te, nv: (layer, te[i], 0, 1), pipeline_mode=_RESIDENT),
                      pl.BlockSpec((None, None, f, d), lambda i, te, nv: (layer, te[i], 0, 0), pipeline_mode=_RESIDENT)],
            out_specs=pl.BlockSpec((MOE_TILE, d), lambda i, te, nv: (i, 0))),
        out_shape=jax.ShapeDtypeStruct((rows, d), F32),
        compiler_params=_cparams(("arbitrary",)),
        name="moe_ffn",
    )(tile_expert, n_valid, xs, w_gu, w_gu, w_down)


def _combine_kernel(d0_ref, d1_ref, x_ref, mod_ref, rw_ref, y_hbm, *rest, tm, final):
    if final:
        fg_ref, o_ref, ya_ref, yb_ref, sem = rest
        tile = pl.program_id(0) * SEGS_PER_BATCH + pl.program_id(1)
    else:
        o_ref, ya_ref, yb_ref, sem = rest
        tile = pl.program_id(0)
    base = tile * tm

    def issue(r, carry):
        t = base + r
        pltpu.make_async_copy(_row(y_hbm, d0_ref[t]), _row(ya_ref, r), sem.at[0]).start()
        pltpu.make_async_copy(_row(y_hbm, d1_ref[t]), _row(yb_ref, r), sem.at[1]).start()
        return carry

    lax.fori_loop(0, tm, issue, 0, unroll=8)
    rw = rw_ref[...]
    lane = lax.broadcasted_iota(jnp.int32, rw.shape, 1)
    w1 = jnp.sum(jnp.where(lane == 0, rw, 0.0), axis=-1, keepdims=True)
    w2 = jnp.sum(jnp.where(lane == 1, rw, 0.0), axis=-1, keepdims=True)
    pltpu.make_async_copy(y_hbm.at[pl.ds(0, tm)], ya_ref, sem.at[0]).wait()
    pltpu.make_async_copy(y_hbm.at[pl.ds(0, tm)], yb_ref, sem.at[1]).wait()
    y = w1 * ya_ref[...] + w2 * yb_ref[...]
    if final:
        o_ref[0] = _rms(x_ref[...] + mod_ref[0][5:6] * y, fg_ref[...])
    else:
        _gated_add(x_ref, mod_ref, y, 5, o_ref)


def _combine(d0, d1, x, modtab, rw, y, final_g=None):
    n, d = x.shape
    final = final_g is not None
    tm = SEG if final else MOE_DMA_TILE
    if final:
        nb = n // T_TOK
        grid = (nb, LAT_SEGS)
        row = lambda bi, j, d0, d1: (bi * SEGS_PER_BATCH + j, 0)
        mrow = lambda bi, j, d0, d1: (bi * SEGS_PER_BATCH + j, 0, 0)
        extra = [pl.BlockSpec(final_g.shape, lambda bi, j, d0, d1: (0, 0))]
        out_spec = pl.BlockSpec((1, tm, d), lambda bi, j, d0, d1: (bi, j, 0))
        out_shape = jax.ShapeDtypeStruct((nb, SEQ, d), F32)
        args = (final_g,)
    else:
        grid = (n // tm,)
        row = lambda g, d0, d1: (g, 0)
        mrow = lambda g, d0, d1: (g, 0, 0)
        extra = []
        out_spec = pl.BlockSpec((tm, d), row)
        out_shape = jax.ShapeDtypeStruct((n, d), F32)
        args = ()
    return pl.pallas_call(
        functools.partial(_combine_kernel, tm=tm, final=final),
        grid_spec=pltpu.PrefetchScalarGridSpec(
            num_scalar_prefetch=2,
            grid=grid,
            in_specs=[pl.BlockSpec((tm, d), row),
                      pl.BlockSpec((tm // SEG, N_MOD, d), mrow),
                      pl.BlockSpec((tm, HEAD_PAD), row),
                      pl.BlockSpec(memory_space=pl.ANY)] + extra,
            out_specs=out_spec,
            scratch_shapes=[pltpu.VMEM((tm, d), F32), pltpu.VMEM((tm, d), F32), pltpu.SemaphoreType.DMA((2,))]),
        out_shape=out_shape,
        compiler_params=_cparams(("arbitrary",) * len(grid)),
        name="moe_combine",
    )(d0, d1, x, modtab, rw, y, *args)


def _moe(x, modtab, g2, w_router, b_router, w_gu, w_down, layer, final_g=None):
    n = x.shape[0]
    h, rw, ri, cnt = _router(x, modtab, g2, w_router, b_router)
    d0, d1, pad_dest, tile_expert, n_valid, n_tiles = _moe_plan(ri, cnt, n)
    xs = _scatter_rows(d0, d1, pad_dest, h, n_tiles * MOE_TILE)
    ys = _moe_ffn(xs, n_tiles * MOE_TILE, tile_expert, n_valid, w_gu, w_down, layer)
    return _combine(d0, d1, x, modtab, rw, ys, final_g)


def kernel(x, c, ctx, c_ctx, norm1_g, norm2_g, ada_w, ada_b, final_norm_g, mla_w_dq, mla_q_norm, mla_w_uq, mla_w_dkv, mla_kv_norm, mla_w_ukv, mla_w_o, s5_a_re, s5_a_im, s5_log_dt, s5_b_re, s5_b_im, s5_c_re, s5_c_im, s5_d, s5_w_glu, pool_w, pool_scale, ffn_w_gu, ffn_w_down, moe_w_router, moe_b_router, moe_w_gu, moe_w_down):
    b, l, d = x.shape
    n = b * T_TOK
    xs = jnp.concatenate([x, ctx], axis=1).reshape(n, d)

    rows = -(-(b + 1) // 8) * 8
    cc = jnp.concatenate([c, c_ctx[None], jnp.zeros((rows - b - 1, d), F32)], axis=0)
    mods = _ada(cc, ada_w, ada_b)
    seg_src = np.array([bi if sj < LAT_SEGS else b for bi in range(b) for sj in range(SEGS_PER_BATCH)], np.int32)

    rope = _rope_tables()
    ffn_w_gu_bf, ffn_w_down_bf = ffn_w_gu.astype(BF16), ffn_w_down.astype(BF16)
    moe_w_gu_bf, moe_w_down_bf = moe_w_gu.astype(BF16), moe_w_down.astype(BF16)
    for i in range(DEPTH):
        last = i == DEPTH - 1
        modtab = mods[i].reshape(rows, N_MOD, d)[seg_src]
        g1 = norm1_g[i].reshape(1, d)
        g2 = norm2_g[i].reshape(1, d)
        kind = i % N_MIXERS
        j = i // N_MIXERS
        if kind == 0:
            pw = _mla_weights(mla_w_dq[j], mla_q_norm[j], mla_w_uq[j], mla_w_dkv[j], mla_kv_norm[j], mla_w_ukv[j])
            q, k, vt = _mla_proj(xs, modtab, g1, pw, rope)
            o = _attention(q.reshape(b, T_TOK, -1), k.reshape(b, T_TOK, -1), vt)
            xs = _proj_res(xs, modtab, o.reshape(n, -1), mla_w_o[j].astype(BF16))
        elif kind == 1:
            mod9 = mods[i].reshape(rows, N_MOD, d)
            modsel = jnp.stack([mod9[:b], jnp.broadcast_to(mod9[b], (b, N_MOD, d))])
            x3 = xs.reshape(b, T_TOK, d)
            sp = _s5_params(s5_a_re[j], s5_a_im[j], s5_log_dt[j], s5_b_re[j], s5_b_im[j], s5_c_re[j], s5_c_im[j], s5_d[j])
            y = _s5_core(_s5_in(x3, modsel, g1), sp, b)
            xs = _s5_out(x3, modsel, y, s5_w_glu[j].astype(BF16)).reshape(n, d)
        else:
            xs = _pool(xs, modtab, g1, pool_w[j].astype(BF16), pool_scale[j].reshape(1, d))
        kk = i // 2
        if i % 2 == 0:
            xs = _ffn(xs, modtab, g2, ffn_w_gu_bf, ffn_w_down_bf, kk)
        else:
            xs = _moe(xs, modtab, g2, moe_w_router[kk], moe_b_router[kk], moe_w_gu_bf, moe_w_down_bf, kk,
                      final_norm_g.reshape(1, d) if last else None)
    return xs
```

```python
import functools
import math

import jax
import jax.numpy as jnp
import numpy as np
from jax import lax
from jax.experimental import pallas as pl
from jax.experimental.pallas import tpu as pltpu

F32 = jnp.float32
BF16 = jnp.bfloat16

D_MODEL = 1024
SEQ = 4096
DEPTH = 4
GRID_W = 64
CTX_LEN = 256
N_MIXERS = 3
NORM_EPS = 1e-6

MLA_HEADS = 16
MLA_Q_LORA = 384
MLA_KV_LORA = 256
MLA_NOPE = 64
MLA_ROPE = 32
MLA_V = 64
MLA_QK = MLA_NOPE + MLA_ROPE
ROPE_AXIS_FREQS = MLA_ROPE // 4
ROPE_THETA = 10000.0
HEAD_PAD = 128

S5_GROUP = 16
S5_GROUPS = D_MODEL // S5_GROUP
S5_STATE = 64
S5_MAX_RE = -1e-4
S5_CHUNK = 16

POOL_WINDOWS = (2, 4, 8, 16)
POOL_GROUP = D_MODEL // len(POOL_WINDOWS)

FFN_DIM = 2816
N_EXPERTS = 8
TOP_K = 2

assert DEPTH % 2 == 0
T_TOK = SEQ + CTX_LEN
SEG = CTX_LEN
SEGS_PER_BATCH = T_TOK // SEG
LAT_SEGS = SEQ // SEG
N_MOD = 6

VMEM_LIMIT = 56 * 1024 * 1024


def _cparams(sem, vmem=VMEM_LIMIT):
    return pltpu.CompilerParams(dimension_semantics=sem, vmem_limit_bytes=vmem)


def _rms(x, g):
    return x * lax.rsqrt(jnp.mean(x * x, axis=-1, keepdims=True) + NORM_EPS) * g


def _norm_mod(x, g, shift, scale):
    return _rms(x, g) * (1.0 + scale) + shift


def _norm_mod_tile(x_ref, mod_ref, g_ref, shift_idx, out_dtype=BF16):
    parts = []
    for s in range(x_ref.shape[0] // SEG):
        m = mod_ref[s]
        x = x_ref[s * SEG:(s + 1) * SEG, :]
        parts.append(_norm_mod(x, g_ref[...], m[shift_idx:shift_idx + 1], m[shift_idx + 1:shift_idx + 2]).astype(out_dtype))
    return parts[0] if len(parts) == 1 else jnp.concatenate(parts, axis=0)


def _ada_kernel(c_ref, w_ref, b_ref, o_ref):
    s = jax.nn.silu(c_ref[...])
    o_ref[0] = jnp.dot(s, w_ref[0], precision=lax.Precision.HIGHEST, preferred_element_type=F32) + b_ref[0]


def _ada(cc, ada_w, ada_b):
    depth, d, n6 = ada_w.shape
    rows = cc.shape[0]
    tn = 1024
    return pl.pallas_call(
        _ada_kernel,
        grid=(depth, n6 // tn),
        in_specs=[pl.BlockSpec((rows, d), lambda i, j: (0, 0)),
                  pl.BlockSpec((1, d, tn), lambda i, j: (i, 0, j)),
                  pl.BlockSpec((1, 1, tn), lambda i, j: (i, 0, j))],
        out_specs=pl.BlockSpec((1, rows, tn), lambda i, j: (i, 0, j)),
        out_shape=jax.ShapeDtypeStruct((depth, rows, n6), F32),
        compiler_params=_cparams(("arbitrary", "arbitrary")),
        name="ada",
    )(cc, ada_w, ada_b.reshape(depth, 1, n6))


_QW = MLA_HEADS * HEAD_PAD


def _rope(x, c, s):
    lane = lax.broadcasted_iota(jnp.int32, x.shape, 1)
    first_half = ((lane - MLA_NOPE) & ROPE_AXIS_FREQS) == 0
    partner = jnp.where(first_half, pltpu.roll(x, HEAD_PAD - ROPE_AXIS_FREQS, 1), pltpu.roll(x, ROPE_AXIS_FREQS, 1))
    return x * c + partner * s


def _mla_proj_kernel(x_ref, mod_ref, g1_ref, w1_ref, qn_ref, wq_ref, kvn_ref, wk_ref, wv_ref,
                     cq_ref, sq_ref, ck_ref, sk_ref, q_ref, k_ref, vt_ref):
    h = _norm_mod_tile(x_ref, mod_ref, g1_ref, 0)
    d = jnp.dot(h, w1_ref[...], preferred_element_type=F32)
    dq = d[:, :MLA_Q_LORA]
    ckv = d[:, MLA_Q_LORA:MLA_Q_LORA + MLA_KV_LORA]
    kr = d[:, MLA_Q_LORA + MLA_KV_LORA:]
    qn = _rms(dq, qn_ref[...]).astype(BF16)
    qq = jnp.dot(qn, wq_ref[...], preferred_element_type=F32)
    cq = cq_ref[...]
    sq = sq_ref[...]
    c = _rms(ckv, kvn_ref[...]).astype(BF16)
    kr_roped = _rope(kr, ck_ref[...], sk_ref[...])
    kn = jnp.dot(c, wk_ref[...], preferred_element_type=F32)
    for hd in range(MLA_HEADS):
        slab = slice(hd * HEAD_PAD, (hd + 1) * HEAD_PAD)
        q_ref[:, slab] = _rope(qq[:, slab], cq, sq).astype(BF16)
        k_ref[:, slab] = (kn[:, slab] + kr_roped).astype(BF16)
    v = jnp.dot(c, wv_ref[...], preferred_element_type=F32)
    lane = lax.broadcasted_iota(jnp.int32, v.shape, 1)
    vt_ref[...] = jnp.where(lane % HEAD_PAD == MLA_V, 1.0, v).T.astype(BF16)


def _mla_proj(x, modtab, g1, pw, rope):
    n, d = x.shape
    tm = SEG
    full = lambda a: pl.BlockSpec(a.shape, lambda g: (0,) * a.ndim)
    pos = lambda g: (g % SEGS_PER_BATCH, 0)
    cq, sq, ck, sk = rope
    return pl.pallas_call(
        _mla_proj_kernel,
        grid=(n // tm,),
        in_specs=[pl.BlockSpec((tm, d), lambda g: (g, 0)),
                  pl.BlockSpec((tm // SEG, N_MOD, d), lambda g: (g, 0, 0)),
                  full(g1), full(pw["w1"]), full(pw["qn"]), full(pw["wq"]), full(pw["kvn"]),
                  full(pw["wk"]), full(pw["wv"]),
                  pl.BlockSpec((tm, HEAD_PAD), pos), pl.BlockSpec((tm, HEAD_PAD), pos),
                  pl.BlockSpec((tm, HEAD_PAD), pos), pl.BlockSpec((tm, HEAD_PAD), pos)],
        out_specs=[pl.BlockSpec((tm, _QW), lambda g: (g, 0)),
                   pl.BlockSpec((tm, _QW), lambda g: (g, 0)),
                   pl.BlockSpec((_QW, tm), lambda g: (0, g))],
        out_shape=[jax.ShapeDtypeStruct((n, _QW), BF16)] * 2 + [jax.ShapeDtypeStruct((_QW, n), BF16)],
        compiler_params=_cparams(("parallel",)),
        name="mla_proj",
    )(x, modtab, g1, pw["w1"], pw["qn"], pw["wq"], pw["kvn"], pw["wk"], pw["wv"], cq, sq, ck, sk)


def _mla_weights(w_dq, q_norm, w_uq, w_dkv, kv_norm, w_ukv):
    w_kr = jnp.pad(w_dkv[:, MLA_KV_LORA:], ((0, 0), (MLA_NOPE, HEAD_PAD - MLA_QK)))
    w1 = jnp.concatenate([w_dq, w_dkv[:, :MLA_KV_LORA], w_kr], axis=1)
    uq = w_uq.reshape(MLA_Q_LORA, MLA_HEADS, MLA_QK)
    zpad = jnp.zeros((MLA_Q_LORA, MLA_HEADS, HEAD_PAD - MLA_QK), w_uq.dtype)
    wq = jnp.concatenate([uq, zpad], axis=-1).reshape(MLA_Q_LORA, _QW)
    ukv = w_ukv.reshape(MLA_KV_LORA, MLA_HEADS, MLA_NOPE + MLA_V)
    wk = jnp.concatenate([ukv[:, :, :MLA_NOPE],
                          jnp.zeros((MLA_KV_LORA, MLA_HEADS, HEAD_PAD - MLA_NOPE), w_ukv.dtype)], axis=-1)
    wk = wk.reshape(MLA_KV_LORA, _QW)
    wv = jnp.concatenate([ukv[:, :, MLA_NOPE:],
                          jnp.zeros((MLA_KV_LORA, MLA_HEADS, HEAD_PAD - MLA_V), w_ukv.dtype)], axis=-1)
    wv = wv.reshape(MLA_KV_LORA, _QW)
    return {"w1": w1.astype(BF16), "qn": q_norm.reshape(1, -1), "wq": wq.astype(BF16),
            "kvn": kv_norm.reshape(1, -1), "wk": wk.astype(BF16), "wv": wv.astype(BF16)}


def _rope_tables():
    rows = SEQ // GRID_W
    row = jnp.repeat(jnp.arange(rows, dtype=F32), GRID_W)
    col = jnp.tile(jnp.arange(GRID_W, dtype=F32), rows)
    inv_freq = 1.0 / (ROPE_THETA ** (jnp.arange(ROPE_AXIS_FREQS, dtype=F32) / ROPE_AXIS_FREQS))
    ang = jnp.stack([row[:, None] * inv_freq, col[:, None] * inv_freq], axis=1)
    ang = jnp.concatenate([ang, jnp.zeros((CTX_LEN, 2, ROPE_AXIS_FREQS), F32)], axis=0)
    cos = jnp.cos(ang)
    sin = jnp.sin(ang)
    c32 = jnp.stack([cos, cos], axis=2).reshape(T_TOK, MLA_ROPE)
    s32 = jnp.stack([-sin, sin], axis=2).reshape(T_TOK, MLA_ROPE)
    scale = MLA_QK ** -0.5 * math.log2(math.e)
    zq = jnp.zeros((T_TOK, HEAD_PAD - MLA_QK), F32)
    cq = jnp.concatenate([jnp.full((T_TOK, MLA_NOPE), scale, F32), c32 * scale, zq], axis=1)
    sq = jnp.concatenate([jnp.zeros((T_TOK, MLA_NOPE), F32), s32 * scale, zq], axis=1)
    zn = jnp.zeros((T_TOK, MLA_NOPE), F32)
    ck = jnp.concatenate([zn, c32, zq], axis=1)
    sk = jnp.concatenate([zn, s32, zq], axis=1)
    return cq, sq, ck, sk


ATTN_TQ = 2048
ATTN_KEY_CHUNK = 1024
_CTX_CHUNK = ((SEQ, CTX_LEN),)
_ALL_CHUNKS = tuple((lo, ATTN_KEY_CHUNK) for lo in range(0, SEQ, ATTN_KEY_CHUNK)) + _CTX_CHUNK


def _attend_pair(q, k_ref, vt_ref, chunks):
    m = [None, None]
    acc = [None, None]
    for lo, size in chunks:
        for j in range(2):
            cols = slice(j * HEAD_PAD, (j + 1) * HEAD_PAD)
            st = lax.dot_general(k_ref[0, lo:lo + size, cols], q[:, cols], (((1,), (1,)), ((), ())), preferred_element_type=F32)
            mc = jnp.max(st, axis=0, keepdims=True)
            m_new = mc if m[j] is None else jnp.maximum(m[j], mc)
            pv = jnp.dot(vt_ref[cols, lo:lo + size], jnp.exp2(st - m_new).astype(BF16), preferred_element_type=F32)
            acc[j] = pv if m[j] is None else acc[j] * jnp.exp2(m[j] - m_new) + pv
            m[j] = m_new
    outs = [a / a[MLA_V:MLA_V + 1, :] for a in acc]
    return jnp.concatenate([outs[0][:MLA_V], outs[1][:MLA_V]], axis=0).T


def _attn_kernel(q_ref, k_ref, vt_ref, o_ref, *, lat_tiles):
    qi = pl.program_id(2)

    @pl.when(qi < lat_tiles)
    def _():
        o_ref[0] = _attend_pair(q_ref[0], k_ref, vt_ref, _ALL_CHUNKS).astype(BF16)

    @pl.when(qi >= lat_tiles)
    def _():
        o_ref[0, :CTX_LEN, :] = _attend_pair(q_ref[0, :CTX_LEN, :], k_ref, vt_ref, _CTX_CHUNK).astype(BF16)


def _attention(q, k, vt):
    b = q.shape[0]
    tq = ATTN_TQ
    lat_tiles = SEQ // tq
    return pl.pallas_call(
        functools.partial(_attn_kernel, lat_tiles=lat_tiles),
        grid=(b, MLA_HEADS // 2, lat_tiles + 1),
        in_specs=[pl.BlockSpec((1, tq, 2 * HEAD_PAD), lambda bi, hp, qi: (bi, qi, hp)),
                  pl.BlockSpec((1, T_TOK, 2 * HEAD_PAD), lambda bi, hp, qi: (bi, 0, hp)),
                  pl.BlockSpec((2 * HEAD_PAD, T_TOK), lambda bi, hp, qi: (hp, bi))],
        out_specs=pl.BlockSpec((1, tq, 2 * MLA_V), lambda bi, hp, qi: (bi, qi, hp)),
        out_shape=jax.ShapeDtypeStruct((b, T_TOK, MLA_HEADS * MLA_V), BF16),
        compiler_params=_cparams(("parallel", "parallel", "arbitrary")),
        name="attention",
    )(q, k, vt)


def _gated_add(x_ref, mod_ref, y, gate_idx, o_ref):
    for s in range(x_ref.shape[0] // SEG):
        rows = slice(s * SEG, (s + 1) * SEG)
        g = mod_ref[s][gate_idx:gate_idx + 1]
        o_ref[rows, :] = x_ref[rows, :] + g * y[rows, :]


def _proj_res_kernel(x_ref, mod_ref, y_ref, w_ref, o_ref):
    y = jnp.dot(y_ref[...], w_ref[...], preferred_element_type=F32)
    _gated_add(x_ref, mod_ref, y, 2, o_ref)


def _proj_res(x, modtab, y, w, tm=512):
    n, d = x.shape
    return pl.pallas_call(
        _proj_res_kernel,
        grid=(n // tm,),
        in_specs=[pl.BlockSpec((tm, d), lambda g: (g, 0)),
                  pl.BlockSpec((tm // SEG, N_MOD, d), lambda g: (g, 0, 0)),
                  pl.BlockSpec((tm, y.shape[1]), lambda g: (g, 0)),
                  pl.BlockSpec(w.shape, lambda g: (0, 0))],
        out_specs=pl.BlockSpec((tm, d), lambda g: (g, 0)),
        out_shape=jax.ShapeDtypeStruct((n, d), F32),
        compiler_params=_cparams(("parallel",)),
        name="proj_res",
    )(x, modtab, y, w)


_S5_LAT_CHUNKS = SEQ // S5_CHUNK
_S5_CHUNKS = T_TOK // S5_CHUNK
_S5_W = S5_CHUNK * S5_GROUP
_S5_BLK_CHUNKS = 2
_S5_BLK_TOK = _S5_BLK_CHUNKS * S5_CHUNK
LANES = 128
_SLOTS = LANES // S5_GROUP


def _s5_in_kernel(x_ref, mod_ref, g_ref, u_ref, h_scr, *, nb):
    for b in range(nb):
        m = mod_ref[0, b]
        h = _norm_mod(x_ref[b], g_ref[...], m[0:1], m[1:2])
        for j in range(D_MODEL // LANES):
            h_scr[j, b * _S5_BLK_TOK:(b + 1) * _S5_BLK_TOK, :] = h[:, j * LANES:(j + 1) * LANES]
    slot = lax.broadcasted_iota(jnp.int32, (_S5_BLK_CHUNKS * nb, LANES), 1) // S5_GROUP
    for j in range(D_MODEL // LANES):
        for half in range(_S5_W // LANES):
            srcs = []
            for p in range(_SLOTS):
                t = _SLOTS * half + p
                rows = [h_scr[j, pl.ds(c * S5_CHUNK + t, nb, stride=_S5_BLK_TOK), :] for c in range(_S5_BLK_CHUNKS)]
                srcs.append(jnp.concatenate(rows, axis=0))
            for gs in range(_SLOTS):
                acc = None
                for p in range(_SLOTS):
                    k = (p - gs) % _SLOTS
                    r = pltpu.roll(srcs[p], k * S5_GROUP, 1) if k else srcs[p]
                    acc = r if acc is None else jnp.where(slot == p, r, acc)
                u_ref[_SLOTS * j + gs, :, half * LANES:(half + 1) * LANES] = acc.astype(BF16)


def _s5_in(x3, modsel, g1):
    nb, t, d = x3.shape
    nblk = t // _S5_BLK_TOK
    rows = _S5_BLK_CHUNKS * nb
    return pl.pallas_call(
        functools.partial(_s5_in_kernel, nb=nb),
        grid=(nblk,),
        in_specs=[pl.BlockSpec((nb, _S5_BLK_TOK, d), lambda k: (0, k, 0)),
                  pl.BlockSpec((1, nb, N_MOD, d), lambda k: (k // (SEQ // _S5_BLK_TOK), 0, 0, 0)),
                  pl.BlockSpec(g1.shape, lambda k: (0, 0))],
        out_specs=pl.BlockSpec((S5_GROUPS, rows, _S5_W), lambda k: (0, k, 0)),
        out_shape=jax.ShapeDtypeStruct((S5_GROUPS, nblk * rows, _S5_W), BF16),
        scratch_shapes=[pltpu.VMEM((d // LANES, nb * _S5_BLK_TOK, LANES), F32)],
        compiler_params=_cparams(("parallel",)),
        name="s5_in",
    )(x3, modsel, g1)


def _s5_out_kernel(x_ref, mod_ref, y_ref, w_ref, o_ref, nat_scr, *, nb):
    d = x_ref.shape[2]
    slot = lax.broadcasted_iota(jnp.int32, (nb, LANES), 1) // S5_GROUP
    for j in range(d // LANES):
        for half in range(_S5_W // LANES):
            for c in range(_S5_BLK_CHUNKS):
                srcs = [y_ref[_SLOTS * j + gs, c * nb:(c + 1) * nb, half * LANES:(half + 1) * LANES] for gs in range(_SLOTS)]
                for p in range(_SLOTS):
                    acc = None
                    for gs in range(_SLOTS):
                        k = (gs - p) % _SLOTS
                        r = pltpu.roll(srcs[gs], k * S5_GROUP, 1) if k else srcs[gs]
                        acc = r if acc is None else jnp.where(slot == gs, r, acc)
                    nat_scr[j, pl.ds(c * S5_CHUNK + _SLOTS * half + p, nb, stride=_S5_BLK_TOK), :] = acc
    y_nat = jnp.concatenate([nat_scr[j] for j in range(d // LANES)], axis=1)
    g = jax.nn.gelu(y_nat).astype(BF16)
    z = jnp.dot(g, w_ref[...], preferred_element_type=F32)
    y = z[:, :d] * jax.nn.sigmoid(z[:, d:])
    for b in range(nb):
        gate = mod_ref[0, b][2:3]
        o_ref[b] = x_ref[b] + gate * y[b * _S5_BLK_TOK:(b + 1) * _S5_BLK_TOK, :]


def _s5_out(x3, modsel, y, w):
    nb, t, d = x3.shape
    nblk = t // _S5_BLK_TOK
    rows = _S5_BLK_CHUNKS * nb
    return pl.pallas_call(
        functools.partial(_s5_out_kernel, nb=nb),
        grid=(nblk,),
        in_specs=[pl.BlockSpec((nb, _S5_BLK_TOK, d), lambda k: (0, k, 0)),
                  pl.BlockSpec((1, nb, N_MOD, d), lambda k: (k // (SEQ // _S5_BLK_TOK), 0, 0, 0)),
                  pl.BlockSpec((S5_GROUPS, rows, _S5_W), lambda k: (0, k, 0)),
                  pl.BlockSpec(w.shape, lambda k: (0, 0))],
        out_specs=pl.BlockSpec((nb, _S5_BLK_TOK, d), lambda k: (0, k, 0)),
        out_shape=jax.ShapeDtypeStruct((nb, t, d), F32),
        scratch_shapes=[pltpu.VMEM((d // LANES, nb * _S5_BLK_TOK, LANES), F32)],
        compiler_params=_cparams(("parallel",)),
        name="s5_out",
    )(x3, modsel, y, w)


def _s5_kernel(u_ref, m_ref, win_ref, wof_ref, wor_ref, ar_ref, ai_ref, y_ref, z_ref, sf_ref, sr_ref, *, nb):
    u = u_ref[0]
    z_ref[...] = jnp.dot(u, win_ref[0], preferred_element_type=F32)
    half = _S5_W // 2
    ar = ar_ref[0]
    ai = ai_ref[0]
    lane = lax.broadcasted_iota(jnp.int32, (nb, _S5_W), 1)
    is_fwd = (lane % half) < S5_STATE

    def step(i, s):
        cf = jnp.where(i < _S5_CHUNKS - _S5_LAT_CHUNKS, i + _S5_LAT_CHUNKS, i - (_S5_CHUNKS - _S5_LAT_CHUNKS))
        cr = _S5_CHUNKS - 1 - i
        rf = pl.multiple_of(cf * nb, nb)
        rr = pl.multiple_of(cr * nb, nb)
        sf_ref[pl.ds(rf, nb), :] = s
        sr_ref[pl.ds(rr, nb), :] = s
        z = jnp.where(is_fwd, z_ref[pl.ds(rf, nb), :], z_ref[pl.ds(rr, nb), :])
        re = s[:, :half]
        im = s[:, half:]
        return jnp.concatenate([ar * re - ai * im + z[:, :half], ar * im + ai * re + z[:, half:]], axis=1)

    lax.fori_loop(0, _S5_CHUNKS, step, jnp.zeros((nb, _S5_W), F32))
    y = jnp.dot(u, m_ref[0], preferred_element_type=F32)
    y = y + jnp.dot(sf_ref[...].astype(BF16), wof_ref[0], preferred_element_type=F32)
    y = y + jnp.dot(sr_ref[...].astype(BF16), wor_ref[0], preferred_element_type=F32)
    y_ref[0] = y


def _s5_core(u, sp, nb):
    g, rows, w = u.shape
    blk = lambda a: pl.BlockSpec((1,) + a.shape[1:], lambda i: (i,) + (0,) * (a.ndim - 1))
    return pl.pallas_call(
        functools.partial(_s5_kernel, nb=nb),
        grid=(g,),
        in_specs=[blk(u), blk(sp["m"]), blk(sp["win"]), blk(sp["wof"]), blk(sp["wor"]), blk(sp["ar"]), blk(sp["ai"])],
        out_specs=pl.BlockSpec((1, rows, w), lambda i: (i, 0, 0)),
        out_shape=jax.ShapeDtypeStruct((g, rows, w), F32),
        scratch_shapes=[pltpu.VMEM((rows, w), F32), pltpu.VMEM((rows, w), F32), pltpu.VMEM((rows, w), F32)],
        compiler_params=_cparams(("parallel",)),
        name="s5_core",
    )(u, sp["m"], sp["win"], sp["wof"], sp["wor"], sp["ar"], sp["ai"])


def _s5_params(a_re, a_im, log_dt, b_re, b_im, c_re, c_im, d):
    c64 = jnp.complex64
    lam = lax.complex(jnp.minimum(a_re, S5_MAX_RE), a_im)
    lam_dt = lam * jnp.exp(log_dt)[..., None]
    lam_bar = jnp.exp(lam_dt)
    b_bar = ((lam_bar - 1.0) / lam)[..., None] * lax.complex(b_re, b_im)
    c_mat = lax.complex(c_re, c_im)
    taus = jnp.arange(S5_CHUNK + 1, dtype=F32)
    pw = jnp.exp(lam_dt[None] * taus[:, None, None, None].astype(c64))
    hi = lax.Precision.HIGHEST
    kern = jnp.real(jnp.einsum('dgnp,tdgp,dgpm->dgtnm', c_mat, pw[:S5_CHUNK], b_bar, precision=hi))
    eye = jnp.eye(S5_GROUP, dtype=F32)
    k0 = kern[0, :, 0] + kern[1, :, 0] + d.reshape(S5_GROUPS, S5_GROUP)[:, :, None] * eye
    lags = jnp.concatenate([kern[1, :, :0:-1], k0[:, None], kern[0, :, 1:]], axis=1)
    idx = (np.arange(S5_CHUNK)[None, :] - np.arange(S5_CHUNK)[:, None]) + S5_CHUNK - 1
    m = lags[:, idx]
    m = m.transpose(0, 1, 4, 2, 3).reshape(S5_GROUPS, _S5_W, _S5_W)
    wf = pw[S5_CHUNK - 1::-1][:S5_CHUNK, 0][..., None] * b_bar[0][None]
    wr = pw[:S5_CHUNK, 1][..., None] * b_bar[1][None]
    to_rows = lambda w: w.transpose(1, 0, 3, 2).reshape(S5_GROUPS, _S5_W, S5_STATE)
    wf, wr = to_rows(wf), to_rows(wr)
    win = jnp.concatenate([jnp.real(wf), jnp.real(wr), jnp.imag(wf), jnp.imag(wr)], axis=-1)
    of = c_mat[0][None] * pw[1:, 0][:, :, None, :]
    orv = c_mat[1][None] * pw[S5_CHUNK:0:-1, 1][:, :, None, :]
    to_cols = lambda w: w.transpose(1, 3, 0, 2).reshape(S5_GROUPS, S5_STATE, _S5_W)
    of, orv = to_cols(of), to_cols(orv)
    zeros = jnp.zeros_like(jnp.real(of))
    wof = jnp.concatenate([jnp.real(of), zeros, -jnp.imag(of), zeros], axis=1)
    wor = jnp.concatenate([zeros, jnp.real(orv), zeros, -jnp.imag(orv)], axis=1)
    a16 = pw[S5_CHUNK]
    ar = jnp.concatenate([jnp.real(a16[0]), jnp.real(a16[1])], axis=-1)[:, None, :]
    ai = jnp.concatenate([jnp.imag(a16[0]), jnp.imag(a16[1])], axis=-1)[:, None, :]
    return {"m": m.astype(BF16), "win": win.astype(BF16), "wof": wof.astype(BF16), "wor": wor.astype(BF16),
            "ar": ar, "ai": ai}


POOL_HALO = 16


def _pool_kernel(xp_ref, xc_ref, xn_ref, mod_ref, g_ref, wp_ref, sc_ref, o_ref):
    seg = pl.program_id(0) % SEGS_PER_BATCH
    is_ctx = seg >= LAT_SEGS
    p0 = jnp.where(is_ctx, 0, seg * SEG)
    lseq = jnp.where(is_ctx, CTX_LEN, SEQ)
    m = mod_ref[0]
    g = g_ref[...]
    hs = [_norm_mod(r[...], g, m[0:1], m[1:2]) for r in (xp_ref, xc_ref, xn_ref)]
    hcat = jnp.concatenate([h.astype(BF16) for h in hs], axis=0)
    r = lax.broadcasted_iota(jnp.int32, (SEG, SEG + 2 * POOL_HALO), 0)
    s = lax.broadcasted_iota(jnp.int32, (SEG, SEG + 2 * POOL_HALO), 1)
    pt = p0 + r
    ps = p0 - POOL_HALO + s
    rcol = lax.broadcasted_iota(jnp.int32, (SEG, 1), 0) + p0
    x = xc_ref[...]
    gate = m[2:3]
    for gi, w in enumerate(POOL_WINDOWS):
        cols = slice(gi * POOL_GROUP, (gi + 1) * POOL_GROUP)
        lo = jnp.maximum(pt - w // 2, 0)
        hi = jnp.minimum(pt + w - w // 2, lseq)
        band = jnp.where((ps >= lo) & (ps < hi), 1.0, 0.0).astype(BF16)
        tot = jnp.dot(band, hcat[:, cols], preferred_element_type=F32)
        cnt = (jnp.minimum(rcol + w - w // 2, lseq) - jnp.maximum(rcol - w // 2, 0)).astype(F32)
        resid = tot / cnt - hs[1][:, cols]
        y = jnp.dot(resid.astype(BF16), wp_ref[gi], preferred_element_type=F32) * sc_ref[:, cols]
        o_ref[:, cols] = x[:, cols] + gate[:, cols] * y


def _pool(x, modtab, g1, w_pool, scale):
    n, d = x.shape
    nseg = n // SEG

    halo_per_seg = SEG // POOL_HALO

    def prev(g):
        return (jnp.maximum(g * halo_per_seg - 1, 0), 0)

    def nxt(g):
        return (jnp.minimum((g + 1) * halo_per_seg, nseg * halo_per_seg - 1), 0)

    return pl.pallas_call(
        _pool_kernel,
        grid=(nseg,),
        in_specs=[pl.BlockSpec((POOL_HALO, d), prev), pl.BlockSpec((SEG, d), lambda g: (g, 0)), pl.BlockSpec((POOL_HALO, d), nxt),
                  pl.BlockSpec((1, N_MOD, d), lambda g: (g, 0, 0)),
                  pl.BlockSpec(g1.shape, lambda g: (0, 0)),
                  pl.BlockSpec(w_pool.shape, lambda g: (0, 0, 0)),
                  pl.BlockSpec(scale.shape, lambda g: (0, 0))],
        out_specs=pl.BlockSpec((SEG, d), lambda g: (g, 0)),
        out_shape=jax.ShapeDtypeStruct((n, d), F32),
        compiler_params=_cparams(("parallel",)),
        name="pool",
    )(x, x, x, modtab, g1, w_pool, scale)


def _swiglu_partial(h, wg, wu, wd):
    a = jax.nn.silu(jnp.dot(h, wg, preferred_element_type=F32)) * jnp.dot(h, wu, preferred_element_type=F32)
    return jnp.dot(a.astype(BF16), wd, preferred_element_type=F32)


def _ffn_kernel(x_ref, mod_ref, g_ref, wg_ref, wu_ref, wd_ref, o_ref):
    h = _norm_mod_tile(x_ref, mod_ref, g_ref, 3)
    _gated_add(x_ref, mod_ref, _swiglu_partial(h, wg_ref[...], wu_ref[...], wd_ref[...]), 5, o_ref)


_RESIDENT = pl.Buffered(1)


def _ffn(x, modtab, g2, w_gu, w_down, layer, tm=512):
    n, d = x.shape
    f = w_down.shape[1]
    return pl.pallas_call(
        _ffn_kernel,
        grid=(n // tm,),
        in_specs=[pl.BlockSpec((tm, d), lambda g: (g, 0)),
                  pl.BlockSpec((tm // SEG, N_MOD, d), lambda g: (g, 0, 0)),
                  pl.BlockSpec(g2.shape, lambda g: (0, 0)),
                  pl.BlockSpec((None, d, f), lambda g: (layer, 0, 0), pipeline_mode=_RESIDENT),
                  pl.BlockSpec((None, d, f), lambda g: (layer, 0, 1), pipeline_mode=_RESIDENT),
                  pl.BlockSpec((None, f, d), lambda g: (layer, 0, 0), pipeline_mode=_RESIDENT)],
        out_specs=pl.BlockSpec((tm, d), lambda g: (g, 0)),
        out_shape=jax.ShapeDtypeStruct((n, d), F32),
        compiler_params=_cparams(("parallel",)),
        name="ffn",
    )(x, modtab, g2, w_gu, w_gu, w_down)


MOE_TILE = 512
MOE_DMA_TILE = 512


def _router_kernel(x_ref, mod_ref, g_ref, w_ref, b_ref, h_ref, rw_ref, ri_ref, cnt_ref, run_ref):
    @pl.when(pl.program_id(0) == 0)
    def _():
        run_ref[...] = jnp.zeros_like(run_ref)

    h = _norm_mod_tile(x_ref, mod_ref, g_ref, 3, out_dtype=F32)
    h_ref[...] = h
    logits = jnp.dot(h, w_ref[...], precision=lax.Precision.HIGHEST, preferred_element_type=F32) + b_ref[...]
    lane = lax.broadcasted_iota(jnp.int32, logits.shape, 1)
    neg = jnp.float32(-jnp.inf)
    lg = jnp.where(lane < N_EXPERTS, logits, neg)
    m1 = jnp.max(lg, axis=-1, keepdims=True)
    i1 = jnp.min(jnp.where(lg == m1, lane, HEAD_PAD), axis=-1, keepdims=True)
    lg2 = jnp.where(lane == i1, neg, lg)
    m2 = jnp.max(lg2, axis=-1, keepdims=True)
    i2 = jnp.min(jnp.where(lg2 == m2, lane, HEAD_PAD), axis=-1, keepdims=True)
    e2 = jnp.exp(m2 - m1)
    den = 1.0 + e2
    tm = logits.shape[0]
    sel = jnp.where((lane == i1) | (lane == i2), 1.0, 0.0).astype(BF16)
    tri = jnp.where(lax.broadcasted_iota(jnp.int32, (tm, tm), 1) <= lax.broadcasted_iota(jnp.int32, (tm, tm), 0), 1.0, 0.0)
    within = jnp.dot(tri.astype(BF16), sel, preferred_element_type=F32)
    rank = run_ref[0:1, :] + within - 1.0
    r1 = jnp.sum(jnp.where(lane == i1, rank, 0.0), axis=-1, keepdims=True).astype(jnp.int32)
    r2 = jnp.sum(jnp.where(lane == i2, rank, 0.0), axis=-1, keepdims=True).astype(jnp.int32)
    total = run_ref[0:1, :] + within[tm - 1:tm, :]
    run_ref[...] = jnp.broadcast_to(total, run_ref.shape)
    cnt_ref[...] = jnp.broadcast_to(total, cnt_ref.shape).astype(jnp.int32)
    rw_ref[...] = jnp.where(lane == 0, 1.0 / den, jnp.where(lane == 1, e2 / den, 0.0))
    ri_ref[...] = jnp.where(lane == 0, i1, jnp.where(lane == 1, i2, jnp.where(lane == 2, r1, jnp.where(lane == 3, r2, 0))))


def _router(x, modtab, g2, w_router, b_router, tm=512):
    n, d = x.shape
    wr = jnp.pad(w_router, ((0, 0), (0, HEAD_PAD - N_EXPERTS)))
    br = jnp.pad(b_router, (0, HEAD_PAD - N_EXPERTS)).reshape(1, HEAD_PAD)
    return pl.pallas_call(
        _router_kernel,
        grid=(n // tm,),
        in_specs=[pl.BlockSpec((tm, d), lambda g: (g, 0)),
                  pl.BlockSpec((tm // SEG, N_MOD, d), lambda g: (g, 0, 0)),
                  pl.BlockSpec(g2.shape, lambda g: (0, 0)),
                  pl.BlockSpec(wr.shape, lambda g: (0, 0)),
                  pl.BlockSpec(br.shape, lambda g: (0, 0))],
        out_specs=[pl.BlockSpec((tm, d), lambda g: (g, 0)),
                   pl.BlockSpec((tm, HEAD_PAD), lambda g: (g, 0)),
                   pl.BlockSpec((tm, HEAD_PAD), lambda g: (g, 0)),
                   pl.BlockSpec((8, HEAD_PAD), lambda g: (0, 0))],
        out_shape=[jax.ShapeDtypeStruct((n, d), F32),
                   jax.ShapeDtypeStruct((n, HEAD_PAD), F32),
                   jax.ShapeDtypeStruct((n, HEAD_PAD), jnp.int32),
                   jax.ShapeDtypeStruct((8, HEAD_PAD), jnp.int32)],
        scratch_shapes=[pltpu.VMEM((8, HEAD_PAD), F32)],
        compiler_params=_cparams(("arbitrary",)),
        name="router",
    )(x, modtab, g2, wr, br)


def _moe_plan(ri, cnt, n):
    e = ri[:, :TOP_K]
    counts = cnt[0, :N_EXPERTS]
    padded = (counts + MOE_TILE - 1) // MOE_TILE * MOE_TILE
    ends = jnp.cumsum(padded)
    starts = ends - padded
    dest = starts[e] + ri[:, TOP_K:2 * TOP_K]
    n_tiles = (TOP_K * n) // MOE_TILE + N_EXPERTS
    n_valid = (ends[-1] // MOE_TILE).astype(jnp.int32).reshape(1)
    tile_ids = jnp.arange(n_tiles, dtype=jnp.int32)
    tile_expert = jnp.searchsorted(ends // MOE_TILE, jnp.minimum(tile_ids, n_valid - 1), side="right").astype(jnp.int32)
    seg_len = jnp.concatenate([padded - counts, (n_tiles * MOE_TILE - ends[-1])[None]])
    seg_base = jnp.concatenate([starts + counts, ends[-1:]])
    seg_end = jnp.cumsum(seg_len)
    k = jnp.arange(N_EXPERTS * MOE_TILE, dtype=jnp.int32)
    seg = jnp.searchsorted(seg_end, k, side="right")
    pad_dest = seg_base[seg] + k - (seg_end - seg_len)[seg]
    return dest[:, 0].astype(jnp.int32), dest[:, 1].astype(jnp.int32), pad_dest.astype(jnp.int32), tile_expert, n_valid, n_tiles


def _row(ref, i):
    return ref.at[pl.ds(i, 1)]


def _scatter_rows_kernel(d0_ref, d1_ref, pad_ref, h_ref, o_hbm, zero_ref, sem):
    @pl.when(pl.program_id(0) == 0)
    def _():
        zero_ref[...] = jnp.zeros_like(zero_ref)

        def issue_zero(k, carry):
            pltpu.make_async_copy(_row(zero_ref, 0), _row(o_hbm, pad_ref[k]), sem.at[2]).start()
            return carry

        lax.fori_loop(0, N_EXPERTS * MOE_TILE, issue_zero, 0, unroll=8)
        for _ in range(N_EXPERTS * MOE_TILE // MOE_DMA_TILE):
            pltpu.make_async_copy(h_ref, o_hbm.at[pl.ds(0, MOE_DMA_TILE)], sem.at[2]).wait()

    base = pl.program_id(0) * MOE_DMA_TILE

    def issue(r, carry):
        t = base + r
        pltpu.make_async_copy(_row(h_ref, r), _row(o_hbm, d0_ref[t]), sem.at[0]).start()
        pltpu.make_async_copy(_row(h_ref, r), _row(o_hbm, d1_ref[t]), sem.at[1]).start()
        return carry

    lax.fori_loop(0, MOE_DMA_TILE, issue, 0, unroll=8)
    pltpu.make_async_copy(h_ref, o_hbm.at[pl.ds(0, MOE_DMA_TILE)], sem.at[0]).wait()
    pltpu.make_async_copy(h_ref, o_hbm.at[pl.ds(0, MOE_DMA_TILE)], sem.at[1]).wait()


def _scatter_rows(d0, d1, pad_dest, h, n_rows):
    n, d = h.shape
    return pl.pallas_call(
        _scatter_rows_kernel,
        grid_spec=pltpu.PrefetchScalarGridSpec(
            num_scalar_prefetch=3,
            grid=(n // MOE_DMA_TILE,),
            in_specs=[pl.BlockSpec((MOE_DMA_TILE, d), lambda g, d0, d1, pd: (g, 0))],
            out_specs=pl.BlockSpec(memory_space=pl.ANY),
            scratch_shapes=[pltpu.VMEM((8, d), F32), pltpu.SemaphoreType.DMA((3,))]),
        out_shape=jax.ShapeDtypeStruct((n_rows, d), F32),
        compiler_params=_cparams(("arbitrary",)),
        name="moe_scatter",
    )(d0, d1, pad_dest, h)


def _moe_ffn_kernel(te_ref, nv_ref, xs_ref, wg_ref, wu_ref, wd_ref, o_ref):
    del te_ref
    i = pl.program_id(0)

    @pl.when(i < nv_ref[0])
    def _():
        o_ref[...] = _swiglu_partial(xs_ref[...].astype(BF16), wg_ref[...], wu_ref[...], wd_ref[...])

    @pl.when(i >= nv_ref[0])
    def _():
        o_ref[...] = jnp.zeros_like(o_ref)


def _moe_ffn(xs, n_rows, tile_expert, n_valid, w_gu, w_down, layer):
    d = xs.shape[1]
    rows = n_rows
    f = w_down.shape[2]
    return pl.pallas_call(
        _moe_ffn_kernel,
        grid_spec=pltpu.PrefetchScalarGridSpec(
            num_scalar_prefetch=2,
            grid=(rows // MOE_TILE,),
            in_specs=[pl.BlockSpec((MOE_TILE, d), lambda i, te, nv: (i, 0)),
                      pl.BlockSpec((None, None, d, f), lambda i, te, nv: (layer, te[i], 0, 0), pipeline_mode=_RESIDENT),
                      pl.BlockSpec((None, None, d, f), lambda i, te, nv: (layer, te[i], 0, 1), pipeline_mode=_RESIDENT),
                      pl.BlockSpec((None, None, f, d), lambda i, te, nv: (layer, te[i], 0, 0), pipeline_mode=_RESIDENT)],
            out_specs=pl.BlockSpec((MOE_TILE, d), lambda i, te, nv: (i, 0))),
        out_shape=jax.ShapeDtypeStruct((rows, d), F32),
        compiler_params=_cparams(("arbitrary",)),
        name="moe_ffn",
    )(tile_expert, n_valid, xs, w_gu, w_gu, w_down)


def _combine_kernel(d0_ref, d1_ref, x_ref, mod_ref, rw_ref, y_hbm, *rest, tm, final):
    if final:
        fg_ref, o_ref, ya_ref, yb_ref, sem = rest
        tile = pl.program_id(0) * SEGS_PER_BATCH + pl.program_id(1)
    else:
        o_ref, ya_ref, yb_ref, sem = rest
        tile = pl.program_id(0)
    base = tile * tm

    def issue(r, carry):
        t = base + r
        pltpu.make_async_copy(_row(y_hbm, d0_ref[t]), _row(ya_ref, r), sem.at[0]).start()
        pltpu.make_async_copy(_row(y_hbm, d1_ref[t]), _row(yb_ref, r), sem.at[1]).start()
        return carry

    lax.fori_loop(0, tm, issue, 0, unroll=8)
    rw = rw_ref[...]
    lane = lax.broadcasted_iota(jnp.int32, rw.shape, 1)
    w1 = jnp.sum(jnp.where(lane == 0, rw, 0.0), axis=-1, keepdims=True)
    w2 = jnp.sum(jnp.where(lane == 1, rw, 0.0), axis=-1, keepdims=True)
    pltpu.make_async_copy(y_hbm.at[pl.ds(0, tm)], ya_ref, sem.at[0]).wait()
    pltpu.make_async_copy(y_hbm.at[pl.ds(0, tm)], yb_ref, sem.at[1]).wait()
    y = w1 * ya_ref[...] + w2 * yb_ref[...]
    if final:
        o_ref[0] = _rms(x_ref[...] + mod_ref[0][5:6] * y, fg_ref[...])
    else:
        _gated_add(x_ref, mod_ref, y, 5, o_ref)


def _combine(d0, d1, x, modtab, rw, y, final_g=None):
    n, d = x.shape
    final = final_g is not None
    tm = SEG if final else MOE_DMA_TILE
    if final:
        nb = n // T_TOK
        grid = (nb, LAT_SEGS)
        row = lambda bi, j, d0, d1: (bi * SEGS_PER_BATCH + j, 0)
        mrow = lambda bi, j, d0, d1: (bi * SEGS_PER_BATCH + j, 0, 0)
        extra = [pl.BlockSpec(final_g.shape, lambda bi, j, d0, d1: (0, 0))]
        out_spec = pl.BlockSpec((1, tm, d), lambda bi, j, d0, d1: (bi, j, 0))
        out_shape = jax.ShapeDtypeStruct((nb, SEQ, d), F32)
        args = (final_g,)
    else:
        grid = (n // tm,)
        row = lambda g, d0, d1: (g, 0)
        mrow = lambda g, d0, d1: (g, 0, 0)
        extra = []
        out_spec = pl.BlockSpec((tm, d), row)
        out_shape = jax.ShapeDtypeStruct((n, d), F32)
        args = ()
    return pl.pallas_call(
        functools.partial(_combine_kernel, tm=tm, final=final),
        grid_spec=pltpu.PrefetchScalarGridSpec(
            num_scalar_prefetch=2,
            grid=grid,
            in_specs=[pl.BlockSpec((tm, d), row),
                      pl.BlockSpec((tm // SEG, N_MOD, d), mrow),
                      pl.BlockSpec((tm, HEAD_PAD), row),
                      pl.BlockSpec(memory_space=pl.ANY)] + extra,
            out_specs=out_spec,
            scratch_shapes=[pltpu.VMEM((tm, d), F32), pltpu.VMEM((tm, d), F32), pltpu.SemaphoreType.DMA((2,))]),
        out_shape=out_shape,
        compiler_params=_cparams(("arbitrary",) * len(grid)),
        name="moe_combine",
    )(d0, d1, x, modtab, rw, y, *args)


def _moe(x, modtab, g2, w_router, b_router, w_gu, w_down, layer, final_g=None):
    n = x.shape[0]
    h, rw, ri, cnt = _router(x, modtab, g2, w_router, b_router)
    d0, d1, pad_dest, tile_expert, n_valid, n_tiles = _moe_plan(ri, cnt, n)
    xs = _scatter_rows(d0, d1, pad_dest, h, n_tiles * MOE_TILE)
    ys = _moe_ffn(xs, n_tiles * MOE_TILE, tile_expert, n_valid, w_gu, w_down, layer)
    return _combine(d0, d1, x, modtab, rw, ys, final_g)


def kernel(x, c, ctx, c_ctx, norm1_g, norm2_g, ada_w, ada_b, final_norm_g, mla_w_dq, mla_q_norm, mla_w_uq, mla_w_dkv, mla_kv_norm, mla_w_ukv, mla_w_o, s5_a_re, s5_a_im, s5_log_dt, s5_b_re, s5_b_im, s5_c_re, s5_c_im, s5_d, s5_w_glu, pool_w, pool_scale, ffn_w_gu, ffn_w_down, moe_w_router, moe_b_router, moe_w_gu, moe_w_down):
    b, l, d = x.shape
    n = b * T_TOK
    xs = jnp.concatenate([x, ctx], axis=1).reshape(n, d)

    rows = -(-(b + 1) // 8) * 8
    cc = jnp.concatenate([c, c_ctx[None], jnp.zeros((rows - b - 1, d), F32)], axis=0)
    mods = _ada(cc, ada_w, ada_b)
    seg_src = np.array([bi if sj < LAT_SEGS else b for bi in range(b) for sj in range(SEGS_PER_BATCH)], np.int32)

    rope = _rope_tables()
    ffn_w_gu_bf, ffn_w_down_bf = ffn_w_gu.astype(BF16), ffn_w_down.astype(BF16)
    moe_w_gu_bf, moe_w_down_bf = moe_w_gu.astype(BF16), moe_w_down.astype(BF16)
    for i in range(DEPTH):
        last = i == DEPTH - 1
        modtab = mods[i].reshape(rows, N_MOD, d)[seg_src]
        g1 = norm1_g[i].reshape(1, d)
        g2 = norm2_g[i].reshape(1, d)
        kind = i % N_MIXERS
        j = i // N_MIXERS
        if kind == 0:
            pw = _mla_weights(mla_w_dq[j], mla_q_norm[j], mla_w_uq[j], mla_w_dkv[j], mla_kv_norm[j], mla_w_ukv[j])
            q, k, vt = _mla_proj(xs, modtab, g1, pw, rope)
            o = _attention(q.reshape(b, T_TOK, -1), k.reshape(b, T_TOK, -1), vt)
            xs = _proj_res(xs, modtab, o.reshape(n, -1), mla_w_o[j].astype(BF16))
        elif kind == 1:
            mod9 = mods[i].reshape(rows, N_MOD, d)
            modsel = jnp.stack([mod9[:b], jnp.broadcast_to(mod9[b], (b, N_MOD, d))])
            x3 = xs.reshape(b, T_TOK, d)
            sp = _s5_params(s5_a_re[j], s5_a_im[j], s5_log_dt[j], s5_b_re[j], s5_b_im[j], s5_c_re[j], s5_c_im[j], s5_d[j])
            y = _s5_core(_s5_in(x3, modsel, g1), sp, b)
            xs = _s5_out(x3, modsel, y, s5_w_glu[j].astype(BF16)).reshape(n, d)
        else:
            xs = _pool(xs, modtab, g1, pool_w[j].astype(BF16), pool_scale[j].reshape(1, d))
        kk = i // 2
        if i % 2 == 0:
            xs = _ffn(xs, modtab, g2, ffn_w_gu_bf, ffn_w_down_bf, kk)
        else:
            xs = _moe(xs, modtab, g2, moe_w_router[kk], moe_b_router[kk], moe_w_gu_bf, moe_w_down_bf, kk,
                      final_norm_g.reshape(1, d) if last else None)
    return xs
```

```python
import functools
import math

import jax
import jax.numpy as jnp
import numpy as np
from jax import lax
from jax.experimental import pallas as pl
from jax.experimental.pallas import tpu as pltpu

F32 = jnp.float32
BF16 = jnp.bfloat16

D_MODEL = 1024
SEQ = 4096
DEPTH = 4
GRID_W = 64
CTX_LEN = 256
N_MIXERS = 3
NORM_EPS = 1e-6

MLA_HEADS = 16
MLA_Q_LORA = 384
MLA_KV_LORA = 256
MLA_NOPE = 64
MLA_ROPE = 32
MLA_V = 64
MLA_QK = MLA_NOPE + MLA_ROPE
ROPE_AXIS_FREQS = MLA_ROPE // 4
ROPE_THETA = 10000.0
HEAD_PAD = 128

S5_GROUP = 16
S5_GROUPS = D_MODEL // S5_GROUP
S5_STATE = 64
S5_MAX_RE = -1e-4
S5_CHUNK = 16

POOL_WINDOWS = (2, 4, 8, 16)
POOL_GROUP = D_MODEL // len(POOL_WINDOWS)

FFN_DIM = 2816
N_EXPERTS = 8
TOP_K = 2

assert DEPTH % 2 == 0
T_TOK = SEQ + CTX_LEN
SEG = CTX_LEN
SEGS_PER_BATCH = T_TOK // SEG
LAT_SEGS = SEQ // SEG
N_MOD = 6

VMEM_LIMIT = 56 * 1024 * 1024


def _cparams(sem, vmem=VMEM_LIMIT):
    return pltpu.CompilerParams(dimension_semantics=sem, vmem_limit_bytes=vmem)


def _rms(x, g):
    return x * lax.rsqrt(jnp.mean(x * x, axis=-1, keepdims=True) + NORM_EPS) * g


def _norm_mod(x, g, shift, scale):
    return _rms(x, g) * (1.0 + scale) + shift


def _norm_mod_tile(x_ref, mod_ref, g_ref, shift_idx, out_dtype=BF16):
    parts = []
    for s in range(x_ref.shape[0] // SEG):
        m = mod_ref[s]
        x = x_ref[s * SEG:(s + 1) * SEG, :]
        parts.append(_norm_mod(x, g_ref[...], m[shift_idx:shift_idx + 1], m[shift_idx + 1:shift_idx + 2]).astype(out_dtype))
    return parts[0] if len(parts) == 1 else jnp.concatenate(parts, axis=0)


def _ada_kernel(c_ref, w_ref, b_ref, o_ref):
    s = jax.nn.silu(c_ref[...])
    o_ref[0] = jnp.dot(s, w_ref[0], precision=lax.Precision.HIGHEST, preferred_element_type=F32) + b_ref[0]


def _ada(cc, ada_w, ada_b):
    depth, d, n6 = ada_w.shape
    rows = cc.shape[0]
    tn = 1024
    return pl.pallas_call(
        _ada_kernel,
        grid=(depth, n6 // tn),
        in_specs=[pl.BlockSpec((rows, d), lambda i, j: (0, 0)),
                  pl.BlockSpec((1, d, tn), lambda i, j: (i, 0, j)),
                  pl.BlockSpec((1, 1, tn), lambda i, j: (i, 0, j))],
        out_specs=pl.BlockSpec((1, rows, tn), lambda i, j: (i, 0, j)),
        out_shape=jax.ShapeDtypeStruct((depth, rows, n6), F32),
        compiler_params=_cparams(("arbitrary", "arbitrary")),
        name="ada",
    )(cc, ada_w, ada_b.reshape(depth, 1, n6))


_QW = MLA_HEADS * HEAD_PAD


def _rope(x, c, s):
    lane = lax.broadcasted_iota(jnp.int32, x.shape, 1)
    first_half = ((lane - MLA_NOPE) & ROPE_AXIS_FREQS) == 0
    partner = jnp.where(first_half, pltpu.roll(x, HEAD_PAD - ROPE_AXIS_FREQS, 1), pltpu.roll(x, ROPE_AXIS_FREQS, 1))
    return x * c + partner * s


def _mla_proj_kernel(x_ref, mod_ref, g1_ref, w1_ref, qn_ref, wq_ref, kvn_ref, wk_ref, wv_ref,
                     cq_ref, sq_ref, ck_ref, sk_ref, q_ref, k_ref, vt_ref):
    h = _norm_mod_tile(x_ref, mod_ref, g1_ref, 0)
    d = jnp.dot(h, w1_ref[...], preferred_element_type=F32)
    dq = d[:, :MLA_Q_LORA]
    ckv = d[:, MLA_Q_LORA:MLA_Q_LORA + MLA_KV_LORA]
    kr = d[:, MLA_Q_LORA + MLA_KV_LORA:]
    qn = _rms(dq, qn_ref[...]).astype(BF16)
    qq = jnp.dot(qn, wq_ref[...], preferred_element_type=F32)
    cq = cq_ref[...]
    sq = sq_ref[...]
    c = _rms(ckv, kvn_ref[...]).astype(BF16)
    kr_roped = _rope(kr, ck_ref[...], sk_ref[...])
    kn = jnp.dot(c, wk_ref[...], preferred_element_type=F32)
    for hd in range(MLA_HEADS):
        slab = slice(hd * HEAD_PAD, (hd + 1) * HEAD_PAD)
        q_ref[:, slab] = _rope(qq[:, slab], cq, sq).astype(BF16)
        k_ref[:, slab] = (kn[:, slab] + kr_roped).astype(BF16)
    v = jnp.dot(c, wv_ref[...], preferred_element_type=F32)
    lane = lax.broadcasted_iota(jnp.int32, v.shape, 1)
    vt_ref[...] = jnp.where(lane % HEAD_PAD == MLA_V, 1.0, v).T.astype(BF16)


def _mla_proj(x, modtab, g1, pw, rope):
    n, d = x.shape
    tm = SEG
    full = lambda a: pl.BlockSpec(a.shape, lambda g: (0,) * a.ndim)
    pos = lambda g: (g % SEGS_PER_BATCH, 0)
    cq, sq, ck, sk = rope
    return pl.pallas_call(
        _mla_proj_kernel,
        grid=(n // tm,),
        in_specs=[pl.BlockSpec((tm, d), lambda g: (g, 0)),
                  pl.BlockSpec((tm // SEG, N_MOD, d), lambda g: (g, 0, 0)),
                  full(g1), full(pw["w1"]), full(pw["qn"]), full(pw["wq"]), full(pw["kvn"]),
                  full(pw["wk"]), full(pw["wv"]),
                  pl.BlockSpec((tm, HEAD_PAD), pos), pl.BlockSpec((tm, HEAD_PAD), pos),
                  pl.BlockSpec((tm, HEAD_PAD), pos), pl.BlockSpec((tm, HEAD_PAD), pos)],
        out_specs=[pl.BlockSpec((tm, _QW), lambda g: (g, 0)),
                   pl.BlockSpec((tm, _QW), lambda g: (g, 0)),
                   pl.BlockSpec((_QW, tm), lambda g: (0, g))],
        out_shape=[jax.ShapeDtypeStruct((n, _QW), BF16)] * 2 + [jax.ShapeDtypeStruct((_QW, n), BF16)],
        compiler_params=_cparams(("parallel",)),
        name="mla_proj",
    )(x, modtab, g1, pw["w1"], pw["qn"], pw["wq"], pw["kvn"], pw["wk"], pw["wv"], cq, sq, ck, sk)


def _mla_weights(w_dq, q_norm, w_uq, w_dkv, kv_norm, w_ukv):
    w_kr = jnp.pad(w_dkv[:, MLA_KV_LORA:], ((0, 0), (MLA_NOPE, HEAD_PAD - MLA_QK)))
    w1 = jnp.concatenate([w_dq, w_dkv[:, :MLA_KV_LORA], w_kr], axis=1)
    uq = w_uq.reshape(MLA_Q_LORA, MLA_HEADS, MLA_QK)
    zpad = jnp.zeros((MLA_Q_LORA, MLA_HEADS, HEAD_PAD - MLA_QK), w_uq.dtype)
    wq = jnp.concatenate([uq, zpad], axis=-1).reshape(MLA_Q_LORA, _QW)
    ukv = w_ukv.reshape(MLA_KV_LORA, MLA_HEADS, MLA_NOPE + MLA_V)
    wk = jnp.concatenate([ukv[:, :, :MLA_NOPE],
                          jnp.zeros((MLA_KV_LORA, MLA_HEADS, HEAD_PAD - MLA_NOPE), w_ukv.dtype)], axis=-1)
    wk = wk.reshape(MLA_KV_LORA, _QW)
    wv = jnp.concatenate([ukv[:, :, MLA_NOPE:],
                          jnp.zeros((MLA_KV_LORA, MLA_HEADS, HEAD_PAD - MLA_V), w_ukv.dtype)], axis=-1)
    wv = wv.reshape(MLA_KV_LORA, _QW)
    return {"w1": w1.astype(BF16), "qn": q_norm.reshape(1, -1), "wq": wq.astype(BF16),
            "kvn": kv_norm.reshape(1, -1), "wk": wk.astype(BF16), "wv": wv.astype(BF16)}


def _rope_tables():
    rows = SEQ // GRID_W
    row = jnp.repeat(jnp.arange(rows, dtype=F32), GRID_W)
    col = jnp.tile(jnp.arange(GRID_W, dtype=F32), rows)
    inv_freq = 1.0 / (ROPE_THETA ** (jnp.arange(ROPE_AXIS_FREQS, dtype=F32) / ROPE_AXIS_FREQS))
    ang = jnp.stack([row[:, None] * inv_freq, col[:, None] * inv_freq], axis=1)
    ang = jnp.concatenate([ang, jnp.zeros((CTX_LEN, 2, ROPE_AXIS_FREQS), F32)], axis=0)
    cos = jnp.cos(ang)
    sin = jnp.sin(ang)
    c32 = jnp.stack([cos, cos], axis=2).reshape(T_TOK, MLA_ROPE)
    s32 = jnp.stack([-sin, sin], axis=2).reshape(T_TOK, MLA_ROPE)
    scale = MLA_QK ** -0.5 * math.log2(math.e)
    zq = jnp.zeros((T_TOK, HEAD_PAD - MLA_QK), F32)
    cq = jnp.concatenate([jnp.full((T_TOK, MLA_NOPE), scale, F32), c32 * scale, zq], axis=1)
    sq = jnp.concatenate([jnp.zeros((T_TOK, MLA_NOPE), F32), s32 * scale, zq], axis=1)
    zn = jnp.zeros((T_TOK, MLA_NOPE), F32)
    ck = jnp.concatenate([zn, c32, zq], axis=1)
    sk = jnp.concatenate([zn, s32, zq], axis=1)
    return cq, sq, ck, sk


ATTN_TQ = 2048
ATTN_KEY_CHUNK = 1024
_CTX_CHUNK = ((SEQ, CTX_LEN),)
_ALL_CHUNKS = tuple((lo, ATTN_KEY_CHUNK) for lo in range(0, SEQ, ATTN_KEY_CHUNK)) + _CTX_CHUNK


def _attend_pair(q, k_ref, vt_ref, chunks):
    m = [None, None]
    acc = [None, None]
    for lo, size in chunks:
        for j in range(2):
            cols = slice(j * HEAD_PAD, (j + 1) * HEAD_PAD)
            st = lax.dot_general(k_ref[0, lo:lo + size, cols], q[:, cols], (((1,), (1,)), ((), ())), preferred_element_type=F32)
            mc = jnp.max(st, axis=0, keepdims=True)
            m_new = mc if m[j] is None else jnp.maximum(m[j], mc)
            pv = jnp.dot(vt_ref[cols, lo:lo + size], jnp.exp2(st - m_new).astype(BF16), preferred_element_type=F32)
            acc[j] = pv if m[j] is None else acc[j] * jnp.exp2(m[j] - m_new) + pv
            m[j] = m_new
    outs = [a / a[MLA_V:MLA_V + 1, :] for a in acc]
    return jnp.concatenate([outs[0][:MLA_V], outs[1][:MLA_V]], axis=0).T


def _attn_kernel(q_ref, k_ref, vt_ref, o_ref, *, lat_tiles):
    qi = pl.program_id(2)

    @pl.when(qi < lat_tiles)
    def _():
        o_ref[0] = _attend_pair(q_ref[0], k_ref, vt_ref, _ALL_CHUNKS).astype(BF16)

    @pl.when(qi >= lat_tiles)
    def _():
        o_ref[0, :CTX_LEN, :] = _attend_pair(q_ref[0, :CTX_LEN, :], k_ref, vt_ref, _CTX_CHUNK).astype(BF16)


def _attention(q, k, vt):
    b = q.shape[0]
    tq = ATTN_TQ
    lat_tiles = SEQ // tq
    return pl.pallas_call(
        functools.partial(_attn_kernel, lat_tiles=lat_tiles),
        grid=(b, MLA_HEADS // 2, lat_tiles + 1),
        in_specs=[pl.BlockSpec((1, tq, 2 * HEAD_PAD), lambda bi, hp, qi: (bi, qi, hp)),
                  pl.BlockSpec((1, T_TOK, 2 * HEAD_PAD), lambda bi, hp, qi: (bi, 0, hp)),
                  pl.BlockSpec((2 * HEAD_PAD, T_TOK), lambda bi, hp, qi: (hp, bi))],
        out_specs=pl.BlockSpec((1, tq, 2 * MLA_V), lambda bi, hp, qi: (bi, qi, hp)),
        out_shape=jax.ShapeDtypeStruct((b, T_TOK, MLA_HEADS * MLA_V), BF16),
        compiler_params=_cparams(("parallel", "parallel", "arbitrary")),
        name="attention",
    )(q, k, vt)


def _gated_add(x_ref, mod_ref, y, gate_idx, o_ref):
    for s in range(x_ref.shape[0] // SEG):
        rows = slice(s * SEG, (s + 1) * SEG)
        g = mod_ref[s][gate_idx:gate_idx + 1]
        o_ref[rows, :] = x_ref[rows, :] + g * y[rows, :]


def _proj_res_kernel(x_ref, mod_ref, y_ref, w_ref, o_ref):
    y = jnp.dot(y_ref[...], w_ref[...], preferred_element_type=F32)
    _gated_add(x_ref, mod_ref, y, 2, o_ref)


def _proj_res(x, modtab, y, w, tm=1024):
    n, d = x.shape
    return pl.pallas_call(
        _proj_res_kernel,
        grid=(n // tm,),
        in_specs=[pl.BlockSpec((tm, d), lambda g: (g, 0)),
                  pl.BlockSpec((tm // SEG, N_MOD, d), lambda g: (g, 0, 0)),
                  pl.BlockSpec((tm, y.shape[1]), lambda g: (g, 0)),
                  pl.BlockSpec(w.shape, lambda g: (0, 0))],
        out_specs=pl.BlockSpec((tm, d), lambda g: (g, 0)),
        out_shape=jax.ShapeDtypeStruct((n, d), F32),
        compiler_params=_cparams(("parallel",)),
        name="proj_res",
    )(x, modtab, y, w)


_S5_LAT_CHUNKS = SEQ // S5_CHUNK
_S5_CHUNKS = T_TOK // S5_CHUNK
_S5_W = S5_CHUNK * S5_GROUP
_S5_BLK_CHUNKS = 2
_S5_BLK_TOK = _S5_BLK_CHUNKS * S5_CHUNK
LANES = 128
_SLOTS = LANES // S5_GROUP


def _s5_in_kernel(x_ref, mod_ref, g_ref, u_ref, h_scr, *, nb):
    for b in range(nb):
        m = mod_ref[0, b]
        h = _norm_mod(x_ref[b], g_ref[...], m[0:1], m[1:2])
        for j in range(D_MODEL // LANES):
            h_scr[j, b * _S5_BLK_TOK:(b + 1) * _S5_BLK_TOK, :] = h[:, j * LANES:(j + 1) * LANES]
    slot = lax.broadcasted_iota(jnp.int32, (_S5_BLK_CHUNKS * nb, LANES), 1) // S5_GROUP
    for j in range(D_MODEL // LANES):
        for half in range(_S5_W // LANES):
            srcs = []
            for p in range(_SLOTS):
                t = _SLOTS * half + p
                rows = [h_scr[j, pl.ds(c * S5_CHUNK + t, nb, stride=_S5_BLK_TOK), :] for c in range(_S5_BLK_CHUNKS)]
                srcs.append(jnp.concatenate(rows, axis=0))
            for gs in range(_SLOTS):
                acc = None
                for p in range(_SLOTS):
                    k = (p - gs) % _SLOTS
                    r = pltpu.roll(srcs[p], k * S5_GROUP, 1) if k else srcs[p]
                    acc = r if acc is None else jnp.where(slot == p, r, acc)
                u_ref[_SLOTS * j + gs, :, half * LANES:(half + 1) * LANES] = acc.astype(BF16)


def _s5_in(x3, modsel, g1):
    nb, t, d = x3.shape
    nblk = t // _S5_BLK_TOK
    rows = _S5_BLK_CHUNKS * nb
    return pl.pallas_call(
        functools.partial(_s5_in_kernel, nb=nb),
        grid=(nblk,),
        in_specs=[pl.BlockSpec((nb, _S5_BLK_TOK, d), lambda k: (0, k, 0)),
                  pl.BlockSpec((1, nb, N_MOD, d), lambda k: (k // (SEQ // _S5_BLK_TOK), 0, 0, 0)),
                  pl.BlockSpec(g1.shape, lambda k: (0, 0))],
        out_specs=pl.BlockSpec((S5_GROUPS, rows, _S5_W), lambda k: (0, k, 0)),
        out_shape=jax.ShapeDtypeStruct((S5_GROUPS, nblk * rows, _S5_W), BF16),
        scratch_shapes=[pltpu.VMEM((d // LANES, nb * _S5_BLK_TOK, LANES), F32)],
        compiler_params=_cparams(("parallel",)),
        name="s5_in",
    )(x3, modsel, g1)


def _s5_out_kernel(x_ref, mod_ref, y_ref, w_ref, o_ref, nat_scr, *, nb):
    d = x_ref.shape[2]
    slot = lax.broadcasted_iota(jnp.int32, (nb, LANES), 1) // S5_GROUP
    for j in range(d // LANES):
        for half in range(_S5_W // LANES):
            for c in range(_S5_BLK_CHUNKS):
                srcs = [y_ref[_SLOTS * j + gs, c * nb:(c + 1) * nb, half * LANES:(half + 1) * LANES] for gs in range(_SLOTS)]
                for p in range(_SLOTS):
                    acc = None
                    for gs in range(_SLOTS):
                        k = (gs - p) % _SLOTS
                        r = pltpu.roll(srcs[gs], k * S5_GROUP, 1) if k else srcs[gs]
                        acc = r if acc is None else jnp.where(slot == gs, r, acc)
                    nat_scr[j, pl.ds(c * S5_CHUNK + _SLOTS * half + p, nb, stride=_S5_BLK_TOK), :] = acc
    y_nat = jnp.concatenate([nat_scr[j] for j in range(d // LANES)], axis=1)
    g = jax.nn.gelu(y_nat).astype(BF16)
    z = jnp.dot(g, w_ref[...], preferred_element_type=F32)
    y = z[:, :d] * jax.nn.sigmoid(z[:, d:])
    for b in range(nb):
        gate = mod_ref[0, b][2:3]
        o_ref[b] = x_ref[b] + gate * y[b * _S5_BLK_TOK:(b + 1) * _S5_BLK_TOK, :]


def _s5_out(x3, modsel, y, w):
    nb, t, d = x3.shape
    nblk = t // _S5_BLK_TOK
    rows = _S5_BLK_CHUNKS * nb
    return pl.pallas_call(
        functools.partial(_s5_out_kernel, nb=nb),
        grid=(nblk,),
        in_specs=[pl.BlockSpec((nb, _S5_BLK_TOK, d), lambda k: (0, k, 0)),
                  pl.BlockSpec((1, nb, N_MOD, d), lambda k: (k // (SEQ // _S5_BLK_TOK), 0, 0, 0)),
                  pl.BlockSpec((S5_GROUPS, rows, _S5_W), lambda k: (0, k, 0)),
                  pl.BlockSpec(w.shape, lambda k: (0, 0))],
        out_specs=pl.BlockSpec((nb, _S5_BLK_TOK, d), lambda k: (0, k, 0)),
        out_shape=jax.ShapeDtypeStruct((nb, t, d), F32),
        scratch_shapes=[pltpu.VMEM((d // LANES, nb * _S5_BLK_TOK, LANES), F32)],
        compiler_params=_cparams(("parallel",)),
        name="s5_out",
    )(x3, modsel, y, w)


S5_GROUPS_PER_STEP = 2


def _s5_kernel(u_ref, m_ref, win_ref, wof_ref, wor_ref, ar_ref, ai_ref, y_ref, z_ref, sf_ref, sr_ref, *, nb):
    gp = u_ref.shape[0]
    for g in range(gp):
        z_ref[g] = jnp.dot(u_ref[g], win_ref[g], preferred_element_type=F32)
    half = _S5_W // 2
    ars = [ar_ref[g] for g in range(gp)]
    ais = [ai_ref[g] for g in range(gp)]
    lane = lax.broadcasted_iota(jnp.int32, (nb, _S5_W), 1)
    is_fwd = (lane % half) < S5_STATE

    def step(i, states):
        cf = jnp.where(i < _S5_CHUNKS - _S5_LAT_CHUNKS, i + _S5_LAT_CHUNKS, i - (_S5_CHUNKS - _S5_LAT_CHUNKS))
        cr = _S5_CHUNKS - 1 - i
        rf = pl.multiple_of(cf * nb, nb)
        rr = pl.multiple_of(cr * nb, nb)
        out = []
        for g in range(gp):
            s = states[g]
            sf_ref[g, pl.ds(rf, nb), :] = s
            sr_ref[g, pl.ds(rr, nb), :] = s
            z = jnp.where(is_fwd, z_ref[g, pl.ds(rf, nb), :], z_ref[g, pl.ds(rr, nb), :])
            re = s[:, :half]
            im = s[:, half:]
            out.append(jnp.concatenate([ars[g] * re - ais[g] * im + z[:, :half], ars[g] * im + ais[g] * re + z[:, half:]], axis=1))
        return tuple(out)

    lax.fori_loop(0, _S5_CHUNKS, step, tuple(jnp.zeros((nb, _S5_W), F32) for _ in range(gp)))
    for g in range(gp):
        y = jnp.dot(u_ref[g], m_ref[g], preferred_element_type=F32)
        y = y + jnp.dot(sf_ref[g].astype(BF16), wof_ref[g], preferred_element_type=F32)
        y = y + jnp.dot(sr_ref[g].astype(BF16), wor_ref[g], preferred_element_type=F32)
        y_ref[g] = y


def _s5_core(u, sp, nb):
    g, rows, w = u.shape
    gp = S5_GROUPS_PER_STEP
    blk = lambda a: pl.BlockSpec((gp,) + a.shape[1:], lambda i: (i,) + (0,) * (a.ndim - 1))
    return pl.pallas_call(
        functools.partial(_s5_kernel, nb=nb),
        grid=(g // gp,),
        in_specs=[blk(u), blk(sp["m"]), blk(sp["win"]), blk(sp["wof"]), blk(sp["wor"]), blk(sp["ar"]), blk(sp["ai"])],
        out_specs=pl.BlockSpec((gp, rows, w), lambda i: (i, 0, 0)),
        out_shape=jax.ShapeDtypeStruct((g, rows, w), F32),
        scratch_shapes=[pltpu.VMEM((gp, rows, w), F32), pltpu.VMEM((gp, rows, w), F32), pltpu.VMEM((gp, rows, w), F32)],
        compiler_params=_cparams(("parallel",)),
        name="s5_core",
    )(u, sp["m"], sp["win"], sp["wof"], sp["wor"], sp["ar"], sp["ai"])


def _s5_params(a_re, a_im, log_dt, b_re, b_im, c_re, c_im, d):
    c64 = jnp.complex64
    lam = lax.complex(jnp.minimum(a_re, S5_MAX_RE), a_im)
    lam_dt = lam * jnp.exp(log_dt)[..., None]
    lam_bar = jnp.exp(lam_dt)
    b_bar = ((lam_bar - 1.0) / lam)[..., None] * lax.complex(b_re, b_im)
    c_mat = lax.complex(c_re, c_im)
    taus = jnp.arange(S5_CHUNK + 1, dtype=F32)
    pw = jnp.exp(lam_dt[None] * taus[:, None, None, None].astype(c64))
    hi = lax.Precision.HIGHEST
    kern = jnp.real(jnp.einsum('dgnp,tdgp,dgpm->dgtnm', c_mat, pw[:S5_CHUNK], b_bar, precision=hi))
    eye = jnp.eye(S5_GROUP, dtype=F32)
    k0 = kern[0, :, 0] + kern[1, :, 0] + d.reshape(S5_GROUPS, S5_GROUP)[:, :, None] * eye
    lags = jnp.concatenate([kern[1, :, :0:-1], k0[:, None], kern[0, :, 1:]], axis=1)
    idx = (np.arange(S5_CHUNK)[None, :] - np.arange(S5_CHUNK)[:, None]) + S5_CHUNK - 1
    m = lags[:, idx]
    m = m.transpose(0, 1, 4, 2, 3).reshape(S5_GROUPS, _S5_W, _S5_W)
    wf = pw[S5_CHUNK - 1::-1][:S5_CHUNK, 0][..., None] * b_bar[0][None]
    wr = pw[:S5_CHUNK, 1][..., None] * b_bar[1][None]
    to_rows = lambda w: w.transpose(1, 0, 3, 2).reshape(S5_GROUPS, _S5_W, S5_STATE)
    wf, wr = to_rows(wf), to_rows(wr)
    win = jnp.concatenate([jnp.real(wf), jnp.real(wr), jnp.imag(wf), jnp.imag(wr)], axis=-1)
    of = c_mat[0][None] * pw[1:, 0][:, :, None, :]
    orv = c_mat[1][None] * pw[S5_CHUNK:0:-1, 1][:, :, None, :]
    to_cols = lambda w: w.transpose(1, 3, 0, 2).reshape(S5_GROUPS, S5_STATE, _S5_W)
    of, orv = to_cols(of), to_cols(orv)
    zeros = jnp.zeros_like(jnp.real(of))
    wof = jnp.concatenate([jnp.real(of), zeros, -jnp.imag(of), zeros], axis=1)
    wor = jnp.concatenate([zeros, jnp.real(orv), zeros, -jnp.imag(orv)], axis=1)
    a16 = pw[S5_CHUNK]
    ar = jnp.concatenate([jnp.real(a16[0]), jnp.real(a16[1])], axis=-1)[:, None, :]
    ai = jnp.concatenate([jnp.imag(a16[0]), jnp.imag(a16[1])], axis=-1)[:, None, :]
    return {"m": m.astype(BF16), "win": win.astype(BF16), "wof": wof.astype(BF16), "wor": wor.astype(BF16),
            "ar": ar, "ai": ai}


POOL_HALO = 16


def _pool_kernel(xp_ref, xc_ref, xn_ref, mod_ref, g_ref, wp_ref, sc_ref, o_ref):
    seg = pl.program_id(0) % SEGS_PER_BATCH
    is_ctx = seg >= LAT_SEGS
    p0 = jnp.where(is_ctx, 0, seg * SEG)
    lseq = jnp.where(is_ctx, CTX_LEN, SEQ)
    m = mod_ref[0]
    g = g_ref[...]
    hs = [_norm_mod(r[...], g, m[0:1], m[1:2]) for r in (xp_ref, xc_ref, xn_ref)]
    hcat = jnp.concatenate([h.astype(BF16) for h in hs], axis=0)
    r = lax.broadcasted_iota(jnp.int32, (SEG, SEG + 2 * POOL_HALO), 0)
    s = lax.broadcasted_iota(jnp.int32, (SEG, SEG + 2 * POOL_HALO), 1)
    pt = p0 + r
    ps = p0 - POOL_HALO + s
    rcol = lax.broadcasted_iota(jnp.int32, (SEG, 1), 0) + p0
    x = xc_ref[...]
    gate = m[2:3]
    for gi, w in enumerate(POOL_WINDOWS):
        cols = slice(gi * POOL_GROUP, (gi + 1) * POOL_GROUP)
        lo = jnp.maximum(pt - w // 2, 0)
        hi = jnp.minimum(pt + w - w // 2, lseq)
        band = jnp.where((ps >= lo) & (ps < hi), 1.0, 0.0).astype(BF16)
        tot = jnp.dot(band, hcat[:, cols], preferred_element_type=F32)
        cnt = (jnp.minimum(rcol + w - w // 2, lseq) - jnp.maximum(rcol - w // 2, 0)).astype(F32)
        resid = tot / cnt - hs[1][:, cols]
        y = jnp.dot(resid.astype(BF16), wp_ref[gi], preferred_element_type=F32) * sc_ref[:, cols]
        o_ref[:, cols] = x[:, cols] + gate[:, cols] * y


def _pool(x, modtab, g1, w_pool, scale):
    n, d = x.shape
    nseg = n // SEG

    halo_per_seg = SEG // POOL_HALO

    def prev(g):
        return (jnp.maximum(g * halo_per_seg - 1, 0), 0)

    def nxt(g):
        return (jnp.minimum((g + 1) * halo_per_seg, nseg * halo_per_seg - 1), 0)

    return pl.pallas_call(
        _pool_kernel,
        grid=(nseg,),
        in_specs=[pl.BlockSpec((POOL_HALO, d), prev), pl.BlockSpec((SEG, d), lambda g: (g, 0)), pl.BlockSpec((POOL_HALO, d), nxt),
                  pl.BlockSpec((1, N_MOD, d), lambda g: (g, 0, 0)),
                  pl.BlockSpec(g1.shape, lambda g: (0, 0)),
                  pl.BlockSpec(w_pool.shape, lambda g: (0, 0, 0)),
                  pl.BlockSpec(scale.shape, lambda g: (0, 0))],
        out_specs=pl.BlockSpec((SEG, d), lambda g: (g, 0)),
        out_shape=jax.ShapeDtypeStruct((n, d), F32),
        compiler_params=_cparams(("parallel",)),
        name="pool",
    )(x, x, x, modtab, g1, w_pool, scale)


def _swiglu_partial(h, wg, wu, wd):
    a = jax.nn.silu(jnp.dot(h, wg, preferred_element_type=F32)) * jnp.dot(h, wu, preferred_element_type=F32)
    return jnp.dot(a.astype(BF16), wd, preferred_element_type=F32)


def _ffn_kernel(x_ref, mod_ref, g_ref, wg_ref, wu_ref, wd_ref, o_ref):
    h = _norm_mod_tile(x_ref, mod_ref, g_ref, 3)
    _gated_add(x_ref, mod_ref, _swiglu_partial(h, wg_ref[...], wu_ref[...], wd_ref[...]), 5, o_ref)


_RESIDENT = pl.Buffered(1)


def _ffn(x, modtab, g2, w_gu, w_down, layer, tm=512):
    n, d = x.shape
    f = w_down.shape[1]
    return pl.pallas_call(
        _ffn_kernel,
        grid=(n // tm,),
        in_specs=[pl.BlockSpec((tm, d), lambda g: (g, 0)),
                  pl.BlockSpec((tm // SEG, N_MOD, d), lambda g: (g, 0, 0)),
                  pl.BlockSpec(g2.shape, lambda g: (0, 0)),
                  pl.BlockSpec((None, d, f), lambda g: (layer, 0, 0), pipeline_mode=_RESIDENT),
                  pl.BlockSpec((None, d, f), lambda g: (layer, 0, 1), pipeline_mode=_RESIDENT),
                  pl.BlockSpec((None, f, d), lambda g: (layer, 0, 0), pipeline_mode=_RESIDENT)],
        out_specs=pl.BlockSpec((tm, d), lambda g: (g, 0)),
        out_shape=jax.ShapeDtypeStruct((n, d), F32),
        compiler_params=_cparams(("parallel",)),
        name="ffn",
    )(x, modtab, g2, w_gu, w_gu, w_down)


MOE_TILE = 512
MOE_DMA_TILE = 1024


def _router_kernel(x_ref, mod_ref, g_ref, w_ref, b_ref, h_ref, rw_ref, ri_ref, cnt_ref, run_ref):
    @pl.when(pl.program_id(0) == 0)
    def _():
        run_ref[...] = jnp.zeros_like(run_ref)

    h = _norm_mod_tile(x_ref, mod_ref, g_ref, 3, out_dtype=F32)
    h_ref[...] = h
    logits = jnp.dot(h, w_ref[...], precision=lax.Precision.HIGHEST, preferred_element_type=F32) + b_ref[...]
    lane = lax.broadcasted_iota(jnp.int32, logits.shape, 1)
    neg = jnp.float32(-jnp.inf)
    lg = jnp.where(lane < N_EXPERTS, logits, neg)
    m1 = jnp.max(lg, axis=-1, keepdims=True)
    i1 = jnp.min(jnp.where(lg == m1, lane, HEAD_PAD), axis=-1, keepdims=True)
    lg2 = jnp.where(lane == i1, neg, lg)
    m2 = jnp.max(lg2, axis=-1, keepdims=True)
    i2 = jnp.min(jnp.where(lg2 == m2, lane, HEAD_PAD), axis=-1, keepdims=True)
    e2 = jnp.exp(m2 - m1)
    den = 1.0 + e2
    tm = logits.shape[0]
    sel = jnp.where((lane == i1) | (lane == i2), 1.0, 0.0).astype(BF16)
    tri = jnp.where(lax.broadcasted_iota(jnp.int32, (tm, tm), 1) <= lax.broadcasted_iota(jnp.int32, (tm, tm), 0), 1.0, 0.0)
    within = jnp.dot(tri.astype(BF16), sel, preferred_element_type=F32)
    rank = run_ref[0:1, :] + within - 1.0
    r1 = jnp.sum(jnp.where(lane == i1, rank, 0.0), axis=-1, keepdims=True).astype(jnp.int32)
    r2 = jnp.sum(jnp.where(lane == i2, rank, 0.0), axis=-1, keepdims=True).astype(jnp.int32)
    total = run_ref[0:1, :] + within[tm - 1:tm, :]
    run_ref[...] = jnp.broadcast_to(total, run_ref.shape)
    cnt_ref[...] = jnp.broadcast_to(total, cnt_ref.shape).astype(jnp.int32)
    rw_ref[...] = jnp.where(lane == 0, 1.0 / den, jnp.where(lane == 1, e2 / den, 0.0))
    ri_ref[...] = jnp.where(lane == 0, i1, jnp.where(lane == 1, i2, jnp.where(lane == 2, r1, jnp.where(lane == 3, r2, 0))))


def _router(x, modtab, g2, w_router, b_router, tm=512):
    n, d = x.shape
    wr = jnp.pad(w_router, ((0, 0), (0, HEAD_PAD - N_EXPERTS)))
    br = jnp.pad(b_router, (0, HEAD_PAD - N_EXPERTS)).reshape(1, HEAD_PAD)
    return pl.pallas_call(
        _router_kernel,
        grid=(n // tm,),
        in_specs=[pl.BlockSpec((tm, d), lambda g: (g, 0)),
                  pl.BlockSpec((tm // SEG, N_MOD, d), lambda g: (g, 0, 0)),
                  pl.BlockSpec(g2.shape, lambda g: (0, 0)),
                  pl.BlockSpec(wr.shape, lambda g: (0, 0)),
                  pl.BlockSpec(br.shape, lambda g: (0, 0))],
        out_specs=[pl.BlockSpec((tm, d), lambda g: (g, 0)),
                   pl.BlockSpec((tm, HEAD_PAD), lambda g: (g, 0)),
                   pl.BlockSpec((tm, HEAD_PAD), lambda g: (g, 0)),
                   pl.BlockSpec((8, HEAD_PAD), lambda g: (0, 0))],
        out_shape=[jax.ShapeDtypeStruct((n, d), F32),
                   jax.ShapeDtypeStruct((n, HEAD_PAD), F32),
                   jax.ShapeDtypeStruct((n, HEAD_PAD), jnp.int32),
                   jax.ShapeDtypeStruct((8, HEAD_PAD), jnp.int32)],
        scratch_shapes=[pltpu.VMEM((8, HEAD_PAD), F32)],
        compiler_params=_cparams(("arbitrary",)),
        name="router",
    )(x, modtab, g2, wr, br)


def _moe_plan(ri, cnt, n):
    e = ri[:, :TOP_K]
    counts = cnt[0, :N_EXPERTS]
    padded = (counts + MOE_TILE - 1) // MOE_TILE * MOE_TILE
    ends = jnp.cumsum(padded)
    starts = ends - padded
    dest = starts[e] + ri[:, TOP_K:2 * TOP_K]
    n_tiles = (TOP_K * n) // MOE_TILE + N_EXPERTS
    n_valid = (ends[-1] // MOE_TILE).astype(jnp.int32).reshape(1)
    tile_ids = jnp.arange(n_tiles, dtype=jnp.int32)
    tile_expert = jnp.searchsorted(ends // MOE_TILE, jnp.minimum(tile_ids, n_valid - 1), side="right").astype(jnp.int32)
    seg_len = jnp.concatenate([padded - counts, (n_tiles * MOE_TILE - ends[-1])[None]])
    seg_base = jnp.concatenate([starts + counts, ends[-1:]])
    seg_end = jnp.cumsum(seg_len)
    k = jnp.arange(N_EXPERTS * MOE_TILE, dtype=jnp.int32)
    seg = jnp.searchsorted(seg_end, k, side="right")
    pad_dest = seg_base[seg] + k - (seg_end - seg_len)[seg]
    return dest[:, 0].astype(jnp.int32), dest[:, 1].astype(jnp.int32), pad_dest.astype(jnp.int32), tile_expert, n_valid, n_tiles


def _row(ref, i):
    return ref.at[pl.ds(i, 1)]


def _scatter_rows_kernel(d0_ref, d1_ref, pad_ref, h_ref, o_hbm, zero_ref, sem):
    @pl.when(pl.program_id(0) == 0)
    def _():
        zero_ref[...] = jnp.zeros_like(zero_ref)

        def issue_zero(k, carry):
            pltpu.make_async_copy(_row(zero_ref, 0), _row(o_hbm, pad_ref[k]), sem.at[2]).start()
            return carry

        lax.fori_loop(0, N_EXPERTS * MOE_TILE, issue_zero, 0, unroll=8)
        for _ in range(N_EXPERTS * MOE_TILE // MOE_DMA_TILE):
            pltpu.make_async_copy(h_ref, o_hbm.at[pl.ds(0, MOE_DMA_TILE)], sem.at[2]).wait()

    base = pl.program_id(0) * MOE_DMA_TILE

    def issue(r, carry):
        t = base + r
        pltpu.make_async_copy(_row(h_ref, r), _row(o_hbm, d0_ref[t]), sem.at[0]).start()
        pltpu.make_async_copy(_row(h_ref, r), _row(o_hbm, d1_ref[t]), sem.at[1]).start()
        return carry

    lax.fori_loop(0, MOE_DMA_TILE, issue, 0, unroll=8)
    pltpu.make_async_copy(h_ref, o_hbm.at[pl.ds(0, MOE_DMA_TILE)], sem.at[0]).wait()
    pltpu.make_async_copy(h_ref, o_hbm.at[pl.ds(0, MOE_DMA_TILE)], sem.at[1]).wait()


def _scatter_rows(d0, d1, pad_dest, h, n_rows):
    n, d = h.shape
    return pl.pallas_call(
        _scatter_rows_kernel,
        grid_spec=pltpu.PrefetchScalarGridSpec(
            num_scalar_prefetch=3,
            grid=(n // MOE_DMA_TILE,),
            in_specs=[pl.BlockSpec((MOE_DMA_TILE, d), lambda g, d0, d1, pd: (g, 0))],
            out_specs=pl.BlockSpec(memory_space=pl.ANY),
            scratch_shapes=[pltpu.VMEM((8, d), F32), pltpu.SemaphoreType.DMA((3,))]),
        out_shape=jax.ShapeDtypeStruct((n_rows, d), F32),
        compiler_params=_cparams(("arbitrary",)),
        name="moe_scatter",
    )(d0, d1, pad_dest, h)


def _moe_ffn_kernel(te_ref, nv_ref, xs_ref, wg_ref, wu_ref, wd_ref, o_ref):
    del te_ref
    i = pl.program_id(0)

    @pl.when(i < nv_ref[0])
    def _():
        o_ref[...] = _swiglu_partial(xs_ref[...].astype(BF16), wg_ref[...], wu_ref[...], wd_ref[...])

    @pl.when(i >= nv_ref[0])
    def _():
        o_ref[...] = jnp.zeros_like(o_ref)


def _moe_ffn(xs, n_rows, tile_expert, n_valid, w_gu, w_down, layer):
    d = xs.shape[1]
    rows = n_rows
    f = w_down.shape[2]
    return pl.pallas_call(
        _moe_ffn_kernel,
        grid_spec=pltpu.PrefetchScalarGridSpec(
            num_scalar_prefetch=2,
            grid=(rows // MOE_TILE,),
            in_specs=[pl.BlockSpec((MOE_TILE, d), lambda i, te, nv: (i, 0)),
                      pl.BlockSpec((None, None, d, f), lambda i, te, nv: (layer, te[i], 0, 0), pipeline_mode=_RESIDENT),
                      pl.BlockSpec((None, None, d, f), lambda i, te, nv: (layer, te[i], 0, 1), pipeline_mode=_RESIDENT),
                      pl.BlockSpec((None, None, f, d), lambda i, te, nv: (layer, te[i], 0, 0), pipeline_mode=_RESIDENT)],
            out_specs=pl.BlockSpec((MOE_TILE, d), lambda i, te, nv: (i, 0))),
        out_shape=jax.ShapeDtypeStruct((rows, d), F32),
        compiler_params=_cparams(("arbitrary",)),
        name="moe_ffn",
    )(tile_expert, n_valid, xs, w_gu, w_gu, w_down)


def _combine_kernel(d0_ref, d1_ref, x_ref, mod_ref, rw_ref, y_hbm, *rest, tm, final):
    if final:
        fg_ref, o_ref, ya_ref, yb_ref, sem = rest
        tile = pl.program_id(0) * SEGS_PER_BATCH + pl.program_id(1)
    else:
        o_ref, ya_ref, yb_ref, sem = rest
        tile = pl.program_id(0)
    base = tile * tm

    def issue(r, carry):
        t = base + r
        pltpu.make_async_copy(_row(y_hbm, d0_ref[t]), _row(ya_ref, r), sem.at[0]).start()
        pltpu.make_async_copy(_row(y_hbm, d1_ref[t]), _row(yb_ref, r), sem.at[1]).start()
        return carry

    lax.fori_loop(0, tm, issue, 0, unroll=8)
    rw = rw_ref[...]
    lane = lax.broadcasted_iota(jnp.int32, rw.shape, 1)
    w1 = jnp.sum(jnp.where(lane == 0, rw, 0.0), axis=-1, keepdims=True)
    w2 = jnp.sum(jnp.where(lane == 1, rw, 0.0), axis=-1, keepdims=True)
    pltpu.make_async_copy(y_hbm.at[pl.ds(0, tm)], ya_ref, sem.at[0]).wait()
    pltpu.make_async_copy(y_hbm.at[pl.ds(0, tm)], yb_ref, sem.at[1]).wait()
    y = w1 * ya_ref[...] + w2 * yb_ref[...]
    if final:
        o_ref[0] = _rms(x_ref[...] + mod_ref[0][5:6] * y, fg_ref[...])
    else:
        _gated_add(x_ref, mod_ref, y, 5, o_ref)


def _combine(d0, d1, x, modtab, rw, y, final_g=None):
    n, d = x.shape
    final = final_g is not None
    tm = SEG if final else MOE_DMA_TILE
    if final:
        nb = n // T_TOK
        grid = (nb, LAT_SEGS)
        row = lambda bi, j, d0, d1: (bi * SEGS_PER_BATCH + j, 0)
        mrow = lambda bi, j, d0, d1: (bi * SEGS_PER_BATCH + j, 0, 0)
        extra = [pl.BlockSpec(final_g.shape, lambda bi, j, d0, d1: (0, 0))]
        out_spec = pl.BlockSpec((1, tm, d), lambda bi, j, d0, d1: (bi, j, 0))
        out_shape = jax.ShapeDtypeStruct((nb, SEQ, d), F32)
        args = (final_g,)
    else:
        grid = (n // tm,)
        row = lambda g, d0, d1: (g, 0)
        mrow = lambda g, d0, d1: (g, 0, 0)
        extra = []
        out_spec = pl.BlockSpec((tm, d), row)
        out_shape = jax.ShapeDtypeStruct((n, d), F32)
        args = ()
    return pl.pallas_call(
        functools.partial(_combine_kernel, tm=tm, final=final),
        grid_spec=pltpu.PrefetchScalarGridSpec(
            num_scalar_prefetch=2,
            grid=grid,
            in_specs=[pl.BlockSpec((tm, d), row),
                      pl.BlockSpec((tm // SEG, N_MOD, d), mrow),
                      pl.BlockSpec((tm, HEAD_PAD), row),
                      pl.BlockSpec(memory_space=pl.ANY)] + extra,
            out_specs=out_spec,
            scratch_shapes=[pltpu.VMEM((tm, d), F32), pltpu.VMEM((tm, d), F32), pltpu.SemaphoreType.DMA((2,))]),
        out_shape=out_shape,
        compiler_params=_cparams(("arbitrary",) * len(grid)),
        name="moe_combine",
    )(d0, d1, x, modtab, rw, y, *args)


def _moe(x, modtab, g2, w_router, b_router, w_gu, w_down, layer, final_g=None):
    n = x.shape[0]
    h, rw, ri, cnt = _router(x, modtab, g2, w_router, b_router)
    d0, d1, pad_dest, tile_expert, n_valid, n_tiles = _moe_plan(ri, cnt, n)
    xs = _scatter_rows(d0, d1, pad_dest, h, n_tiles * MOE_TILE)
    ys = _moe_ffn(xs, n_tiles * MOE_TILE, tile_expert, n_valid, w_gu, w_down, layer)
    return _combine(d0, d1, x, modtab, rw, ys, final_g)


def kernel(x, c, ctx, c_ctx, norm1_g, norm2_g, ada_w, ada_b, final_norm_g, mla_w_dq, mla_q_norm, mla_w_uq, mla_w_dkv, mla_kv_norm, mla_w_ukv, mla_w_o, s5_a_re, s5_a_im, s5_log_dt, s5_b_re, s5_b_im, s5_c_re, s5_c_im, s5_d, s5_w_glu, pool_w, pool_scale, ffn_w_gu, ffn_w_down, moe_w_router, moe_b_router, moe_w_gu, moe_w_down):
    b, l, d = x.shape
    n = b * T_TOK
    xs = jnp.concatenate([x, ctx], axis=1).reshape(n, d)

    rows = -(-(b + 1) // 8) * 8
    cc = jnp.concatenate([c, c_ctx[None], jnp.zeros((rows - b - 1, d), F32)], axis=0)
    mods = _ada(cc, ada_w, ada_b)
    seg_src = np.array([bi if sj < LAT_SEGS else b for bi in range(b) for sj in range(SEGS_PER_BATCH)], np.int32)

    rope = _rope_tables()
    ffn_w_gu_bf, ffn_w_down_bf = ffn_w_gu.astype(BF16), ffn_w_down.astype(BF16)
    moe_w_gu_bf, moe_w_down_bf = moe_w_gu.astype(BF16), moe_w_down.astype(BF16)
    for i in range(DEPTH):
        last = i == DEPTH - 1
        modtab = mods[i].reshape(rows, N_MOD, d)[seg_src]
        g1 = norm1_g[i].reshape(1, d)
        g2 = norm2_g[i].reshape(1, d)
        kind = i % N_MIXERS
        j = i // N_MIXERS
        if kind == 0:
            pw = _mla_weights(mla_w_dq[j], mla_q_norm[j], mla_w_uq[j], mla_w_dkv[j], mla_kv_norm[j], mla_w_ukv[j])
            q, k, vt = _mla_proj(xs, modtab, g1, pw, rope)
            o = _attention(q.reshape(b, T_TOK, -1), k.reshape(b, T_TOK, -1), vt)
            xs = _proj_res(xs, modtab, o.reshape(n, -1), mla_w_o[j].astype(BF16))
        elif kind == 1:
            mod9 = mods[i].reshape(rows, N_MOD, d)
            modsel = jnp.stack([mod9[:b], jnp.broadcast_to(mod9[b], (b, N_MOD, d))])
            x3 = xs.reshape(b, T_TOK, d)
            sp = _s5_params(s5_a_re[j], s5_a_im[j], s5_log_dt[j], s5_b_re[j], s5_b_im[j], s5_c_re[j], s5_c_im[j], s5_d[j])
            y = _s5_core(_s5_in(x3, modsel, g1), sp, b)
            xs = _s5_out(x3, modsel, y, s5_w_glu[j].astype(BF16)).reshape(n, d)
        else:
            xs = _pool(xs, modtab, g1, pool_w[j].astype(BF16), pool_scale[j].reshape(1, d))
        kk = i // 2
        if i % 2 == 0:
            xs = _ffn(xs, modtab, g2, ffn_w_gu_bf, ffn_w_down_bf, kk)
        else:
            xs = _moe(xs, modtab, g2, moe_w_router[kk], moe_b_router[kk], moe_w_gu_bf, moe_w_down_bf, kk,
                      final_norm_g.reshape(1, d) if last else None)
    return xs
```

```python
import functools
import math

import jax
import jax.numpy as jnp
import numpy as np
from jax import lax
from jax.experimental import pallas as pl
from jax.experimental.pallas import tpu as pltpu

F32 = jnp.float32
BF16 = jnp.bfloat16

D_MODEL = 1024
SEQ = 4096
DEPTH = 4
GRID_W = 64
CTX_LEN = 256
N_MIXERS = 3
NORM_EPS = 1e-6

MLA_HEADS = 16
MLA_Q_LORA = 384
MLA_KV_LORA = 256
MLA_NOPE = 64
MLA_ROPE = 32
MLA_V = 64
MLA_QK = MLA_NOPE + MLA_ROPE
ROPE_AXIS_FREQS = MLA_ROPE // 4
ROPE_THETA = 10000.0
HEAD_PAD = 128

S5_GROUP = 16
S5_GROUPS = D_MODEL // S5_GROUP
S5_STATE = 64
S5_MAX_RE = -1e-4
S5_CHUNK = 16

POOL_WINDOWS = (2, 4, 8, 16)
POOL_GROUP = D_MODEL // len(POOL_WINDOWS)

FFN_DIM = 2816
N_EXPERTS = 8
TOP_K = 2

assert DEPTH % 2 == 0
T_TOK = SEQ + CTX_LEN
SEG = CTX_LEN
SEGS_PER_BATCH = T_TOK // SEG
LAT_SEGS = SEQ // SEG
N_MOD = 6

VMEM_LIMIT = 56 * 1024 * 1024


def _cparams(sem, vmem=VMEM_LIMIT):
    return pltpu.CompilerParams(dimension_semantics=sem, vmem_limit_bytes=vmem)


def _rms(x, g):
    return x * lax.rsqrt(jnp.mean(x * x, axis=-1, keepdims=True) + NORM_EPS) * g


def _norm_mod(x, g, shift, scale):
    return _rms(x, g) * (1.0 + scale) + shift


def _norm_mod_tile(x_ref, mod_ref, g_ref, shift_idx, out_dtype=BF16):
    parts = []
    for s in range(x_ref.shape[0] // SEG):
        m = mod_ref[s]
        x = x_ref[s * SEG:(s + 1) * SEG, :]
        parts.append(_norm_mod(x, g_ref[...], m[shift_idx:shift_idx + 1], m[shift_idx + 1:shift_idx + 2]).astype(out_dtype))
    return parts[0] if len(parts) == 1 else jnp.concatenate(parts, axis=0)


def _ada_kernel(c_ref, w_ref, b_ref, o_ref):
    s = jax.nn.silu(c_ref[...])
    o_ref[0] = jnp.dot(s, w_ref[0], precision=lax.Precision.HIGHEST, preferred_element_type=F32) + b_ref[0]


def _ada(cc, ada_w, ada_b):
    depth, d, n6 = ada_w.shape
    rows = cc.shape[0]
    tn = 1024
    return pl.pallas_call(
        _ada_kernel,
        grid=(depth, n6 // tn),
        in_specs=[pl.BlockSpec((rows, d), lambda i, j: (0, 0)),
                  pl.BlockSpec((1, d, tn), lambda i, j: (i, 0, j)),
                  pl.BlockSpec((1, 1, tn), lambda i, j: (i, 0, j))],
        out_specs=pl.BlockSpec((1, rows, tn), lambda i, j: (i, 0, j)),
        out_shape=jax.ShapeDtypeStruct((depth, rows, n6), F32),
        compiler_params=_cparams(("arbitrary", "arbitrary")),
        name="ada",
    )(cc, ada_w, ada_b.reshape(depth, 1, n6))


_QW = MLA_HEADS * HEAD_PAD


def _rope(x, c, s):
    lane = lax.broadcasted_iota(jnp.int32, x.shape, 1)
    first_half = ((lane - MLA_NOPE) & ROPE_AXIS_FREQS) == 0
    partner = jnp.where(first_half, pltpu.roll(x, HEAD_PAD - ROPE_AXIS_FREQS, 1), pltpu.roll(x, ROPE_AXIS_FREQS, 1))
    return x * c + partner * s


def _mla_proj_kernel(x_ref, mod_ref, g1_ref, w1_ref, qn_ref, wq_ref, kvn_ref, wk_ref, wv_ref,
                     cq_ref, sq_ref, ck_ref, sk_ref, q_ref, k_ref, vt_ref):
    h = _norm_mod_tile(x_ref, mod_ref, g1_ref, 0)
    d = jnp.dot(h, w1_ref[...], preferred_element_type=F32)
    dq = d[:, :MLA_Q_LORA]
    ckv = d[:, MLA_Q_LORA:MLA_Q_LORA + MLA_KV_LORA]
    kr = d[:, MLA_Q_LORA + MLA_KV_LORA:]
    qn = _rms(dq, qn_ref[...]).astype(BF16)
    qq = jnp.dot(qn, wq_ref[...], preferred_element_type=F32)
    cq = cq_ref[...]
    sq = sq_ref[...]
    c = _rms(ckv, kvn_ref[...]).astype(BF16)
    kr_roped = _rope(kr, ck_ref[...], sk_ref[...])
    kn = jnp.dot(c, wk_ref[...], preferred_element_type=F32)
    for hd in range(MLA_HEADS):
        slab = slice(hd * HEAD_PAD, (hd + 1) * HEAD_PAD)
        q_ref[:, slab] = _rope(qq[:, slab], cq, sq).astype(BF16)
        k_ref[:, slab] = (kn[:, slab] + kr_roped).astype(BF16)
    vt = lax.dot_general(wv_ref[...], c, (((1,), (1,)), ((), ())), preferred_element_type=F32)
    row = lax.broadcasted_iota(jnp.int32, vt.shape, 0)
    vt_ref[...] = jnp.where(row % HEAD_PAD == MLA_V, 1.0, vt).astype(BF16)


def _mla_proj(x, modtab, g1, pw, rope):
    n, d = x.shape
    tm = SEG
    full = lambda a: pl.BlockSpec(a.shape, lambda g: (0,) * a.ndim)
    pos = lambda g: (g % SEGS_PER_BATCH, 0)
    cq, sq, ck, sk = rope
    return pl.pallas_call(
        _mla_proj_kernel,
        grid=(n // tm,),
        in_specs=[pl.BlockSpec((tm, d), lambda g: (g, 0)),
                  pl.BlockSpec((tm // SEG, N_MOD, d), lambda g: (g, 0, 0)),
                  full(g1), full(pw["w1"]), full(pw["qn"]), full(pw["wq"]), full(pw["kvn"]),
                  full(pw["wk"]), full(pw["wv"]),
                  pl.BlockSpec((tm, HEAD_PAD), pos), pl.BlockSpec((tm, HEAD_PAD), pos),
                  pl.BlockSpec((tm, HEAD_PAD), pos), pl.BlockSpec((tm, HEAD_PAD), pos)],
        out_specs=[pl.BlockSpec((tm, _QW), lambda g: (g, 0)),
                   pl.BlockSpec((tm, _QW), lambda g: (g, 0)),
                   pl.BlockSpec((_QW, tm), lambda g: (0, g))],
        out_shape=[jax.ShapeDtypeStruct((n, _QW), BF16)] * 2 + [jax.ShapeDtypeStruct((_QW, n), BF16)],
        compiler_params=_cparams(("parallel",)),
        name="mla_proj",
    )(x, modtab, g1, pw["w1"], pw["qn"], pw["wq"], pw["kvn"], pw["wk"], pw["wv"], cq, sq, ck, sk)


def _mla_weights(w_dq, q_norm, w_uq, w_dkv, kv_norm, w_ukv):
    w_kr = jnp.pad(w_dkv[:, MLA_KV_LORA:], ((0, 0), (MLA_NOPE, HEAD_PAD - MLA_QK)))
    w1 = jnp.concatenate([w_dq, w_dkv[:, :MLA_KV_LORA], w_kr], axis=1)
    uq = w_uq.reshape(MLA_Q_LORA, MLA_HEADS, MLA_QK)
    zpad = jnp.zeros((MLA_Q_LORA, MLA_HEADS, HEAD_PAD - MLA_QK), w_uq.dtype)
    wq = jnp.concatenate([uq, zpad], axis=-1).reshape(MLA_Q_LORA, _QW)
    ukv = w_ukv.reshape(MLA_KV_LORA, MLA_HEADS, MLA_NOPE + MLA_V)
    wk = jnp.concatenate([ukv[:, :, :MLA_NOPE],
                          jnp.zeros((MLA_KV_LORA, MLA_HEADS, HEAD_PAD - MLA_NOPE), w_ukv.dtype)], axis=-1)
    wk = wk.reshape(MLA_KV_LORA, _QW)
    wv = jnp.concatenate([ukv[:, :, MLA_NOPE:],
                          jnp.zeros((MLA_KV_LORA, MLA_HEADS, HEAD_PAD - MLA_V), w_ukv.dtype)], axis=-1)
    wv = wv.reshape(MLA_KV_LORA, _QW).T
    return {"w1": w1.astype(BF16), "qn": q_norm.reshape(1, -1), "wq": wq.astype(BF16),
            "kvn": kv_norm.reshape(1, -1), "wk": wk.astype(BF16), "wv": wv.astype(BF16)}


def _rope_tables():
    rows = SEQ // GRID_W
    row = jnp.repeat(jnp.arange(rows, dtype=F32), GRID_W)
    col = jnp.tile(jnp.arange(GRID_W, dtype=F32), rows)
    inv_freq = 1.0 / (ROPE_THETA ** (jnp.arange(ROPE_AXIS_FREQS, dtype=F32) / ROPE_AXIS_FREQS))
    ang = jnp.stack([row[:, None] * inv_freq, col[:, None] * inv_freq], axis=1)
    ang = jnp.concatenate([ang, jnp.zeros((CTX_LEN, 2, ROPE_AXIS_FREQS), F32)], axis=0)
    cos = jnp.cos(ang)
    sin = jnp.sin(ang)
    c32 = jnp.stack([cos, cos], axis=2).reshape(T_TOK, MLA_ROPE)
    s32 = jnp.stack([-sin, sin], axis=2).reshape(T_TOK, MLA_ROPE)
    scale = MLA_QK ** -0.5 * math.log2(math.e)
    zq = jnp.zeros((T_TOK, HEAD_PAD - MLA_QK), F32)
    cq = jnp.concatenate([jnp.full((T_TOK, MLA_NOPE), scale, F32), c32 * scale, zq], axis=1)
    sq = jnp.concatenate([jnp.zeros((T_TOK, MLA_NOPE), F32), s32 * scale, zq], axis=1)
    zn = jnp.zeros((T_TOK, MLA_NOPE), F32)
    ck = jnp.concatenate([zn, c32, zq], axis=1)
    sk = jnp.concatenate([zn, s32, zq], axis=1)
    return cq, sq, ck, sk


ATTN_TQ = 2048
ATTN_KEY_CHUNK = 1024
_CTX_CHUNK = ((SEQ, CTX_LEN),)
_ALL_CHUNKS = tuple((lo, ATTN_KEY_CHUNK) for lo in range(0, SEQ, ATTN_KEY_CHUNK)) + _CTX_CHUNK


def _attend_pair(q, k_ref, vt_ref, chunks):
    m = [None, None]
    acc = [None, None]
    for lo, size in chunks:
        for j in range(2):
            cols = slice(j * HEAD_PAD, (j + 1) * HEAD_PAD)
            st = lax.dot_general(k_ref[0, lo:lo + size, cols], q[:, cols], (((1,), (1,)), ((), ())), preferred_element_type=F32)
            mc = jnp.max(st, axis=0, keepdims=True)
            m_new = mc if m[j] is None else jnp.maximum(m[j], mc)
            pv = jnp.dot(vt_ref[cols, lo:lo + size], jnp.exp2(st - m_new).astype(BF16), preferred_element_type=F32)
            acc[j] = pv if m[j] is None else acc[j] * jnp.exp2(m[j] - m_new) + pv
            m[j] = m_new
    outs = [a / a[MLA_V:MLA_V + 1, :] for a in acc]
    return jnp.concatenate([outs[0][:MLA_V], outs[1][:MLA_V]], axis=0).T


def _attn_kernel(q_ref, k_ref, vt_ref, o_ref, *, lat_tiles):
    qi = pl.program_id(2)

    @pl.when(qi < lat_tiles)
    def _():
        o_ref[0] = _attend_pair(q_ref[0], k_ref, vt_ref, _ALL_CHUNKS).astype(BF16)

    @pl.when(qi >= lat_tiles)
    def _():
        o_ref[0, :CTX_LEN, :] = _attend_pair(q_ref[0, :CTX_LEN, :], k_ref, vt_ref, _CTX_CHUNK).astype(BF16)


def _attention(q, k, vt):
    b = q.shape[0]
    tq = ATTN_TQ
    lat_tiles = SEQ // tq
    return pl.pallas_call(
        functools.partial(_attn_kernel, lat_tiles=lat_tiles),
        grid=(b, MLA_HEADS // 2, lat_tiles + 1),
        in_specs=[pl.BlockSpec((1, tq, 2 * HEAD_PAD), lambda bi, hp, qi: (bi, qi, hp)),
                  pl.BlockSpec((1, T_TOK, 2 * HEAD_PAD), lambda bi, hp, qi: (bi, 0, hp)),
                  pl.BlockSpec((2 * HEAD_PAD, T_TOK), lambda bi, hp, qi: (hp, bi))],
        out_specs=pl.BlockSpec((1, tq, 2 * MLA_V), lambda bi, hp, qi: (bi, qi, hp)),
        out_shape=jax.ShapeDtypeStruct((b, T_TOK, MLA_HEADS * MLA_V), BF16),
        compiler_params=_cparams(("parallel", "parallel", "arbitrary")),
        name="attention",
    )(q, k, vt)


def _gated_add(x_ref, mod_ref, y, gate_idx, o_ref):
    for s in range(x_ref.shape[0] // SEG):
        rows = slice(s * SEG, (s + 1) * SEG)
        g = mod_ref[s][gate_idx:gate_idx + 1]
        o_ref[rows, :] = x_ref[rows, :] + g * y[rows, :]


def _proj_res_kernel(x_ref, mod_ref, y_ref, w_ref, o_ref):
    y = jnp.dot(y_ref[...], w_ref[...], preferred_element_type=F32)
    _gated_add(x_ref, mod_ref, y, 2, o_ref)


def _proj_res(x, modtab, y, w, tm=1024):
    n, d = x.shape
    return pl.pallas_call(
        _proj_res_kernel,
        grid=(n // tm,),
        in_specs=[pl.BlockSpec((tm, d), lambda g: (g, 0)),
                  pl.BlockSpec((tm // SEG, N_MOD, d), lambda g: (g, 0, 0)),
                  pl.BlockSpec((tm, y.shape[1]), lambda g: (g, 0)),
                  pl.BlockSpec(w.shape, lambda g: (0, 0))],
        out_specs=pl.BlockSpec((tm, d), lambda g: (g, 0)),
        out_shape=jax.ShapeDtypeStruct((n, d), F32),
        compiler_params=_cparams(("parallel",)),
        name="proj_res",
    )(x, modtab, y, w)


_S5_LAT_CHUNKS = SEQ // S5_CHUNK
_S5_CHUNKS = T_TOK // S5_CHUNK
_S5_W = S5_CHUNK * S5_GROUP
_S5_BLK_CHUNKS = 8
_S5_BLK_TOK = _S5_BLK_CHUNKS * S5_CHUNK
LANES = 128
_SLOTS = LANES // S5_GROUP


def _s5_in_kernel(x_ref, mod_ref, g_ref, u_ref, h_scr, *, nb):
    for b in range(nb):
        m = mod_ref[0, b]
        h = _norm_mod(x_ref[b], g_ref[...], m[0:1], m[1:2])
        for j in range(D_MODEL // LANES):
            h_scr[j, b * _S5_BLK_TOK:(b + 1) * _S5_BLK_TOK, :] = h[:, j * LANES:(j + 1) * LANES]
    slot = lax.broadcasted_iota(jnp.int32, (_S5_BLK_CHUNKS * nb, LANES), 1) // S5_GROUP
    for j in range(D_MODEL // LANES):
        for half in range(_S5_W // LANES):
            srcs = []
            for p in range(_SLOTS):
                t = _SLOTS * half + p
                rows = [h_scr[j, pl.ds(c * S5_CHUNK + t, nb, stride=_S5_BLK_TOK), :] for c in range(_S5_BLK_CHUNKS)]
                srcs.append(jnp.concatenate(rows, axis=0))
            for gs in range(_SLOTS):
                acc = None
                for p in range(_SLOTS):
                    k = (p - gs) % _SLOTS
                    r = pltpu.roll(srcs[p], k * S5_GROUP, 1) if k else srcs[p]
                    acc = r if acc is None else jnp.where(slot == p, r, acc)
                u_ref[_SLOTS * j + gs, :, half * LANES:(half + 1) * LANES] = acc.astype(BF16)


def _s5_in(x3, modsel, g1):
    nb, t, d = x3.shape
    nblk = t // _S5_BLK_TOK
    rows = _S5_BLK_CHUNKS * nb
    return pl.pallas_call(
        functools.partial(_s5_in_kernel, nb=nb),
        grid=(nblk,),
        in_specs=[pl.BlockSpec((nb, _S5_BLK_TOK, d), lambda k: (0, k, 0)),
                  pl.BlockSpec((1, nb, N_MOD, d), lambda k: (k // (SEQ // _S5_BLK_TOK), 0, 0, 0)),
                  pl.BlockSpec(g1.shape, lambda k: (0, 0))],
        out_specs=pl.BlockSpec((S5_GROUPS, rows, _S5_W), lambda k: (0, k, 0)),
        out_shape=jax.ShapeDtypeStruct((S5_GROUPS, nblk * rows, _S5_W), BF16),
        scratch_shapes=[pltpu.VMEM((d // LANES, nb * _S5_BLK_TOK, LANES), F32)],
        compiler_params=_cparams(("parallel",)),
        name="s5_in",
    )(x3, modsel, g1)


def _s5_out_kernel(x_ref, mod_ref, y_ref, w_ref, o_ref, nat_scr, *, nb):
    d = x_ref.shape[2]
    slot = lax.broadcasted_iota(jnp.int32, (nb, LANES), 1) // S5_GROUP
    for j in range(d // LANES):
        for half in range(_S5_W // LANES):
            for c in range(_S5_BLK_CHUNKS):
                srcs = [y_ref[_SLOTS * j + gs, c * nb:(c + 1) * nb, half * LANES:(half + 1) * LANES] for gs in range(_SLOTS)]
                for p in range(_SLOTS):
                    acc = None
                    for gs in range(_SLOTS):
                        k = (gs - p) % _SLOTS
                        r = pltpu.roll(srcs[gs], k * S5_GROUP, 1) if k else srcs[gs]
                        acc = r if acc is None else jnp.where(slot == gs, r, acc)
                    nat_scr[j, pl.ds(c * S5_CHUNK + _SLOTS * half + p, nb, stride=_S5_BLK_TOK), :] = acc
    y_nat = jnp.concatenate([nat_scr[j] for j in range(d // LANES)], axis=1)
    g = jax.nn.gelu(y_nat).astype(BF16)
    z = jnp.dot(g, w_ref[...], preferred_element_type=F32)
    y = z[:, :d] * jax.nn.sigmoid(z[:, d:])
    for b in range(nb):
        gate = mod_ref[0, b][2:3]
        o_ref[b] = x_ref[b] + gate * y[b * _S5_BLK_TOK:(b + 1) * _S5_BLK_TOK, :]


def _s5_out(x3, modsel, y, w):
    nb, t, d = x3.shape
    nblk = t // _S5_BLK_TOK
    rows = _S5_BLK_CHUNKS * nb
    return pl.pallas_call(
        functools.partial(_s5_out_kernel, nb=nb),
        grid=(nblk,),
        in_specs=[pl.BlockSpec((nb, _S5_BLK_TOK, d), lambda k: (0, k, 0)),
                  pl.BlockSpec((1, nb, N_MOD, d), lambda k: (k // (SEQ // _S5_BLK_TOK), 0, 0, 0)),
                  pl.BlockSpec((S5_GROUPS, rows, _S5_W), lambda k: (0, k, 0)),
                  pl.BlockSpec(w.shape, lambda k: (0, 0))],
        out_specs=pl.BlockSpec((nb, _S5_BLK_TOK, d), lambda k: (0, k, 0)),
        out_shape=jax.ShapeDtypeStruct((nb, t, d), F32),
        scratch_shapes=[pltpu.VMEM((d // LANES, nb * _S5_BLK_TOK, LANES), F32)],
        compiler_params=_cparams(("parallel",)),
        name="s5_out",
    )(x3, modsel, y, w)


S5_GROUPS_PER_STEP = 2


def _s5_kernel(u_ref, m_ref, win_ref, wof_ref, wor_ref, ar_ref, ai_ref, y_ref, z_ref, sf_ref, sr_ref, *, nb):
    gp = u_ref.shape[0]
    for g in range(gp):
        z_ref[g] = jnp.dot(u_ref[g], win_ref[g], preferred_element_type=F32)
    half = _S5_W // 2
    ars = [ar_ref[g] for g in range(gp)]
    ais = [ai_ref[g] for g in range(gp)]
    lane = lax.broadcasted_iota(jnp.int32, (nb, _S5_W), 1)
    is_fwd = (lane % half) < S5_STATE

    def step(i, states):
        cf = jnp.where(i < _S5_CHUNKS - _S5_LAT_CHUNKS, i + _S5_LAT_CHUNKS, i - (_S5_CHUNKS - _S5_LAT_CHUNKS))
        cr = _S5_CHUNKS - 1 - i
        rf = pl.multiple_of(cf * nb, nb)
        rr = pl.multiple_of(cr * nb, nb)
        out = []
        for g in range(gp):
            s = states[g]
            sf_ref[g, pl.ds(rf, nb), :] = s
            sr_ref[g, pl.ds(rr, nb), :] = s
            z = jnp.where(is_fwd, z_ref[g, pl.ds(rf, nb), :], z_ref[g, pl.ds(rr, nb), :])
            re = s[:, :half]
            im = s[:, half:]
            out.append(jnp.concatenate([ars[g] * re - ais[g] * im + z[:, :half], ars[g] * im + ais[g] * re + z[:, half:]], axis=1))
        return tuple(out)

    lax.fori_loop(0, _S5_CHUNKS, step, tuple(jnp.zeros((nb, _S5_W), F32) for _ in range(gp)))
    for g in range(gp):
        y = jnp.dot(u_ref[g], m_ref[g], preferred_element_type=F32)
        y = y + jnp.dot(sf_ref[g].astype(BF16), wof_ref[g], preferred_element_type=F32)
        y = y + jnp.dot(sr_ref[g].astype(BF16), wor_ref[g], preferred_element_type=F32)
        y_ref[g] = y


def _s5_core(u, sp, nb):
    g, rows, w = u.shape
    gp = S5_GROUPS_PER_STEP
    blk = lambda a: pl.BlockSpec((gp,) + a.shape[1:], lambda i: (i,) + (0,) * (a.ndim - 1))
    return pl.pallas_call(
        functools.partial(_s5_kernel, nb=nb),
        grid=(g // gp,),
        in_specs=[blk(u), blk(sp["m"]), blk(sp["win"]), blk(sp["wof"]), blk(sp["wor"]), blk(sp["ar"]), blk(sp["ai"])],
        out_specs=pl.BlockSpec((gp, rows, w), lambda i: (i, 0, 0)),
        out_shape=jax.ShapeDtypeStruct((g, rows, w), F32),
        scratch_shapes=[pltpu.VMEM((gp, rows, w), F32), pltpu.VMEM((gp, rows, w), F32), pltpu.VMEM((gp, rows, w), F32)],
        compiler_params=_cparams(("parallel",)),
        name="s5_core",
    )(u, sp["m"], sp["win"], sp["wof"], sp["wor"], sp["ar"], sp["ai"])


def _s5_params(a_re, a_im, log_dt, b_re, b_im, c_re, c_im, d):
    c64 = jnp.complex64
    lam = lax.complex(jnp.minimum(a_re, S5_MAX_RE), a_im)
    lam_dt = lam * jnp.exp(log_dt)[..., None]
    lam_bar = jnp.exp(lam_dt)
    b_bar = ((lam_bar - 1.0) / lam)[..., None] * lax.complex(b_re, b_im)
    c_mat = lax.complex(c_re, c_im)
    taus = jnp.arange(S5_CHUNK + 1, dtype=F32)
    pw = jnp.exp(lam_dt[None] * taus[:, None, None, None].astype(c64))
    hi = lax.Precision.HIGHEST
    kern = jnp.real(jnp.einsum('dgnp,tdgp,dgpm->dgtnm', c_mat, pw[:S5_CHUNK], b_bar, precision=hi))
    eye = jnp.eye(S5_GROUP, dtype=F32)
    k0 = kern[0, :, 0] + kern[1, :, 0] + d.reshape(S5_GROUPS, S5_GROUP)[:, :, None] * eye
    lags = jnp.concatenate([kern[1, :, :0:-1], k0[:, None], kern[0, :, 1:]], axis=1)
    idx = (np.arange(S5_CHUNK)[None, :] - np.arange(S5_CHUNK)[:, None]) + S5_CHUNK - 1
    m = lags[:, idx]
    m = m.transpose(0, 1, 4, 2, 3).reshape(S5_GROUPS, _S5_W, _S5_W)
    wf = pw[S5_CHUNK - 1::-1][:S5_CHUNK, 0][..., None] * b_bar[0][None]
    wr = pw[:S5_CHUNK, 1][..., None] * b_bar[1][None]
    to_rows = lambda w: w.transpose(1, 0, 3, 2).reshape(S5_GROUPS, _S5_W, S5_STATE)
    wf, wr = to_rows(wf), to_rows(wr)
    win = jnp.concatenate([jnp.real(wf), jnp.real(wr), jnp.imag(wf), jnp.imag(wr)], axis=-1)
    of = c_mat[0][None] * pw[1:, 0][:, :, None, :]
    orv = c_mat[1][None] * pw[S5_CHUNK:0:-1, 1][:, :, None, :]
    to_cols = lambda w: w.transpose(1, 3, 0, 2).reshape(S5_GROUPS, S5_STATE, _S5_W)
    of, orv = to_cols(of), to_cols(orv)
    zeros = jnp.zeros_like(jnp.real(of))
    wof = jnp.concatenate([jnp.real(of), zeros, -jnp.imag(of), zeros], axis=1)
    wor = jnp.concatenate([zeros, jnp.real(orv), zeros, -jnp.imag(orv)], axis=1)
    a16 = pw[S5_CHUNK]
    ar = jnp.concatenate([jnp.real(a16[0]), jnp.real(a16[1])], axis=-1)[:, None, :]
    ai = jnp.concatenate([jnp.imag(a16[0]), jnp.imag(a16[1])], axis=-1)[:, None, :]
    return {"m": m.astype(BF16), "win": win.astype(BF16), "wof": wof.astype(BF16), "wor": wor.astype(BF16),
            "ar": ar, "ai": ai}


POOL_HALO = 16


def _pool_kernel(xp_ref, xc_ref, xn_ref, mod_ref, g_ref, wp_ref, sc_ref, o_ref):
    seg = pl.program_id(0) % SEGS_PER_BATCH
    is_ctx = seg >= LAT_SEGS
    p0 = jnp.where(is_ctx, 0, seg * SEG)
    lseq = jnp.where(is_ctx, CTX_LEN, SEQ)
    m = mod_ref[0]
    g = g_ref[...]
    hs = [_norm_mod(r[...], g, m[0:1], m[1:2]) for r in (xp_ref, xc_ref, xn_ref)]
    hcat = jnp.concatenate([h.astype(BF16) for h in hs], axis=0)
    r = lax.broadcasted_iota(jnp.int32, (SEG, SEG + 2 * POOL_HALO), 0)
    s = lax.broadcasted_iota(jnp.int32, (SEG, SEG + 2 * POOL_HALO), 1)
    pt = p0 + r
    ps = p0 - POOL_HALO + s
    rcol = lax.broadcasted_iota(jnp.int32, (SEG, 1), 0) + p0
    x = xc_ref[...]
    gate = m[2:3]
    for gi, w in enumerate(POOL_WINDOWS):
        cols = slice(gi * POOL_GROUP, (gi + 1) * POOL_GROUP)
        lo = jnp.maximum(pt - w // 2, 0)
        hi = jnp.minimum(pt + w - w // 2, lseq)
        band = jnp.where((ps >= lo) & (ps < hi), 1.0, 0.0).astype(BF16)
        tot = jnp.dot(band, hcat[:, cols], preferred_element_type=F32)
        cnt = (jnp.minimum(rcol + w - w // 2, lseq) - jnp.maximum(rcol - w // 2, 0)).astype(F32)
        resid = tot / cnt - hs[1][:, cols]
        y = jnp.dot(resid.astype(BF16), wp_ref[gi], preferred_element_type=F32) * sc_ref[:, cols]
        o_ref[:, cols] = x[:, cols] + gate[:, cols] * y


def _pool(x, modtab, g1, w_pool, scale):
    n, d = x.shape
    nseg = n // SEG

    halo_per_seg = SEG // POOL_HALO

    def prev(g):
        return (jnp.maximum(g * halo_per_seg - 1, 0), 0)

    def nxt(g):
        return (jnp.minimum((g + 1) * halo_per_seg, nseg * halo_per_seg - 1), 0)

    return pl.pallas_call(
        _pool_kernel,
        grid=(nseg,),
        in_specs=[pl.BlockSpec((POOL_HALO, d), prev), pl.BlockSpec((SEG, d), lambda g: (g, 0)), pl.BlockSpec((POOL_HALO, d), nxt),
                  pl.BlockSpec((1, N_MOD, d), lambda g: (g, 0, 0)),
                  pl.BlockSpec(g1.shape, lambda g: (0, 0)),
                  pl.BlockSpec(w_pool.shape, lambda g: (0, 0, 0)),
                  pl.BlockSpec(scale.shape, lambda g: (0, 0))],
        out_specs=pl.BlockSpec((SEG, d), lambda g: (g, 0)),
        out_shape=jax.ShapeDtypeStruct((n, d), F32),
        compiler_params=_cparams(("parallel",)),
        name="pool",
    )(x, x, x, modtab, g1, w_pool, scale)


def _swiglu_partial(h, wg, wu, wd):
    a = jax.nn.silu(jnp.dot(h, wg, preferred_element_type=F32)) * jnp.dot(h, wu, preferred_element_type=F32)
    return jnp.dot(a.astype(BF16), wd, preferred_element_type=F32)


def _ffn_kernel(x_ref, mod_ref, g_ref, wg_ref, wu_ref, wd_ref, o_ref):
    h = _norm_mod_tile(x_ref, mod_ref, g_ref, 3)
    _gated_add(x_ref, mod_ref, _swiglu_partial(h, wg_ref[...], wu_ref[...], wd_ref[...]), 5, o_ref)


_RESIDENT = pl.Buffered(1)


def _ffn(x, modtab, g2, w_gu, w_down, layer, tm=512):
    n, d = x.shape
    f = w_down.shape[1]
    return pl.pallas_call(
        _ffn_kernel,
        grid=(n // tm,),
        in_specs=[pl.BlockSpec((tm, d), lambda g: (g, 0)),
                  pl.BlockSpec((tm // SEG, N_MOD, d), lambda g: (g, 0, 0)),
                  pl.BlockSpec(g2.shape, lambda g: (0, 0)),
                  pl.BlockSpec((None, d, f), lambda g: (layer, 0, 0), pipeline_mode=_RESIDENT),
                  pl.BlockSpec((None, d, f), lambda g: (layer, 0, 1), pipeline_mode=_RESIDENT),
                  pl.BlockSpec((None, f, d), lambda g: (layer, 0, 0), pipeline_mode=_RESIDENT)],
        out_specs=pl.BlockSpec((tm, d), lambda g: (g, 0)),
        out_shape=jax.ShapeDtypeStruct((n, d), F32),
        compiler_params=_cparams(("parallel",)),
        name="ffn",
    )(x, modtab, g2, w_gu, w_gu, w_down)


MOE_TILE = 512
MOE_DMA_TILE = 1024


def _router_kernel(x_ref, mod_ref, g_ref, w_ref, b_ref, h_ref, rw_ref, ri_ref, cnt_ref, run_ref):
    @pl.when(pl.program_id(0) == 0)
    def _():
        run_ref[...] = jnp.zeros_like(run_ref)

    h = _norm_mod_tile(x_ref, mod_ref, g_ref, 3, out_dtype=F32)
    h_ref[...] = h
    logits = jnp.dot(h, w_ref[...], precision=lax.Precision.HIGHEST, preferred_element_type=F32) + b_ref[...]
    lane = lax.broadcasted_iota(jnp.int32, logits.shape, 1)
    neg = jnp.float32(-jnp.inf)
    lg = jnp.where(lane < N_EXPERTS, logits, neg)
    m1 = jnp.max(lg, axis=-1, keepdims=True)
    i1 = jnp.min(jnp.where(lg == m1, lane, HEAD_PAD), axis=-1, keepdims=True)
    lg2 = jnp.where(lane == i1, neg, lg)
    m2 = jnp.max(lg2, axis=-1, keepdims=True)
    i2 = jnp.min(jnp.where(lg2 == m2, lane, HEAD_PAD), axis=-1, keepdims=True)
    e2 = jnp.exp(m2 - m1)
    den = 1.0 + e2
    tm = logits.shape[0]
    sel = jnp.where((lane == i1) | (lane == i2), 1.0, 0.0).astype(BF16)
    tri = jnp.where(lax.broadcasted_iota(jnp.int32, (tm, tm), 1) <= lax.broadcasted_iota(jnp.int32, (tm, tm), 0), 1.0, 0.0)
    within = jnp.dot(tri.astype(BF16), sel, preferred_element_type=F32)
    rank = run_ref[0:1, :] + within - 1.0
    r1 = jnp.sum(jnp.where(lane == i1, rank, 0.0), axis=-1, keepdims=True).astype(jnp.int32)
    r2 = jnp.sum(jnp.where(lane == i2, rank, 0.0), axis=-1, keepdims=True).astype(jnp.int32)
    total = run_ref[0:1, :] + within[tm - 1:tm, :]
    run_ref[...] = jnp.broadcast_to(total, run_ref.shape)
    cnt_ref[...] = jnp.broadcast_to(total, cnt_ref.shape).astype(jnp.int32)
    rw_ref[...] = jnp.where(lane == 0, 1.0 / den, jnp.where(lane == 1, e2 / den, 0.0))
    ri_ref[...] = jnp.where(lane == 0, i1, jnp.where(lane == 1, i2, jnp.where(lane == 2, r1, jnp.where(lane == 3, r2, 0))))


def _router(x, modtab, g2, w_router, b_router, tm=512):
    n, d = x.shape
    wr = jnp.pad(w_router, ((0, 0), (0, HEAD_PAD - N_EXPERTS)))
    br = jnp.pad(b_router, (0, HEAD_PAD - N_EXPERTS)).reshape(1, HEAD_PAD)
    return pl.pallas_call(
        _router_kernel,
        grid=(n // tm,),
        in_specs=[pl.BlockSpec((tm, d), lambda g: (g, 0)),
                  pl.BlockSpec((tm // SEG, N_MOD, d), lambda g: (g, 0, 0)),
                  pl.BlockSpec(g2.shape, lambda g: (0, 0)),
                  pl.BlockSpec(wr.shape, lambda g: (0, 0)),
                  pl.BlockSpec(br.shape, lambda g: (0, 0))],
        out_specs=[pl.BlockSpec((tm, d), lambda g: (g, 0)),
                   pl.BlockSpec((tm, HEAD_PAD), lambda g: (g, 0)),
                   pl.BlockSpec((tm, HEAD_PAD), lambda g: (g, 0)),
                   pl.BlockSpec((8, HEAD_PAD), lambda g: (0, 0))],
        out_shape=[jax.ShapeDtypeStruct((n, d), F32),
                   jax.ShapeDtypeStruct((n, HEAD_PAD), F32),
                   jax.ShapeDtypeStruct((n, HEAD_PAD), jnp.int32),
                   jax.ShapeDtypeStruct((8, HEAD_PAD), jnp.int32)],
        scratch_shapes=[pltpu.VMEM((8, HEAD_PAD), F32)],
        compiler_params=_cparams(("arbitrary",)),
        name="router",
    )(x, modtab, g2, wr, br)


def _moe_plan(ri, cnt, n):
    e = ri[:, :TOP_K]
    counts = cnt[0, :N_EXPERTS]
    padded = (counts + MOE_TILE - 1) // MOE_TILE * MOE_TILE
    ends = jnp.cumsum(padded)
    starts = ends - padded
    dest = starts[e] + ri[:, TOP_K:2 * TOP_K]
    n_tiles = (TOP_K * n) // MOE_TILE + N_EXPERTS
    n_valid = (ends[-1] // MOE_TILE).astype(jnp.int32).reshape(1)
    tile_ids = jnp.arange(n_tiles, dtype=jnp.int32)
    tile_expert = jnp.sum((ends // MOE_TILE)[None, :] <= jnp.minimum(tile_ids, n_valid - 1)[:, None], axis=1).astype(jnp.int32)
    seg_len = jnp.concatenate([padded - counts, (n_tiles * MOE_TILE - ends[-1])[None]])
    seg_base = jnp.concatenate([starts + counts, ends[-1:]])
    seg_end = jnp.cumsum(seg_len)
    k = jnp.arange(N_EXPERTS * MOE_TILE, dtype=jnp.int32)
    seg = jnp.sum(seg_end[None, :] <= k[:, None], axis=1)
    pad_dest = seg_base[seg] + k - (seg_end - seg_len)[seg]
    return dest[:, 0].astype(jnp.int32), dest[:, 1].astype(jnp.int32), pad_dest.astype(jnp.int32), tile_expert, n_valid, n_tiles


def _row(ref, i):
    return ref.at[pl.ds(i, 1)]


def _scatter_rows_kernel(d0_ref, d1_ref, pad_ref, h_ref, o_hbm, zero_ref, sem):
    @pl.when(pl.program_id(0) == 0)
    def _():
        zero_ref[...] = jnp.zeros_like(zero_ref)

        def issue_zero(k, carry):
            pltpu.make_async_copy(_row(zero_ref, 0), _row(o_hbm, pad_ref[k]), sem.at[2]).start()
            return carry

        lax.fori_loop(0, N_EXPERTS * MOE_TILE, issue_zero, 0, unroll=8)
        for _ in range(N_EXPERTS * MOE_TILE // MOE_DMA_TILE):
            pltpu.make_async_copy(h_ref, o_hbm.at[pl.ds(0, MOE_DMA_TILE)], sem.at[2]).wait()

    base = pl.program_id(0) * MOE_DMA_TILE

    def issue(r, carry):
        t = base + r
        pltpu.make_async_copy(_row(h_ref, r), _row(o_hbm, d0_ref[t]), sem.at[0]).start()
        pltpu.make_async_copy(_row(h_ref, r), _row(o_hbm, d1_ref[t]), sem.at[1]).start()
        return carry

    lax.fori_loop(0, MOE_DMA_TILE, issue, 0, unroll=8)
    pltpu.make_async_copy(h_ref, o_hbm.at[pl.ds(0, MOE_DMA_TILE)], sem.at[0]).wait()
    pltpu.make_async_copy(h_ref, o_hbm.at[pl.ds(0, MOE_DMA_TILE)], sem.at[1]).wait()


def _scatter_rows(d0, d1, pad_dest, h, n_rows):
    n, d = h.shape
    return pl.pallas_call(
        _scatter_rows_kernel,
        grid_spec=pltpu.PrefetchScalarGridSpec(
            num_scalar_prefetch=3,
            grid=(n // MOE_DMA_TILE,),
            in_specs=[pl.BlockSpec((MOE_DMA_TILE, d), lambda g, d0, d1, pd: (g, 0))],
            out_specs=pl.BlockSpec(memory_space=pl.ANY),
            scratch_shapes=[pltpu.VMEM((8, d), F32), pltpu.SemaphoreType.DMA((3,))]),
        out_shape=jax.ShapeDtypeStruct((n_rows, d), F32),
        compiler_params=_cparams(("arbitrary",)),
        name="moe_scatter",
    )(d0, d1, pad_dest, h)


def _moe_ffn_kernel(te_ref, nv_ref, xs_ref, wg_ref, wu_ref, wd_ref, o_ref):
    del te_ref
    i = pl.program_id(0)

    @pl.when(i < nv_ref[0])
    def _():
        o_ref[...] = _swiglu_partial(xs_ref[...].astype(BF16), wg_ref[...], wu_ref[...], wd_ref[...])

    @pl.when(i >= nv_ref[0])
    def _():
        o_ref[...] = jnp.zeros_like(o_ref)


def _moe_ffn(xs, n_rows, tile_expert, n_valid, w_gu, w_down, layer):
    d = xs.shape[1]
    rows = n_rows
    f = w_down.shape[2]
    return pl.pallas_call(
        _moe_ffn_kernel,
        grid_spec=pltpu.PrefetchScalarGridSpec(
            num_scalar_prefetch=2,
            grid=(rows // MOE_TILE,),
            in_specs=[pl.BlockSpec((MOE_TILE, d), lambda i, te, nv: (i, 0)),
                      pl.BlockSpec((None, None, d, f), lambda i, te, nv: (layer, te[i], 0, 0), pipeline_mode=_RESIDENT),
                      pl.BlockSpec((None, None, d, f), lambda i, te, nv: (layer, te[i], 0, 1), pipeline_mode=_RESIDENT),
                      pl.BlockSpec((None, None, f, d), lambda i, te, nv: (layer, te[i], 0, 0), pipeline_mode=_RESIDENT)],
            out_specs=pl.BlockSpec((MOE_TILE, d), lambda i, te, nv: (i, 0))),
        out_shape=jax.ShapeDtypeStruct((rows, d), F32),
        compiler_params=_cparams(("arbitrary",)),
        name="moe_ffn",
    )(tile_expert, n_valid, xs, w_gu, w_gu, w_down)


def _combine_kernel(d0_ref, d1_ref, x_ref, mod_ref, rw_ref, y_hbm, *rest, tm, final):
    if final:
        fg_ref, o_ref, ya_ref, yb_ref, sem = rest
        tile = pl.program_id(0) * SEGS_PER_BATCH + pl.program_id(1)
    else:
        o_ref, ya_ref, yb_ref, sem = rest
        tile = pl.program_id(0)
    base = tile * tm

    def issue(r, carry):
        t = base + r
        pltpu.make_async_copy(_row(y_hbm, d0_ref[t]), _row(ya_ref, r), sem.at[0]).start()
        pltpu.make_async_copy(_row(y_hbm, d1_ref[t]), _row(yb_ref, r), sem.at[1]).start()
        return carry

    lax.fori_loop(0, tm, issue, 0, unroll=8)
    rw = rw_ref[...]
    lane = lax.broadcasted_iota(jnp.int32, rw.shape, 1)
    w1 = jnp.sum(jnp.where(lane == 0, rw, 0.0), axis=-1, keepdims=True)
    w2 = jnp.sum(jnp.where(lane == 1, rw, 0.0), axis=-1, keepdims=True)
    pltpu.make_async_copy(y_hbm.at[pl.ds(0, tm)], ya_ref, sem.at[0]).wait()
    pltpu.make_async_copy(y_hbm.at[pl.ds(0, tm)], yb_ref, sem.at[1]).wait()
    y = w1 * ya_ref[...] + w2 * yb_ref[...]
    if final:
        o_ref[0] = _rms(x_ref[...] + mod_ref[0][5:6] * y, fg_ref[...])
    else:
        _gated_add(x_ref, mod_ref, y, 5, o_ref)


def _combine(d0, d1, x, modtab, rw, y, final_g=None):
    n, d = x.shape
    final = final_g is not None
    tm = SEG if final else MOE_DMA_TILE
    if final:
        nb = n // T_TOK
        grid = (nb, LAT_SEGS)
        row = lambda bi, j, d0, d1: (bi * SEGS_PER_BATCH + j, 0)
        mrow = lambda bi, j, d0, d1: (bi * SEGS_PER_BATCH + j, 0, 0)
        extra = [pl.BlockSpec(final_g.shape, lambda bi, j, d0, d1: (0, 0))]
        out_spec = pl.BlockSpec((1, tm, d), lambda bi, j, d0, d1: (bi, j, 0))
        out_shape = jax.ShapeDtypeStruct((nb, SEQ, d), F32)
        args = (final_g,)
    else:
        grid = (n // tm,)
        row = lambda g, d0, d1: (g, 0)
        mrow = lambda g, d0, d1: (g, 0, 0)
        extra = []
        out_spec = pl.BlockSpec((tm, d), row)
        out_shape = jax.ShapeDtypeStruct((n, d), F32)
        args = ()
    return pl.pallas_call(
        functools.partial(_combine_kernel, tm=tm, final=final),
        grid_spec=pltpu.PrefetchScalarGridSpec(
            num_scalar_prefetch=2,
            grid=grid,
            in_specs=[pl.BlockSpec((tm, d), row),
                      pl.BlockSpec((tm // SEG, N_MOD, d), mrow),
                      pl.BlockSpec((tm, HEAD_PAD), row),
                      pl.BlockSpec(memory_space=pl.ANY)] + extra,
            out_specs=out_spec,
            scratch_shapes=[pltpu.VMEM((tm, d), F32), pltpu.VMEM((tm, d), F32), pltpu.SemaphoreType.DMA((2,))]),
        out_shape=out_shape,
        compiler_params=_cparams(("arbitrary",) * len(grid)),
        name="moe_combine",
    )(d0, d1, x, modtab, rw, y, *args)


def _moe(x, modtab, g2, w_router, b_router, w_gu, w_down, layer, final_g=None):
    n = x.shape[0]
    h, rw, ri, cnt = _router(x, modtab, g2, w_router, b_router)
    d0, d1, pad_dest, tile_expert, n_valid, n_tiles = _moe_plan(ri, cnt, n)
    xs = _scatter_rows(d0, d1, pad_dest, h, n_tiles * MOE_TILE)
    ys = _moe_ffn(xs, n_tiles * MOE_TILE, tile_expert, n_valid, w_gu, w_down, layer)
    return _combine(d0, d1, x, modtab, rw, ys, final_g)


def kernel(x, c, ctx, c_ctx, norm1_g, norm2_g, ada_w, ada_b, final_norm_g, mla_w_dq, mla_q_norm, mla_w_uq, mla_w_dkv, mla_kv_norm, mla_w_ukv, mla_w_o, s5_a_re, s5_a_im, s5_log_dt, s5_b_re, s5_b_im, s5_c_re, s5_c_im, s5_d, s5_w_glu, pool_w, pool_scale, ffn_w_gu, ffn_w_down, moe_w_router, moe_b_router, moe_w_gu, moe_w_down):
    b, l, d = x.shape
    n = b * T_TOK
    xs = jnp.concatenate([x, ctx], axis=1).reshape(n, d)

    rows = -(-(b + 1) // 8) * 8
    cc = jnp.concatenate([c, c_ctx[None], jnp.zeros((rows - b - 1, d), F32)], axis=0)
    mods = _ada(cc, ada_w, ada_b)
    seg_src = np.array([bi if sj < LAT_SEGS else b for bi in range(b) for sj in range(SEGS_PER_BATCH)], np.int32)

    rope = _rope_tables()
    ffn_w_gu_bf, ffn_w_down_bf = ffn_w_gu.astype(BF16), ffn_w_down.astype(BF16)
    moe_w_gu_bf, moe_w_down_bf = moe_w_gu.astype(BF16), moe_w_down.astype(BF16)
    for i in range(DEPTH):
        last = i == DEPTH - 1
        modtab = mods[i].reshape(rows, N_MOD, d)[seg_src]
        g1 = norm1_g[i].reshape(1, d)
        g2 = norm2_g[i].reshape(1, d)
        kind = i % N_MIXERS
        j = i // N_MIXERS
        if kind == 0:
            pw = _mla_weights(mla_w_dq[j], mla_q_norm[j], mla_w_uq[j], mla_w_dkv[j], mla_kv_norm[j], mla_w_ukv[j])
            q, k, vt = _mla_proj(xs, modtab, g1, pw, rope)
            o = _attention(q.reshape(b, T_TOK, -1), k.reshape(b, T_TOK, -1), vt)
            xs = _proj_res(xs, modtab, o.reshape(n, -1), mla_w_o[j].astype(BF16))
        elif kind == 1:
            mod9 = mods[i].reshape(rows, N_MOD, d)
            modsel = jnp.stack([mod9[:b], jnp.broadcast_to(mod9[b], (b, N_MOD, d))])
            x3 = xs.reshape(b, T_TOK, d)
            sp = _s5_params(s5_a_re[j], s5_a_im[j], s5_log_dt[j], s5_b_re[j], s5_b_im[j], s5_c_re[j], s5_c_im[j], s5_d[j])
            y = _s5_core(_s5_in(x3, modsel, g1), sp, b)
            xs = _s5_out(x3, modsel, y, s5_w_glu[j].astype(BF16)).reshape(n, d)
        else:
            xs = _pool(xs, modtab, g1, pool_w[j].astype(BF16), pool_scale[j].reshape(1, d))
        kk = i // 2
        if i % 2 == 0:
            xs = _ffn(xs, modtab, g2, ffn_w_gu_bf, ffn_w_down_bf, kk)
        else:
            xs = _moe(xs, modtab, g2, moe_w_router[kk], moe_b_router[kk], moe_w_gu_bf, moe_w_down_bf, kk,
                      final_norm_g.reshape(1, d) if last else None)
    return xs
```

```python
import functools
import math

import jax
import jax.numpy as jnp
import numpy as np
from jax import lax
from jax.experimental import pallas as pl
from jax.experimental.pallas import tpu as pltpu

F32 = jnp.float32
BF16 = jnp.bfloat16

D_MODEL = 1024
SEQ = 4096
DEPTH = 4
GRID_W = 64
CTX_LEN = 256
N_MIXERS = 3
NORM_EPS = 1e-6

MLA_HEADS = 16
MLA_Q_LORA = 384
MLA_KV_LORA = 256
MLA_NOPE = 64
MLA_ROPE = 32
MLA_V = 64
MLA_QK = MLA_NOPE + MLA_ROPE
ROPE_AXIS_FREQS = MLA_ROPE // 4
ROPE_THETA = 10000.0
HEAD_PAD = 128

S5_GROUP = 16
S5_GROUPS = D_MODEL // S5_GROUP
S5_STATE = 64
S5_MAX_RE = -1e-4
S5_CHUNK = 16

POOL_WINDOWS = (2, 4, 8, 16)
POOL_GROUP = D_MODEL // len(POOL_WINDOWS)

FFN_DIM = 2816
N_EXPERTS = 8
TOP_K = 2

assert DEPTH % 2 == 0
T_TOK = SEQ + CTX_LEN
SEG = CTX_LEN
SEGS_PER_BATCH = T_TOK // SEG
LAT_SEGS = SEQ // SEG
N_MOD = 6

VMEM_LIMIT = 56 * 1024 * 1024


def _cparams(sem, vmem=VMEM_LIMIT):
    return pltpu.CompilerParams(dimension_semantics=sem, vmem_limit_bytes=vmem)


def _rms(x, g):
    return x * lax.rsqrt(jnp.mean(x * x, axis=-1, keepdims=True) + NORM_EPS) * g


def _norm_mod(x, g, shift, scale):
    return _rms(x, g) * (1.0 + scale) + shift


def _norm_mod_tile(x_ref, mod_ref, g_ref, shift_idx, out_dtype=BF16):
    parts = []
    for s in range(x_ref.shape[0] // SEG):
        m = mod_ref[s]
        x = x_ref[s * SEG:(s + 1) * SEG, :]
        parts.append(_norm_mod(x, g_ref[...], m[shift_idx:shift_idx + 1], m[shift_idx + 1:shift_idx + 2]).astype(out_dtype))
    return parts[0] if len(parts) == 1 else jnp.concatenate(parts, axis=0)


def _ada_kernel(c_ref, w_ref, b_ref, o_ref):
    s = jax.nn.silu(c_ref[...])
    o_ref[0] = jnp.dot(s, w_ref[0], precision=lax.Precision.HIGHEST, preferred_element_type=F32) + b_ref[0]


def _ada(cc, ada_w, ada_b):
    depth, d, n6 = ada_w.shape
    rows = cc.shape[0]
    tn = 1024
    return pl.pallas_call(
        _ada_kernel,
        grid=(depth, n6 // tn),
        in_specs=[pl.BlockSpec((rows, d), lambda i, j: (0, 0)),
                  pl.BlockSpec((1, d, tn), lambda i, j: (i, 0, j)),
                  pl.BlockSpec((1, 1, tn), lambda i, j: (i, 0, j))],
        out_specs=pl.BlockSpec((1, rows, tn), lambda i, j: (i, 0, j)),
        out_shape=jax.ShapeDtypeStruct((depth, rows, n6), F32),
        compiler_params=_cparams(("arbitrary", "arbitrary")),
        name="ada",
    )(cc, ada_w, ada_b.reshape(depth, 1, n6))


_QW = MLA_HEADS * HEAD_PAD


def _rope(x, c, s):
    lane = lax.broadcasted_iota(jnp.int32, x.shape, 1)
    first_half = ((lane - MLA_NOPE) & ROPE_AXIS_FREQS) == 0
    partner = jnp.where(first_half, pltpu.roll(x, HEAD_PAD - ROPE_AXIS_FREQS, 1), pltpu.roll(x, ROPE_AXIS_FREQS, 1))
    return x * c + partner * s


def _mla_proj_kernel(x_ref, mod_ref, g1_ref, w1_ref, qn_ref, wq_ref, kvn_ref, wk_ref, wv_ref,
                     cq_ref, sq_ref, ck_ref, sk_ref, q_ref, k_ref, vt_ref):
    h = _norm_mod_tile(x_ref, mod_ref, g1_ref, 0)
    d = jnp.dot(h, w1_ref[...], preferred_element_type=F32)
    dq = d[:, :MLA_Q_LORA]
    ckv = d[:, MLA_Q_LORA:MLA_Q_LORA + MLA_KV_LORA]
    kr = d[:, MLA_Q_LORA + MLA_KV_LORA:]
    qn = _rms(dq, qn_ref[...]).astype(BF16)
    qq = jnp.dot(qn, wq_ref[...], preferred_element_type=F32)
    cq = cq_ref[...]
    sq = sq_ref[...]
    c = _rms(ckv, kvn_ref[...]).astype(BF16)
    kr_roped = _rope(kr, ck_ref[...], sk_ref[...])
    kn = jnp.dot(c, wk_ref[...], preferred_element_type=F32)
    for hd in range(MLA_HEADS):
        slab = slice(hd * HEAD_PAD, (hd + 1) * HEAD_PAD)
        q_ref[:, slab] = _rope(qq[:, slab], cq, sq).astype(BF16)
        k_ref[:, slab] = (kn[:, slab] + kr_roped).astype(BF16)
    vt = lax.dot_general(wv_ref[...], c, (((1,), (1,)), ((), ())), preferred_element_type=F32)
    row = lax.broadcasted_iota(jnp.int32, vt.shape, 0)
    vt_ref[...] = jnp.where(row % HEAD_PAD == MLA_V, 1.0, vt).astype(BF16)


def _mla_proj(x, modtab, g1, pw, rope):
    n, d = x.shape
    tm = SEG
    full = lambda a: pl.BlockSpec(a.shape, lambda g: (0,) * a.ndim)
    pos = lambda g: (g % SEGS_PER_BATCH, 0)
    cq, sq, ck, sk = rope
    return pl.pallas_call(
        _mla_proj_kernel,
        grid=(n // tm,),
        in_specs=[pl.BlockSpec((tm, d), lambda g: (g, 0)),
                  pl.BlockSpec((tm // SEG, N_MOD, d), lambda g: (g, 0, 0)),
                  full(g1), full(pw["w1"]), full(pw["qn"]), full(pw["wq"]), full(pw["kvn"]),
                  full(pw["wk"]), full(pw["wv"]),
                  pl.BlockSpec((tm, HEAD_PAD), pos), pl.BlockSpec((tm, HEAD_PAD), pos),
                  pl.BlockSpec((tm, HEAD_PAD), pos), pl.BlockSpec((tm, HEAD_PAD), pos)],
        out_specs=[pl.BlockSpec((tm, _QW), lambda g: (g, 0)),
                   pl.BlockSpec((tm, _QW), lambda g: (g, 0)),
                   pl.BlockSpec((_QW, tm), lambda g: (0, g))],
        out_shape=[jax.ShapeDtypeStruct((n, _QW), BF16)] * 2 + [jax.ShapeDtypeStruct((_QW, n), BF16)],
        compiler_params=_cparams(("parallel",)),
        name="mla_proj",
    )(x, modtab, g1, pw["w1"], pw["qn"], pw["wq"], pw["kvn"], pw["wk"], pw["wv"], cq, sq, ck, sk)


def _mla_weights(w_dq, q_norm, w_uq, w_dkv, kv_norm, w_ukv):
    w_kr = jnp.pad(w_dkv[:, MLA_KV_LORA:], ((0, 0), (MLA_NOPE, HEAD_PAD - MLA_QK)))
    w1 = jnp.concatenate([w_dq, w_dkv[:, :MLA_KV_LORA], w_kr], axis=1)
    uq = w_uq.reshape(MLA_Q_LORA, MLA_HEADS, MLA_QK)
    zpad = jnp.zeros((MLA_Q_LORA, MLA_HEADS, HEAD_PAD - MLA_QK), w_uq.dtype)
    wq = jnp.concatenate([uq, zpad], axis=-1).reshape(MLA_Q_LORA, _QW)
    ukv = w_ukv.reshape(MLA_KV_LORA, MLA_HEADS, MLA_NOPE + MLA_V)
    wk = jnp.concatenate([ukv[:, :, :MLA_NOPE],
                          jnp.zeros((MLA_KV_LORA, MLA_HEADS, HEAD_PAD - MLA_NOPE), w_ukv.dtype)], axis=-1)
    wk = wk.reshape(MLA_KV_LORA, _QW)
    wv = jnp.concatenate([ukv[:, :, MLA_NOPE:],
                          jnp.zeros((MLA_KV_LORA, MLA_HEADS, HEAD_PAD - MLA_V), w_ukv.dtype)], axis=-1)
    wv = wv.reshape(MLA_KV_LORA, _QW).T
    return {"w1": w1.astype(BF16), "qn": q_norm.reshape(1, -1), "wq": wq.astype(BF16),
            "kvn": kv_norm.reshape(1, -1), "wk": wk.astype(BF16), "wv": wv.astype(BF16)}


def _rope_tables():
    rows = SEQ // GRID_W
    row = jnp.repeat(jnp.arange(rows, dtype=F32), GRID_W)
    col = jnp.tile(jnp.arange(GRID_W, dtype=F32), rows)
    inv_freq = 1.0 / (ROPE_THETA ** (jnp.arange(ROPE_AXIS_FREQS, dtype=F32) / ROPE_AXIS_FREQS))
    ang = jnp.stack([row[:, None] * inv_freq, col[:, None] * inv_freq], axis=1)
    ang = jnp.concatenate([ang, jnp.zeros((CTX_LEN, 2, ROPE_AXIS_FREQS), F32)], axis=0)
    cos = jnp.cos(ang)
    sin = jnp.sin(ang)
    c32 = jnp.stack([cos, cos], axis=2).reshape(T_TOK, MLA_ROPE)
    s32 = jnp.stack([-sin, sin], axis=2).reshape(T_TOK, MLA_ROPE)
    scale = MLA_QK ** -0.5 * math.log2(math.e)
    zq = jnp.zeros((T_TOK, HEAD_PAD - MLA_QK), F32)
    cq = jnp.concatenate([jnp.full((T_TOK, MLA_NOPE), scale, F32), c32 * scale, zq], axis=1)
    sq = jnp.concatenate([jnp.zeros((T_TOK, MLA_NOPE), F32), s32 * scale, zq], axis=1)
    zn = jnp.zeros((T_TOK, MLA_NOPE), F32)
    ck = jnp.concatenate([zn, c32, zq], axis=1)
    sk = jnp.concatenate([zn, s32, zq], axis=1)
    return cq, sq, ck, sk


ATTN_TQ = 2048
ATTN_KEY_CHUNK = 1024
_CTX_CHUNK = ((SEQ, CTX_LEN),)
_ALL_CHUNKS = tuple((lo, ATTN_KEY_CHUNK) for lo in range(0, SEQ, ATTN_KEY_CHUNK)) + _CTX_CHUNK


def _attend_pair(q, k_ref, vt_ref, chunks):
    m = [None, None]
    acc = [None, None]
    for lo, size in chunks:
        for j in range(2):
            cols = slice(j * HEAD_PAD, (j + 1) * HEAD_PAD)
            st = lax.dot_general(k_ref[0, lo:lo + size, cols], q[:, cols], (((1,), (1,)), ((), ())), preferred_element_type=F32)
            mc = jnp.max(st, axis=0, keepdims=True)
            m_new = mc if m[j] is None else jnp.maximum(m[j], mc)
            pv = jnp.dot(vt_ref[cols, lo:lo + size], jnp.exp2(st - m_new).astype(BF16), preferred_element_type=F32)
            acc[j] = pv if m[j] is None else acc[j] * jnp.exp2(m[j] - m_new) + pv
            m[j] = m_new
    outs = [a / a[MLA_V:MLA_V + 1, :] for a in acc]
    return jnp.concatenate([outs[0][:MLA_V], outs[1][:MLA_V]], axis=0).T


def _attn_kernel(q_ref, k_ref, vt_ref, o_ref, *, lat_tiles):
    qi = pl.program_id(2)

    @pl.when(qi < lat_tiles)
    def _():
        o_ref[0] = _attend_pair(q_ref[0], k_ref, vt_ref, _ALL_CHUNKS).astype(BF16)

    @pl.when(qi >= lat_tiles)
    def _():
        o_ref[0, :CTX_LEN, :] = _attend_pair(q_ref[0, :CTX_LEN, :], k_ref, vt_ref, _CTX_CHUNK).astype(BF16)


def _attention(q, k, vt):
    b = q.shape[0]
    tq = ATTN_TQ
    lat_tiles = SEQ // tq
    return pl.pallas_call(
        functools.partial(_attn_kernel, lat_tiles=lat_tiles),
        grid=(b, MLA_HEADS // 2, lat_tiles + 1),
        in_specs=[pl.BlockSpec((1, tq, 2 * HEAD_PAD), lambda bi, hp, qi: (bi, qi, hp)),
                  pl.BlockSpec((1, T_TOK, 2 * HEAD_PAD), lambda bi, hp, qi: (bi, 0, hp)),
                  pl.BlockSpec((2 * HEAD_PAD, T_TOK), lambda bi, hp, qi: (hp, bi))],
        out_specs=pl.BlockSpec((1, tq, 2 * MLA_V), lambda bi, hp, qi: (bi, qi, hp)),
        out_shape=jax.ShapeDtypeStruct((b, T_TOK, MLA_HEADS * MLA_V), BF16),
        compiler_params=_cparams(("parallel", "parallel", "arbitrary")),
        name="attention",
    )(q, k, vt)


def _gated_add(x_ref, mod_ref, y, gate_idx, o_ref):
    for s in range(x_ref.shape[0] // SEG):
        rows = slice(s * SEG, (s + 1) * SEG)
        g = mod_ref[s][gate_idx:gate_idx + 1]
        o_ref[rows, :] = x_ref[rows, :] + g * y[rows, :]


def _proj_res_kernel(x_ref, mod_ref, y_ref, w_ref, o_ref):
    y = jnp.dot(y_ref[...], w_ref[...], preferred_element_type=F32)
    _gated_add(x_ref, mod_ref, y, 2, o_ref)


def _proj_res(x, modtab, y, w, tm=1024):
    n, d = x.shape
    return pl.pallas_call(
        _proj_res_kernel,
        grid=(n // tm,),
        in_specs=[pl.BlockSpec((tm, d), lambda g: (g, 0)),
                  pl.BlockSpec((tm // SEG, N_MOD, d), lambda g: (g, 0, 0)),
                  pl.BlockSpec((tm, y.shape[1]), lambda g: (g, 0)),
                  pl.BlockSpec(w.shape, lambda g: (0, 0))],
        out_specs=pl.BlockSpec((tm, d), lambda g: (g, 0)),
        out_shape=jax.ShapeDtypeStruct((n, d), F32),
        compiler_params=_cparams(("parallel",)),
        name="proj_res",
    )(x, modtab, y, w)


_S5_LAT_CHUNKS = SEQ // S5_CHUNK
_S5_CHUNKS = T_TOK // S5_CHUNK
_S5_W = S5_CHUNK * S5_GROUP
_S5_BLK_CHUNKS = 8
_S5_BLK_TOK = _S5_BLK_CHUNKS * S5_CHUNK
LANES = 128
_SLOTS = LANES // S5_GROUP


def _s5_in_kernel(x_ref, mod_ref, g_ref, u_ref, h_scr, *, nb):
    for b in range(nb):
        m = mod_ref[0, b]
        h = _norm_mod(x_ref[b], g_ref[...], m[0:1], m[1:2])
        for j in range(D_MODEL // LANES):
            h_scr[j, b * _S5_BLK_TOK:(b + 1) * _S5_BLK_TOK, :] = h[:, j * LANES:(j + 1) * LANES]
    slot = lax.broadcasted_iota(jnp.int32, (_S5_BLK_CHUNKS * nb, LANES), 1) // S5_GROUP
    for j in range(D_MODEL // LANES):
        for half in range(_S5_W // LANES):
            srcs = []
            for p in range(_SLOTS):
                t = _SLOTS * half + p
                rows = [h_scr[j, pl.ds(c * S5_CHUNK + t, nb, stride=_S5_BLK_TOK), :] for c in range(_S5_BLK_CHUNKS)]
                srcs.append(jnp.concatenate(rows, axis=0))
            for gs in range(_SLOTS):
                acc = None
                for p in range(_SLOTS):
                    k = (p - gs) % _SLOTS
                    r = pltpu.roll(srcs[p], k * S5_GROUP, 1) if k else srcs[p]
                    acc = r if acc is None else jnp.where(slot == p, r, acc)
                u_ref[_SLOTS * j + gs, :, half * LANES:(half + 1) * LANES] = acc.astype(BF16)


def _s5_in(x3, modsel, g1):
    nb, t, d = x3.shape
    nblk = t // _S5_BLK_TOK
    rows = _S5_BLK_CHUNKS * nb
    return pl.pallas_call(
        functools.partial(_s5_in_kernel, nb=nb),
        grid=(nblk,),
        in_specs=[pl.BlockSpec((nb, _S5_BLK_TOK, d), lambda k: (0, k, 0)),
                  pl.BlockSpec((1, nb, N_MOD, d), lambda k: (k // (SEQ // _S5_BLK_TOK), 0, 0, 0)),
                  pl.BlockSpec(g1.shape, lambda k: (0, 0))],
        out_specs=pl.BlockSpec((S5_GROUPS, rows, _S5_W), lambda k: (0, k, 0)),
        out_shape=jax.ShapeDtypeStruct((S5_GROUPS, nblk * rows, _S5_W), BF16),
        scratch_shapes=[pltpu.VMEM((d // LANES, nb * _S5_BLK_TOK, LANES), F32)],
        compiler_params=_cparams(("parallel",)),
        name="s5_in",
    )(x3, modsel, g1)


def _s5_out_kernel(x_ref, mod_ref, y_ref, w_ref, o_ref, nat_scr, *, nb):
    d = x_ref.shape[2]
    slot = lax.broadcasted_iota(jnp.int32, (nb, LANES), 1) // S5_GROUP
    for j in range(d // LANES):
        for half in range(_S5_W // LANES):
            for c in range(_S5_BLK_CHUNKS):
                srcs = [y_ref[_SLOTS * j + gs, c * nb:(c + 1) * nb, half * LANES:(half + 1) * LANES] for gs in range(_SLOTS)]
                for p in range(_SLOTS):
                    acc = None
                    for gs in range(_SLOTS):
                        k = (gs - p) % _SLOTS
                        r = pltpu.roll(srcs[gs], k * S5_GROUP, 1) if k else srcs[gs]
                        acc = r if acc is None else jnp.where(slot == gs, r, acc)
                    nat_scr[j, pl.ds(c * S5_CHUNK + _SLOTS * half + p, nb, stride=_S5_BLK_TOK), :] = acc
    y_nat = jnp.concatenate([nat_scr[j] for j in range(d // LANES)], axis=1)
    g = jax.nn.gelu(y_nat).astype(BF16)
    z = jnp.dot(g, w_ref[...], preferred_element_type=F32)
    y = z[:, :d] * jax.nn.sigmoid(z[:, d:])
    for b in range(nb):
        gate = mod_ref[0, b][2:3]
        o_ref[b] = x_ref[b] + gate * y[b * _S5_BLK_TOK:(b + 1) * _S5_BLK_TOK, :]


def _s5_out(x3, modsel, y, w):
    nb, t, d = x3.shape
    nblk = t // _S5_BLK_TOK
    rows = _S5_BLK_CHUNKS * nb
    return pl.pallas_call(
        functools.partial(_s5_out_kernel, nb=nb),
        grid=(nblk,),
        in_specs=[pl.BlockSpec((nb, _S5_BLK_TOK, d), lambda k: (0, k, 0)),
                  pl.BlockSpec((1, nb, N_MOD, d), lambda k: (k // (SEQ // _S5_BLK_TOK), 0, 0, 0)),
                  pl.BlockSpec((S5_GROUPS, rows, _S5_W), lambda k: (0, k, 0)),
                  pl.BlockSpec(w.shape, lambda k: (0, 0))],
        out_specs=pl.BlockSpec((nb, _S5_BLK_TOK, d), lambda k: (0, k, 0)),
        out_shape=jax.ShapeDtypeStruct((nb, t, d), F32),
        scratch_shapes=[pltpu.VMEM((d // LANES, nb * _S5_BLK_TOK, LANES), F32)],
        compiler_params=_cparams(("parallel",)),
        name="s5_out",
    )(x3, modsel, y, w)


S5_GROUPS_PER_STEP = 2


def _s5_kernel(u_ref, m_ref, win_ref, wof_ref, wor_ref, ar_ref, ai_ref, y_ref, z_ref, sf_ref, sr_ref, *, nb):
    gp = u_ref.shape[0]
    for g in range(gp):
        z_ref[g] = jnp.dot(u_ref[g], win_ref[g], preferred_element_type=F32)
    half = _S5_W // 2
    ars = [ar_ref[g] for g in range(gp)]
    ais = [ai_ref[g] for g in range(gp)]
    lane = lax.broadcasted_iota(jnp.int32, (nb, _S5_W), 1)
    is_fwd = (lane % half) < S5_STATE

    def step(i, states):
        cf = jnp.where(i < _S5_CHUNKS - _S5_LAT_CHUNKS, i + _S5_LAT_CHUNKS, i - (_S5_CHUNKS - _S5_LAT_CHUNKS))
        cr = _S5_CHUNKS - 1 - i
        rf = pl.multiple_of(cf * nb, nb)
        rr = pl.multiple_of(cr * nb, nb)
        out = []
        for g in range(gp):
            s = states[g]
            sf_ref[g, pl.ds(rf, nb), :] = s
            sr_ref[g, pl.ds(rr, nb), :] = s
            z = jnp.where(is_fwd, z_ref[g, pl.ds(rf, nb), :], z_ref[g, pl.ds(rr, nb), :])
            re = s[:, :half]
            im = s[:, half:]
            out.append(jnp.concatenate([ars[g] * re - ais[g] * im + z[:, :half], ars[g] * im + ais[g] * re + z[:, half:]], axis=1))
        return tuple(out)

    lax.fori_loop(0, _S5_CHUNKS, step, tuple(jnp.zeros((nb, _S5_W), F32) for _ in range(gp)))
    for g in range(gp):
        y = jnp.dot(u_ref[g], m_ref[g], preferred_element_type=F32)
        y = y + jnp.dot(sf_ref[g].astype(BF16), wof_ref[g], preferred_element_type=F32)
        y = y + jnp.dot(sr_ref[g].astype(BF16), wor_ref[g], preferred_element_type=F32)
        y_ref[g] = y


def _s5_core(u, sp, nb):
    g, rows, w = u.shape
    gp = S5_GROUPS_PER_STEP
    blk = lambda a: pl.BlockSpec((gp,) + a.shape[1:], lambda i: (i,) + (0,) * (a.ndim - 1))
    return pl.pallas_call(
        functools.partial(_s5_kernel, nb=nb),
        grid=(g // gp,),
        in_specs=[blk(u), blk(sp["m"]), blk(sp["win"]), blk(sp["wof"]), blk(sp["wor"]), blk(sp["ar"]), blk(sp["ai"])],
        out_specs=pl.BlockSpec((gp, rows, w), lambda i: (i, 0, 0)),
        out_shape=jax.ShapeDtypeStruct((g, rows, w), F32),
        scratch_shapes=[pltpu.VMEM((gp, rows, w), F32), pltpu.VMEM((gp, rows, w), F32), pltpu.VMEM((gp, rows, w), F32)],
        compiler_params=_cparams(("parallel",)),
        name="s5_core",
    )(u, sp["m"], sp["win"], sp["wof"], sp["wor"], sp["ar"], sp["ai"])


def _s5_params(a_re, a_im, log_dt, b_re, b_im, c_re, c_im, d):
    c64 = jnp.complex64
    lam = lax.complex(jnp.minimum(a_re, S5_MAX_RE), a_im)
    lam_dt = lam * jnp.exp(log_dt)[..., None]
    lam_bar = jnp.exp(lam_dt)
    b_bar = ((lam_bar - 1.0) / lam)[..., None] * lax.complex(b_re, b_im)
    c_mat = lax.complex(c_re, c_im)
    taus = jnp.arange(S5_CHUNK + 1, dtype=F32)
    pw = jnp.exp(lam_dt[None] * taus[:, None, None, None].astype(c64))
    hi = lax.Precision.HIGHEST
    kern = jnp.real(jnp.einsum('dgnp,tdgp,dgpm->dgtnm', c_mat, pw[:S5_CHUNK], b_bar, precision=hi))
    eye = jnp.eye(S5_GROUP, dtype=F32)
    k0 = kern[0, :, 0] + kern[1, :, 0] + d.reshape(S5_GROUPS, S5_GROUP)[:, :, None] * eye
    lags = jnp.concatenate([kern[1, :, :0:-1], k0[:, None], kern[0, :, 1:]], axis=1)
    idx = (np.arange(S5_CHUNK)[None, :] - np.arange(S5_CHUNK)[:, None]) + S5_CHUNK - 1
    m = lags[:, idx]
    m = m.transpose(0, 1, 4, 2, 3).reshape(S5_GROUPS, _S5_W, _S5_W)
    wf = pw[S5_CHUNK - 1::-1][:S5_CHUNK, 0][..., None] * b_bar[0][None]
    wr = pw[:S5_CHUNK, 1][..., None] * b_bar[1][None]
    to_rows = lambda w: w.transpose(1, 0, 3, 2).reshape(S5_GROUPS, _S5_W, S5_STATE)
    wf, wr = to_rows(wf), to_rows(wr)
    win = jnp.concatenate([jnp.real(wf), jnp.real(wr), jnp.imag(wf), jnp.imag(wr)], axis=-1)
    of = c_mat[0][None] * pw[1:, 0][:, :, None, :]
    orv = c_mat[1][None] * pw[S5_CHUNK:0:-1, 1][:, :, None, :]
    to_cols = lambda w: w.transpose(1, 3, 0, 2).reshape(S5_GROUPS, S5_STATE, _S5_W)
    of, orv = to_cols(of), to_cols(orv)
    zeros = jnp.zeros_like(jnp.real(of))
    wof = jnp.concatenate([jnp.real(of), zeros, -jnp.imag(of), zeros], axis=1)
    wor = jnp.concatenate([zeros, jnp.real(orv), zeros, -jnp.imag(orv)], axis=1)
    a16 = pw[S5_CHUNK]
    ar = jnp.concatenate([jnp.real(a16[0]), jnp.real(a16[1])], axis=-1)[:, None, :]
    ai = jnp.concatenate([jnp.imag(a16[0]), jnp.imag(a16[1])], axis=-1)[:, None, :]
    return {"m": m.astype(BF16), "win": win.astype(BF16), "wof": wof.astype(BF16), "wor": wor.astype(BF16),
            "ar": ar, "ai": ai}


POOL_HALO = 16


def _pool_kernel(xp_ref, xc_ref, xn_ref, mod_ref, g_ref, wp_ref, sc_ref, o_ref):
    seg = pl.program_id(0) % SEGS_PER_BATCH
    is_ctx = seg >= LAT_SEGS
    p0 = jnp.where(is_ctx, 0, seg * SEG)
    lseq = jnp.where(is_ctx, CTX_LEN, SEQ)
    m = mod_ref[0]
    g = g_ref[...]
    hs = [_norm_mod(r[...], g, m[0:1], m[1:2]) for r in (xp_ref, xc_ref, xn_ref)]
    hcat = jnp.concatenate([h.astype(BF16) for h in hs], axis=0)
    r = lax.broadcasted_iota(jnp.int32, (SEG, SEG + 2 * POOL_HALO), 0)
    s = lax.broadcasted_iota(jnp.int32, (SEG, SEG + 2 * POOL_HALO), 1)
    pt = p0 + r
    ps = p0 - POOL_HALO + s
    rcol = lax.broadcasted_iota(jnp.int32, (SEG, 1), 0) + p0
    x = xc_ref[...]
    gate = m[2:3]
    for gi, w in enumerate(POOL_WINDOWS):
        cols = slice(gi * POOL_GROUP, (gi + 1) * POOL_GROUP)
        lo = jnp.maximum(pt - w // 2, 0)
        hi = jnp.minimum(pt + w - w // 2, lseq)
        band = jnp.where((ps >= lo) & (ps < hi), 1.0, 0.0).astype(BF16)
        tot = jnp.dot(band, hcat[:, cols], preferred_element_type=F32)
        cnt = (jnp.minimum(rcol + w - w // 2, lseq) - jnp.maximum(rcol - w // 2, 0)).astype(F32)
        resid = tot / cnt - hs[1][:, cols]
        y = jnp.dot(resid.astype(BF16), wp_ref[gi], preferred_element_type=F32) * sc_ref[:, cols]
        o_ref[:, cols] = x[:, cols] + gate[:, cols] * y


def _pool(x, modtab, g1, w_pool, scale):
    n, d = x.shape
    nseg = n // SEG

    halo_per_seg = SEG // POOL_HALO

    def prev(g):
        return (jnp.maximum(g * halo_per_seg - 1, 0), 0)

    def nxt(g):
        return (jnp.minimum((g + 1) * halo_per_seg, nseg * halo_per_seg - 1), 0)

    return pl.pallas_call(
        _pool_kernel,
        grid=(nseg,),
        in_specs=[pl.BlockSpec((POOL_HALO, d), prev), pl.BlockSpec((SEG, d), lambda g: (g, 0)), pl.BlockSpec((POOL_HALO, d), nxt),
                  pl.BlockSpec((1, N_MOD, d), lambda g: (g, 0, 0)),
                  pl.BlockSpec(g1.shape, lambda g: (0, 0)),
                  pl.BlockSpec(w_pool.shape, lambda g: (0, 0, 0)),
                  pl.BlockSpec(scale.shape, lambda g: (0, 0))],
        out_specs=pl.BlockSpec((SEG, d), lambda g: (g, 0)),
        out_shape=jax.ShapeDtypeStruct((n, d), F32),
        compiler_params=_cparams(("parallel",)),
        name="pool",
    )(x, x, x, modtab, g1, w_pool, scale)


def _swiglu_partial(h, wg, wu, wd):
    a = jax.nn.silu(jnp.dot(h, wg, preferred_element_type=F32)) * jnp.dot(h, wu, preferred_element_type=F32)
    return jnp.dot(a.astype(BF16), wd, preferred_element_type=F32)


def _ffn_kernel(x_ref, mod_ref, g_ref, wg_ref, wu_ref, wd_ref, o_ref):
    h = _norm_mod_tile(x_ref, mod_ref, g_ref, 3)
    _gated_add(x_ref, mod_ref, _swiglu_partial(h, wg_ref[...], wu_ref[...], wd_ref[...]), 5, o_ref)


_RESIDENT = pl.Buffered(1)


def _ffn(x, modtab, g2, w_gu, w_down, layer, tm=512):
    n, d = x.shape
    f = w_down.shape[1]
    return pl.pallas_call(
        _ffn_kernel,
        grid=(n // tm,),
        in_specs=[pl.BlockSpec((tm, d), lambda g: (g, 0)),
                  pl.BlockSpec((tm // SEG, N_MOD, d), lambda g: (g, 0, 0)),
                  pl.BlockSpec(g2.shape, lambda g: (0, 0)),
                  pl.BlockSpec((None, d, f), lambda g: (layer, 0, 0), pipeline_mode=_RESIDENT),
                  pl.BlockSpec((None, d, f), lambda g: (layer, 0, 1), pipeline_mode=_RESIDENT),
                  pl.BlockSpec((None, f, d), lambda g: (layer, 0, 0), pipeline_mode=_RESIDENT)],
        out_specs=pl.BlockSpec((tm, d), lambda g: (g, 0)),
        out_shape=jax.ShapeDtypeStruct((n, d), F32),
        compiler_params=_cparams(("parallel",)),
        name="ffn",
    )(x, modtab, g2, w_gu, w_gu, w_down)


MOE_TILE = 512
MOE_DMA_TILE = 1024


def _router_kernel(x_ref, mod_ref, g_ref, w_ref, b_ref, h_ref, rw_ref, ri_ref, cnt_ref, run_ref):
    @pl.when(pl.program_id(0) == 0)
    def _():
        run_ref[...] = jnp.zeros_like(run_ref)

    h = _norm_mod_tile(x_ref, mod_ref, g_ref, 3, out_dtype=F32)
    h_ref[...] = h
    logits = jnp.dot(h, w_ref[...], precision=lax.Precision.HIGHEST, preferred_element_type=F32) + b_ref[...]
    lane = lax.broadcasted_iota(jnp.int32, logits.shape, 1)
    neg = jnp.float32(-jnp.inf)
    lg = jnp.where(lane < N_EXPERTS, logits, neg)
    m1 = jnp.max(lg, axis=-1, keepdims=True)
    i1 = jnp.min(jnp.where(lg == m1, lane, HEAD_PAD), axis=-1, keepdims=True)
    lg2 = jnp.where(lane == i1, neg, lg)
    m2 = jnp.max(lg2, axis=-1, keepdims=True)
    i2 = jnp.min(jnp.where(lg2 == m2, lane, HEAD_PAD), axis=-1, keepdims=True)
    e2 = jnp.exp(m2 - m1)
    den = 1.0 + e2
    tm = logits.shape[0]
    sel = jnp.where((lane == i1) | (lane == i2), 1.0, 0.0).astype(BF16)
    tri = jnp.where(lax.broadcasted_iota(jnp.int32, (tm, tm), 1) <= lax.broadcasted_iota(jnp.int32, (tm, tm), 0), 1.0, 0.0)
    within = jnp.dot(tri.astype(BF16), sel, preferred_element_type=F32)
    rank = run_ref[0:1, :] + within - 1.0
    r1 = jnp.sum(jnp.where(lane == i1, rank, 0.0), axis=-1, keepdims=True).astype(jnp.int32)
    r2 = jnp.sum(jnp.where(lane == i2, rank, 0.0), axis=-1, keepdims=True).astype(jnp.int32)
    total = run_ref[0:1, :] + within[tm - 1:tm, :]
    run_ref[...] = jnp.broadcast_to(total, run_ref.shape)
    cnt_ref[...] = jnp.broadcast_to(total, cnt_ref.shape).astype(jnp.int32)
    rw_ref[...] = jnp.where(lane == 0, 1.0 / den, jnp.where(lane == 1, e2 / den, 0.0))
    ri_ref[...] = jnp.where(lane == 0, i1, jnp.where(lane == 1, i2, jnp.where(lane == 2, r1, jnp.where(lane == 3, r2, 0))))


def _router(x, modtab, g2, w_router, b_router, tm=512):
    n, d = x.shape
    wr = jnp.pad(w_router, ((0, 0), (0, HEAD_PAD - N_EXPERTS)))
    br = jnp.pad(b_router, (0, HEAD_PAD - N_EXPERTS)).reshape(1, HEAD_PAD)
    return pl.pallas_call(
        _router_kernel,
        grid=(n // tm,),
        in_specs=[pl.BlockSpec((tm, d), lambda g: (g, 0)),
                  pl.BlockSpec((tm // SEG, N_MOD, d), lambda g: (g, 0, 0)),
                  pl.BlockSpec(g2.shape, lambda g: (0, 0)),
                  pl.BlockSpec(wr.shape, lambda g: (0, 0)),
                  pl.BlockSpec(br.shape, lambda g: (0, 0))],
        out_specs=[pl.BlockSpec((tm, d), lambda g: (g, 0)),
                   pl.BlockSpec((tm, HEAD_PAD), lambda g: (g, 0)),
                   pl.BlockSpec((tm, HEAD_PAD), lambda g: (g, 0)),
                   pl.BlockSpec((8, HEAD_PAD), lambda g: (0, 0))],
        out_shape=[jax.ShapeDtypeStruct((n, d), F32),
                   jax.ShapeDtypeStruct((n, HEAD_PAD), F32),
                   jax.ShapeDtypeStruct((n, HEAD_PAD), jnp.int32),
                   jax.ShapeDtypeStruct((8, HEAD_PAD), jnp.int32)],
        scratch_shapes=[pltpu.VMEM((8, HEAD_PAD), F32)],
        compiler_params=_cparams(("arbitrary",)),
        name="router",
    )(x, modtab, g2, wr, br)


def _moe_plan(ri, cnt, n):
    e = ri[:, :TOP_K]
    counts = cnt[0, :N_EXPERTS]
    padded = (counts + MOE_TILE - 1) // MOE_TILE * MOE_TILE
    ends = jnp.cumsum(padded)
    starts = ends - padded
    dest = starts[e] + ri[:, TOP_K:2 * TOP_K]
    n_tiles = (TOP_K * n) // MOE_TILE + N_EXPERTS
    n_valid = (ends[-1] // MOE_TILE).astype(jnp.int32).reshape(1)
    tile_ids = jnp.arange(n_tiles, dtype=jnp.int32)
    tile_expert = jnp.sum((ends // MOE_TILE)[None, :] <= jnp.minimum(tile_ids, n_valid - 1)[:, None], axis=1).astype(jnp.int32)
    seg_len = jnp.concatenate([padded - counts, (n_tiles * MOE_TILE - ends[-1])[None]])
    seg_base = jnp.concatenate([starts + counts, ends[-1:]])
    seg_end = jnp.cumsum(seg_len)
    k = jnp.arange(N_EXPERTS * MOE_TILE, dtype=jnp.int32)
    seg = jnp.sum(seg_end[None, :] <= k[:, None], axis=1)
    pad_dest = seg_base[seg] + k - (seg_end - seg_len)[seg]
    return dest[:, 0].astype(jnp.int32), dest[:, 1].astype(jnp.int32), pad_dest.astype(jnp.int32), tile_expert, n_valid, n_tiles


def _row(ref, i):
    return ref.at[pl.ds(i, 1)]


def _scatter_rows_kernel(d0_ref, d1_ref, pad_ref, h_ref, o_hbm, zero_ref, sem):
    @pl.when(pl.program_id(0) == 0)
    def _():
        zero_ref[...] = jnp.zeros_like(zero_ref)

        def issue_zero(k, carry):
            pltpu.make_async_copy(_row(zero_ref, 0), _row(o_hbm, pad_ref[k]), sem.at[2]).start()
            return carry

        lax.fori_loop(0, N_EXPERTS * MOE_TILE, issue_zero, 0, unroll=8)
        for _ in range(N_EXPERTS * MOE_TILE // MOE_DMA_TILE):
            pltpu.make_async_copy(h_ref, o_hbm.at[pl.ds(0, MOE_DMA_TILE)], sem.at[2]).wait()

    base = pl.program_id(0) * MOE_DMA_TILE

    def issue(r, carry):
        t = base + r
        pltpu.make_async_copy(_row(h_ref, r), _row(o_hbm, d0_ref[t]), sem.at[0]).start(priority=0)
        pltpu.make_async_copy(_row(h_ref, r), _row(o_hbm, d1_ref[t]), sem.at[1]).start(priority=1)
        return carry

    lax.fori_loop(0, MOE_DMA_TILE, issue, 0, unroll=8)
    pltpu.make_async_copy(h_ref, o_hbm.at[pl.ds(0, MOE_DMA_TILE)], sem.at[0]).wait()
    pltpu.make_async_copy(h_ref, o_hbm.at[pl.ds(0, MOE_DMA_TILE)], sem.at[1]).wait()


def _scatter_rows(d0, d1, pad_dest, h, n_rows):
    n, d = h.shape
    return pl.pallas_call(
        _scatter_rows_kernel,
        grid_spec=pltpu.PrefetchScalarGridSpec(
            num_scalar_prefetch=3,
            grid=(n // MOE_DMA_TILE,),
            in_specs=[pl.BlockSpec((MOE_DMA_TILE, d), lambda g, d0, d1, pd: (g, 0))],
            out_specs=pl.BlockSpec(memory_space=pl.ANY),
            scratch_shapes=[pltpu.VMEM((8, d), F32), pltpu.SemaphoreType.DMA((3,))]),
        out_shape=jax.ShapeDtypeStruct((n_rows, d), F32),
        compiler_params=_cparams(("arbitrary",)),
        name="moe_scatter",
    )(d0, d1, pad_dest, h)


def _moe_ffn_kernel(te_ref, nv_ref, xs_ref, wg_ref, wu_ref, wd_ref, o_ref):
    del te_ref
    i = pl.program_id(0)

    @pl.when(i < nv_ref[0])
    def _():
        o_ref[...] = _swiglu_partial(xs_ref[...].astype(BF16), wg_ref[...], wu_ref[...], wd_ref[...])

    @pl.when(i >= nv_ref[0])
    def _():
        o_ref[...] = jnp.zeros_like(o_ref)


def _moe_ffn(xs, n_rows, tile_expert, n_valid, w_gu, w_down, layer):
    d = xs.shape[1]
    rows = n_rows
    f = w_down.shape[2]
    return pl.pallas_call(
        _moe_ffn_kernel,
        grid_spec=pltpu.PrefetchScalarGridSpec(
            num_scalar_prefetch=2,
            grid=(rows // MOE_TILE,),
            in_specs=[pl.BlockSpec((MOE_TILE, d), lambda i, te, nv: (i, 0)),
                      pl.BlockSpec((None, None, d, f), lambda i, te, nv: (layer, te[i], 0, 0), pipeline_mode=_RESIDENT),
                      pl.BlockSpec((None, None, d, f), lambda i, te, nv: (layer, te[i], 0, 1), pipeline_mode=_RESIDENT),
                      pl.BlockSpec((None, None, f, d), lambda i, te, nv: (layer, te[i], 0, 0), pipeline_mode=_RESIDENT)],
            out_specs=pl.BlockSpec((MOE_TILE, d), lambda i, te, nv: (i, 0))),
        out_shape=jax.ShapeDtypeStruct((rows, d), F32),
        compiler_params=_cparams(("arbitrary",)),
        name="moe_ffn",
    )(tile_expert, n_valid, xs, w_gu, w_gu, w_down)


def _combine_kernel(d0_ref, d1_ref, x_ref, mod_ref, rw_ref, y_hbm, *rest, tm, final):
    if final:
        fg_ref, o_ref, ya_ref, yb_ref, sem = rest
        tile = pl.program_id(0) * SEGS_PER_BATCH + pl.program_id(1)
    else:
        o_ref, ya_ref, yb_ref, sem = rest
        tile = pl.program_id(0)
    base = tile * tm

    def issue(r, carry):
        t = base + r
        pltpu.make_async_copy(_row(y_hbm, d0_ref[t]), _row(ya_ref, r), sem.at[0]).start(priority=0)
        pltpu.make_async_copy(_row(y_hbm, d1_ref[t]), _row(yb_ref, r), sem.at[1]).start(priority=1)
        return carry

    lax.fori_loop(0, tm, issue, 0, unroll=8)
    rw = rw_ref[...]
    lane = lax.broadcasted_iota(jnp.int32, rw.shape, 1)
    w1 = jnp.sum(jnp.where(lane == 0, rw, 0.0), axis=-1, keepdims=True)
    w2 = jnp.sum(jnp.where(lane == 1, rw, 0.0), axis=-1, keepdims=True)
    pltpu.make_async_copy(y_hbm.at[pl.ds(0, tm)], ya_ref, sem.at[0]).wait()
    pltpu.make_async_copy(y_hbm.at[pl.ds(0, tm)], yb_ref, sem.at[1]).wait()
    y = w1 * ya_ref[...] + w2 * yb_ref[...]
    if final:
        o_ref[0] = _rms(x_ref[...] + mod_ref[0][5:6] * y, fg_ref[...])
    else:
        _gated_add(x_ref, mod_ref, y, 5, o_ref)


def _combine(d0, d1, x, modtab, rw, y, final_g=None):
    n, d = x.shape
    final = final_g is not None
    tm = SEG if final else MOE_DMA_TILE
    if final:
        nb = n // T_TOK
        grid = (nb, LAT_SEGS)
        row = lambda bi, j, d0, d1: (bi * SEGS_PER_BATCH + j, 0)
        mrow = lambda bi, j, d0, d1: (bi * SEGS_PER_BATCH + j, 0, 0)
        extra = [pl.BlockSpec(final_g.shape, lambda bi, j, d0, d1: (0, 0))]
        out_spec = pl.BlockSpec((1, tm, d), lambda bi, j, d0, d1: (bi, j, 0))
        out_shape = jax.ShapeDtypeStruct((nb, SEQ, d), F32)
        args = (final_g,)
    else:
        grid = (n // tm,)
        row = lambda g, d0, d1: (g, 0)
        mrow = lambda g, d0, d1: (g, 0, 0)
        extra = []
        out_spec = pl.BlockSpec((tm, d), row)
        out_shape = jax.ShapeDtypeStruct((n, d), F32)
        args = ()
    return pl.pallas_call(
        functools.partial(_combine_kernel, tm=tm, final=final),
        grid_spec=pltpu.PrefetchScalarGridSpec(
            num_scalar_prefetch=2,
            grid=grid,
            in_specs=[pl.BlockSpec((tm, d), row),
                      pl.BlockSpec((tm // SEG, N_MOD, d), mrow),
                      pl.BlockSpec((tm, HEAD_PAD), row),
                      pl.BlockSpec(memory_space=pl.ANY)] + extra,
            out_specs=out_spec,
            scratch_shapes=[pltpu.VMEM((tm, d), F32), pltpu.VMEM((tm, d), F32), pltpu.SemaphoreType.DMA((2,))]),
        out_shape=out_shape,
        compiler_params=_cparams(("arbitrary",) * len(grid)),
        name="moe_combine",
    )(d0, d1, x, modtab, rw, y, *args)


def _moe(x, modtab, g2, w_router, b_router, w_gu, w_down, layer, final_g=None):
    n = x.shape[0]
    h, rw, ri, cnt = _router(x, modtab, g2, w_router, b_router)
    d0, d1, pad_dest, tile_expert, n_valid, n_tiles = _moe_plan(ri, cnt, n)
    xs = _scatter_rows(d0, d1, pad_dest, h, n_tiles * MOE_TILE)
    ys = _moe_ffn(xs, n_tiles * MOE_TILE, tile_expert, n_valid, w_gu, w_down, layer)
    return _combine(d0, d1, x, modtab, rw, ys, final_g)


def kernel(x, c, ctx, c_ctx, norm1_g, norm2_g, ada_w, ada_b, final_norm_g, mla_w_dq, mla_q_norm, mla_w_uq, mla_w_dkv, mla_kv_norm, mla_w_ukv, mla_w_o, s5_a_re, s5_a_im, s5_log_dt, s5_b_re, s5_b_im, s5_c_re, s5_c_im, s5_d, s5_w_glu, pool_w, pool_scale, ffn_w_gu, ffn_w_down, moe_w_router, moe_b_router, moe_w_gu, moe_w_down):
    b, l, d = x.shape
    n = b * T_TOK
    xs = jnp.concatenate([x, ctx], axis=1).reshape(n, d)

    rows = -(-(b + 1) // 8) * 8
    cc = jnp.concatenate([c, c_ctx[None], jnp.zeros((rows - b - 1, d), F32)], axis=0)
    mods = _ada(cc, ada_w, ada_b)
    seg_src = np.array([bi if sj < LAT_SEGS else b for bi in range(b) for sj in range(SEGS_PER_BATCH)], np.int32)

    rope = _rope_tables()
    ffn_w_gu_bf, ffn_w_down_bf = ffn_w_gu.astype(BF16), ffn_w_down.astype(BF16)
    moe_w_gu_bf, moe_w_down_bf = moe_w_gu.astype(BF16), moe_w_down.astype(BF16)
    for i in range(DEPTH):
        last = i == DEPTH - 1
        modtab = mods[i].reshape(rows, N_MOD, d)[seg_src]
        g1 = norm1_g[i].reshape(1, d)
        g2 = norm2_g[i].reshape(1, d)
        kind = i % N_MIXERS
        j = i // N_MIXERS
        if kind == 0:
            pw = _mla_weights(mla_w_dq[j], mla_q_norm[j], mla_w_uq[j], mla_w_dkv[j], mla_kv_norm[j], mla_w_ukv[j])
            q, k, vt = _mla_proj(xs, modtab, g1, pw, rope)
            o = _attention(q.reshape(b, T_TOK, -1), k.reshape(b, T_TOK, -1), vt)
            xs = _proj_res(xs, modtab, o.reshape(n, -1), mla_w_o[j].astype(BF16))
        elif kind == 1:
            mod9 = mods[i].reshape(rows, N_MOD, d)
            modsel = jnp.stack([mod9[:b], jnp.broadcast_to(mod9[b], (b, N_MOD, d))])
            x3 = xs.reshape(b, T_TOK, d)
            sp = _s5_params(s5_a_re[j], s5_a_im[j], s5_log_dt[j], s5_b_re[j], s5_b_im[j], s5_c_re[j], s5_c_im[j], s5_d[j])
            y = _s5_core(_s5_in(x3, modsel, g1), sp, b)
            xs = _s5_out(x3, modsel, y, s5_w_glu[j].astype(BF16)).reshape(n, d)
        else:
            xs = _pool(xs, modtab, g1, pool_w[j].astype(BF16), pool_scale[j].reshape(1, d))
        kk = i // 2
        if i % 2 == 0:
            xs = _ffn(xs, modtab, g2, ffn_w_gu_bf, ffn_w_down_bf, kk)
        else:
            xs = _moe(xs, modtab, g2, moe_w_router[kk], moe_b_router[kk], moe_w_gu_bf, moe_w_down_bf, kk,
                      final_norm_g.reshape(1, d) if last else None)
    return xs
```
